```python
import jax, jax.numpy as jnp
from jax import lax
import numpy as np

D_MODEL = 2048
BATCH = 16
SEQ = 256
DEPTH = 2
DEC_BATCH = 8
DEC_SEQ = 1024
PAST_LEN = 256

GRID_W = 64
BLK = 128
WINDOW = 128
N_HEADS = 16
N_KV_HEADS = 4
HEAD_DIM = 64
Q_GROUPS = N_HEADS // N_KV_HEADS
ATTN_WIDTH = N_HEADS * HEAD_DIM
KV_WIDTH = N_KV_HEADS * HEAD_DIM
ROPE_BASE = 10000.0
SGU_GROUPS = 4
SGU_WIDTH = D_MODEL // 2
SGU_GROUP_DIM = SGU_WIDTH // SGU_GROUPS
CHUNK = 128
IN_WIDTH_EVEN = ATTN_WIDTH + 2 * KV_WIDTH + 2 * SGU_WIDTH
MIX_WIDTH_EVEN = ATTN_WIDTH + SGU_WIDTH
CONV_DIM = D_MODEL
CONV_K = 31
N_GROUPS_MOE = 4
EXPERTS_PER_GROUP = 4
N_EXPERTS = N_GROUPS_MOE * EXPERTS_PER_GROUP
TOP_K = 2
D_EXPERT = 512
MOE_BLK = 128
N_EVEN = (DEPTH + 1) // 2
N_ODD = DEPTH // 2
EPS = 1e-6
NEG_INF = -1e30

kernel_name = "hybrid_diffusion_prefix_step"


def rmsnorm(x, g):
    xf = x.astype(jnp.float32)
    y = xf * lax.rsqrt(jnp.mean(xf * xf, axis=-1, keepdims=True) + EPS)
    return (y * g.astype(jnp.float32)).astype(x.dtype)


def adaln_params(cvec, w, b):
    m = jax.nn.silu(cvec) @ w + b
    return jnp.split(m[:, None, :], 6, axis=-1)


def rope_2d(x):
    n = x.shape[1]
    rows = n // GRID_W
    row, col = jnp.meshgrid(jnp.arange(rows), jnp.arange(GRID_W), indexing="ij")
    row = row.reshape(-1).astype(jnp.float32)
    col = col.reshape(-1).astype(jnp.float32)
    half = HEAD_DIM // 2
    nf = half // 2
    inv_freq = ROPE_BASE ** (-jnp.arange(nf, dtype=jnp.float32) / nf)

    def rot(xp, pos):
        ang = pos[:, None] * inv_freq[None, :]
        cos = jnp.cos(ang)[None, :, None, :]
        sin = jnp.sin(ang)[None, :, None, :]
        x1 = xp[..., :nf].astype(jnp.float32)
        x2 = xp[..., nf:].astype(jnp.float32)
        return jnp.concatenate([x1 * cos - x2 * sin, x2 * cos + x1 * sin], axis=-1)

    return jnp.concatenate([rot(x[..., :half], row), rot(x[..., half:], col)], axis=-1).astype(x.dtype)


def sink_softmax(s, sink):
    sb = sink.astype(jnp.float32).reshape(1, N_KV_HEADS, Q_GROUPS, 1, 1)
    m = jnp.maximum(jnp.max(s, axis=-1, keepdims=True), sb)
    e = jnp.exp(s - m)
    return e / (jnp.sum(e, axis=-1, keepdims=True) + jnp.exp(sb - m))


def ctx_attention(q, k, v, sink):
    b, s = q.shape[:2]
    scale = HEAD_DIM ** -0.5
    qb = q.reshape(b, s // BLK, BLK, N_KV_HEADS, Q_GROUPS, HEAD_DIM).swapaxes(0, 1)

    def block(qblk):
        sc = jnp.einsum("bqkgd,bskd->bkgqs", qblk, k, preferred_element_type=jnp.float32) * scale
        p = sink_softmax(sc, sink).astype(v.dtype)
        return jnp.einsum("bkgqs,bskd->bqkgd", p, v)

    o = lax.map(block, qb)
    return o.swapaxes(0, 1).reshape(b, s, ATTN_WIDTH)


def latent_attention(q, k, v, kc, vc, sink):
    b, n = q.shape[:2]
    p_len = kc.shape[1]
    scale = HEAD_DIM ** -0.5
    pad = ((0, 0), (BLK, BLK), (0, 0), (0, 0))
    kp = jnp.pad(k, pad)
    vp = jnp.pad(v, pad)
    qg = q.reshape(b, n, N_KV_HEADS, Q_GROUPS, HEAD_DIM)

    def block(i):
        start = i * BLK
        qblk = lax.dynamic_slice_in_dim(qg, start, BLK, axis=1)
        kw = lax.dynamic_slice_in_dim(kp, start, 3 * BLK, axis=1)
        vw = lax.dynamic_slice_in_dim(vp, start, 3 * BLK, axis=1)
        qpos = start + jnp.arange(BLK)
        kpos = start - BLK + jnp.arange(3 * BLK)
        valid = (jnp.abs(qpos[:, None] - kpos[None, :]) <= WINDOW) & (kpos >= 0)[None, :] & (kpos < n)[None, :]
        s_loc = jnp.einsum("bqkgd,bskd->bkgqs", qblk, kw, preferred_element_type=jnp.float32) * scale
        s_loc = jnp.where(valid, s_loc, NEG_INF)
        s_ctx = jnp.einsum("bqkgd,bskd->bkgqs", qblk, kc, preferred_element_type=jnp.float32) * scale
        p = sink_softmax(jnp.concatenate([s_ctx, s_loc], axis=-1), sink).astype(v.dtype)
        return (jnp.einsum("bkgqs,bskd->bqkgd", p[..., :p_len], vc)
                + jnp.einsum("bkgqs,bskd->bqkgd", p[..., p_len:], vw))

    o = lax.map(block, jnp.arange(n // BLK))
    return o.swapaxes(0, 1).reshape(b, n, ATTN_WIDTH)


def spatial_gating(gu, norm_g, w_s, b_s):
    b, n, _ = gu.shape
    u, v = jnp.split(jax.nn.gelu(gu), 2, axis=-1)
    v = rmsnorm(v, norm_g).reshape(b, n // CHUNK, CHUNK, SGU_GROUPS, SGU_GROUP_DIM)
    mixed = jnp.einsum("gpq,bnqgc->bnpgc", w_s, v) + jnp.swapaxes(b_s, 0, 1)[None, None, :, :, None]
    return u * mixed.reshape(b, n, SGU_WIDTH)


def attn_gmlp_mixer(h, w_in, sink, sgu_norm, sgu_w, sgu_b, w_out, ctx_kv):
    b, n, _ = h.shape
    proj = h @ w_in
    q, k, v, gu = jnp.split(proj, [ATTN_WIDTH, ATTN_WIDTH + KV_WIDTH, ATTN_WIDTH + 2 * KV_WIDTH], axis=-1)
    q = q.reshape(b, n, N_HEADS, HEAD_DIM)
    k = k.reshape(b, n, N_KV_HEADS, HEAD_DIM)
    v = v.reshape(b, n, N_KV_HEADS, HEAD_DIM)
    if ctx_kv is None:
        a = ctx_attention(q, k, v, sink)
        kv_out = (k, v)
    else:
        a = latent_attention(rope_2d(q), rope_2d(k), v, ctx_kv[0], ctx_kv[1], sink)
        kv_out = None
    z = spatial_gating(gu, sgu_norm, sgu_w, sgu_b)
    return jnp.concatenate([a, z], axis=-1) @ w_out, kv_out


def conv_mixer(h, w_in, dw, dw_b, norm_g, w_out):
    a, g = jnp.split(h @ w_in, 2, axis=-1)
    x = a * jax.nn.sigmoid(g)
    y = lax.conv_general_dilated(x, dw[:, None, :], window_strides=(1,), padding=[(CONV_K // 2, CONV_K // 2)],
                                 dimension_numbers=("NWC", "WIO", "NWC"), feature_group_count=CONV_DIM) + dw_b
    return jax.nn.silu(rmsnorm(y, norm_g)) @ w_out


def hier_moe(x, w_rg, b_rg, w_re, b_re, w_gate, w_up, w_down):
    t, d = x.shape
    g_prob = jax.nn.softmax(jnp.dot(x, w_rg, preferred_element_type=jnp.float32) + b_rg.astype(jnp.float32), axis=-1)
    g_idx = jnp.argmax(g_prob, axis=-1)
    p_g = jnp.take_along_axis(g_prob, g_idx[:, None], axis=-1)
    e_logits = (jnp.dot(x, w_re, preferred_element_type=jnp.float32) + b_re.astype(jnp.float32)).reshape(t, N_GROUPS_MOE, EXPERTS_PER_GROUP)
    e_grp = jnp.take_along_axis(e_logits, g_idx[:, None, None], axis=1)[:, 0]
    e_top, e_loc = lax.top_k(e_grp, TOP_K)
    gate = jax.nn.softmax(e_top, axis=-1) * p_g
    expert = (g_idx[:, None] * EXPERTS_PER_GROUP + e_loc).reshape(-1).astype(jnp.int32)
    a = t * TOP_K
    onehot = jax.nn.one_hot(expert, N_EXPERTS, dtype=jnp.int32)
    rank = jnp.sum((jnp.cumsum(onehot, axis=0) - onehot) * onehot, axis=-1)
    counts = jnp.sum(onehot, axis=0)
    padded = (counts + MOE_BLK - 1) // MOE_BLK * MOE_BLK
    ends = jnp.cumsum(padded)
    starts = ends - padded
    dest = starts[expert] + rank
    n_blocks = -(-(a + N_EXPERTS * (MOE_BLK - 1)) // MOE_BLK)
    buf = jnp.zeros((n_blocks * MOE_BLK, d), x.dtype).at[dest].set(jnp.repeat(x, TOP_K, axis=0))
    block_expert = jnp.minimum(jnp.searchsorted(ends, jnp.arange(n_blocks, dtype=jnp.int32) * MOE_BLK, side="right"), N_EXPERTS - 1)

    def run_block(args):
        xb, e = args
        hb = jax.nn.silu(xb @ w_gate[e]) * (xb @ w_up[e])
        return hb @ w_down[e]

    out = lax.map(run_block, (buf.reshape(n_blocks, MOE_BLK, d), block_expert)).reshape(n_blocks * MOE_BLK, d)
    y = out[dest].reshape(t, TOP_K, d)
    return jnp.sum(y * gate[..., None].astype(x.dtype), axis=1)


def moe_sublayer(x, shift, scale, gate, gain, w_rg, b_rg, w_re, b_re, w_gate, w_up, w_down):
    b, n, d = x.shape
    h = rmsnorm(x, gain) * (1 + scale) + shift
    y = hier_moe(h.reshape(b * n, d), w_rg, b_rg, w_re, b_re, w_gate, w_up, w_down)
    return x + gate * y.reshape(b, n, d)


def setup_inputs(seed: int = 0) -> dict:
    key = jax.random.key(seed)
    ks = jax.random.split(key, 32)
    D = D_MODEL

    def nrm(k, shape, scale):
        return jax.random.normal(k, shape, jnp.float32) * scale

    return {
        "x_prompt": nrm(ks[0], (BATCH, SEQ, D), 1.0),
        "x_sample": nrm(ks[1], (DEC_BATCH, DEC_SEQ, D), 1.0),
        "cache_k": nrm(ks[2], (DEC_BATCH, N_EVEN, PAST_LEN, N_KV_HEADS, HEAD_DIM), 1.0),
        "cache_v": nrm(ks[3], (DEC_BATCH, N_EVEN, PAST_LEN, N_KV_HEADS, HEAD_DIM), 1.0),
        "c": nrm(ks[4], (DEC_BATCH, D), 1.0),
        "c_ctx": nrm(ks[5], (D,), 1.0),
        "w_in_even": nrm(ks[6], (N_EVEN, D, IN_WIDTH_EVEN), D ** -0.5),
        "attn_sink": nrm(ks[7], (N_EVEN, N_HEADS), 1.0),
        "sgu_norm": 1.0 + nrm(ks[8], (N_EVEN, SGU_WIDTH), 0.1),
        "sgu_w": nrm(ks[9], (N_EVEN, SGU_GROUPS, CHUNK, CHUNK), CHUNK ** -0.5),
        "sgu_b": 1.0 + nrm(ks[10], (N_EVEN, SGU_GROUPS, CHUNK), 0.1),
        "w_out_even": nrm(ks[11], (N_EVEN, MIX_WIDTH_EVEN, D), MIX_WIDTH_EVEN ** -0.5),
        "conv_w_in": nrm(ks[12], (N_ODD, D, 2 * CONV_DIM), D ** -0.5),
        "conv_dw": nrm(ks[13], (N_ODD, CONV_K, CONV_DIM), CONV_K ** -0.5),
        "conv_dw_b": nrm(ks[14], (N_ODD, CONV_DIM), 0.02),
        "conv_norm": 1.0 + nrm(ks[15], (N_ODD, CONV_DIM), 0.1),
        "conv_w_out": nrm(ks[16], (N_ODD, CONV_DIM, D), CONV_DIM ** -0.5),
        "ada_w": nrm(ks[17], (DEPTH, D, 6 * D), 0.5 * D ** -0.5),
        "ada_b": nrm(ks[18], (DEPTH, 6 * D), 0.02),
        "norm_mix": 1.0 + nrm(ks[19], (DEPTH, D), 0.1),
        "norm_ffn": 1.0 + nrm(ks[20], (DEPTH, D), 0.1),
        "router_group_w": nrm(ks[21], (DEPTH, D, N_GROUPS_MOE), D ** -0.5),
        "router_group_b": nrm(ks[22], (DEPTH, N_GROUPS_MOE), 0.01),
        "router_expert_w": nrm(ks[23], (DEPTH, D, N_EXPERTS), D ** -0.5),
        "router_expert_b": nrm(ks[24], (DEPTH, N_EXPERTS), 0.01),
        "expert_w_gate": nrm(ks[25], (DEPTH, N_EXPERTS, D, D_EXPERT), D ** -0.5),
        "expert_w_up": nrm(ks[26], (DEPTH, N_EXPERTS, D, D_EXPERT), D ** -0.5),
        "expert_w_down": nrm(ks[27], (DEPTH, N_EXPERTS, D_EXPERT, D), D_EXPERT ** -0.5),
        "final_norm": 1.0 + nrm(ks[28], (D,), 0.1),
    }


def reference(x_prompt, x_sample, cache_k, cache_v, c, c_ctx, w_in_even, attn_sink, sgu_norm, sgu_w, sgu_b,
              w_out_even, conv_w_in, conv_dw, conv_dw_b, conv_norm, conv_w_out, ada_w, ada_b, norm_mix, norm_ffn,
              router_group_w, router_group_b, router_expert_w, router_expert_b, expert_w_gate, expert_w_up,
              expert_w_down, final_norm):
    xp = x_prompt
    xs = x_sample
    new_k = []
    new_v = []
    for l in range(DEPTH):
        j = l // 2
        mp = adaln_params(c_ctx[None, :], ada_w[l], ada_b[l])
        ms = adaln_params(c, ada_w[l], ada_b[l])
        hp = rmsnorm(xp, norm_mix[l]) * (1 + mp[1]) + mp[0]
        hs = rmsnorm(xs, norm_mix[l]) * (1 + ms[1]) + ms[0]
        if l % 2 == 0:
            yp, kv_ctx = attn_gmlp_mixer(hp, w_in_even[j], attn_sink[j], sgu_norm[j], sgu_w[j], sgu_b[j],
                                         w_out_even[j], None)
            ys, _ = attn_gmlp_mixer(hs, w_in_even[j], attn_sink[j], sgu_norm[j], sgu_w[j], sgu_b[j],
                                    w_out_even[j], (cache_k[:, j], cache_v[:, j]))
            new_k.append(kv_ctx[0])
            new_v.append(kv_ctx[1])
        else:
            yp = conv_mixer(hp, conv_w_in[j], conv_dw[j], conv_dw_b[j], conv_norm[j], conv_w_out[j])
            ys = conv_mixer(hs, conv_w_in[j], conv_dw[j], conv_dw_b[j], conv_norm[j], conv_w_out[j])
        xp = xp + mp[2] * yp
        xs = xs + ms[2] * ys
        xp = moe_sublayer(xp, mp[3], mp[4], mp[5], norm_ffn[l], router_group_w[l], router_group_b[l],
                          router_expert_w[l], router_expert_b[l], expert_w_gate[l], expert_w_up[l], expert_w_down[l])
        xs = moe_sublayer(xs, ms[3], ms[4], ms[5], norm_ffn[l], router_group_w[l], router_group_b[l],
                          router_expert_w[l], router_expert_b[l], expert_w_gate[l], expert_w_up[l], expert_w_down[l])
    y_prompt = rmsnorm(xp, final_norm)
    y_sample = rmsnorm(xs, final_norm)
    new_cache_k = jnp.stack(new_k, axis=1)
    new_cache_v = jnp.stack(new_v, axis=1)
    return (y_prompt, y_sample, new_cache_k, new_cache_v)
```

```python
import functools

import jax
import jax.numpy as jnp
from jax import lax
from jax.experimental import pallas as pl
from jax.experimental.pallas import tpu as pltpu

F32 = jnp.float32
BF16 = jnp.bfloat16

D_MODEL = 2048
GRID_W = 64
ATTN_BLK = 128
N_HEADS = 16
N_KV_HEADS = 4
Q_GROUPS = N_HEADS // N_KV_HEADS
HEAD_DIM = 64
ATTN_WIDTH = N_HEADS * HEAD_DIM
KV_WIDTH = N_KV_HEADS * HEAD_DIM
ROPE_BASE = 10000.0
SGU_GROUPS = 4
SGU_WIDTH = D_MODEL // 2
SGU_GROUP_DIM = SGU_WIDTH // SGU_GROUPS
CHUNK = 128
IN_WIDTH_EVEN = ATTN_WIDTH + 2 * KV_WIDTH + 2 * SGU_WIDTH
CONV_K = 31
CONV_HALO = 16
N_GROUPS_MOE = 4
EXPERTS_PER_GROUP = 4
N_EXPERTS = N_GROUPS_MOE * EXPERTS_PER_GROUP
PAIRS_PER_GROUP = 6
N_CLASSES = N_GROUPS_MOE * PAIRS_PER_GROUP
D_EXPERT = 512
MOE_BLK = 128
EPS = 1e-6
NEG_INF = -1e30

LANES = 128
ROUTER_LANES = 128
CLASS_ROWS = 32
MAX_MOE_BLOCKS = 128
TM_PROJ = 512
TN_PROJ = 512
TM_OUT = 256
TM_SGU = 512
CUM_CHUNK = 256
VMEM_LIMIT = 56 * 1024 * 1024


def _cparams(sem):
    return pltpu.CompilerParams(dimension_semantics=sem, vmem_limit_bytes=VMEM_LIMIT)


def _rms_mod(x, gain, scale, shift):
    ms = jnp.mean(x * x, axis=-1, keepdims=True)
    y = x * lax.rsqrt(ms + EPS) * gain
    return y * (1.0 + scale) + shift


def _ada_kernel(c_ref, w_ref, b_ref, o_ref):
    c = c_ref[...]
    s = (c * jax.nn.sigmoid(c)).astype(BF16)
    o_ref[...] = jnp.dot(s, w_ref[...].astype(BF16), preferred_element_type=F32) + b_ref[...]


def _ada_params(cvec, ada_w, ada_b):
    depth, d, n = ada_w.shape
    mp = cvec.shape[0]
    tn = 1024
    return pl.pallas_call(
        _ada_kernel,
        grid=(depth, n // tn),
        in_specs=[
            pl.BlockSpec((mp, d), lambda l, j: (0, 0)),
            pl.BlockSpec((None, d, tn), lambda l, j: (l, 0, j)),
            pl.BlockSpec((None, 1, tn), lambda l, j: (l, 0, j)),
        ],
        out_specs=pl.BlockSpec((None, mp, tn), lambda l, j: (l, 0, j)),
        out_shape=jax.ShapeDtypeStruct((depth, mp, n), F32),
        compiler_params=_cparams(("arbitrary", "arbitrary")),
        name="ada_params",
    )(cvec, ada_w, ada_b.reshape(depth, 1, n))


def _swap16(x):
    width = x.shape[-1]
    lane = lax.broadcasted_iota(jnp.int32, x.shape, 1)
    return jnp.where((lane % 32) < 16, pltpu.roll(x, width - 16, 1), pltpu.roll(x, 16, 1))


def _proj_even_kernel(x_ref, mod_ref, g_ref, w_ref, cos_ref, sin_ref, o_ref, kv_ref, h_scr, *, kv_tile):
    j = pl.program_id(1)

    @pl.when(j == 0)
    def _():
        h_scr[...] = _rms_mod(x_ref[...], g_ref[...], mod_ref[1:2, :], mod_ref[0:1, :]).astype(BF16)

    acc = jnp.dot(h_scr[...], w_ref[...], preferred_element_type=F32)

    @pl.when((j < 2) | (j == kv_tile))
    def _():
        o_ref[...] = (acc * cos_ref[...] + _swap16(acc) * sin_ref[...]).astype(BF16)

    @pl.when(j == kv_tile)
    def _():
        kv_ref[...] = acc

    @pl.when((j >= 2) & (j < kv_tile))
    def _():
        o_ref[...] = jax.nn.gelu(acc).astype(BF16)


def _proj_even(x, mods, gain, w, cos_t, sin_t, n_prompt_blocks, blocks_per_seq):
    t, d = x.shape
    n = w.shape[1]
    tm, tn = TM_PROJ, TN_PROJ
    n_tiles = n // tn
    kv_tile = n_tiles - 1

    def mod_idx(i, j):
        return (jnp.where(i < n_prompt_blocks, 0, 1 + (i - n_prompt_blocks) // blocks_per_seq), 0, 0)

    def tab_idx(i, j):
        row = jnp.where(i < n_prompt_blocks, blocks_per_seq, (i - n_prompt_blocks) % blocks_per_seq)
        return (row, jnp.where(j == kv_tile, 1, 0))

    return pl.pallas_call(
        functools.partial(_proj_even_kernel, kv_tile=kv_tile),
        grid=(t // tm, n_tiles),
        in_specs=[
            pl.BlockSpec((tm, d), lambda i, j: (i, 0)),
            pl.BlockSpec((None, 6, d), mod_idx),
            pl.BlockSpec((1, d), lambda i, j: (0, 0)),
            pl.BlockSpec((d, tn), lambda i, j: (0, j)),
            pl.BlockSpec((tm, tn), tab_idx),
            pl.BlockSpec((tm, tn), tab_idx),
        ],
        out_specs=[
            pl.BlockSpec((tm, tn), lambda i, j: (i, j)),
            pl.BlockSpec((tm, tn), lambda i, j: (i, 0)),
        ],
        out_shape=[jax.ShapeDtypeStruct((t, n), BF16), jax.ShapeDtypeStruct((t, tn), F32)],
        scratch_shapes=[pltpu.VMEM((tm, d), BF16)],
        compiler_params=_cparams(("arbitrary", "arbitrary")),
        name="proj_even",
    )(x, mods, gain, w, cos_t, sin_t)


def _rope_tables(dec_seq, tm):
    n = jnp.arange(dec_seq)
    row = (n // GRID_W).astype(F32)
    col = (n % GRID_W).astype(F32)
    nf = HEAD_DIM // 4
    inv_freq = ROPE_BASE ** (-jnp.arange(nf, dtype=F32) / nf)
    ar = row[:, None] * inv_freq[None, :]
    ac = col[:, None] * inv_freq[None, :]
    cos_h = jnp.concatenate([jnp.cos(ar), jnp.cos(ar), jnp.cos(ac), jnp.cos(ac)], axis=-1)
    sin_h = jnp.concatenate([-jnp.sin(ar), jnp.sin(ar), -jnp.sin(ac), jnp.sin(ac)], axis=-1)
    scale = HEAD_DIM ** -0.5
    ones = jnp.ones((dec_seq, KV_WIDTH), F32)
    zeros = jnp.zeros((dec_seq, KV_WIDTH), F32)
    q_heads = TN_PROJ // HEAD_DIM
    cos_t = jnp.concatenate([jnp.tile(cos_h, (1, q_heads)) * scale, jnp.tile(cos_h, (1, N_KV_HEADS)), ones], axis=-1)
    sin_t = jnp.concatenate([jnp.tile(sin_h, (1, q_heads)) * scale, jnp.tile(sin_h, (1, N_KV_HEADS)), zeros], axis=-1)
    id_cos = jnp.concatenate([jnp.full((tm, TN_PROJ), scale, F32), jnp.ones((tm, TN_PROJ), F32)], axis=-1)
    id_sin = jnp.zeros((tm, 2 * TN_PROJ), F32)
    return jnp.concatenate([cos_t, id_cos], axis=0), jnp.concatenate([sin_t, id_sin], axis=0)


def _sink_attend(q, keys, vals, masks, sink):
    scores = []
    for k, mask in zip(keys, masks):
        s = lax.dot_general(q, k, (((1,), (1,)), ((), ())), preferred_element_type=F32)
        scores.append(s if mask is None else jnp.where(mask, s, NEG_INF))
    m = sink
    for s in scores:
        m = jnp.maximum(m, jnp.max(s, axis=-1, keepdims=True))
    es = [jnp.exp(s - m) for s in scores]
    den = jnp.exp(sink - m)
    for e in es:
        den = den + jnp.sum(e, axis=-1, keepdims=True)
    out = None
    for e, v in zip(es, vals):
        o = jnp.dot((e / den).astype(BF16), v, preferred_element_type=F32)
        out = o if out is None else out + o
    return out


def _ctx_attn_kernel(sink_ref, q_ref, kv_ref, o_ref):
    for kh in range(N_KV_HEADS):
        k = kv_ref[:, kh * HEAD_DIM:(kh + 1) * HEAD_DIM]
        v = kv_ref[:, KV_WIDTH + kh * HEAD_DIM:KV_WIDTH + (kh + 1) * HEAD_DIM]
        for g in range(Q_GROUPS):
            h = kh * Q_GROUPS + g
            q = q_ref[:, h * HEAD_DIM:(h + 1) * HEAD_DIM]
            o = _sink_attend(q, [k], [v], [None], sink_ref[h])
            o_ref[:, h * HEAD_DIM:(h + 1) * HEAD_DIM] = o.astype(BF16)


def _ctx_attention(proj, sink, batch, seq, kv_col_block):
    return pl.pallas_call(
        _ctx_attn_kernel,
        grid=(batch,),
        in_specs=[
            pl.BlockSpec(memory_space=pltpu.SMEM),
            pl.BlockSpec((seq, ATTN_WIDTH), lambda b: (b, 0)),
            pl.BlockSpec((seq, 2 * KV_WIDTH), lambda b: (b, kv_col_block)),
        ],
        out_specs=pl.BlockSpec((seq, ATTN_WIDTH), lambda b: (b, 0)),
        out_shape=jax.ShapeDtypeStruct((batch * seq, ATTN_WIDTH), BF16),
        compiler_params=_cparams(("arbitrary",)),
        name="ctx_attention",
    )(sink, proj, proj)


def _lat_attn_kernel(sink_ref, q_ref, kv_ref, ck_ref, cv_ref, o_ref, *, seq):
    i = pl.program_id(1)
    win = 3 * ATTN_BLK
    ws = pl.multiple_of(jnp.clip((i - 1) * ATTN_BLK, 0, seq - win), ATTN_BLK)
    qpos = i * ATTN_BLK + lax.broadcasted_iota(jnp.int32, (ATTN_BLK, win), 0)
    kpos = ws + lax.broadcasted_iota(jnp.int32, (ATTN_BLK, win), 1)
    valid = jnp.abs(qpos - kpos) <= ATTN_BLK
    kvw = kv_ref[pl.ds(ws, win), :]
    for kh in range(N_KV_HEADS):
        k_loc = kvw[:, kh * HEAD_DIM:(kh + 1) * HEAD_DIM]
        v_loc = kvw[:, KV_WIDTH + kh * HEAD_DIM:KV_WIDTH + (kh + 1) * HEAD_DIM]
        k_ctx = ck_ref[:, kh * HEAD_DIM:(kh + 1) * HEAD_DIM]
        v_ctx = cv_ref[:, kh * HEAD_DIM:(kh + 1) * HEAD_DIM]
        for g in range(Q_GROUPS):
            h = kh * Q_GROUPS + g
            q = q_ref[:, h * HEAD_DIM:(h + 1) * HEAD_DIM]
            o = _sink_attend(q, [k_ctx, k_loc], [v_ctx, v_loc], [None, valid], sink_ref[h])
            o_ref[:, h * HEAD_DIM:(h + 1) * HEAD_DIM] = o.astype(BF16)


def _lat_attention(proj, sink, cache_k, cache_v, n_prompt, dec_batch, dec_seq, kv_col_block):
    qb = dec_seq // ATTN_BLK
    q_off = n_prompt // ATTN_BLK
    s_off = n_prompt // dec_seq
    past = cache_k.shape[1]
    return pl.pallas_call(
        functools.partial(_lat_attn_kernel, seq=dec_seq),
        grid=(dec_batch, qb),
        in_specs=[
            pl.BlockSpec(memory_space=pltpu.SMEM),
            pl.BlockSpec((ATTN_BLK, ATTN_WIDTH), lambda b, i: (q_off + b * qb + i, 0)),
            pl.BlockSpec((dec_seq, 2 * KV_WIDTH), lambda b, i: (s_off + b, kv_col_block)),
            pl.BlockSpec((None, past, KV_WIDTH), lambda b, i: (b, 0, 0)),
            pl.BlockSpec((None, past, KV_WIDTH), lambda b, i: (b, 0, 0)),
        ],
        out_specs=pl.BlockSpec((ATTN_BLK, ATTN_WIDTH), lambda b, i: (b * qb + i, 0)),
        out_shape=jax.ShapeDtypeStruct((dec_batch * dec_seq, ATTN_WIDTH), BF16),
        compiler_params=_cparams(("arbitrary", "arbitrary")),
        name="lat_attention",
    )(sink, proj, proj, cache_k, cache_v)


def _sgu_kernel(u_ref, v_ref, g_ref, w_ref, bt_ref, z_ref):
    for c in range(TM_SGU // CHUNK):
        rows = slice(c * CHUNK, (c + 1) * CHUNK)
        v = v_ref[rows, :].astype(F32)
        ms = jnp.mean(v * v, axis=-1, keepdims=True)
        vn = (v * lax.rsqrt(ms + EPS) * g_ref[...]).astype(BF16)
        for g in range(SGU_GROUPS):
            cols = slice(g * SGU_GROUP_DIM, (g + 1) * SGU_GROUP_DIM)
            mixed = jnp.dot(w_ref[g], vn[:, cols], preferred_element_type=F32) + bt_ref[:, g:g + 1]
            z_ref[rows, cols] = (u_ref[rows, cols].astype(F32) * mixed).astype(BF16)


def _sgu(proj, norm_g, w_s, b_t):
    t = proj.shape[0]
    return pl.pallas_call(
        _sgu_kernel,
        grid=(t // TM_SGU,),
        in_specs=[
            pl.BlockSpec((TM_SGU, SGU_WIDTH), lambda i: (i, 1)),
            pl.BlockSpec((TM_SGU, SGU_WIDTH), lambda i: (i, 2)),
            pl.BlockSpec((1, SGU_WIDTH), lambda i: (0, 0)),
            pl.BlockSpec((SGU_GROUPS, CHUNK, CHUNK), lambda i: (0, 0, 0)),
            pl.BlockSpec((CHUNK, SGU_GROUPS), lambda i: (0, 0)),
        ],
        out_specs=pl.BlockSpec((TM_SGU, SGU_WIDTH), lambda i: (i, 0)),
        out_shape=jax.ShapeDtypeStruct((t, SGU_WIDTH), BF16),
        compiler_params=_cparams(("arbitrary",)),
        name="sgu",
    )(proj, proj, norm_g, w_s, b_t)


def _residual_router(x, acc, mod_ref, g2_ref, wrt_ref, brt_ref, x1_ref, h2_ref, lg_ref):
    x1 = x + mod_ref[2:3, :] * acc
    x1_ref[...] = x1
    h2 = _rms_mod(x1, g2_ref[...], mod_ref[4:5, :], mod_ref[3:4, :])
    h2_ref[...] = h2
    lg = lax.dot_general(wrt_ref[...], h2.astype(BF16), (((1,), (1,)), ((), ())), preferred_element_type=F32)
    lg_ref[...] = lg + brt_ref[...]


def _out_even_kernel(x_ref, ap_ref, as_ref, z_ref, w_ref, mod_ref, g2_ref, wrt_ref, brt_ref,
                     x1_ref, h2_ref, lg_ref, *, n_prompt_blocks):
    i = pl.program_id(0)
    a = jnp.where(i < n_prompt_blocks, ap_ref[...].astype(F32), as_ref[...].astype(F32)).astype(BF16)
    acc = jnp.dot(a, w_ref[0:ATTN_WIDTH, :], preferred_element_type=F32)
    acc = acc + jnp.dot(z_ref[...], w_ref[ATTN_WIDTH:, :], preferred_element_type=F32)
    _residual_router(x_ref[...], acc, mod_ref, g2_ref, wrt_ref, brt_ref, x1_ref, h2_ref, lg_ref)


def _out_odd_kernel(x_ref, hc_ref, w_ref, mod_ref, g2_ref, wrt_ref, brt_ref, x1_ref, h2_ref, lg_ref):
    acc = jnp.dot(hc_ref[...], w_ref[...], preferred_element_type=F32)
    _residual_router(x_ref[...], acc, mod_ref, g2_ref, wrt_ref, brt_ref, x1_ref, h2_ref, lg_ref)


def _out_proj(x, lhs, w, mods, gain2, wrt, brt, n_prompt, dec_seq):
    t, d = x.shape
    tm = TM_OUT
    npb = n_prompt // tm
    bps = dec_seq // tm

    def mod_idx(i):
        return (jnp.where(i < npb, 0, 1 + (i - npb) // bps), 0, 0)

    if len(lhs) == 3:
        a_p, a_s, z = lhs
        nsb = a_s.shape[0] // tm
        body = functools.partial(_out_even_kernel, n_prompt_blocks=npb)
        lhs_specs = [
            pl.BlockSpec((tm, ATTN_WIDTH), lambda i: (jnp.minimum(i, npb - 1), 0)),
            pl.BlockSpec((tm, ATTN_WIDTH), lambda i: (jnp.clip(i - npb, 0, nsb - 1), 0)),
            pl.BlockSpec((tm, SGU_WIDTH), lambda i: (i, 0)),
        ]
    else:
        body = _out_odd_kernel
        lhs_specs = [pl.BlockSpec((tm, d), lambda i: (i, 0))]
    return pl.pallas_call(
        body,
        grid=(t // tm,),
        in_specs=[pl.BlockSpec((tm, d), lambda i: (i, 0))] + lhs_specs + [
            pl.BlockSpec((w.shape[0], d), lambda i: (0, 0)),
            pl.BlockSpec((None, 6, d), mod_idx),
            pl.BlockSpec((1, d), lambda i: (0, 0)),
            pl.BlockSpec((ROUTER_LANES, d), lambda i: (0, 0)),
            pl.BlockSpec((ROUTER_LANES, 1), lambda i: (0, 0)),
        ],
        out_specs=[
            pl.BlockSpec((tm, d), lambda i: (i, 0)),
            pl.BlockSpec((tm, d), lambda i: (i, 0)),
            pl.BlockSpec((ROUTER_LANES, tm), lambda i: (0, i)),
        ],
        out_shape=[
            jax.ShapeDtypeStruct((t, d), F32),
            jax.ShapeDtypeStruct((t, d), F32),
            jax.ShapeDtypeStruct((ROUTER_LANES, t), F32),
        ],
        compiler_params=_cparams(("arbitrary",)),
        name="out_proj",
    )(x, *lhs, w, mods, gain2, wrt, brt)


def _first_argmax4(v0, v1, v2, v3):
    m = jnp.maximum(jnp.maximum(v0, v1), jnp.maximum(v2, v3))
    idx = jnp.where(v0 == m, 0.0, jnp.where(v1 == m, 1.0, jnp.where(v2 == m, 2.0, 3.0)))
    return m, idx


def _route_kernel(lg_ref, pos_ref, ea_ref, eb_ref, nu_ref, oh_scr, rank_scr, *, n_tokens):
    lg = lg_ref[...]
    rows = [lg[r:r + 1, :] for r in range(N_GROUPS_MOE + N_EXPERTS)]
    _, gidx = _first_argmax4(*rows[:N_GROUPS_MOE])
    e = []
    for k in range(EXPERTS_PER_GROUP):
        cand = [rows[N_GROUPS_MOE + g * EXPERTS_PER_GROUP + k] for g in range(N_GROUPS_MOE)]
        e.append(jnp.where(gidx == 0.0, cand[0], jnp.where(gidx == 1.0, cand[1], jnp.where(gidx == 2.0, cand[2], cand[3]))))
    _, l1 = _first_argmax4(*e)
    e2 = [jnp.where(l1 == float(k), -jnp.inf, e[k]) for k in range(EXPERTS_PER_GROUP)]
    _, l2 = _first_argmax4(*e2)
    lo = jnp.minimum(l1, l2)
    hi = jnp.maximum(l1, l2)
    pair = jnp.where(lo == 0.0, hi - 1.0, jnp.where(lo == 1.0, hi + 1.0, 5.0))
    cls = gidx * float(PAIRS_PER_GROUP) + pair

    crow = lax.broadcasted_iota(jnp.int32, (CLASS_ROWS, n_tokens), 0).astype(F32)
    oh_scr[...] = jnp.where(crow == cls, 1.0, 0.0).astype(BF16)

    jj = lax.broadcasted_iota(jnp.int32, (CUM_CHUNK, CUM_CHUNK), 0)
    ii = lax.broadcasted_iota(jnp.int32, (CUM_CHUNK, CUM_CHUNK), 1)
    tri = jnp.where(jj < ii, 1.0, 0.0).astype(BF16)

    def chunk(c, carry):
        off = pl.multiple_of(c * CUM_CHUNK, CUM_CHUNK)
        oh = oh_scr[:, pl.ds(off, CUM_CHUNK)]
        ohf = oh.astype(F32)
        cum = jnp.dot(oh, tri, preferred_element_type=F32) + carry
        rank_scr[:, pl.ds(off, CUM_CHUNK)] = jnp.sum(cum * ohf, axis=0, keepdims=True)
        return carry + jnp.sum(ohf, axis=1, keepdims=True)

    counts = lax.fori_loop(0, n_tokens // CUM_CHUNK, chunk, jnp.zeros((CLASS_ROWS, 1), F32))
    padded = jnp.floor((counts + float(MOE_BLK - 1)) * (1.0 / MOE_BLK)) * float(MOE_BLK)
    run = jnp.zeros((1, 1), F32)
    starts_rows = []
    for c in range(CLASS_ROWS):
        starts_rows.append(run)
        run = run + padded[c:c + 1, :]
    starts = jnp.concatenate(starts_rows, axis=0)
    ends = starts + padded
    total = run

    ohf = oh_scr[...].astype(F32)
    pos = rank_scr[...] + jnp.sum(ohf * starts, axis=0, keepdims=True)
    pos_ref[...] = pos.astype(jnp.int32)

    bstart = lax.broadcasted_iota(jnp.int32, (CLASS_ROWS, MAX_MOE_BLOCKS), 1).astype(F32) * float(MOE_BLK)
    brow = lax.broadcasted_iota(jnp.int32, (CLASS_ROWS, MAX_MOE_BLOCKS), 0)
    done = jnp.where((ends <= bstart) & (brow < N_CLASSES), 1.0, 0.0)
    bcls = jnp.minimum(jnp.sum(done, axis=0, keepdims=True), float(N_CLASSES - 1))
    grp = (jnp.where(bcls >= 6.0, 1.0, 0.0) + jnp.where(bcls >= 12.0, 1.0, 0.0) + jnp.where(bcls >= 18.0, 1.0, 0.0))
    bp = bcls - grp * float(PAIRS_PER_GROUP)
    ge3 = jnp.where(bp >= 3.0, 1.0, 0.0)
    ge5 = jnp.where(bp >= 5.0, 1.0, 0.0)
    blo = ge3 + ge5
    bhi = bp + 1.0 - 2.0 * ge3 - ge5
    ea_ref[...] = (grp * float(EXPERTS_PER_GROUP) + blo).astype(jnp.int32)
    eb_ref[...] = (grp * float(EXPERTS_PER_GROUP) + bhi).astype(jnp.int32)
    nu_ref[...] = jnp.broadcast_to(total * (1.0 / MOE_BLK), (1, MAX_MOE_BLOCKS)).astype(jnp.int32)


def _route(logits_t):
    t = logits_t.shape[1]
    rows = N_CLASSES
    i32 = jnp.int32
    return pl.pallas_call(
        functools.partial(_route_kernel, n_tokens=t),
        grid=(1,),
        in_specs=[pl.BlockSpec((rows, t), lambda i: (0, 0))],
        out_specs=[
            pl.BlockSpec((1, t), lambda i: (0, 0)),
            pl.BlockSpec((1, MAX_MOE_BLOCKS), lambda i: (0, 0)),
            pl.BlockSpec((1, MAX_MOE_BLOCKS), lambda i: (0, 0)),
            pl.BlockSpec((1, MAX_MOE_BLOCKS), lambda i: (0, 0)),
        ],
        out_shape=[
            jax.ShapeDtypeStruct((1, t), i32),
            jax.ShapeDtypeStruct((1, MAX_MOE_BLOCKS), i32),
            jax.ShapeDtypeStruct((1, MAX_MOE_BLOCKS), i32),
            jax.ShapeDtypeStruct((1, MAX_MOE_BLOCKS), i32),
        ],
        scratch_shapes=[pltpu.VMEM((CLASS_ROWS, t), BF16), pltpu.VMEM((1, t), F32)],
        compiler_params=_cparams(("arbitrary",)),
        name="route",
    )(logits_t)


def _row_copy(src_ref, src_row, dst_ref, dst_row, sem):
    return pltpu.make_async_copy(src_ref.at[pl.ds(src_row, 1), :], dst_ref.at[pl.ds(dst_row, 1), :], sem)


def _dispatch_kernel(pos_ref, h_ref, init_ref, o_ref, sem):
    del init_ref
    base = pl.program_id(0) * TM_OUT

    def issue(r, carry):
        _row_copy(h_ref, r, o_ref, pos_ref[base + r], sem).start()
        return carry

    lax.fori_loop(0, TM_OUT, issue, 0)

    def drain(r, carry):
        _row_copy(h_ref, r, o_ref, pos_ref[base + r], sem).wait()
        return carry

    lax.fori_loop(0, TM_OUT, drain, 0)


def _dispatch(pos, h2, n_rows):
    t, d = h2.shape
    init = jnp.zeros((n_rows, d), F32)
    return pl.pallas_call(
        _dispatch_kernel,
        grid_spec=pltpu.PrefetchScalarGridSpec(
            num_scalar_prefetch=1,
            grid=(t // TM_OUT,),
            in_specs=[
                pl.BlockSpec((TM_OUT, d), lambda i, pos: (i, 0)),
                pl.BlockSpec(memory_space=pl.ANY),
            ],
            out_specs=pl.BlockSpec(memory_space=pl.ANY),
            scratch_shapes=[pltpu.SemaphoreType.DMA(())],
        ),
        out_shape=jax.ShapeDtypeStruct((n_rows, d), F32),
        input_output_aliases={2: 0},
        compiler_params=_cparams(("arbitrary",)),
        name="moe_dispatch",
    )(pos, h2, init)


def _expert_ffn(x, wg_ref, wu_ref, wd_ref):
    gate = jnp.dot(x, wg_ref[...], preferred_element_type=F32)
    up = jnp.dot(x, wu_ref[...], preferred_element_type=F32)
    hidden = (gate * jax.nn.sigmoid(gate) * up).astype(BF16)
    return jnp.dot(hidden, wd_ref[...], preferred_element_type=F32)


def _moe_kernel(ea_ref, eb_ref, nu_ref, x_ref, wr_ref, br_ref, wga, wua, wda, wgb, wub, wdb, y_ref):
    b = pl.program_id(0)

    @pl.when(b < nu_ref[0])
    def _():
        x = x_ref[...].astype(BF16)
        lg = jnp.dot(x, wr_ref[...], preferred_element_type=F32) + br_ref[...]
        lane = lax.broadcasted_iota(jnp.int32, lg.shape, 1)
        ea = ea_ref[b]
        eb = eb_ref[b]
        grp = ea // EXPERTS_PER_GROUP
        is_grp = lane < N_GROUPS_MOE
        gm = jnp.max(jnp.where(is_grp, lg, -jnp.inf), axis=-1, keepdims=True)
        ge = jnp.where(is_grp, jnp.exp(lg - gm), 0.0)
        p_g = jnp.sum(jnp.where(lane == grp, ge, 0.0), axis=-1, keepdims=True) / jnp.sum(ge, axis=-1, keepdims=True)
        la = jnp.sum(jnp.where(lane == N_GROUPS_MOE + ea, lg, 0.0), axis=-1, keepdims=True)
        lb = jnp.sum(jnp.where(lane == N_GROUPS_MOE + eb, lg, 0.0), axis=-1, keepdims=True)
        mm = jnp.maximum(la, lb)
        xa = jnp.exp(la - mm)
        xb = jnp.exp(lb - mm)
        ga = xa / (xa + xb) * p_g
        gb = xb / (xa + xb) * p_g
        ya = _expert_ffn(x, wga, wua, wda)
        yb = _expert_ffn(x, wgb, wub, wdb)
        y_ref[...] = ya * ga + yb * gb

    @pl.when(b >= nu_ref[0])
    def _():
        y_ref[...] = jnp.zeros_like(y_ref)


def _moe_experts(ea, eb, nused, hs, wr, br, w_gate, w_up, w_down, n_blocks):
    d = hs.shape[1]
    de = w_gate.shape[2]

    def wa_idx(b, ea, eb, nu):
        return (ea[b], 0, 0)

    def wb_idx(b, ea, eb, nu):
        return (eb[b], 0, 0)

    return pl.pallas_call(
        _moe_kernel,
        grid_spec=pltpu.PrefetchScalarGridSpec(
            num_scalar_prefetch=3,
            grid=(n_blocks,),
            in_specs=[
                pl.BlockSpec((MOE_BLK, d), lambda b, ea, eb, nu: (b, 0)),
                pl.BlockSpec((d, ROUTER_LANES), lambda b, ea, eb, nu: (0, 0)),
                pl.BlockSpec((1, ROUTER_LANES), lambda b, ea, eb, nu: (0, 0)),
                pl.BlockSpec((None, d, de), wa_idx),
                pl.BlockSpec((None, d, de), wa_idx),
                pl.BlockSpec((None, de, d), wa_idx),
                pl.BlockSpec((None, d, de), wb_idx),
                pl.BlockSpec((None, d, de), wb_idx),
                pl.BlockSpec((None, de, d), wb_idx),
            ],
            out_specs=pl.BlockSpec((MOE_BLK, d), lambda b, ea, eb, nu: (b, 0)),
        ),
        out_shape=jax.ShapeDtypeStruct((n_blocks * MOE_BLK, d), F32),
        compiler_params=_cparams(("arbitrary",)),
        name="moe_experts",
    )(ea, eb, nused, hs, wr, br, w_gate, w_up, w_down, w_gate, w_up, w_down)


def _combine_kernel(pos_ref, ys_ref, x1_ref, mod_ref, gf_ref, o_ref, ybuf, sem, *, block_off, final):
    base = (pl.program_id(0) + block_off) * TM_OUT

    def issue(r, carry):
        _row_copy(ys_ref, pos_ref[base + r], ybuf, r, sem).start()
        return carry

    lax.fori_loop(0, TM_OUT, issue, 0)

    def drain(r, carry):
        _row_copy(ys_ref, pos_ref[base + r], ybuf, r, sem).wait()
        return carry

    lax.fori_loop(0, TM_OUT, drain, 0)
    x2 = x1_ref[...] + mod_ref[5:6, :] * ybuf[...]
    if final:
        ms = jnp.mean(x2 * x2, axis=-1, keepdims=True)
        x2 = x2 * lax.rsqrt(ms + EPS) * gf_ref[...]
    o_ref[...] = x2


def _combine(pos, ys, x1, mods, gain_f, n_prompt, dec_seq, row_off, n_rows, final):
    d = x1.shape[1]
    tm = TM_OUT
    npb = n_prompt // tm
    bps = dec_seq // tm
    boff = row_off // tm

    def mod_idx(i, pos):
        blk = i + boff
        return (jnp.where(blk < npb, 0, 1 + (blk - npb) // bps), 0, 0)

    return pl.pallas_call(
        functools.partial(_combine_kernel, block_off=boff, final=final),
        grid_spec=pltpu.PrefetchScalarGridSpec(
            num_scalar_prefetch=1,
            grid=(n_rows // tm,),
            in_specs=[
                pl.BlockSpec(memory_space=pl.ANY),
                pl.BlockSpec((tm, d), lambda i, pos: (i + boff, 0)),
                pl.BlockSpec((None, 6, d), mod_idx),
                pl.BlockSpec((1, d), lambda i, pos: (0, 0)),
            ],
            out_specs=pl.BlockSpec((tm, d), lambda i, pos: (i, 0)),
            scratch_shapes=[pltpu.VMEM((tm, d), F32), pltpu.SemaphoreType.DMA(())],
        ),
        out_shape=jax.ShapeDtypeStruct((n_rows, d), F32),
        compiler_params=_cparams(("arbitrary",)),
        name="moe_combine",
    )(pos, ys, x1, mods, gain_f)


def _proj_glu_kernel(x_ref, mod_ref, g_ref, w_ref, o_ref, h_scr):
    j = pl.program_id(1)

    @pl.when(j == 0)
    def _():
        h_scr[...] = _rms_mod(x_ref[...], g_ref[...], mod_ref[1:2, :], mod_ref[0:1, :]).astype(BF16)

    acc = jnp.dot(h_scr[...], w_ref[...], preferred_element_type=F32)
    half = acc.shape[1] // 2
    o_ref[...] = (acc[:, :half] * jax.nn.sigmoid(acc[:, half:])).astype(BF16)


def _proj_glu(x, mods, gain, w, n_prompt, dec_seq):
    t, d = x.shape
    n = w.shape[1]
    tm, tn = TM_PROJ, TN_PROJ
    npb = n_prompt // tm
    bps = dec_seq // tm

    def mod_idx(i, j):
        return (jnp.where(i < npb, 0, 1 + (i - npb) // bps), 0, 0)

    return pl.pallas_call(
        _proj_glu_kernel,
        grid=(t // tm, n // tn),
        in_specs=[
            pl.BlockSpec((tm, d), lambda i, j: (i, 0)),
            pl.BlockSpec((None, 6, d), mod_idx),
            pl.BlockSpec((1, d), lambda i, j: (0, 0)),
            pl.BlockSpec((d, tn), lambda i, j: (0, j)),
        ],
        out_specs=pl.BlockSpec((tm, tn // 2), lambda i, j: (i, j)),
        out_shape=jax.ShapeDtypeStruct((t, n // 2), BF16),
        scratch_shapes=[pltpu.VMEM((tm, d), BF16)],
        compiler_params=_cparams(("arbitrary", "arbitrary")),
        name="proj_glu",
    )(x, mods, gain, w)


CONV_ROWS = 64
CONV_COLS = 256


def _conv_kernel(x_ref, prev_ref, next_ref, dw_ref, b_ref, g_ref, o_ref, xpad, ybuf, *, n_prompt_blocks, blocks_per_seq):
    i = pl.program_id(0)
    in_sample = i >= n_prompt_blocks
    j = i - n_prompt_blocks
    is_start = jnp.logical_or(jnp.logical_not(in_sample), j % blocks_per_seq == 0)
    is_end = jnp.logical_or(jnp.logical_not(in_sample), j % blocks_per_seq == blocks_per_seq - 1)
    tm = TM_OUT
    xpad[0:CONV_HALO, :] = jnp.where(is_start, 0.0, prev_ref[...].astype(F32))
    xpad[CONV_HALO:CONV_HALO + tm, :] = x_ref[...].astype(F32)
    xpad[CONV_HALO + tm:, :] = jnp.where(is_end, 0.0, next_ref[...].astype(F32))
    first = CONV_HALO - CONV_K // 2

    def col_chunk(c, ssq):
        c0 = pl.multiple_of(c * CONV_COLS, CONV_COLS)
        cols = pl.ds(c0, CONV_COLS)
        w = dw_ref[:, cols]
        bias = b_ref[:, cols]
        parts = []
        for rc in range(tm // CONV_ROWS):
            r0 = rc * CONV_ROWS
            acc = jnp.zeros((CONV_ROWS, CONV_COLS), F32)
            for k in range(CONV_K):
                acc = acc + w[k:k + 1, :] * xpad[r0 + first + k:r0 + first + k + CONV_ROWS, cols]
            acc = acc + bias
            ybuf[r0:r0 + CONV_ROWS, cols] = acc
            parts.append(jnp.sum(acc * acc, axis=-1, keepdims=True))
        return ssq + jnp.concatenate(parts, axis=0)

    d = x_ref.shape[1]
    ssq = lax.fori_loop(0, d // CONV_COLS, col_chunk, jnp.zeros((tm, 1), F32))
    y = ybuf[...] * lax.rsqrt(ssq * (1.0 / d) + EPS) * g_ref[...]
    o_ref[...] = (y * jax.nn.sigmoid(y)).astype(BF16)


def _conv_mix(xg, dw, dw_b, norm_g, n_prompt, seq, dec_seq):
    t, d = xg.shape
    tm = TM_OUT
    assert seq == tm, "context sequences must be exactly one conv row block"
    npb = n_prompt // tm
    bps = dec_seq // tm
    hb = tm // CONV_HALO
    last = t // CONV_HALO - 1
    return pl.pallas_call(
        functools.partial(_conv_kernel, n_prompt_blocks=npb, blocks_per_seq=bps),
        grid=(t // tm,),
        in_specs=[
            pl.BlockSpec((tm, d), lambda i: (i, 0)),
            pl.BlockSpec((CONV_HALO, d), lambda i: (jnp.maximum(i * hb - 1, 0), 0)),
            pl.BlockSpec((CONV_HALO, d), lambda i: (jnp.minimum((i + 1) * hb, last), 0)),
            pl.BlockSpec((dw.shape[0], d), lambda i: (0, 0)),
            pl.BlockSpec((1, d), lambda i: (0, 0)),
            pl.BlockSpec((1, d), lambda i: (0, 0)),
        ],
        out_specs=pl.BlockSpec((tm, d), lambda i: (i, 0)),
        out_shape=jax.ShapeDtypeStruct((t, d), BF16),
        scratch_shapes=[pltpu.VMEM((tm + 2 * CONV_HALO, d), F32), pltpu.VMEM((tm, d), F32)],
        compiler_params=_cparams(("arbitrary",)),
        name="conv_mix",
    )(xg, xg, xg, dw, dw_b, norm_g)


def _router_operands(w_rg, b_rg, w_re, b_re):
    d = w_rg.shape[0]
    used = N_GROUPS_MOE + N_EXPERTS
    w = jnp.concatenate([w_rg, w_re, jnp.zeros((d, ROUTER_LANES - used), F32)], axis=1)
    b = jnp.concatenate([b_rg, b_re, jnp.zeros((ROUTER_LANES - used,), F32)])
    return w.astype(BF16), b.reshape(1, ROUTER_LANES), w.T.astype(BF16), b.reshape(ROUTER_LANES, 1)


def _moe_rows(h2, logits_t, wr, br, w_gate, w_up, w_down):
    t = h2.shape[0]
    n_blocks = -(-(t + N_CLASSES * (MOE_BLK - 1)) // MOE_BLK)
    assert n_blocks <= MAX_MOE_BLOCKS
    pos, ea, eb, nused = _route(logits_t)
    pos = pos.reshape(t)
    hs = _dispatch(pos, h2, n_blocks * MOE_BLK)
    ys = _moe_experts(ea.reshape(-1), eb.reshape(-1), nused.reshape(-1)[:1], hs, wr, br, w_gate, w_up, w_down, n_blocks)
    return pos, ys


def kernel(x_prompt, x_sample, cache_k, cache_v, c, c_ctx, w_in_even, attn_sink, sgu_norm, sgu_w, sgu_b,
           w_out_even, conv_w_in, conv_dw, conv_dw_b, conv_norm, conv_w_out, ada_w, ada_b, norm_mix, norm_ffn,
           router_group_w, router_group_b, router_expert_w, router_expert_b, expert_w_gate, expert_w_up,
           expert_w_down, final_norm):
    batch, seq, d = x_prompt.shape
    dec_batch, dec_seq, _ = x_sample.shape
    depth = ada_w.shape[0]
    n_prompt = batch * seq
    n_sample = dec_batch * dec_seq
    assert d == D_MODEL and depth == 2 and n_prompt % dec_seq == 0 and dec_seq % TM_PROJ == 0

    x = jnp.concatenate([x_prompt.reshape(n_prompt, d), x_sample.reshape(n_sample, d)], axis=0)

    n_mod = -(-(1 + dec_batch) // 8) * 8
    cvec = jnp.concatenate([c_ctx[None, :], c, jnp.zeros((n_mod - 1 - dec_batch, d), F32)], axis=0)
    mods = _ada_params(cvec, ada_w, ada_b).reshape(depth, n_mod, 6, d)

    wi = w_in_even[0]
    q_end, k_end, v_end = ATTN_WIDTH, ATTN_WIDTH + KV_WIDTH, ATTN_WIDTH + 2 * KV_WIDTH
    w_in = jnp.concatenate([wi[:, :q_end], wi[:, v_end:], wi[:, q_end:v_end]], axis=1).astype(BF16)
    cos_t, sin_t = _rope_tables(dec_seq, TM_PROJ)
    proj, kv32 = _proj_even(x, mods[0], norm_mix[0][None, :], w_in, cos_t, sin_t,
                            n_prompt // TM_PROJ, dec_seq // TM_PROJ)
    kv_col_block = (IN_WIDTH_EVEN - 2 * KV_WIDTH) // (2 * KV_WIDTH)
    sink = attn_sink[0]
    a_p = _ctx_attention(proj, sink, batch, seq, kv_col_block)
    past = cache_k.shape[2]
    ck = cache_k[:, 0].reshape(dec_batch, past, KV_WIDTH).astype(BF16)
    cv = cache_v[:, 0].reshape(dec_batch, past, KV_WIDTH).astype(BF16)
    a_s = _lat_attention(proj, sink, ck, cv, n_prompt, dec_batch, dec_seq, kv_col_block)
    z = _sgu(proj, sgu_norm[0][None, :], sgu_w[0].astype(BF16), sgu_b[0].T)

    wr0, br0, wrt0, brt0 = _router_operands(router_group_w[0], router_group_b[0], router_expert_w[0], router_expert_b[0])
    x1, h2, lg = _out_proj(x, (a_p, a_s, z), w_out_even[0].astype(BF16), mods[0], norm_ffn[0][None, :],
                           wrt0, brt0, n_prompt, dec_seq)
    pos, ys = _moe_rows(h2, lg, wr0, br0, expert_w_gate[0].astype(BF16), expert_w_up[0].astype(BF16),
                        expert_w_down[0].astype(BF16))
    gain_f = final_norm[None, :]
    x = _combine(pos, ys, x1, mods[0], gain_f, n_prompt, dec_seq, 0, n_prompt + n_sample, final=False)

    cw = conv_w_in[0]
    half = TN_PROJ // 2
    n_ct = d // half
    cw = jnp.concatenate([cw[:, :d].reshape(d, n_ct, half), cw[:, d:].reshape(d, n_ct, half)], axis=2)
    xg = _proj_glu(x, mods[1], norm_mix[1][None, :], cw.reshape(d, 2 * d).astype(BF16), n_prompt, dec_seq)
    dw = jnp.concatenate([conv_dw[0], jnp.zeros((1, d), F32)], axis=0)
    hc = _conv_mix(xg, dw, conv_dw_b[0][None, :], conv_norm[0][None, :], n_prompt, seq, dec_seq)
    wr1, br1, wrt1, brt1 = _router_operands(router_group_w[1], router_group_b[1], router_expert_w[1], router_expert_b[1])
    x1, h2, lg = _out_proj(x, (hc,), conv_w_out[0].astype(BF16), mods[1], norm_ffn[1][None, :],
                           wrt1, brt1, n_prompt, dec_seq)
    pos, ys = _moe_rows(h2, lg, wr1, br1, expert_w_gate[1].astype(BF16), expert_w_up[1].astype(BF16),
                        expert_w_down[1].astype(BF16))
    y_prompt = _combine(pos, ys, x1, mods[1], gain_f, n_prompt, dec_seq, 0, n_prompt, final=True)
    y_sample = _combine(pos, ys, x1, mods[1], gain_f, n_prompt, dec_seq, n_prompt, n_sample, final=True)

    new_k = kv32[:n_prompt, :KV_WIDTH].reshape(batch, 1, seq, N_KV_HEADS, HEAD_DIM)
    new_v = kv32[:n_prompt, KV_WIDTH:].reshape(batch, 1, seq, N_KV_HEADS, HEAD_DIM)
    return (y_prompt.reshape(batch, seq, d), y_sample.reshape(dec_batch, dec_seq, d), new_k, new_v)
```

```python
import functools

import jax
import jax.numpy as jnp
from jax import lax
from jax.experimental import pallas as pl
from jax.experimental.pallas import tpu as pltpu

F32 = jnp.float32
BF16 = jnp.bfloat16

D_MODEL = 2048
GRID_W = 64
ATTN_BLK = 128
N_HEADS = 16
N_KV_HEADS = 4
Q_GROUPS = N_HEADS // N_KV_HEADS
HEAD_DIM = 64
ATTN_WIDTH = N_HEADS * HEAD_DIM
KV_WIDTH = N_KV_HEADS * HEAD_DIM
ROPE_BASE = 10000.0
SGU_GROUPS = 4
SGU_WIDTH = D_MODEL // 2
SGU_GROUP_DIM = SGU_WIDTH // SGU_GROUPS
CHUNK = 128
IN_WIDTH_EVEN = ATTN_WIDTH + 2 * KV_WIDTH + 2 * SGU_WIDTH
CONV_K = 31
CONV_HALO = 16
N_GROUPS_MOE = 4
EXPERTS_PER_GROUP = 4
N_EXPERTS = N_GROUPS_MOE * EXPERTS_PER_GROUP
PAIRS_PER_GROUP = 6
N_CLASSES = N_GROUPS_MOE * PAIRS_PER_GROUP
D_EXPERT = 512
MOE_BLK = 256
EPS = 1e-6
NEG_INF = -1e30

LANES = 128
ROUTER_LANES = 128
CLASS_ROWS = 32
MAX_MOE_BLOCKS = 128
TM_PROJ = 512
TN_PROJ = 512
TM_OUT = 256
TM_SGU = 512
CUM_CHUNK = 256
VMEM_LIMIT = 56 * 1024 * 1024


def _cparams(sem):
    return pltpu.CompilerParams(dimension_semantics=sem, vmem_limit_bytes=VMEM_LIMIT)


def _rms_mod(x, gain, scale, shift):
    ms = jnp.mean(x * x, axis=-1, keepdims=True)
    y = x * lax.rsqrt(ms + EPS) * gain
    return y * (1.0 + scale) + shift


def _ada_kernel(c_ref, w_ref, b_ref, o_ref):
    c = c_ref[...]
    s = (c * jax.nn.sigmoid(c)).astype(BF16)
    o_ref[...] = jnp.dot(s, w_ref[...].astype(BF16), preferred_element_type=F32) + b_ref[...]


def _ada_params(cvec, ada_w, ada_b):
    depth, d, n = ada_w.shape
    mp = cvec.shape[0]
    tn = 1024
    return pl.pallas_call(
        _ada_kernel,
        grid=(depth, n // tn),
        in_specs=[
            pl.BlockSpec((mp, d), lambda l, j: (0, 0)),
            pl.BlockSpec((None, d, tn), lambda l, j: (l, 0, j)),
            pl.BlockSpec((None, 1, tn), lambda l, j: (l, 0, j)),
        ],
        out_specs=pl.BlockSpec((None, mp, tn), lambda l, j: (l, 0, j)),
        out_shape=jax.ShapeDtypeStruct((depth, mp, n), F32),
        compiler_params=_cparams(("arbitrary", "arbitrary")),
        name="ada_params",
    )(cvec, ada_w, ada_b.reshape(depth, 1, n))


def _swap16(x):
    width = x.shape[-1]
    lane = lax.broadcasted_iota(jnp.int32, x.shape, 1)
    return jnp.where((lane % 32) < 16, pltpu.roll(x, width - 16, 1), pltpu.roll(x, 16, 1))


def _proj_even_kernel(x_ref, mod_ref, g_ref, w_ref, cos_ref, sin_ref, o_ref, kv_ref, h_scr):
    h_scr[...] = _rms_mod(x_ref[...], g_ref[...], mod_ref[1:2, :], mod_ref[0:1, :]).astype(BF16)
    n_tiles = w_ref.shape[1] // TN_PROJ
    for j in range(n_tiles):
        cols = slice(j * TN_PROJ, (j + 1) * TN_PROJ)
        acc = jnp.dot(h_scr[...], w_ref[:, cols], preferred_element_type=F32)
        if j < 2 or j == n_tiles - 1:
            tab = slice(0, TN_PROJ) if j < 2 else slice(TN_PROJ, 2 * TN_PROJ)
            o_ref[:, cols] = (acc * cos_ref[:, tab] + _swap16(acc) * sin_ref[:, tab]).astype(BF16)
            if j == n_tiles - 1:
                kv_ref[...] = acc
        else:
            o_ref[:, cols] = jax.nn.gelu(acc).astype(BF16)


def _proj_even(x, mods, gain, w, cos_t, sin_t, n_prompt_blocks, blocks_per_seq):
    t, d = x.shape
    n = w.shape[1]
    tm, tn = TM_PROJ, TN_PROJ

    def mod_idx(i):
        return (jnp.where(i < n_prompt_blocks, 0, 1 + (i - n_prompt_blocks) // blocks_per_seq), 0, 0)

    def tab_idx(i):
        return (jnp.where(i < n_prompt_blocks, blocks_per_seq, (i - n_prompt_blocks) % blocks_per_seq), 0)

    return pl.pallas_call(
        _proj_even_kernel,
        grid=(t // tm,),
        in_specs=[
            pl.BlockSpec((tm, d), lambda i: (i, 0)),
            pl.BlockSpec((None, 6, d), mod_idx),
            pl.BlockSpec((1, d), lambda i: (0, 0)),
            pl.BlockSpec((d, n), lambda i: (0, 0), pipeline_mode=pl.Buffered(1)),
            pl.BlockSpec((tm, 2 * tn), tab_idx),
            pl.BlockSpec((tm, 2 * tn), tab_idx),
        ],
        out_specs=[
            pl.BlockSpec((tm, n), lambda i: (i, 0)),
            pl.BlockSpec((tm, tn), lambda i: (i, 0)),
        ],
        out_shape=[jax.ShapeDtypeStruct((t, n), BF16), jax.ShapeDtypeStruct((t, tn), F32)],
        scratch_shapes=[pltpu.VMEM((tm, d), BF16)],
        compiler_params=_cparams(("arbitrary",)),
        name="proj_even",
    )(x, mods, gain, w, cos_t, sin_t)


def _rope_tables(dec_seq, tm):
    n = jnp.arange(dec_seq)
    row = (n // GRID_W).astype(F32)
    col = (n % GRID_W).astype(F32)
    nf = HEAD_DIM // 4
    inv_freq = ROPE_BASE ** (-jnp.arange(nf, dtype=F32) / nf)
    ar = row[:, None] * inv_freq[None, :]
    ac = col[:, None] * inv_freq[None, :]
    cos_h = jnp.concatenate([jnp.cos(ar), jnp.cos(ar), jnp.cos(ac), jnp.cos(ac)], axis=-1)
    sin_h = jnp.concatenate([-jnp.sin(ar), jnp.sin(ar), -jnp.sin(ac), jnp.sin(ac)], axis=-1)
    scale = HEAD_DIM ** -0.5
    ones = jnp.ones((dec_seq, KV_WIDTH), F32)
    zeros = jnp.zeros((dec_seq, KV_WIDTH), F32)
    q_heads = TN_PROJ // HEAD_DIM
    cos_t = jnp.concatenate([jnp.tile(cos_h, (1, q_heads)) * scale, jnp.tile(cos_h, (1, N_KV_HEADS)), ones], axis=-1)
    sin_t = jnp.concatenate([jnp.tile(sin_h, (1, q_heads)) * scale, jnp.tile(sin_h, (1, N_KV_HEADS)), zeros], axis=-1)
    id_cos = jnp.concatenate([jnp.full((tm, TN_PROJ), scale, F32), jnp.ones((tm, TN_PROJ), F32)], axis=-1)
    id_sin = jnp.zeros((tm, 2 * TN_PROJ), F32)
    return jnp.concatenate([cos_t, id_cos], axis=0), jnp.concatenate([sin_t, id_sin], axis=0)


def _sink_attend(q, keys, vals, masks, sink):
    scores = []
    for k, mask in zip(keys, masks):
        s = lax.dot_general(q, k, (((1,), (1,)), ((), ())), preferred_element_type=F32)
        scores.append(s if mask is None else jnp.where(mask, s, NEG_INF))
    m = sink
    for s in scores:
        m = jnp.maximum(m, jnp.max(s, axis=-1, keepdims=True))
    es = [jnp.exp(s - m) for s in scores]
    den = jnp.exp(sink - m)
    for e in es:
        den = den + jnp.sum(e, axis=-1, keepdims=True)
    inv = 1.0 / den
    out = None
    for e, v in zip(es, vals):
        o = jnp.dot((e * inv).astype(BF16), v, preferred_element_type=F32)
        out = o if out is None else out + o
    return out


def _grouped_heads_attend(sink_ref, q_ref, o_ref, kh, keys, vals, masks):
    rows = q_ref.shape[0]
    heads = [kh * Q_GROUPS + g for g in range(Q_GROUPS)]
    q = jnp.concatenate([q_ref[:, h * HEAD_DIM:(h + 1) * HEAD_DIM] for h in heads], axis=0)
    ridx = lax.broadcasted_iota(jnp.int32, (Q_GROUPS * rows, 1), 0)
    sink = jnp.full((Q_GROUPS * rows, 1), sink_ref[heads[-1]], F32)
    for g in range(Q_GROUPS - 2, -1, -1):
        sink = jnp.where(ridx < (g + 1) * rows, sink_ref[heads[g]], sink)
    o = _sink_attend(q, keys, vals, masks, sink)
    for g, h in enumerate(heads):
        o_ref[:, h * HEAD_DIM:(h + 1) * HEAD_DIM] = o[g * rows:(g + 1) * rows, :].astype(BF16)


def _ctx_attn_kernel(sink_ref, q_ref, kv_ref, o_ref):
    for kh in range(N_KV_HEADS):
        k = kv_ref[:, kh * HEAD_DIM:(kh + 1) * HEAD_DIM]
        v = kv_ref[:, KV_WIDTH + kh * HEAD_DIM:KV_WIDTH + (kh + 1) * HEAD_DIM]
        _grouped_heads_attend(sink_ref, q_ref, o_ref, kh, [k], [v], [None])


def _ctx_attention(proj, sink, batch, seq, kv_col_block):
    return pl.pallas_call(
        _ctx_attn_kernel,
        grid=(batch,),
        in_specs=[
            pl.BlockSpec(memory_space=pltpu.SMEM),
            pl.BlockSpec((seq, ATTN_WIDTH), lambda b: (b, 0)),
            pl.BlockSpec((seq, 2 * KV_WIDTH), lambda b: (b, kv_col_block)),
        ],
        out_specs=pl.BlockSpec((seq, ATTN_WIDTH), lambda b: (b, 0)),
        out_shape=jax.ShapeDtypeStruct((batch * seq, ATTN_WIDTH), BF16),
        compiler_params=_cparams(("arbitrary",)),
        name="ctx_attention",
    )(sink, proj, proj)


def _lat_attn_kernel(sink_ref, q_ref, kv_ref, ck_ref, cv_ref, o_ref, *, seq):
    i = pl.program_id(1)
    win = 3 * ATTN_BLK
    ws = pl.multiple_of(jnp.clip((i - 1) * ATTN_BLK, 0, seq - win), ATTN_BLK)
    shape = (Q_GROUPS * ATTN_BLK, win)
    qpos = i * ATTN_BLK + (lax.broadcasted_iota(jnp.int32, shape, 0) % ATTN_BLK)
    kpos = ws + lax.broadcasted_iota(jnp.int32, shape, 1)
    valid = jnp.abs(qpos - kpos) <= ATTN_BLK
    kvw = kv_ref[pl.ds(ws, win), :]
    for kh in range(N_KV_HEADS):
        k_loc = kvw[:, kh * HEAD_DIM:(kh + 1) * HEAD_DIM]
        v_loc = kvw[:, KV_WIDTH + kh * HEAD_DIM:KV_WIDTH + (kh + 1) * HEAD_DIM]
        k_ctx = ck_ref[:, kh * HEAD_DIM:(kh + 1) * HEAD_DIM]
        v_ctx = cv_ref[:, kh * HEAD_DIM:(kh + 1) * HEAD_DIM]
        _grouped_heads_attend(sink_ref, q_ref, o_ref, kh, [k_ctx, k_loc], [v_ctx, v_loc], [None, valid])


def _lat_attention(proj, sink, cache_k, cache_v, n_prompt, dec_batch, dec_seq, kv_col_block):
    qb = dec_seq // ATTN_BLK
    q_off = n_prompt // ATTN_BLK
    s_off = n_prompt // dec_seq
    past = cache_k.shape[1]
    return pl.pallas_call(
        functools.partial(_lat_attn_kernel, seq=dec_seq),
        grid=(dec_batch, qb),
        in_specs=[
            pl.BlockSpec(memory_space=pltpu.SMEM),
            pl.BlockSpec((ATTN_BLK, ATTN_WIDTH), lambda b, i: (q_off + b * qb + i, 0)),
            pl.BlockSpec((dec_seq, 2 * KV_WIDTH), lambda b, i: (s_off + b, kv_col_block)),
            pl.BlockSpec((None, past, KV_WIDTH), lambda b, i: (b, 0, 0)),
            pl.BlockSpec((None, past, KV_WIDTH), lambda b, i: (b, 0, 0)),
        ],
        out_specs=pl.BlockSpec((ATTN_BLK, ATTN_WIDTH), lambda b, i: (b * qb + i, 0)),
        out_shape=jax.ShapeDtypeStruct((dec_batch * dec_seq, ATTN_WIDTH), BF16),
        compiler_params=_cparams(("arbitrary", "arbitrary")),
        name="lat_attention",
    )(sink, proj, proj, cache_k, cache_v)


def _sgu_kernel(u_ref, v_ref, g_ref, w_ref, bt_ref, z_ref):
    for c in range(TM_SGU // CHUNK):
        rows = slice(c * CHUNK, (c + 1) * CHUNK)
        v = v_ref[rows, :].astype(F32)
        ms = jnp.mean(v * v, axis=-1, keepdims=True)
        vn = (v * lax.rsqrt(ms + EPS) * g_ref[...]).astype(BF16)
        for g in range(SGU_GROUPS):
            cols = slice(g * SGU_GROUP_DIM, (g + 1) * SGU_GROUP_DIM)
            mixed = jnp.dot(w_ref[g], vn[:, cols], preferred_element_type=F32) + bt_ref[:, g:g + 1]
            z_ref[rows, cols] = (u_ref[rows, cols].astype(F32) * mixed).astype(BF16)


def _sgu(proj, norm_g, w_s, b_t):
    t = proj.shape[0]
    return pl.pallas_call(
        _sgu_kernel,
        grid=(t // TM_SGU,),
        in_specs=[
            pl.BlockSpec((TM_SGU, SGU_WIDTH), lambda i: (i, 1)),
            pl.BlockSpec((TM_SGU, SGU_WIDTH), lambda i: (i, 2)),
            pl.BlockSpec((1, SGU_WIDTH), lambda i: (0, 0)),
            pl.BlockSpec((SGU_GROUPS, CHUNK, CHUNK), lambda i: (0, 0, 0)),
            pl.BlockSpec((CHUNK, SGU_GROUPS), lambda i: (0, 0)),
        ],
        out_specs=pl.BlockSpec((TM_SGU, SGU_WIDTH), lambda i: (i, 0)),
        out_shape=jax.ShapeDtypeStruct((t, SGU_WIDTH), BF16),
        compiler_params=_cparams(("arbitrary",)),
        name="sgu",
    )(proj, proj, norm_g, w_s, b_t)


def _residual_router(x, acc, mod_ref, g2_ref, wrt_ref, brt_ref, x1_ref, h2_ref, lg_ref):
    x1 = x + mod_ref[2:3, :] * acc
    x1_ref[...] = x1
    h2 = _rms_mod(x1, g2_ref[...], mod_ref[4:5, :], mod_ref[3:4, :])
    h2_ref[...] = h2
    lg = lax.dot_general(wrt_ref[...], h2.astype(BF16), (((1,), (1,)), ((), ())), preferred_element_type=F32)
    lg_ref[...] = lg + brt_ref[...]


def _out_even_kernel(x_ref, ap_ref, as_ref, z_ref, w_ref, mod_ref, g2_ref, wrt_ref, brt_ref,
                     x1_ref, h2_ref, lg_ref, *, n_prompt_blocks):
    i = pl.program_id(0)
    a = jnp.where(i < n_prompt_blocks, ap_ref[...].astype(F32), as_ref[...].astype(F32)).astype(BF16)
    acc = jnp.dot(a, w_ref[0:ATTN_WIDTH, :], preferred_element_type=F32)
    acc = acc + jnp.dot(z_ref[...], w_ref[ATTN_WIDTH:, :], preferred_element_type=F32)
    _residual_router(x_ref[...], acc, mod_ref, g2_ref, wrt_ref, brt_ref, x1_ref, h2_ref, lg_ref)


def _out_odd_kernel(x_ref, hc_ref, w_ref, mod_ref, g2_ref, wrt_ref, brt_ref, x1_ref, h2_ref, lg_ref):
    acc = jnp.dot(hc_ref[...], w_ref[...], preferred_element_type=F32)
    _residual_router(x_ref[...], acc, mod_ref, g2_ref, wrt_ref, brt_ref, x1_ref, h2_ref, lg_ref)


def _out_proj(x, lhs, w, mods, gain2, wrt, brt, n_prompt, dec_seq):
    t, d = x.shape
    tm = TM_OUT
    npb = n_prompt // tm
    bps = dec_seq // tm

    def mod_idx(i):
        return (jnp.where(i < npb, 0, 1 + (i - npb) // bps), 0, 0)

    if len(lhs) == 3:
        a_p, a_s, z = lhs
        nsb = a_s.shape[0] // tm
        body = functools.partial(_out_even_kernel, n_prompt_blocks=npb)
        lhs_specs = [
            pl.BlockSpec((tm, ATTN_WIDTH), lambda i: (jnp.minimum(i, npb - 1), 0)),
            pl.BlockSpec((tm, ATTN_WIDTH), lambda i: (jnp.clip(i - npb, 0, nsb - 1), 0)),
            pl.BlockSpec((tm, SGU_WIDTH), lambda i: (i, 0)),
        ]
    else:
        body = _out_odd_kernel
        lhs_specs = [pl.BlockSpec((tm, d), lambda i: (i, 0))]
    return pl.pallas_call(
        body,
        grid=(t // tm,),
        in_specs=[pl.BlockSpec((tm, d), lambda i: (i, 0))] + lhs_specs + [
            pl.BlockSpec((w.shape[0], d), lambda i: (0, 0)),
            pl.BlockSpec((None, 6, d), mod_idx),
            pl.BlockSpec((1, d), lambda i: (0, 0)),
            pl.BlockSpec((ROUTER_LANES, d), lambda i: (0, 0)),
            pl.BlockSpec((ROUTER_LANES, 1), lambda i: (0, 0)),
        ],
        out_specs=[
            pl.BlockSpec((tm, d), lambda i: (i, 0)),
            pl.BlockSpec((tm, d), lambda i: (i, 0)),
            pl.BlockSpec((ROUTER_LANES, tm), lambda i: (0, i)),
        ],
        out_shape=[
            jax.ShapeDtypeStruct((t, d), F32),
            jax.ShapeDtypeStruct((t, d), F32),
            jax.ShapeDtypeStruct((ROUTER_LANES, t), F32),
        ],
        compiler_params=_cparams(("arbitrary",)),
        name="out_proj",
    )(x, *lhs, w, mods, gain2, wrt, brt)


def _first_argmax4(v0, v1, v2, v3):
    m = jnp.maximum(jnp.maximum(v0, v1), jnp.maximum(v2, v3))
    idx = jnp.where(v0 == m, 0.0, jnp.where(v1 == m, 1.0, jnp.where(v2 == m, 2.0, 3.0)))
    return m, idx


def _route_kernel(lg_ref, pos_ref, ea_ref, eb_ref, nu_ref, oh_scr, rank_scr, *, n_tokens):
    lg = lg_ref[...]
    rows = [lg[r:r + 1, :] for r in range(N_GROUPS_MOE + N_EXPERTS)]
    _, gidx = _first_argmax4(*rows[:N_GROUPS_MOE])
    e = []
    for k in range(EXPERTS_PER_GROUP):
        cand = [rows[N_GROUPS_MOE + g * EXPERTS_PER_GROUP + k] for g in range(N_GROUPS_MOE)]
        e.append(jnp.where(gidx == 0.0, cand[0], jnp.where(gidx == 1.0, cand[1], jnp.where(gidx == 2.0, cand[2], cand[3]))))
    _, l1 = _first_argmax4(*e)
    e2 = [jnp.where(l1 == float(k), -jnp.inf, e[k]) for k in range(EXPERTS_PER_GROUP)]
    _, l2 = _first_argmax4(*e2)
    lo = jnp.minimum(l1, l2)
    hi = jnp.maximum(l1, l2)
    pair = jnp.where(lo == 0.0, hi - 1.0, jnp.where(lo == 1.0, hi + 1.0, 5.0))
    cls = gidx * float(PAIRS_PER_GROUP) + pair

    crow = lax.broadcasted_iota(jnp.int32, (CLASS_ROWS, n_tokens), 0).astype(F32)
    oh_scr[...] = jnp.where(crow == cls, 1.0, 0.0).astype(BF16)

    jj = lax.broadcasted_iota(jnp.int32, (CUM_CHUNK, CUM_CHUNK), 0)
    ii = lax.broadcasted_iota(jnp.int32, (CUM_CHUNK, CUM_CHUNK), 1)
    tri = jnp.where(jj < ii, 1.0, 0.0).astype(BF16)

    def chunk(c, carry):
        off = pl.multiple_of(c * CUM_CHUNK, CUM_CHUNK)
        oh = oh_scr[:, pl.ds(off, CUM_CHUNK)]
        ohf = oh.astype(F32)
        cum = jnp.dot(oh, tri, preferred_element_type=F32) + carry
        rank_scr[:, pl.ds(off, CUM_CHUNK)] = jnp.sum(cum * ohf, axis=0, keepdims=True)
        return carry + jnp.sum(ohf, axis=1, keepdims=True)

    counts = lax.fori_loop(0, n_tokens // CUM_CHUNK, chunk, jnp.zeros((CLASS_ROWS, 1), F32))
    padded = jnp.floor((counts + float(MOE_BLK - 1)) * (1.0 / MOE_BLK)) * float(MOE_BLK)
    run = jnp.zeros((1, 1), F32)
    starts_rows = []
    for c in range(CLASS_ROWS):
        starts_rows.append(run)
        run = run + padded[c:c + 1, :]
    starts = jnp.concatenate(starts_rows, axis=0)
    ends = starts + padded
    total = run

    ohf = oh_scr[...].astype(F32)
    pos = rank_scr[...] + jnp.sum(ohf * starts, axis=0, keepdims=True)
    pos_ref[...] = pos.astype(jnp.int32)

    bstart = lax.broadcasted_iota(jnp.int32, (CLASS_ROWS, MAX_MOE_BLOCKS), 1).astype(F32) * float(MOE_BLK)
    brow = lax.broadcasted_iota(jnp.int32, (CLASS_ROWS, MAX_MOE_BLOCKS), 0)
    done = jnp.where((ends <= bstart) & (brow < N_CLASSES), 1.0, 0.0)
    bcls = jnp.minimum(jnp.sum(done, axis=0, keepdims=True), float(N_CLASSES - 1))
    grp = (jnp.where(bcls >= 6.0, 1.0, 0.0) + jnp.where(bcls >= 12.0, 1.0, 0.0) + jnp.where(bcls >= 18.0, 1.0, 0.0))
    bp = bcls - grp * float(PAIRS_PER_GROUP)
    ge3 = jnp.where(bp >= 3.0, 1.0, 0.0)
    ge5 = jnp.where(bp >= 5.0, 1.0, 0.0)
    blo = ge3 + ge5
    bhi = bp + 1.0 - 2.0 * ge3 - ge5
    ea_ref[...] = (grp * float(EXPERTS_PER_GROUP) + blo).astype(jnp.int32)
    eb_ref[...] = (grp * float(EXPERTS_PER_GROUP) + bhi).astype(jnp.int32)
    nu_ref[...] = jnp.broadcast_to(total * (1.0 / MOE_BLK), (1, MAX_MOE_BLOCKS)).astype(jnp.int32)


def _route(logits_t):
    t = logits_t.shape[1]
    rows = N_CLASSES
    i32 = jnp.int32
    return pl.pallas_call(
        functools.partial(_route_kernel, n_tokens=t),
        grid=(1,),
        in_specs=[pl.BlockSpec((rows, t), lambda i: (0, 0))],
        out_specs=[
            pl.BlockSpec((1, t), lambda i: (0, 0)),
            pl.BlockSpec((1, MAX_MOE_BLOCKS), lambda i: (0, 0)),
            pl.BlockSpec((1, MAX_MOE_BLOCKS), lambda i: (0, 0)),
            pl.BlockSpec((1, MAX_MOE_BLOCKS), lambda i: (0, 0)),
        ],
        out_shape=[
            jax.ShapeDtypeStruct((1, t), i32),
            jax.ShapeDtypeStruct((1, MAX_MOE_BLOCKS), i32),
            jax.ShapeDtypeStruct((1, MAX_MOE_BLOCKS), i32),
            jax.ShapeDtypeStruct((1, MAX_MOE_BLOCKS), i32),
        ],
        scratch_shapes=[pltpu.VMEM((CLASS_ROWS, t), BF16), pltpu.VMEM((1, t), F32)],
        compiler_params=_cparams(("arbitrary",)),
        name="route",
    )(logits_t)


def _row_copy(src_ref, src_row, dst_ref, dst_row, sem):
    return pltpu.make_async_copy(src_ref.at[pl.ds(src_row, 1), :], dst_ref.at[pl.ds(dst_row, 1), :], sem)


def _dispatch_kernel(pos_ref, h_ref, init_ref, o_ref, sem):
    del init_ref
    base = pl.program_id(0) * TM_OUT

    def issue(r, carry):
        _row_copy(h_ref, r, o_ref, pos_ref[base + r], sem).start()
        return carry

    lax.fori_loop(0, TM_OUT, issue, 0)

    def drain(r, carry):
        _row_copy(h_ref, r, o_ref, pos_ref[base + r], sem).wait()
        return carry

    lax.fori_loop(0, TM_OUT, drain, 0)


def _dispatch(pos, h2, n_rows):
    t, d = h2.shape
    init = jnp.zeros((n_rows, d), F32)
    return pl.pallas_call(
        _dispatch_kernel,
        grid_spec=pltpu.PrefetchScalarGridSpec(
            num_scalar_prefetch=1,
            grid=(t // TM_OUT,),
            in_specs=[
                pl.BlockSpec((TM_OUT, d), lambda i, pos: (i, 0)),
                pl.BlockSpec(memory_space=pl.ANY),
            ],
            out_specs=pl.BlockSpec(memory_space=pl.ANY),
            scratch_shapes=[pltpu.SemaphoreType.DMA(())],
        ),
        out_shape=jax.ShapeDtypeStruct((n_rows, d), F32),
        input_output_aliases={2: 0},
        compiler_params=_cparams(("arbitrary",)),
        name="moe_dispatch",
    )(pos, h2, init)


def _expert_ffn(x, wg_ref, wu_ref, wd_ref):
    gate = jnp.dot(x, wg_ref[...], preferred_element_type=F32)
    up = jnp.dot(x, wu_ref[...], preferred_element_type=F32)
    hidden = (gate * jax.nn.sigmoid(gate) * up).astype(BF16)
    return jnp.dot(hidden, wd_ref[...], preferred_element_type=F32)


def _moe_kernel(ea_ref, eb_ref, nu_ref, x_ref, wr_ref, br_ref, wga, wua, wda, wgb, wub, wdb, y_ref):
    b = pl.program_id(0)

    @pl.when(b < nu_ref[0])
    def _():
        x = x_ref[...].astype(BF16)
        lg = jnp.dot(x, wr_ref[...], preferred_element_type=F32) + br_ref[...]
        lane = lax.broadcasted_iota(jnp.int32, lg.shape, 1)
        ea = ea_ref[b]
        eb = eb_ref[b]
        grp = ea // EXPERTS_PER_GROUP
        is_grp = lane < N_GROUPS_MOE
        gm = jnp.max(jnp.where(is_grp, lg, -jnp.inf), axis=-1, keepdims=True)
        ge = jnp.where(is_grp, jnp.exp(lg - gm), 0.0)
        p_g = jnp.sum(jnp.where(lane == grp, ge, 0.0), axis=-1, keepdims=True) / jnp.sum(ge, axis=-1, keepdims=True)
        la = jnp.sum(jnp.where(lane == N_GROUPS_MOE + ea, lg, 0.0), axis=-1, keepdims=True)
        lb = jnp.sum(jnp.where(lane == N_GROUPS_MOE + eb, lg, 0.0), axis=-1, keepdims=True)
        mm = jnp.maximum(la, lb)
        xa = jnp.exp(la - mm)
        xb = jnp.exp(lb - mm)
        ga = xa / (xa + xb) * p_g
        gb = xb / (xa + xb) * p_g
        ya = _expert_ffn(x, wga, wua, wda)
        yb = _expert_ffn(x, wgb, wub, wdb)
        y_ref[...] = ya * ga + yb * gb

    @pl.when(b >= nu_ref[0])
    def _():
        y_ref[...] = jnp.zeros_like(y_ref)


def _moe_experts(ea, eb, nused, hs, wr, br, w_gate, w_up, w_down, n_blocks):
    d = hs.shape[1]
    de = w_gate.shape[2]

    def wa_idx(b, ea, eb, nu):
        return (ea[b], 0, 0)

    def wb_idx(b, ea, eb, nu):
        return (eb[b], 0, 0)

    return pl.pallas_call(
        _moe_kernel,
        grid_spec=pltpu.PrefetchScalarGridSpec(
            num_scalar_prefetch=3,
            grid=(n_blocks,),
            in_specs=[
                pl.BlockSpec((MOE_BLK, d), lambda b, ea, eb, nu: (b, 0)),
                pl.BlockSpec((d, ROUTER_LANES), lambda b, ea, eb, nu: (0, 0)),
                pl.BlockSpec((1, ROUTER_LANES), lambda b, ea, eb, nu: (0, 0)),
                pl.BlockSpec((None, d, de), wa_idx),
                pl.BlockSpec((None, d, de), wa_idx),
                pl.BlockSpec((None, de, d), wa_idx),
                pl.BlockSpec((None, d, de), wb_idx),
                pl.BlockSpec((None, d, de), wb_idx),
                pl.BlockSpec((None, de, d), wb_idx),
            ],
            out_specs=pl.BlockSpec((MOE_BLK, d), lambda b, ea, eb, nu: (b, 0)),
        ),
        out_shape=jax.ShapeDtypeStruct((n_blocks * MOE_BLK, d), F32),
        compiler_params=_cparams(("arbitrary",)),
        name="moe_experts",
    )(ea, eb, nused, hs, wr, br, w_gate, w_up, w_down, w_gate, w_up, w_down)


def _combine_kernel(pos_ref, ys_ref, x1_ref, mod_ref, gf_ref, o_ref, ybuf, sem, *, block_off, final):
    base = (pl.program_id(0) + block_off) * TM_OUT

    def issue(r, carry):
        _row_copy(ys_ref, pos_ref[base + r], ybuf, r, sem).start()
        return carry

    lax.fori_loop(0, TM_OUT, issue, 0)

    def drain(r, carry):
        _row_copy(ys_ref, pos_ref[base + r], ybuf, r, sem).wait()
        return carry

    lax.fori_loop(0, TM_OUT, drain, 0)
    x2 = x1_ref[...] + mod_ref[5:6, :] * ybuf[...]
    if final:
        ms = jnp.mean(x2 * x2, axis=-1, keepdims=True)
        x2 = x2 * lax.rsqrt(ms + EPS) * gf_ref[...]
    o_ref[...] = x2


def _combine(pos, ys, x1, mods, gain_f, n_prompt, dec_seq, row_off, n_rows, final):
    d = x1.shape[1]
    tm = TM_OUT
    npb = n_prompt // tm
    bps = dec_seq // tm
    boff = row_off // tm

    def mod_idx(i, pos):
        blk = i + boff
        return (jnp.where(blk < npb, 0, 1 + (blk - npb) // bps), 0, 0)

    return pl.pallas_call(
        functools.partial(_combine_kernel, block_off=boff, final=final),
        grid_spec=pltpu.PrefetchScalarGridSpec(
            num_scalar_prefetch=1,
            grid=(n_rows // tm,),
            in_specs=[
                pl.BlockSpec(memory_space=pl.ANY),
                pl.BlockSpec((tm, d), lambda i, pos: (i + boff, 0)),
                pl.BlockSpec((None, 6, d), mod_idx),
                pl.BlockSpec((1, d), lambda i, pos: (0, 0)),
            ],
            out_specs=pl.BlockSpec((tm, d), lambda i, pos: (i, 0)),
            scratch_shapes=[pltpu.VMEM((tm, d), F32), pltpu.SemaphoreType.DMA(())],
        ),
        out_shape=jax.ShapeDtypeStruct((n_rows, d), F32),
        compiler_params=_cparams(("arbitrary",)),
        name="moe_combine",
    )(pos, ys, x1, mods, gain_f)


def _proj_glu_kernel(x_ref, mod_ref, g_ref, w_ref, o_ref, h_scr):
    h_scr[...] = _rms_mod(x_ref[...], g_ref[...], mod_ref[1:2, :], mod_ref[0:1, :]).astype(BF16)
    half = TN_PROJ // 2
    for j in range(w_ref.shape[1] // TN_PROJ):
        acc = jnp.dot(h_scr[...], w_ref[:, j * TN_PROJ:(j + 1) * TN_PROJ], preferred_element_type=F32)
        o_ref[:, j * half:(j + 1) * half] = (acc[:, :half] * jax.nn.sigmoid(acc[:, half:])).astype(BF16)


def _proj_glu(x, mods, gain, w, n_prompt, dec_seq):
    t, d = x.shape
    n = w.shape[1]
    tm = TM_PROJ
    npb = n_prompt // tm
    bps = dec_seq // tm

    def mod_idx(i):
        return (jnp.where(i < npb, 0, 1 + (i - npb) // bps), 0, 0)

    return pl.pallas_call(
        _proj_glu_kernel,
        grid=(t // tm,),
        in_specs=[
            pl.BlockSpec((tm, d), lambda i: (i, 0)),
            pl.BlockSpec((None, 6, d), mod_idx),
            pl.BlockSpec((1, d), lambda i: (0, 0)),
            pl.BlockSpec((d, n), lambda i: (0, 0), pipeline_mode=pl.Buffered(1)),
        ],
        out_specs=pl.BlockSpec((tm, n // 2), lambda i: (i, 0)),
        out_shape=jax.ShapeDtypeStruct((t, n // 2), BF16),
        scratch_shapes=[pltpu.VMEM((tm, d), BF16)],
        compiler_params=_cparams(("arbitrary",)),
        name="proj_glu",
    )(x, mods, gain, w)


CONV_ROWS = 64
CONV_COLS = 256


def _conv_kernel(x_ref, prev_ref, next_ref, dw_ref, b_ref, g_ref, o_ref, xpad, ybuf, *, n_prompt_blocks, blocks_per_seq):
    i = pl.program_id(0)
    in_sample = i >= n_prompt_blocks
    j = i - n_prompt_blocks
    is_start = jnp.logical_or(jnp.logical_not(in_sample), j % blocks_per_seq == 0)
    is_end = jnp.logical_or(jnp.logical_not(in_sample), j % blocks_per_seq == blocks_per_seq - 1)
    tm = TM_OUT
    xpad[0:CONV_HALO, :] = jnp.where(is_start, 0.0, prev_ref[...].astype(F32))
    xpad[CONV_HALO:CONV_HALO + tm, :] = x_ref[...].astype(F32)
    xpad[CONV_HALO + tm:, :] = jnp.where(is_end, 0.0, next_ref[...].astype(F32))
    first = CONV_HALO - CONV_K // 2

    def col_chunk(c, ssq):
        c0 = pl.multiple_of(c * CONV_COLS, CONV_COLS)
        cols = pl.ds(c0, CONV_COLS)
        w = dw_ref[:, cols]
        bias = b_ref[:, cols]
        parts = []
        for rc in range(tm // CONV_ROWS):
            r0 = rc * CONV_ROWS
            acc = jnp.zeros((CONV_ROWS, CONV_COLS), F32)
            for k in range(CONV_K):
                acc = acc + w[k:k + 1, :] * xpad[r0 + first + k:r0 + first + k + CONV_ROWS, cols]
            acc = acc + bias
            ybuf[r0:r0 + CONV_ROWS, cols] = acc
            parts.append(jnp.sum(acc * acc, axis=-1, keepdims=True))
        return ssq + jnp.concatenate(parts, axis=0)

    d = x_ref.shape[1]
    ssq = lax.fori_loop(0, d // CONV_COLS, col_chunk, jnp.zeros((tm, 1), F32))
    y = ybuf[...] * lax.rsqrt(ssq * (1.0 / d) + EPS) * g_ref[...]
    o_ref[...] = (y * jax.nn.sigmoid(y)).astype(BF16)


def _conv_mix(xg, dw, dw_b, norm_g, n_prompt, seq, dec_seq):
    t, d = xg.shape
    tm = TM_OUT
    assert seq == tm, "context sequences must be exactly one conv row block"
    npb = n_prompt // tm
    bps = dec_seq // tm
    hb = tm // CONV_HALO
    last = t // CONV_HALO - 1
    return pl.pallas_call(
        functools.partial(_conv_kernel, n_prompt_blocks=npb, blocks_per_seq=bps),
        grid=(t // tm,),
        in_specs=[
            pl.BlockSpec((tm, d), lambda i: (i, 0)),
            pl.BlockSpec((CONV_HALO, d), lambda i: (jnp.maximum(i * hb - 1, 0), 0)),
            pl.BlockSpec((CONV_HALO, d), lambda i: (jnp.minimum((i + 1) * hb, last), 0)),
            pl.BlockSpec((dw.shape[0], d), lambda i: (0, 0)),
            pl.BlockSpec((1, d), lambda i: (0, 0)),
            pl.BlockSpec((1, d), lambda i: (0, 0)),
        ],
        out_specs=pl.BlockSpec((tm, d), lambda i: (i, 0)),
        out_shape=jax.ShapeDtypeStruct((t, d), BF16),
        scratch_shapes=[pltpu.VMEM((tm + 2 * CONV_HALO, d), F32), pltpu.VMEM((tm, d), F32)],
        compiler_params=_cparams(("arbitrary",)),
        name="conv_mix",
    )(xg, xg, xg, dw, dw_b, norm_g)


def _router_operands(w_rg, b_rg, w_re, b_re):
    d = w_rg.shape[0]
    used = N_GROUPS_MOE + N_EXPERTS
    w = jnp.concatenate([w_rg, w_re, jnp.zeros((d, ROUTER_LANES - used), F32)], axis=1)
    b = jnp.concatenate([b_rg, b_re, jnp.zeros((ROUTER_LANES - used,), F32)])
    return w.astype(BF16), b.reshape(1, ROUTER_LANES), w.T.astype(BF16), b.reshape(ROUTER_LANES, 1)


def _moe_rows(h2, logits_t, wr, br, w_gate, w_up, w_down):
    t = h2.shape[0]
    n_blocks = -(-(t + N_CLASSES * (MOE_BLK - 1)) // MOE_BLK)
    assert n_blocks <= MAX_MOE_BLOCKS
    pos, ea, eb, nused = _route(logits_t)
    pos = pos.reshape(t)
    hs = _dispatch(pos, h2, n_blocks * MOE_BLK)
    ys = _moe_experts(ea.reshape(-1), eb.reshape(-1), nused.reshape(-1)[:1], hs, wr, br, w_gate, w_up, w_down, n_blocks)
    return pos, ys


def kernel(x_prompt, x_sample, cache_k, cache_v, c, c_ctx, w_in_even, attn_sink, sgu_norm, sgu_w, sgu_b,
           w_out_even, conv_w_in, conv_dw, conv_dw_b, conv_norm, conv_w_out, ada_w, ada_b, norm_mix, norm_ffn,
           router_group_w, router_group_b, router_expert_w, router_expert_b, expert_w_gate, expert_w_up,
           expert_w_down, final_norm):
    batch, seq, d = x_prompt.shape
    dec_batch, dec_seq, _ = x_sample.shape
    depth = ada_w.shape[0]
    n_prompt = batch * seq
    n_sample = dec_batch * dec_seq
    assert d == D_MODEL and depth == 2 and n_prompt % dec_seq == 0 and dec_seq % TM_PROJ == 0

    x = jnp.concatenate([x_prompt.reshape(n_prompt, d), x_sample.reshape(n_sample, d)], axis=0)

    n_mod = -(-(1 + dec_batch) // 8) * 8
    cvec = jnp.concatenate([c_ctx[None, :], c, jnp.zeros((n_mod - 1 - dec_batch, d), F32)], axis=0)
    mods = _ada_params(cvec, ada_w, ada_b).reshape(depth, n_mod, 6, d)

    wi = w_in_even[0]
    q_end, k_end, v_end = ATTN_WIDTH, ATTN_WIDTH + KV_WIDTH, ATTN_WIDTH + 2 * KV_WIDTH
    w_in = jnp.concatenate([wi[:, :q_end], wi[:, v_end:], wi[:, q_end:v_end]], axis=1).astype(BF16)
    cos_t, sin_t = _rope_tables(dec_seq, TM_PROJ)
    proj, kv32 = _proj_even(x, mods[0], norm_mix[0][None, :], w_in, cos_t, sin_t,
                            n_prompt // TM_PROJ, dec_seq // TM_PROJ)
    kv_col_block = (IN_WIDTH_EVEN - 2 * KV_WIDTH) // (2 * KV_WIDTH)
    sink = attn_sink[0]
    a_p = _ctx_attention(proj, sink, batch, seq, kv_col_block)
    past = cache_k.shape[2]
    ck = cache_k[:, 0].reshape(dec_batch, past, KV_WIDTH).astype(BF16)
    cv = cache_v[:, 0].reshape(dec_batch, past, KV_WIDTH).astype(BF16)
    a_s = _lat_attention(proj, sink, ck, cv, n_prompt, dec_batch, dec_seq, kv_col_block)
    z = _sgu(proj, sgu_norm[0][None, :], sgu_w[0].astype(BF16), sgu_b[0].T)

    wr0, br0, wrt0, brt0 = _router_operands(router_group_w[0], router_group_b[0], router_expert_w[0], router_expert_b[0])
    x1, h2, lg = _out_proj(x, (a_p, a_s, z), w_out_even[0].astype(BF16), mods[0], norm_ffn[0][None, :],
                           wrt0, brt0, n_prompt, dec_seq)
    pos, ys = _moe_rows(h2, lg, wr0, br0, expert_w_gate[0].astype(BF16), expert_w_up[0].astype(BF16),
                        expert_w_down[0].astype(BF16))
    gain_f = final_norm[None, :]
    x = _combine(pos, ys, x1, mods[0], gain_f, n_prompt, dec_seq, 0, n_prompt + n_sample, final=False)

    cw = conv_w_in[0]
    half = TN_PROJ // 2
    n_ct = d // half
    cw = jnp.concatenate([cw[:, :d].reshape(d, n_ct, half), cw[:, d:].reshape(d, n_ct, half)], axis=2)
    xg = _proj_glu(x, mods[1], norm_mix[1][None, :], cw.reshape(d, 2 * d).astype(BF16), n_prompt, dec_seq)
    dw = jnp.concatenate([conv_dw[0], jnp.zeros((1, d), F32)], axis=0)
    hc = _conv_mix(xg, dw, conv_dw_b[0][None, :], conv_norm[0][None, :], n_prompt, seq, dec_seq)
    wr1, br1, wrt1, brt1 = _router_operands(router_group_w[1], router_group_b[1], router_expert_w[1], router_expert_b[1])
    x1, h2, lg = _out_proj(x, (hc,), conv_w_out[0].astype(BF16), mods[1], norm_ffn[1][None, :],
                           wrt1, brt1, n_prompt, dec_seq)
    pos, ys = _moe_rows(h2, lg, wr1, br1, expert_w_gate[1].astype(BF16), expert_w_up[1].astype(BF16),
                        expert_w_down[1].astype(BF16))
    y_prompt = _combine(pos, ys, x1, mods[1], gain_f, n_prompt, dec_seq, 0, n_prompt, final=True)
    y_sample = _combine(pos, ys, x1, mods[1], gain_f, n_prompt, dec_seq, n_prompt, n_sample, final=True)

    new_k = kv32[:n_prompt, :KV_WIDTH].reshape(batch, 1, seq, N_KV_HEADS, HEAD_DIM)
    new_v = kv32[:n_prompt, KV_WIDTH:].reshape(batch, 1, seq, N_KV_HEADS, HEAD_DIM)
    return (y_prompt.reshape(batch, seq, d), y_sample.reshape(dec_batch, dec_seq, d), new_k, new_v)
```

```python
import functools

import jax
import jax.numpy as jnp
import numpy as np
from jax import lax
from jax.experimental import pallas as pl
from jax.experimental.pallas import tpu as pltpu

F32 = jnp.float32
BF16 = jnp.bfloat16

D_MODEL = 2048
GRID_W = 64
ATTN_BLK = 128
N_HEADS = 16
N_KV_HEADS = 4
Q_GROUPS = N_HEADS // N_KV_HEADS
HEAD_DIM = 64
ATTN_WIDTH = N_HEADS * HEAD_DIM
KV_WIDTH = N_KV_HEADS * HEAD_DIM
ROPE_BASE = 10000.0
SGU_GROUPS = 4
SGU_WIDTH = D_MODEL // 2
SGU_GROUP_DIM = SGU_WIDTH // SGU_GROUPS
CHUNK = 128
IN_WIDTH_EVEN = ATTN_WIDTH + 2 * KV_WIDTH + 2 * SGU_WIDTH
CONV_K = 31
CONV_HALO = 16
N_GROUPS_MOE = 4
EXPERTS_PER_GROUP = 4
N_EXPERTS = N_GROUPS_MOE * EXPERTS_PER_GROUP
PAIRS_PER_GROUP = 6
N_CLASSES = N_GROUPS_MOE * PAIRS_PER_GROUP
D_EXPERT = 512
MOE_BLK = 256
EPS = 1e-6
NEG_INF = -1e30

LANES = 128
ROUTER_LANES = 128
CLASS_ROWS = 32
MAX_MOE_BLOCKS = 128
TM_PROJ = 512
TN_PROJ = 512
TM_OUT = 256
TM_SGU = 512
CUM_CHUNK = 256
VMEM_LIMIT = 56 * 1024 * 1024


def _cparams(sem):
    return pltpu.CompilerParams(dimension_semantics=sem, vmem_limit_bytes=VMEM_LIMIT)


def _rms_mod(x, gain, scale, shift):
    ms = jnp.mean(x * x, axis=-1, keepdims=True)
    y = x * lax.rsqrt(ms + EPS) * gain
    return y * (1.0 + scale) + shift


def _ada_kernel(c_ref, w_ref, b_ref, o_ref):
    c = c_ref[...]
    s = (c * jax.nn.sigmoid(c)).astype(BF16)
    o_ref[...] = jnp.dot(s, w_ref[...].astype(BF16), preferred_element_type=F32) + b_ref[...]


def _ada_params(cvec, ada_w, ada_b):
    depth, d, n = ada_w.shape
    mp = cvec.shape[0]
    tn = 1024
    return pl.pallas_call(
        _ada_kernel,
        grid=(depth, n // tn),
        in_specs=[
            pl.BlockSpec((mp, d), lambda l, j: (0, 0)),
            pl.BlockSpec((None, d, tn), lambda l, j: (l, 0, j)),
            pl.BlockSpec((None, 1, tn), lambda l, j: (l, 0, j)),
        ],
        out_specs=pl.BlockSpec((None, mp, tn), lambda l, j: (l, 0, j)),
        out_shape=jax.ShapeDtypeStruct((depth, mp, n), F32),
        compiler_params=_cparams(("arbitrary", "arbitrary")),
        name="ada_params",
    )(cvec, ada_w, ada_b.reshape(depth, 1, n))


def _swap16(x):
    width = x.shape[-1]
    lane = lax.broadcasted_iota(jnp.int32, x.shape, 1)
    return jnp.where((lane % 32) < 16, pltpu.roll(x, width - 16, 1), pltpu.roll(x, 16, 1))


def _proj_even_kernel(xp_ref, xs_ref, mod_ref, g_ref, w_ref, cos_ref, sin_ref, o_ref, kv_ref, h_scr, *,
                      n_prompt_blocks):
    i = pl.program_id(0)

    def normed(x_ref):
        h_scr[...] = _rms_mod(x_ref[...], g_ref[...], mod_ref[1:2, :], mod_ref[0:1, :]).astype(BF16)

    pl.when(i < n_prompt_blocks)(lambda: normed(xp_ref))
    pl.when(i >= n_prompt_blocks)(lambda: normed(xs_ref))
    kv_tile = ATTN_WIDTH // TN_PROJ
    for j in range(w_ref.shape[1] // TN_PROJ):
        cols = slice(j * TN_PROJ, (j + 1) * TN_PROJ)
        acc = jnp.dot(h_scr[...], w_ref[:, cols], preferred_element_type=F32)
        if j <= kv_tile:
            tab = slice(0, TN_PROJ) if j < kv_tile else slice(TN_PROJ, 2 * TN_PROJ)
            o_ref[:, cols] = (acc * cos_ref[:, tab] + _swap16(acc) * sin_ref[:, tab]).astype(BF16)
            if j == kv_tile:
                kv_ref[...] = acc
        else:
            o_ref[:, cols] = jax.nn.gelu(acc).astype(BF16)


def _proj_even(x_p, x_s, mods, gain, w, cos_t, sin_t, blocks_per_seq):
    d = x_p.shape[1]
    n = w.shape[1]
    tm, tn = TM_PROJ, TN_PROJ
    npb = x_p.shape[0] // tm
    nsb = x_s.shape[0] // tm
    t = x_p.shape[0] + x_s.shape[0]

    def mod_idx(i):
        return (jnp.where(i < npb, 0, 1 + (i - npb) // blocks_per_seq), 0, 0)

    def tab_idx(i):
        return (jnp.where(i < npb, blocks_per_seq, (i - npb) % blocks_per_seq), 0)

    return pl.pallas_call(
        functools.partial(_proj_even_kernel, n_prompt_blocks=npb),
        grid=(npb + nsb,),
        in_specs=[
            pl.BlockSpec((tm, d), lambda i: (jnp.minimum(i, npb - 1), 0)),
            pl.BlockSpec((tm, d), lambda i: (jnp.maximum(i - npb, 0), 0)),
            pl.BlockSpec((None, 6, d), mod_idx),
            pl.BlockSpec((1, d), lambda i: (0, 0)),
            pl.BlockSpec((d, n), lambda i: (0, 0), pipeline_mode=pl.Buffered(1)),
            pl.BlockSpec((tm, 2 * tn), tab_idx),
            pl.BlockSpec((tm, 2 * tn), tab_idx),
        ],
        out_specs=[
            pl.BlockSpec((tm, n), lambda i: (i, 0)),
            pl.BlockSpec((tm, tn), lambda i: (i, 0)),
        ],
        out_shape=[jax.ShapeDtypeStruct((t, n), BF16), jax.ShapeDtypeStruct((t, tn), F32)],
        scratch_shapes=[pltpu.VMEM((tm, d), BF16)],
        compiler_params=_cparams(("arbitrary",)),
        name="proj_even",
    )(x_p, x_s, mods, gain, w, cos_t, sin_t)


def _rope_tables(dec_seq, tm):
    f32 = np.float32
    n = np.arange(dec_seq)
    row = (n // GRID_W).astype(f32)
    col = (n % GRID_W).astype(f32)
    nf = HEAD_DIM // 4
    inv_freq = np.power(f32(ROPE_BASE), -np.arange(nf, dtype=f32) / f32(nf)).astype(f32)
    ar = row[:, None] * inv_freq[None, :]
    ac = col[:, None] * inv_freq[None, :]
    cos_h = np.concatenate([np.cos(ar), np.cos(ar), np.cos(ac), np.cos(ac)], axis=-1).astype(f32)
    sin_h = np.concatenate([-np.sin(ar), np.sin(ar), -np.sin(ac), np.sin(ac)], axis=-1).astype(f32)
    scale = f32(HEAD_DIM ** -0.5)
    ones = np.ones((dec_seq, KV_WIDTH), f32)
    zeros = np.zeros((dec_seq, KV_WIDTH), f32)
    q_heads = TN_PROJ // HEAD_DIM
    cos_t = np.concatenate([np.tile(cos_h, (1, q_heads)) * scale, np.tile(cos_h, (1, N_KV_HEADS)), ones], axis=-1)
    sin_t = np.concatenate([np.tile(sin_h, (1, q_heads)) * scale, np.tile(sin_h, (1, N_KV_HEADS)), zeros], axis=-1)
    id_cos = np.concatenate([np.full((tm, TN_PROJ), scale, f32), np.ones((tm, TN_PROJ), f32)], axis=-1)
    id_sin = np.zeros((tm, 2 * TN_PROJ), f32)
    return (jnp.asarray(np.concatenate([cos_t, id_cos], axis=0), F32),
            jnp.asarray(np.concatenate([sin_t, id_sin], axis=0), F32))


def _sink_attend(q, keys, vals, masks, sink):
    scores = []
    for k, mask in zip(keys, masks):
        s = lax.dot_general(q, k, (((1,), (1,)), ((), ())), preferred_element_type=F32)
        scores.append(s if mask is None else jnp.where(mask, s, NEG_INF))
    m = sink
    for s in scores:
        m = jnp.maximum(m, jnp.max(s, axis=-1, keepdims=True))
    es = [jnp.exp(s - m) for s in scores]
    den = jnp.exp(sink - m)
    for e in es:
        den = den + jnp.sum(e, axis=-1, keepdims=True)
    inv = 1.0 / den
    out = None
    for e, v in zip(es, vals):
        o = jnp.dot((e * inv).astype(BF16), v, preferred_element_type=F32)
        out = o if out is None else out + o
    return out


def _grouped_heads_attend(sink_ref, q_ref, o_ref, kh, keys, vals, masks):
    rows = q_ref.shape[0]
    heads = [kh * Q_GROUPS + g for g in range(Q_GROUPS)]
    q = jnp.concatenate([q_ref[:, h * HEAD_DIM:(h + 1) * HEAD_DIM] for h in heads], axis=0)
    ridx = lax.broadcasted_iota(jnp.int32, (Q_GROUPS * rows, 1), 0)
    sink = jnp.full((Q_GROUPS * rows, 1), sink_ref[heads[-1]], F32)
    for g in range(Q_GROUPS - 2, -1, -1):
        sink = jnp.where(ridx < (g + 1) * rows, sink_ref[heads[g]], sink)
    o = _sink_attend(q, keys, vals, masks, sink)
    for g, h in enumerate(heads):
        o_ref[:, h * HEAD_DIM:(h + 1) * HEAD_DIM] = o[g * rows:(g + 1) * rows, :].astype(BF16)


def _ctx_attn_kernel(sink_ref, q_ref, kv_ref, o_ref):
    for kh in range(N_KV_HEADS):
        k = kv_ref[:, kh * HEAD_DIM:(kh + 1) * HEAD_DIM]
        v = kv_ref[:, KV_WIDTH + kh * HEAD_DIM:KV_WIDTH + (kh + 1) * HEAD_DIM]
        _grouped_heads_attend(sink_ref, q_ref, o_ref, kh, [k], [v], [None])


def _ctx_attention(proj, sink, batch, seq, kv_col_block):
    return pl.pallas_call(
        _ctx_attn_kernel,
        grid=(batch,),
        in_specs=[
            pl.BlockSpec(memory_space=pltpu.SMEM),
            pl.BlockSpec((seq, ATTN_WIDTH), lambda b: (b, 0)),
            pl.BlockSpec((seq, 2 * KV_WIDTH), lambda b: (b, kv_col_block)),
        ],
        out_specs=pl.BlockSpec((seq, ATTN_WIDTH), lambda b: (b, 0)),
        out_shape=jax.ShapeDtypeStruct((batch * seq, ATTN_WIDTH), BF16),
        compiler_params=_cparams(("arbitrary",)),
        name="ctx_attention",
    )(sink, proj, proj)


def _lat_attn_kernel(sink_ref, q_ref, kv_ref, ck_ref, cv_ref, o_ref, *, seq):
    i = pl.program_id(1)
    win = 3 * ATTN_BLK
    ws = pl.multiple_of(jnp.clip((i - 1) * ATTN_BLK, 0, seq - win), ATTN_BLK)
    shape = (Q_GROUPS * ATTN_BLK, win)
    qpos = i * ATTN_BLK + (lax.broadcasted_iota(jnp.int32, shape, 0) % ATTN_BLK)
    kpos = ws + lax.broadcasted_iota(jnp.int32, shape, 1)
    valid = jnp.abs(qpos - kpos) <= ATTN_BLK
    kvw = kv_ref[pl.ds(ws, win), :]
    for kh in range(N_KV_HEADS):
        k_loc = kvw[:, kh * HEAD_DIM:(kh + 1) * HEAD_DIM]
        v_loc = kvw[:, KV_WIDTH + kh * HEAD_DIM:KV_WIDTH + (kh + 1) * HEAD_DIM]
        k_ctx = ck_ref[:, kh * HEAD_DIM:(kh + 1) * HEAD_DIM]
        v_ctx = cv_ref[:, kh * HEAD_DIM:(kh + 1) * HEAD_DIM]
        _grouped_heads_attend(sink_ref, q_ref, o_ref, kh, [k_ctx, k_loc], [v_ctx, v_loc], [None, valid])


def _lat_attention(proj, sink, cache_k, cache_v, n_prompt, dec_batch, dec_seq, kv_col_block):
    qb = dec_seq // ATTN_BLK
    q_off = n_prompt // ATTN_BLK
    s_off = n_prompt // dec_seq
    past = cache_k.shape[1]
    return pl.pallas_call(
        functools.partial(_lat_attn_kernel, seq=dec_seq),
        grid=(dec_batch, qb),
        in_specs=[
            pl.BlockSpec(memory_space=pltpu.SMEM),
            pl.BlockSpec((ATTN_BLK, ATTN_WIDTH), lambda b, i: (q_off + b * qb + i, 0)),
            pl.BlockSpec((dec_seq, 2 * KV_WIDTH), lambda b, i: (s_off + b, kv_col_block)),
            pl.BlockSpec((None, past, KV_WIDTH), lambda b, i: (b, 0, 0)),
            pl.BlockSpec((None, past, KV_WIDTH), lambda b, i: (b, 0, 0)),
        ],
        out_specs=pl.BlockSpec((ATTN_BLK, ATTN_WIDTH), lambda b, i: (b * qb + i, 0)),
        out_shape=jax.ShapeDtypeStruct((dec_batch * dec_seq, ATTN_WIDTH), BF16),
        compiler_params=_cparams(("arbitrary", "arbitrary")),
        name="lat_attention",
    )(sink, proj, proj, cache_k, cache_v)


def _sgu_kernel(ul_ref, uh_ref, vl_ref, vh_ref, g_ref, w_ref, bt_ref, z_ref):
    half = SGU_WIDTH // 2
    per_half = SGU_GROUPS // 2
    u_refs = (ul_ref, uh_ref)
    for c in range(TM_SGU // CHUNK):
        rows = slice(c * CHUNK, (c + 1) * CHUNK)
        vl = vl_ref[rows, :].astype(F32)
        vh = vh_ref[rows, :].astype(F32)
        ssq = jnp.sum(vl * vl, axis=-1, keepdims=True) + jnp.sum(vh * vh, axis=-1, keepdims=True)
        r = lax.rsqrt(ssq * (1.0 / SGU_WIDTH) + EPS)
        vn = ((vl * r * g_ref[:, :half]).astype(BF16), (vh * r * g_ref[:, half:]).astype(BF16))
        for g in range(SGU_GROUPS):
            src = g // per_half
            cols = slice((g % per_half) * SGU_GROUP_DIM, (g % per_half + 1) * SGU_GROUP_DIM)
            mixed = jnp.dot(w_ref[g], vn[src][:, cols], preferred_element_type=F32) + bt_ref[:, g:g + 1]
            z_ref[rows, g * SGU_GROUP_DIM:(g + 1) * SGU_GROUP_DIM] = (
                u_refs[src][rows, cols].astype(F32) * mixed).astype(BF16)


def _sgu(proj, norm_g, w_s, b_t):
    t = proj.shape[0]
    half = SGU_WIDTH // 2
    u0 = (ATTN_WIDTH + 2 * KV_WIDTH) // half
    blocks = [pl.BlockSpec((TM_SGU, half), functools.partial(lambda i, c: (i, c), c=u0 + k)) for k in range(4)]
    return pl.pallas_call(
        _sgu_kernel,
        grid=(t // TM_SGU,),
        in_specs=blocks + [
            pl.BlockSpec((1, SGU_WIDTH), lambda i: (0, 0)),
            pl.BlockSpec((SGU_GROUPS, CHUNK, CHUNK), lambda i: (0, 0, 0)),
            pl.BlockSpec((CHUNK, SGU_GROUPS), lambda i: (0, 0)),
        ],
        out_specs=pl.BlockSpec((TM_SGU, SGU_WIDTH), lambda i: (i, 0)),
        out_shape=jax.ShapeDtypeStruct((t, SGU_WIDTH), BF16),
        compiler_params=_cparams(("arbitrary",)),
        name="sgu",
    )(proj, proj, proj, proj, norm_g, w_s, b_t)


def _residual_router(x, acc, mod_ref, g2_ref, wrt_ref, brt_ref, x1_ref, h2_ref, lg_ref):
    x1 = x + mod_ref[2:3, :] * acc
    x1_ref[...] = x1
    h2 = _rms_mod(x1, g2_ref[...], mod_ref[4:5, :], mod_ref[3:4, :])
    h2_ref[...] = h2
    lg = lax.dot_general(wrt_ref[...], h2.astype(BF16), (((1,), (1,)), ((), ())), preferred_element_type=F32)
    lg_ref[...] = lg + brt_ref[...]


def _out_even_kernel(xp_ref, xs_ref, ap_ref, as_ref, z_ref, w_ref, mod_ref, g2_ref, wrt_ref, brt_ref,
                     x1_ref, h2_ref, lg_ref, *, n_prompt_blocks):
    is_prompt = pl.program_id(0) < n_prompt_blocks
    a = jnp.where(is_prompt, ap_ref[...].astype(F32), as_ref[...].astype(F32)).astype(BF16)
    acc = jnp.dot(a, w_ref[0:ATTN_WIDTH, :], preferred_element_type=F32)
    acc = acc + jnp.dot(z_ref[...], w_ref[ATTN_WIDTH:, :], preferred_element_type=F32)
    x = jnp.where(is_prompt, xp_ref[...], xs_ref[...])
    _residual_router(x, acc, mod_ref, g2_ref, wrt_ref, brt_ref, x1_ref, h2_ref, lg_ref)


def _out_odd_kernel(x_ref, hc_ref, w_ref, mod_ref, g2_ref, wrt_ref, brt_ref, x1_ref, h2_ref, lg_ref):
    acc = jnp.dot(hc_ref[...], w_ref[...], preferred_element_type=F32)
    _residual_router(x_ref[...], acc, mod_ref, g2_ref, wrt_ref, brt_ref, x1_ref, h2_ref, lg_ref)


def _out_proj(xs, lhs, w, mods, gain2, wrt, brt, n_prompt, dec_seq):
    d = xs[0].shape[1]
    t = sum(x.shape[0] for x in xs)
    tm = TM_OUT
    npb = n_prompt // tm
    bps = dec_seq // tm
    nsb = t // tm - npb
    if len(xs) == 2:
        x_specs = [pl.BlockSpec((tm, d), lambda i: (jnp.minimum(i, npb - 1), 0)),
                   pl.BlockSpec((tm, d), lambda i: (jnp.maximum(i - npb, 0), 0))]
    else:
        x_specs = [pl.BlockSpec((tm, d), lambda i: (i, 0))]

    def mod_idx(i):
        return (jnp.where(i < npb, 0, 1 + (i - npb) // bps), 0, 0)

    if len(lhs) == 3:
        a_p, a_s, z = lhs
        body = functools.partial(_out_even_kernel, n_prompt_blocks=npb)
        lhs_specs = [
            pl.BlockSpec((tm, ATTN_WIDTH), lambda i: (jnp.minimum(i, npb - 1), 0)),
            pl.BlockSpec((tm, ATTN_WIDTH), lambda i: (jnp.clip(i - npb, 0, nsb - 1), 0)),
            pl.BlockSpec((tm, SGU_WIDTH), lambda i: (i, 0)),
        ]
    else:
        body = _out_odd_kernel
        lhs_specs = [pl.BlockSpec((tm, d), lambda i: (i, 0))]
    return pl.pallas_call(
        body,
        grid=(t // tm,),
        in_specs=x_specs + lhs_specs + [
            pl.BlockSpec((w.shape[0], d), lambda i: (0, 0)),
            pl.BlockSpec((None, 6, d), mod_idx),
            pl.BlockSpec((1, d), lambda i: (0, 0)),
            pl.BlockSpec((ROUTER_LANES, d), lambda i: (0, 0)),
            pl.BlockSpec((ROUTER_LANES, 1), lambda i: (0, 0)),
        ],
        out_specs=[
            pl.BlockSpec((tm, d), lambda i: (i, 0)),
            pl.BlockSpec((tm, d), lambda i: (i, 0)),
            pl.BlockSpec((ROUTER_LANES, tm), lambda i: (0, i)),
        ],
        out_shape=[
            jax.ShapeDtypeStruct((t, d), F32),
            jax.ShapeDtypeStruct((t, d), F32),
            jax.ShapeDtypeStruct((ROUTER_LANES, t), F32),
        ],
        compiler_params=_cparams(("arbitrary",)),
        name="out_proj",
    )(*xs, *lhs, w, mods, gain2, wrt, brt)


def _first_argmax4(v0, v1, v2, v3):
    m = jnp.maximum(jnp.maximum(v0, v1), jnp.maximum(v2, v3))
    idx = jnp.where(v0 == m, 0.0, jnp.where(v1 == m, 1.0, jnp.where(v2 == m, 2.0, 3.0)))
    return m, idx


def _route_kernel(lg_ref, pos_ref, ea_ref, eb_ref, nu_ref, ends_ref, oh_scr, rank_scr, *, n_tokens):
    lg = lg_ref[...]
    rows = [lg[r:r + 1, :] for r in range(N_GROUPS_MOE + N_EXPERTS)]
    _, gidx = _first_argmax4(*rows[:N_GROUPS_MOE])
    e = []
    for k in range(EXPERTS_PER_GROUP):
        cand = [rows[N_GROUPS_MOE + g * EXPERTS_PER_GROUP + k] for g in range(N_GROUPS_MOE)]
        e.append(jnp.where(gidx == 0.0, cand[0], jnp.where(gidx == 1.0, cand[1], jnp.where(gidx == 2.0, cand[2], cand[3]))))
    _, l1 = _first_argmax4(*e)
    e2 = [jnp.where(l1 == float(k), -jnp.inf, e[k]) for k in range(EXPERTS_PER_GROUP)]
    _, l2 = _first_argmax4(*e2)
    lo = jnp.minimum(l1, l2)
    hi = jnp.maximum(l1, l2)
    pair = jnp.where(lo == 0.0, hi - 1.0, jnp.where(lo == 1.0, hi + 1.0, 5.0))
    cls = gidx * float(PAIRS_PER_GROUP) + pair

    crow = lax.broadcasted_iota(jnp.int32, (CLASS_ROWS, n_tokens), 0).astype(F32)
    oh_scr[...] = jnp.where(crow == cls, 1.0, 0.0).astype(BF16)

    jj = lax.broadcasted_iota(jnp.int32, (CUM_CHUNK, CUM_CHUNK), 0)
    ii = lax.broadcasted_iota(jnp.int32, (CUM_CHUNK, CUM_CHUNK), 1)
    tri = jnp.where(jj < ii, 1.0, 0.0).astype(BF16)

    def chunk(c, carry):
        off = pl.multiple_of(c * CUM_CHUNK, CUM_CHUNK)
        oh = oh_scr[:, pl.ds(off, CUM_CHUNK)]
        ohf = oh.astype(F32)
        cum = jnp.dot(oh, tri, preferred_element_type=F32) + carry
        rank_scr[:, pl.ds(off, CUM_CHUNK)] = jnp.sum(cum * ohf, axis=0, keepdims=True)
        return carry + jnp.sum(ohf, axis=1, keepdims=True)

    counts = lax.fori_loop(0, n_tokens // CUM_CHUNK, chunk, jnp.zeros((CLASS_ROWS, 1), F32))
    padded = jnp.floor((counts + float(MOE_BLK - 1)) * (1.0 / MOE_BLK)) * float(MOE_BLK)
    run = jnp.zeros((1, 1), F32)
    starts_rows = []
    for c in range(CLASS_ROWS):
        starts_rows.append(run)
        run = run + padded[c:c + 1, :]
    starts = jnp.concatenate(starts_rows, axis=0)
    ends = starts + padded
    total = run

    ohf = oh_scr[...].astype(F32)
    pos = rank_scr[...] + jnp.sum(ohf * starts, axis=0, keepdims=True)
    pos_ref[...] = pos.astype(jnp.int32)

    bstart = lax.broadcasted_iota(jnp.int32, (CLASS_ROWS, MAX_MOE_BLOCKS), 1).astype(F32) * float(MOE_BLK)
    brow = lax.broadcasted_iota(jnp.int32, (CLASS_ROWS, MAX_MOE_BLOCKS), 0)
    done = jnp.where((ends <= bstart) & (brow < N_CLASSES), 1.0, 0.0)
    bcls = jnp.minimum(jnp.sum(done, axis=0, keepdims=True), float(N_CLASSES - 1))
    grp = (jnp.where(bcls >= 6.0, 1.0, 0.0) + jnp.where(bcls >= 12.0, 1.0, 0.0) + jnp.where(bcls >= 18.0, 1.0, 0.0))
    bp = bcls - grp * float(PAIRS_PER_GROUP)
    ge3 = jnp.where(bp >= 3.0, 1.0, 0.0)
    ge5 = jnp.where(bp >= 5.0, 1.0, 0.0)
    blo = ge3 + ge5
    bhi = bp + 1.0 - 2.0 * ge3 - ge5
    ea_ref[...] = (grp * float(EXPERTS_PER_GROUP) + blo).astype(jnp.int32)
    eb_ref[...] = (grp * float(EXPERTS_PER_GROUP) + bhi).astype(jnp.int32)
    nu_ref[...] = jnp.broadcast_to(total * (1.0 / MOE_BLK), (1, MAX_MOE_BLOCKS)).astype(jnp.int32)
    lane = lax.broadcasted_iota(jnp.int32, (1, MAX_MOE_BLOCKS), 1)
    ends_row = jnp.zeros((1, MAX_MOE_BLOCKS), F32)
    for c in range(N_CLASSES):
        ends_row = jnp.where(lane == c, starts_rows[c] + padded[c:c + 1, :], ends_row)
    ends_ref[...] = ends_row.astype(jnp.int32)


def _route(logits_t):
    t = logits_t.shape[1]
    rows = N_CLASSES
    i32 = jnp.int32
    return pl.pallas_call(
        functools.partial(_route_kernel, n_tokens=t),
        grid=(1,),
        in_specs=[pl.BlockSpec((rows, t), lambda i: (0, 0))],
        out_specs=[
            pl.BlockSpec((1, t), lambda i: (0, 0)),
        ] + [pl.BlockSpec((1, MAX_MOE_BLOCKS), lambda i: (0, 0))] * 4,
        out_shape=[jax.ShapeDtypeStruct((1, t), i32)] + [jax.ShapeDtypeStruct((1, MAX_MOE_BLOCKS), i32)] * 4,
        scratch_shapes=[pltpu.VMEM((CLASS_ROWS, t), BF16), pltpu.VMEM((1, t), F32)],
        compiler_params=_cparams(("arbitrary",)),
        name="route",
    )(logits_t)


def _row_copy(src_ref, src_row, dst_ref, dst_row, sem):
    return pltpu.make_async_copy(src_ref.at[pl.ds(src_row, 1), :], dst_ref.at[pl.ds(dst_row, 1), :], sem)


def _dispatch_kernel(pos_ref, ends_ref, h_ref, o_ref, zbuf, sem, zsem):
    @pl.when(pl.program_id(0) == 0)
    def _():
        zbuf[...] = jnp.zeros_like(zbuf)
        total = ends_ref[N_CLASSES - 1]
        n_free = (o_ref.shape[0] - total) // MOE_BLK

        def zero_block(first_row, phase):
            first = pl.multiple_of(first_row, MOE_BLK)
            getattr(pltpu.make_async_copy(zbuf, o_ref.at[pl.ds(first, MOE_BLK), :], zsem), phase)()

        for phase in ("start", "wait"):
            prev = 0
            for c in range(N_CLASSES):
                end = ends_ref[c]
                pl.when(end > prev)(functools.partial(zero_block, end - MOE_BLK, phase))
                prev = end

            def free_block(k, carry, phase=phase):
                zero_block(total + k * MOE_BLK, phase)
                return carry

            lax.fori_loop(0, n_free, free_block, 0)

    base = pl.program_id(0) * TM_OUT

    def issue(r, carry):
        _row_copy(h_ref, r, o_ref, pos_ref[base + r], sem).start()
        return carry

    lax.fori_loop(0, TM_OUT, issue, 0)

    def drain(r, carry):
        _row_copy(h_ref, r, o_ref, pos_ref[base + r], sem).wait()
        return carry

    lax.fori_loop(0, TM_OUT, drain, 0)


def _dispatch(pos, ends, h2, n_rows):
    t, d = h2.shape
    return pl.pallas_call(
        _dispatch_kernel,
        grid_spec=pltpu.PrefetchScalarGridSpec(
            num_scalar_prefetch=2,
            grid=(t // TM_OUT,),
            in_specs=[pl.BlockSpec((TM_OUT, d), lambda i, pos, ends: (i, 0))],
            out_specs=pl.BlockSpec(memory_space=pl.ANY),
            scratch_shapes=[pltpu.VMEM((MOE_BLK, d), F32), pltpu.SemaphoreType.DMA(()), pltpu.SemaphoreType.DMA(())],
        ),
        out_shape=jax.ShapeDtypeStruct((n_rows, d), F32),
        compiler_params=_cparams(("arbitrary",)),
        name="moe_dispatch",
    )(pos, ends, h2)


def _expert_ffn(x, wg_ref, wu_ref, wd_ref):
    gate = jnp.dot(x, wg_ref[...], preferred_element_type=F32)
    up = jnp.dot(x, wu_ref[...], preferred_element_type=F32)
    hidden = (gate * jax.nn.sigmoid(gate) * up).astype(BF16)
    return jnp.dot(hidden, wd_ref[...], preferred_element_type=F32)


def _moe_kernel(ea_ref, eb_ref, nu_ref, x_ref, wr_ref, br_ref, wga, wua, wda, wgb, wub, wdb, y_ref):
    b = pl.program_id(0)

    @pl.when(b < nu_ref[0])
    def _():
        x = x_ref[...].astype(BF16)
        lg = jnp.dot(x, wr_ref[...], preferred_element_type=F32) + br_ref[...]
        lane = lax.broadcasted_iota(jnp.int32, lg.shape, 1)
        ea = ea_ref[b]
        eb = eb_ref[b]
        grp = ea // EXPERTS_PER_GROUP
        is_grp = lane < N_GROUPS_MOE
        gm = jnp.max(jnp.where(is_grp, lg, -jnp.inf), axis=-1, keepdims=True)
        ge = jnp.where(is_grp, jnp.exp(lg - gm), 0.0)
        p_g = jnp.sum(jnp.where(lane == grp, ge, 0.0), axis=-1, keepdims=True) / jnp.sum(ge, axis=-1, keepdims=True)
        la = jnp.sum(jnp.where(lane == N_GROUPS_MOE + ea, lg, 0.0), axis=-1, keepdims=True)
        lb = jnp.sum(jnp.where(lane == N_GROUPS_MOE + eb, lg, 0.0), axis=-1, keepdims=True)
        mm = jnp.maximum(la, lb)
        xa = jnp.exp(la - mm)
        xb = jnp.exp(lb - mm)
        ga = xa / (xa + xb) * p_g
        gb = xb / (xa + xb) * p_g
        ya = _expert_ffn(x, wga, wua, wda)
        yb = _expert_ffn(x, wgb, wub, wdb)
        y_ref[...] = ya * ga + yb * gb

    @pl.when(b >= nu_ref[0])
    def _():
        y_ref[...] = jnp.zeros_like(y_ref)


def _moe_experts(ea, eb, nused, hs, wr, br, w_gate, w_up, w_down, layer, n_blocks):
    d = hs.shape[1]
    de = w_gate.shape[3]

    def wa_idx(b, ea, eb, nu):
        return (layer, ea[b], 0, 0)

    def wb_idx(b, ea, eb, nu):
        return (layer, eb[b], 0, 0)

    return pl.pallas_call(
        _moe_kernel,
        grid_spec=pltpu.PrefetchScalarGridSpec(
            num_scalar_prefetch=3,
            grid=(n_blocks,),
            in_specs=[
                pl.BlockSpec((MOE_BLK, d), lambda b, ea, eb, nu: (jnp.minimum(b, nu[0] - 1), 0)),
                pl.BlockSpec((d, ROUTER_LANES), lambda b, ea, eb, nu: (0, 0)),
                pl.BlockSpec((1, ROUTER_LANES), lambda b, ea, eb, nu: (0, 0)),
                pl.BlockSpec((None, None, d, de), wa_idx),
                pl.BlockSpec((None, None, d, de), wa_idx),
                pl.BlockSpec((None, None, de, d), wa_idx),
                pl.BlockSpec((None, None, d, de), wb_idx),
                pl.BlockSpec((None, None, d, de), wb_idx),
                pl.BlockSpec((None, None, de, d), wb_idx),
            ],
            out_specs=pl.BlockSpec((MOE_BLK, d), lambda b, ea, eb, nu: (b, 0)),
        ),
        out_shape=jax.ShapeDtypeStruct((n_blocks * MOE_BLK, d), F32),
        compiler_params=_cparams(("arbitrary",)),
        name="moe_experts",
    )(ea, eb, nused, hs, wr, br, w_gate, w_up, w_down, w_gate, w_up, w_down)


def _combine_kernel(pos_ref, ys_ref, x1_ref, mod_ref, gf_ref, o_ref, ybuf, sem, *, block_off, final):
    base = (pl.program_id(0) + block_off) * TM_OUT

    def issue(r, carry):
        _row_copy(ys_ref, pos_ref[base + r], ybuf, r, sem).start()
        return carry

    lax.fori_loop(0, TM_OUT, issue, 0)

    def drain(r, carry):
        _row_copy(ys_ref, pos_ref[base + r], ybuf, r, sem).wait()
        return carry

    lax.fori_loop(0, TM_OUT, drain, 0)
    x2 = x1_ref[...] + mod_ref[5:6, :] * ybuf[...]
    if final:
        ms = jnp.mean(x2 * x2, axis=-1, keepdims=True)
        x2 = x2 * lax.rsqrt(ms + EPS) * gf_ref[...]
    o_ref[...] = x2


def _combine(pos, ys, x1, mods, gain_f, n_prompt, dec_seq, row_off, n_rows, final):
    d = x1.shape[1]
    tm = TM_OUT
    npb = n_prompt // tm
    bps = dec_seq // tm
    boff = row_off // tm

    def mod_idx(i, pos):
        blk = i + boff
        return (jnp.where(blk < npb, 0, 1 + (blk - npb) // bps), 0, 0)

    return pl.pallas_call(
        functools.partial(_combine_kernel, block_off=boff, final=final),
        grid_spec=pltpu.PrefetchScalarGridSpec(
            num_scalar_prefetch=1,
            grid=(n_rows // tm,),
            in_specs=[
                pl.BlockSpec(memory_space=pl.ANY),
                pl.BlockSpec((tm, d), lambda i, pos: (i + boff, 0)),
                pl.BlockSpec((None, 6, d), mod_idx),
                pl.BlockSpec((1, d), lambda i, pos: (0, 0)),
            ],
            out_specs=pl.BlockSpec((tm, d), lambda i, pos: (i, 0)),
            scratch_shapes=[pltpu.VMEM((tm, d), F32), pltpu.SemaphoreType.DMA(())],
        ),
        out_shape=jax.ShapeDtypeStruct((n_rows, d), F32),
        compiler_params=_cparams(("arbitrary",)),
        name="moe_combine",
    )(pos, ys, x1, mods, gain_f)


def _proj_glu_kernel(x_ref, mod_ref, g_ref, w_ref, o_ref, h_scr):
    h_scr[...] = _rms_mod(x_ref[...], g_ref[...], mod_ref[1:2, :], mod_ref[0:1, :]).astype(BF16)
    width = w_ref.shape[1] // 2
    for j in range(width // TN_PROJ):
        cols = slice(j * TN_PROJ, (j + 1) * TN_PROJ)
        gcols = slice(width + j * TN_PROJ, width + (j + 1) * TN_PROJ)
        val = jnp.dot(h_scr[...], w_ref[:, cols], preferred_element_type=F32)
        gate = jnp.dot(h_scr[...], w_ref[:, gcols], preferred_element_type=F32)
        o_ref[:, cols] = (val * jax.nn.sigmoid(gate)).astype(BF16)


def _proj_glu(x, mods, gain, w, n_prompt, dec_seq):
    t, d = x.shape
    n = w.shape[1]
    tm = TM_PROJ
    npb = n_prompt // tm
    bps = dec_seq // tm

    def mod_idx(i):
        return (jnp.where(i < npb, 0, 1 + (i - npb) // bps), 0, 0)

    return pl.pallas_call(
        _proj_glu_kernel,
        grid=(t // tm,),
        in_specs=[
            pl.BlockSpec((tm, d), lambda i: (i, 0)),
            pl.BlockSpec((None, 6, d), mod_idx),
            pl.BlockSpec((1, d), lambda i: (0, 0)),
            pl.BlockSpec((d, n), lambda i: (0, 0), pipeline_mode=pl.Buffered(1)),
        ],
        out_specs=pl.BlockSpec((tm, n // 2), lambda i: (i, 0)),
        out_shape=jax.ShapeDtypeStruct((t, n // 2), BF16),
        scratch_shapes=[pltpu.VMEM((tm, d), BF16)],
        compiler_params=_cparams(("arbitrary",)),
        name="proj_glu",
    )(x, mods, gain, w)


CONV_ROWS = 64
CONV_COLS = 256
F32_SUBLANES = 8


def _conv_kernel(x_ref, prev_ref, next_ref, dw_ref, b_ref, g_ref, o_ref, xpad, shifted, ybuf, *,
                 n_prompt_blocks, blocks_per_seq):
    i = pl.program_id(0)
    in_sample = i >= n_prompt_blocks
    j = i - n_prompt_blocks
    is_start = jnp.logical_or(jnp.logical_not(in_sample), j % blocks_per_seq == 0)
    is_end = jnp.logical_or(jnp.logical_not(in_sample), j % blocks_per_seq == blocks_per_seq - 1)
    tm = TM_OUT
    xpad[0:CONV_HALO, :] = jnp.where(is_start, 0.0, prev_ref[...].astype(F32))
    xpad[CONV_HALO:CONV_HALO + tm, :] = x_ref[...].astype(F32)
    xpad[CONV_HALO + tm:, :] = jnp.where(is_end, 0.0, next_ref[...].astype(F32))
    first = CONV_HALO - CONV_K // 2
    span = shifted.shape[1]

    def col_chunk(c, ssq):
        c0 = pl.multiple_of(c * CONV_COLS, CONV_COLS)
        cols = pl.ds(c0, CONV_COLS)
        w = dw_ref[:, cols]
        bias = b_ref[:, cols]
        for s in range(1, F32_SUBLANES):
            shifted[s - 1] = xpad[s:s + span, cols]
        parts = []
        for rc in range(tm // CONV_ROWS):
            r0 = rc * CONV_ROWS
            acc = jnp.zeros((CONV_ROWS, CONV_COLS), F32)
            for k in range(CONV_K):
                whole, s = divmod(first + k, F32_SUBLANES)
                base = whole * F32_SUBLANES + r0
                if s == 0:
                    src = xpad[base:base + CONV_ROWS, cols]
                else:
                    src = shifted[s - 1, base:base + CONV_ROWS, :]
                acc = acc + w[k:k + 1, :] * src
            acc = acc + bias
            ybuf[r0:r0 + CONV_ROWS, cols] = acc
            parts.append(jnp.sum(acc * acc, axis=-1, keepdims=True))
        return ssq + jnp.concatenate(parts, axis=0)

    d = x_ref.shape[1]
    ssq = lax.fori_loop(0, d // CONV_COLS, col_chunk, jnp.zeros((tm, 1), F32))
    y = ybuf[...] * lax.rsqrt(ssq * (1.0 / d) + EPS) * g_ref[...]
    o_ref[...] = (y * jax.nn.sigmoid(y)).astype(BF16)


def _conv_mix(xg, dw, dw_b, norm_g, n_prompt, seq, dec_seq):
    t, d = xg.shape
    tm = TM_OUT
    assert seq == tm, "context sequences must be exactly one conv row block"
    npb = n_prompt // tm
    bps = dec_seq // tm
    hb = tm // CONV_HALO
    last = t // CONV_HALO - 1
    return pl.pallas_call(
        functools.partial(_conv_kernel, n_prompt_blocks=npb, blocks_per_seq=bps),
        grid=(t // tm,),
        in_specs=[
            pl.BlockSpec((tm, d), lambda i: (i, 0)),
            pl.BlockSpec((CONV_HALO, d), lambda i: (jnp.maximum(i * hb - 1, 0), 0)),
            pl.BlockSpec((CONV_HALO, d), lambda i: (jnp.minimum((i + 1) * hb, last), 0)),
            pl.BlockSpec((dw.shape[0], d), lambda i: (0, 0)),
            pl.BlockSpec((1, d), lambda i: (0, 0)),
            pl.BlockSpec((1, d), lambda i: (0, 0)),
        ],
        out_specs=pl.BlockSpec((tm, d), lambda i: (i, 0)),
        out_shape=jax.ShapeDtypeStruct((t, d), BF16),
        scratch_shapes=[
            pltpu.VMEM((tm + 2 * CONV_HALO, d), F32),
            pltpu.VMEM((F32_SUBLANES - 1, tm + 2 * CONV_HALO - F32_SUBLANES, CONV_COLS), F32),
            pltpu.VMEM((tm, d), F32),
        ],
        compiler_params=_cparams(("arbitrary",)),
        name="conv_mix",
    )(xg, xg, xg, dw, dw_b, norm_g)


def _router_operands(w_rg, b_rg, w_re, b_re):
    d = w_rg.shape[0]
    used = N_GROUPS_MOE + N_EXPERTS
    w = jnp.concatenate([w_rg, w_re, jnp.zeros((d, ROUTER_LANES - used), F32)], axis=1)
    b = jnp.concatenate([b_rg, b_re, jnp.zeros((ROUTER_LANES - used,), F32)])
    return w.astype(BF16), b.reshape(1, ROUTER_LANES), w.T.astype(BF16), b.reshape(ROUTER_LANES, 1)


def _moe_rows(h2, logits_t, wr, br, w_gate, w_up, w_down, layer):
    t = h2.shape[0]
    n_blocks = -(-(t + N_CLASSES * (MOE_BLK - 1)) // MOE_BLK)
    assert n_blocks <= MAX_MOE_BLOCKS
    pos, ea, eb, nused, ends = _route(logits_t)
    pos = pos.reshape(t)
    hs = _dispatch(pos, ends.reshape(-1), h2, n_blocks * MOE_BLK)
    ys = _moe_experts(ea.reshape(-1), eb.reshape(-1), nused.reshape(-1)[:1], hs, wr, br, w_gate, w_up, w_down,
                      layer, n_blocks)
    return pos, ys


def kernel(x_prompt, x_sample, cache_k, cache_v, c, c_ctx, w_in_even, attn_sink, sgu_norm, sgu_w, sgu_b,
           w_out_even, conv_w_in, conv_dw, conv_dw_b, conv_norm, conv_w_out, ada_w, ada_b, norm_mix, norm_ffn,
           router_group_w, router_group_b, router_expert_w, router_expert_b, expert_w_gate, expert_w_up,
           expert_w_down, final_norm):
    batch, seq, d = x_prompt.shape
    dec_batch, dec_seq, _ = x_sample.shape
    depth = ada_w.shape[0]
    n_prompt = batch * seq
    n_sample = dec_batch * dec_seq
    assert d == D_MODEL and depth == 2 and n_prompt % dec_seq == 0 and dec_seq % TM_PROJ == 0
    x_p = x_prompt.reshape(n_prompt, d)
    x_s = x_sample.reshape(n_sample, d)

    n_mod = -(-(1 + dec_batch) // 8) * 8
    cvec = jnp.concatenate([c_ctx[None, :], c, jnp.zeros((n_mod - 1 - dec_batch, d), F32)], axis=0)
    mods = _ada_params(cvec, ada_w, ada_b).reshape(depth, n_mod, 6, d)
    wg, wu, wd = expert_w_gate.astype(BF16), expert_w_up.astype(BF16), expert_w_down.astype(BF16)

    cos_t, sin_t = _rope_tables(dec_seq, TM_PROJ)
    proj, kv32 = _proj_even(x_p, x_s, mods[0], norm_mix[0][None, :], w_in_even[0].astype(BF16), cos_t, sin_t,
                            dec_seq // TM_PROJ)
    kv_col_block = ATTN_WIDTH // (2 * KV_WIDTH)
    sink = attn_sink[0]
    a_p = _ctx_attention(proj, sink, batch, seq, kv_col_block)
    past = cache_k.shape[2]
    ck = cache_k[:, 0].reshape(dec_batch, past, KV_WIDTH).astype(BF16)
    cv = cache_v[:, 0].reshape(dec_batch, past, KV_WIDTH).astype(BF16)
    a_s = _lat_attention(proj, sink, ck, cv, n_prompt, dec_batch, dec_seq, kv_col_block)
    z = _sgu(proj, sgu_norm[0][None, :], sgu_w[0].astype(BF16), sgu_b[0].T)

    wr0, br0, wrt0, brt0 = _router_operands(router_group_w[0], router_group_b[0], router_expert_w[0], router_expert_b[0])
    x1, h2, lg = _out_proj((x_p, x_s), (a_p, a_s, z), w_out_even[0].astype(BF16), mods[0], norm_ffn[0][None, :],
                           wrt0, brt0, n_prompt, dec_seq)
    pos, ys = _moe_rows(h2, lg, wr0, br0, wg, wu, wd, 0)
    gain_f = final_norm[None, :]
    x = _combine(pos, ys, x1, mods[0], gain_f, n_prompt, dec_seq, 0, n_prompt + n_sample, final=False)

    xg = _proj_glu(x, mods[1], norm_mix[1][None, :], conv_w_in[0].astype(BF16), n_prompt, dec_seq)
    dw = jnp.concatenate([conv_dw[0], jnp.zeros((1, d), F32)], axis=0)
    hc = _conv_mix(xg, dw, conv_dw_b[0][None, :], conv_norm[0][None, :], n_prompt, seq, dec_seq)
    wr1, br1, wrt1, brt1 = _router_operands(router_group_w[1], router_group_b[1], router_expert_w[1], router_expert_b[1])
    x1, h2, lg = _out_proj((x,), (hc,), conv_w_out[0].astype(BF16), mods[1], norm_ffn[1][None, :],
                           wrt1, brt1, n_prompt, dec_seq)
    pos, ys = _moe_rows(h2, lg, wr1, br1, wg, wu, wd, 1)
    y_prompt = _combine(pos, ys, x1, mods[1], gain_f, n_prompt, dec_seq, 0, n_prompt, final=True)
    y_sample = _combine(pos, ys, x1, mods[1], gain_f, n_prompt, dec_seq, n_prompt, n_sample, final=True)

    new_k = kv32[:n_prompt, :KV_WIDTH].reshape(batch, 1, seq, N_KV_HEADS, HEAD_DIM)
    new_v = kv32[:n_prompt, KV_WIDTH:].reshape(batch, 1, seq, N_KV_HEADS, HEAD_DIM)
    return (y_prompt.reshape(batch, seq, d), y_sample.reshape(dec_batch, dec_seq, d), new_k, new_v)
```

```python
import functools

import jax
import jax.numpy as jnp
import numpy as np
from jax import lax
from jax.experimental import pallas as pl
from jax.experimental.pallas import tpu as pltpu

F32 = jnp.float32
BF16 = jnp.bfloat16

D_MODEL = 2048
GRID_W = 64
ATTN_BLK = 128
N_HEADS = 16
N_KV_HEADS = 4
Q_GROUPS = N_HEADS // N_KV_HEADS
HEAD_DIM = 64
ATTN_WIDTH = N_HEADS * HEAD_DIM
KV_WIDTH = N_KV_HEADS * HEAD_DIM
ROPE_BASE = 10000.0
SGU_GROUPS = 4
SGU_WIDTH = D_MODEL // 2
SGU_GROUP_DIM = SGU_WIDTH // SGU_GROUPS
CHUNK = 128
IN_WIDTH_EVEN = ATTN_WIDTH + 2 * KV_WIDTH + 2 * SGU_WIDTH
CONV_K = 31
CONV_HALO = 16
N_GROUPS_MOE = 4
EXPERTS_PER_GROUP = 4
N_EXPERTS = N_GROUPS_MOE * EXPERTS_PER_GROUP
PAIRS_PER_GROUP = 6
N_CLASSES = N_GROUPS_MOE * PAIRS_PER_GROUP
D_EXPERT = 512
MOE_BLK = 256
EPS = 1e-6
NEG_INF = -1e30

LANES = 128
ROUTER_LANES = 128
CLASS_ROWS = 32
MAX_MOE_BLOCKS = 128
TM_PROJ = 512
TN_PROJ = 512
TM_OUT = 256
TM_SGU = 512
CUM_CHUNK = 256
VMEM_LIMIT = 56 * 1024 * 1024


def _cparams(sem):
    return pltpu.CompilerParams(dimension_semantics=sem, vmem_limit_bytes=VMEM_LIMIT)


def _rms_mod(x, gain, scale, shift):
    ms = jnp.mean(x * x, axis=-1, keepdims=True)
    y = x * lax.rsqrt(ms + EPS) * gain
    return y * (1.0 + scale) + shift


def _ada_kernel(c_ref, w_ref, b_ref, o_ref):
    c = c_ref[...]
    s = (c * jax.nn.sigmoid(c)).astype(BF16)
    o_ref[...] = jnp.dot(s, w_ref[...].astype(BF16), preferred_element_type=F32) + b_ref[...]


def _ada_params(cvec, ada_w, ada_b):
    depth, d, n = ada_w.shape
    mp = cvec.shape[0]
    tn = 1024
    return pl.pallas_call(
        _ada_kernel,
        grid=(depth, n // tn),
        in_specs=[
            pl.BlockSpec((mp, d), lambda l, j: (0, 0)),
            pl.BlockSpec((None, d, tn), lambda l, j: (l, 0, j)),
            pl.BlockSpec((None, 1, tn), lambda l, j: (l, 0, j)),
        ],
        out_specs=pl.BlockSpec((None, mp, tn), lambda l, j: (l, 0, j)),
        out_shape=jax.ShapeDtypeStruct((depth, mp, n), F32),
        compiler_params=_cparams(("arbitrary", "arbitrary")),
        name="ada_params",
    )(cvec, ada_w, ada_b.reshape(depth, 1, n))


def _swap16(x):
    width = x.shape[-1]
    lane = lax.broadcasted_iota(jnp.int32, x.shape, 1)
    return jnp.where((lane % 32) < 16, pltpu.roll(x, width - 16, 1), pltpu.roll(x, 16, 1))


def _proj_even_kernel(xp_ref, xs_ref, mod_ref, g_ref, w_ref, cos_ref, sin_ref, o_ref, kv_ref, h_scr, *,
                      n_prompt_blocks):
    i = pl.program_id(0)

    def normed(x_ref):
        h_scr[...] = _rms_mod(x_ref[...], g_ref[...], mod_ref[1:2, :], mod_ref[0:1, :]).astype(BF16)

    pl.when(i < n_prompt_blocks)(lambda: normed(xp_ref))
    pl.when(i >= n_prompt_blocks)(lambda: normed(xs_ref))
    kv_tile = ATTN_WIDTH // TN_PROJ
    for j in range(w_ref.shape[1] // TN_PROJ):
        cols = slice(j * TN_PROJ, (j + 1) * TN_PROJ)
        acc = jnp.dot(h_scr[...], w_ref[:, cols], preferred_element_type=F32)
        if j <= kv_tile:
            tab = slice(0, TN_PROJ) if j < kv_tile else slice(TN_PROJ, 2 * TN_PROJ)
            o_ref[:, cols] = (acc * cos_ref[:, tab] + _swap16(acc) * sin_ref[:, tab]).astype(BF16)
            if j == kv_tile:
                kv_ref[...] = acc
        else:
            o_ref[:, cols] = jax.nn.gelu(acc).astype(BF16)


def _proj_even(x_p, x_s, mods, gain, w, cos_t, sin_t, blocks_per_seq):
    d = x_p.shape[1]
    n = w.shape[1]
    tm, tn = TM_PROJ, TN_PROJ
    npb = x_p.shape[0] // tm
    nsb = x_s.shape[0] // tm
    t = x_p.shape[0] + x_s.shape[0]

    def mod_idx(i):
        return (jnp.where(i < npb, 0, 1 + (i - npb) // blocks_per_seq), 0, 0)

    def tab_idx(i):
        return (jnp.where(i < npb, blocks_per_seq, (i - npb) % blocks_per_seq), 0)

    return pl.pallas_call(
        functools.partial(_proj_even_kernel, n_prompt_blocks=npb),
        grid=(npb + nsb,),
        in_specs=[
            pl.BlockSpec((tm, d), lambda i: (jnp.minimum(i, npb - 1), 0)),
            pl.BlockSpec((tm, d), lambda i: (jnp.maximum(i - npb, 0), 0)),
            pl.BlockSpec((None, 6, d), mod_idx),
            pl.BlockSpec((1, d), lambda i: (0, 0)),
            pl.BlockSpec((d, n), lambda i: (0, 0), pipeline_mode=pl.Buffered(1)),
            pl.BlockSpec((tm, 2 * tn), tab_idx),
            pl.BlockSpec((tm, 2 * tn), tab_idx),
        ],
        out_specs=[
            pl.BlockSpec((tm, n), lambda i: (i, 0)),
            pl.BlockSpec((tm, tn), lambda i: (i, 0)),
        ],
        out_shape=[jax.ShapeDtypeStruct((t, n), BF16), jax.ShapeDtypeStruct((t, tn), F32)],
        scratch_shapes=[pltpu.VMEM((tm, d), BF16)],
        compiler_params=_cparams(("arbitrary",)),
        name="proj_even",
    )(x_p, x_s, mods, gain, w, cos_t, sin_t)


def _rope_tables(dec_seq, tm):
    f32 = np.float32
    n = np.arange(dec_seq)
    row = (n // GRID_W).astype(f32)
    col = (n % GRID_W).astype(f32)
    nf = HEAD_DIM // 4
    inv_freq = np.power(f32(ROPE_BASE), -np.arange(nf, dtype=f32) / f32(nf)).astype(f32)
    ar = row[:, None] * inv_freq[None, :]
    ac = col[:, None] * inv_freq[None, :]
    cos_h = np.concatenate([np.cos(ar), np.cos(ar), np.cos(ac), np.cos(ac)], axis=-1).astype(f32)
    sin_h = np.concatenate([-np.sin(ar), np.sin(ar), -np.sin(ac), np.sin(ac)], axis=-1).astype(f32)
    scale = f32(HEAD_DIM ** -0.5)
    ones = np.ones((dec_seq, KV_WIDTH), f32)
    zeros = np.zeros((dec_seq, KV_WIDTH), f32)
    q_heads = TN_PROJ // HEAD_DIM
    cos_t = np.concatenate([np.tile(cos_h, (1, q_heads)) * scale, np.tile(cos_h, (1, N_KV_HEADS)), ones], axis=-1)
    sin_t = np.concatenate([np.tile(sin_h, (1, q_heads)) * scale, np.tile(sin_h, (1, N_KV_HEADS)), zeros], axis=-1)
    id_cos = np.concatenate([np.full((tm, TN_PROJ), scale, f32), np.ones((tm, TN_PROJ), f32)], axis=-1)
    id_sin = np.zeros((tm, 2 * TN_PROJ), f32)
    return (jnp.asarray(np.concatenate([cos_t, id_cos], axis=0), F32),
            jnp.asarray(np.concatenate([sin_t, id_sin], axis=0), F32))


def _sink_attend(q, keys, vals, masks, sink):
    scores = []
    for k, mask in zip(keys, masks):
        s = lax.dot_general(q, k, (((1,), (1,)), ((), ())), preferred_element_type=F32)
        scores.append(s if mask is None else jnp.where(mask, s, NEG_INF))
    m = sink
    for s in scores:
        m = jnp.maximum(m, jnp.max(s, axis=-1, keepdims=True))
    es = [jnp.exp(s - m) for s in scores]
    den = jnp.exp(sink - m)
    for e in es:
        den = den + jnp.sum(e, axis=-1, keepdims=True)
    inv = 1.0 / den
    out = None
    for e, v in zip(es, vals):
        o = jnp.dot((e * inv).astype(BF16), v, preferred_element_type=F32)
        out = o if out is None else out + o
    return out


def _grouped_heads_attend(sink_ref, q_ref, o_ref, kh, keys, vals, masks):
    rows = q_ref.shape[0]
    heads = [kh * Q_GROUPS + g for g in range(Q_GROUPS)]
    q = jnp.concatenate([q_ref[:, h * HEAD_DIM:(h + 1) * HEAD_DIM] for h in heads], axis=0)
    ridx = lax.broadcasted_iota(jnp.int32, (Q_GROUPS * rows, 1), 0)
    sink = jnp.full((Q_GROUPS * rows, 1), sink_ref[heads[-1]], F32)
    for g in range(Q_GROUPS - 2, -1, -1):
        sink = jnp.where(ridx < (g + 1) * rows, sink_ref[heads[g]], sink)
    o = _sink_attend(q, keys, vals, masks, sink)
    for g, h in enumerate(heads):
        o_ref[:, h * HEAD_DIM:(h + 1) * HEAD_DIM] = o[g * rows:(g + 1) * rows, :].astype(BF16)


def _ctx_attn_kernel(sink_ref, q_ref, kv_ref, o_ref):
    for kh in range(N_KV_HEADS):
        k = kv_ref[:, kh * HEAD_DIM:(kh + 1) * HEAD_DIM]
        v = kv_ref[:, KV_WIDTH + kh * HEAD_DIM:KV_WIDTH + (kh + 1) * HEAD_DIM]
        _grouped_heads_attend(sink_ref, q_ref, o_ref, kh, [k], [v], [None])


def _ctx_attention(proj, sink, batch, seq, kv_col_block):
    return pl.pallas_call(
        _ctx_attn_kernel,
        grid=(batch,),
        in_specs=[
            pl.BlockSpec(memory_space=pltpu.SMEM),
            pl.BlockSpec((seq, ATTN_WIDTH), lambda b: (b, 0)),
            pl.BlockSpec((seq, 2 * KV_WIDTH), lambda b: (b, kv_col_block)),
        ],
        out_specs=pl.BlockSpec((seq, ATTN_WIDTH), lambda b: (b, 0)),
        out_shape=jax.ShapeDtypeStruct((batch * seq, ATTN_WIDTH), BF16),
        compiler_params=_cparams(("arbitrary",)),
        name="ctx_attention",
    )(sink, proj, proj)


def _lat_attn_kernel(sink_ref, q_ref, kv_ref, ck_ref, cv_ref, o_ref, *, seq):
    i = pl.program_id(1)
    win = 3 * ATTN_BLK
    ws = pl.multiple_of(jnp.clip((i - 1) * ATTN_BLK, 0, seq - win), ATTN_BLK)
    shape = (Q_GROUPS * ATTN_BLK, win)
    qpos = i * ATTN_BLK + (lax.broadcasted_iota(jnp.int32, shape, 0) % ATTN_BLK)
    kpos = ws + lax.broadcasted_iota(jnp.int32, shape, 1)
    valid = jnp.abs(qpos - kpos) <= ATTN_BLK
    kvw = kv_ref[pl.ds(ws, win), :]
    for kh in range(N_KV_HEADS):
        k_loc = kvw[:, kh * HEAD_DIM:(kh + 1) * HEAD_DIM]
        v_loc = kvw[:, KV_WIDTH + kh * HEAD_DIM:KV_WIDTH + (kh + 1) * HEAD_DIM]
        k_ctx = ck_ref[:, kh * HEAD_DIM:(kh + 1) * HEAD_DIM]
        v_ctx = cv_ref[:, kh * HEAD_DIM:(kh + 1) * HEAD_DIM]
        _grouped_heads_attend(sink_ref, q_ref, o_ref, kh, [k_ctx, k_loc], [v_ctx, v_loc], [None, valid])


def _lat_attention(proj, sink, cache_k, cache_v, n_prompt, dec_batch, dec_seq, kv_col_block):
    qb = dec_seq // ATTN_BLK
    q_off = n_prompt // ATTN_BLK
    s_off = n_prompt // dec_seq
    past = cache_k.shape[1]
    return pl.pallas_call(
        functools.partial(_lat_attn_kernel, seq=dec_seq),
        grid=(dec_batch, qb),
        in_specs=[
            pl.BlockSpec(memory_space=pltpu.SMEM),
            pl.BlockSpec((ATTN_BLK, ATTN_WIDTH), lambda b, i: (q_off + b * qb + i, 0)),
            pl.BlockSpec((dec_seq, 2 * KV_WIDTH), lambda b, i: (s_off + b, kv_col_block)),
            pl.BlockSpec((None, past, KV_WIDTH), lambda b, i: (b, 0, 0)),
            pl.BlockSpec((None, past, KV_WIDTH), lambda b, i: (b, 0, 0)),
        ],
        out_specs=pl.BlockSpec((ATTN_BLK, ATTN_WIDTH), lambda b, i: (b * qb + i, 0)),
        out_shape=jax.ShapeDtypeStruct((dec_batch * dec_seq, ATTN_WIDTH), BF16),
        compiler_params=_cparams(("arbitrary", "arbitrary")),
        name="lat_attention",
    )(sink, proj, proj, cache_k, cache_v)


def _sgu_kernel(ul_ref, uh_ref, vl_ref, vh_ref, g_ref, w_ref, bt_ref, z_ref):
    half = SGU_WIDTH // 2
    per_half = SGU_GROUPS // 2
    u_refs = (ul_ref, uh_ref)
    for c in range(TM_SGU // CHUNK):
        rows = slice(c * CHUNK, (c + 1) * CHUNK)
        vl = vl_ref[rows, :].astype(F32)
        vh = vh_ref[rows, :].astype(F32)
        ssq = jnp.sum(vl * vl, axis=-1, keepdims=True) + jnp.sum(vh * vh, axis=-1, keepdims=True)
        r = lax.rsqrt(ssq * (1.0 / SGU_WIDTH) + EPS)
        vn = ((vl * r * g_ref[:, :half]).astype(BF16), (vh * r * g_ref[:, half:]).astype(BF16))
        for g in range(SGU_GROUPS):
            src = g // per_half
            cols = slice((g % per_half) * SGU_GROUP_DIM, (g % per_half + 1) * SGU_GROUP_DIM)
            mixed = jnp.dot(w_ref[g], vn[src][:, cols], preferred_element_type=F32) + bt_ref[:, g:g + 1]
            z_ref[rows, g * SGU_GROUP_DIM:(g + 1) * SGU_GROUP_DIM] = (
                u_refs[src][rows, cols].astype(F32) * mixed).astype(BF16)


def _sgu(proj, norm_g, w_s, b_t):
    t = proj.shape[0]
    half = SGU_WIDTH // 2
    u0 = (ATTN_WIDTH + 2 * KV_WIDTH) // half
    blocks = [pl.BlockSpec((TM_SGU, half), functools.partial(lambda i, c: (i, c), c=u0 + k)) for k in range(4)]
    return pl.pallas_call(
        _sgu_kernel,
        grid=(t // TM_SGU,),
        in_specs=blocks + [
            pl.BlockSpec((1, SGU_WIDTH), lambda i: (0, 0)),
            pl.BlockSpec((SGU_GROUPS, CHUNK, CHUNK), lambda i: (0, 0, 0)),
            pl.BlockSpec((CHUNK, SGU_GROUPS), lambda i: (0, 0)),
        ],
        out_specs=pl.BlockSpec((TM_SGU, SGU_WIDTH), lambda i: (i, 0)),
        out_shape=jax.ShapeDtypeStruct((t, SGU_WIDTH), BF16),
        compiler_params=_cparams(("arbitrary",)),
        name="sgu",
    )(proj, proj, proj, proj, norm_g, w_s, b_t)


def _residual_router(x, acc, mod_ref, g2_ref, wrt_ref, brt_ref, x1_ref, h2_ref, lg_ref):
    x1 = x + mod_ref[2:3, :] * acc
    x1_ref[...] = x1
    h2 = _rms_mod(x1, g2_ref[...], mod_ref[4:5, :], mod_ref[3:4, :])
    h2_ref[...] = h2
    lg = lax.dot_general(wrt_ref[...], h2.astype(BF16), (((1,), (1,)), ((), ())), preferred_element_type=F32)
    lg_ref[...] = lg + brt_ref[...]


def _out_even_kernel(xp_ref, xs_ref, ap_ref, as_ref, z_ref, w_ref, mod_ref, g2_ref, wrt_ref, brt_ref,
                     x1_ref, h2_ref, lg_ref, *, n_prompt_blocks):
    is_prompt = pl.program_id(0) < n_prompt_blocks
    a = jnp.where(is_prompt, ap_ref[...].astype(F32), as_ref[...].astype(F32)).astype(BF16)
    acc = jnp.dot(a, w_ref[0:ATTN_WIDTH, :], preferred_element_type=F32)
    acc = acc + jnp.dot(z_ref[...], w_ref[ATTN_WIDTH:, :], preferred_element_type=F32)
    x = jnp.where(is_prompt, xp_ref[...], xs_ref[...])
    _residual_router(x, acc, mod_ref, g2_ref, wrt_ref, brt_ref, x1_ref, h2_ref, lg_ref)


def _out_odd_kernel(x_ref, hc_ref, w_ref, mod_ref, g2_ref, wrt_ref, brt_ref, x1_ref, h2_ref, lg_ref):
    acc = jnp.dot(hc_ref[...], w_ref[...], preferred_element_type=F32)
    _residual_router(x_ref[...], acc, mod_ref, g2_ref, wrt_ref, brt_ref, x1_ref, h2_ref, lg_ref)


def _out_proj(xs, lhs, w, mods, gain2, wrt, brt, n_prompt, dec_seq):
    d = xs[0].shape[1]
    t = sum(x.shape[0] for x in xs)
    tm = TM_OUT
    npb = n_prompt // tm
    bps = dec_seq // tm
    nsb = t // tm - npb
    if len(xs) == 2:
        x_specs = [pl.BlockSpec((tm, d), lambda i: (jnp.minimum(i, npb - 1), 0)),
                   pl.BlockSpec((tm, d), lambda i: (jnp.maximum(i - npb, 0), 0))]
    else:
        x_specs = [pl.BlockSpec((tm, d), lambda i: (i, 0))]

    def mod_idx(i):
        return (jnp.where(i < npb, 0, 1 + (i - npb) // bps), 0, 0)

    if len(lhs) == 3:
        a_p, a_s, z = lhs
        body = functools.partial(_out_even_kernel, n_prompt_blocks=npb)
        lhs_specs = [
            pl.BlockSpec((tm, ATTN_WIDTH), lambda i: (jnp.minimum(i, npb - 1), 0)),
            pl.BlockSpec((tm, ATTN_WIDTH), lambda i: (jnp.clip(i - npb, 0, nsb - 1), 0)),
            pl.BlockSpec((tm, SGU_WIDTH), lambda i: (i, 0)),
        ]
    else:
        body = _out_odd_kernel
        lhs_specs = [pl.BlockSpec((tm, d), lambda i: (i, 0))]
    return pl.pallas_call(
        body,
        grid=(t // tm,),
        in_specs=x_specs + lhs_specs + [
            pl.BlockSpec((w.shape[0], d), lambda i: (0, 0)),
            pl.BlockSpec((None, 6, d), mod_idx),
            pl.BlockSpec((1, d), lambda i: (0, 0)),
            pl.BlockSpec((ROUTER_LANES, d), lambda i: (0, 0)),
            pl.BlockSpec((ROUTER_LANES, 1), lambda i: (0, 0)),
        ],
        out_specs=[
            pl.BlockSpec((tm, d), lambda i: (i, 0)),
            pl.BlockSpec((tm, d), lambda i: (i, 0)),
            pl.BlockSpec((ROUTER_LANES, tm), lambda i: (0, i)),
        ],
        out_shape=[
            jax.ShapeDtypeStruct((t, d), F32),
            jax.ShapeDtypeStruct((t, d), F32),
            jax.ShapeDtypeStruct((ROUTER_LANES, t), F32),
        ],
        compiler_params=_cparams(("arbitrary",)),
        name="out_proj",
    )(*xs, *lhs, w, mods, gain2, wrt, brt)


def _first_argmax4(v0, v1, v2, v3):
    m = jnp.maximum(jnp.maximum(v0, v1), jnp.maximum(v2, v3))
    idx = jnp.where(v0 == m, 0.0, jnp.where(v1 == m, 1.0, jnp.where(v2 == m, 2.0, 3.0)))
    return m, idx


def _route_kernel(lg_ref, pos_ref, ea_ref, eb_ref, nu_ref, ends_ref, oh_scr, rank_scr, *, n_tokens):
    lg = lg_ref[...]
    rows = [lg[r:r + 1, :] for r in range(N_GROUPS_MOE + N_EXPERTS)]
    _, gidx = _first_argmax4(*rows[:N_GROUPS_MOE])
    e = []
    for k in range(EXPERTS_PER_GROUP):
        cand = [rows[N_GROUPS_MOE + g * EXPERTS_PER_GROUP + k] for g in range(N_GROUPS_MOE)]
        e.append(jnp.where(gidx == 0.0, cand[0], jnp.where(gidx == 1.0, cand[1], jnp.where(gidx == 2.0, cand[2], cand[3]))))
    _, l1 = _first_argmax4(*e)
    e2 = [jnp.where(l1 == float(k), -jnp.inf, e[k]) for k in range(EXPERTS_PER_GROUP)]
    _, l2 = _first_argmax4(*e2)
    lo = jnp.minimum(l1, l2)
    hi = jnp.maximum(l1, l2)
    pair = jnp.where(lo == 0.0, hi - 1.0, jnp.where(lo == 1.0, hi + 1.0, 5.0))
    cls = gidx * float(PAIRS_PER_GROUP) + pair

    crow = lax.broadcasted_iota(jnp.int32, (CLASS_ROWS, n_tokens), 0).astype(F32)
    oh_scr[...] = jnp.where(crow == cls, 1.0, 0.0).astype(BF16)

    jj = lax.broadcasted_iota(jnp.int32, (CUM_CHUNK, CUM_CHUNK), 0)
    ii = lax.broadcasted_iota(jnp.int32, (CUM_CHUNK, CUM_CHUNK), 1)
    tri = jnp.where(jj < ii, 1.0, 0.0).astype(BF16)

    def chunk(c, carry):
        off = pl.multiple_of(c * CUM_CHUNK, CUM_CHUNK)
        oh = oh_scr[:, pl.ds(off, CUM_CHUNK)]
        ohf = oh.astype(F32)
        cum = jnp.dot(oh, tri, preferred_element_type=F32) + carry
        rank_scr[:, pl.ds(off, CUM_CHUNK)] = jnp.sum(cum * ohf, axis=0, keepdims=True)
        return carry + jnp.sum(ohf, axis=1, keepdims=True)

    counts = lax.fori_loop(0, n_tokens // CUM_CHUNK, chunk, jnp.zeros((CLASS_ROWS, 1), F32))
    padded = jnp.floor((counts + float(MOE_BLK - 1)) * (1.0 / MOE_BLK)) * float(MOE_BLK)
    run = jnp.zeros((1, 1), F32)
    starts_rows = []
    for c in range(CLASS_ROWS):
        starts_rows.append(run)
        run = run + padded[c:c + 1, :]
    starts = jnp.concatenate(starts_rows, axis=0)
    ends = starts + padded
    total = run

    ohf = oh_scr[...].astype(F32)
    pos = rank_scr[...] + jnp.sum(ohf * starts, axis=0, keepdims=True)
    pos_ref[...] = pos.astype(jnp.int32)

    bstart = lax.broadcasted_iota(jnp.int32, (CLASS_ROWS, MAX_MOE_BLOCKS), 1).astype(F32) * float(MOE_BLK)
    brow = lax.broadcasted_iota(jnp.int32, (CLASS_ROWS, MAX_MOE_BLOCKS), 0)
    done = jnp.where((ends <= bstart) & (brow < N_CLASSES), 1.0, 0.0)
    bcls = jnp.minimum(jnp.sum(done, axis=0, keepdims=True), float(N_CLASSES - 1))
    grp = (jnp.where(bcls >= 6.0, 1.0, 0.0) + jnp.where(bcls >= 12.0, 1.0, 0.0) + jnp.where(bcls >= 18.0, 1.0, 0.0))
    bp = bcls - grp * float(PAIRS_PER_GROUP)
    ge3 = jnp.where(bp >= 3.0, 1.0, 0.0)
    ge5 = jnp.where(bp >= 5.0, 1.0, 0.0)
    blo = ge3 + ge5
    bhi = bp + 1.0 - 2.0 * ge3 - ge5
    ea_ref[...] = (grp * float(EXPERTS_PER_GROUP) + blo).astype(jnp.int32)
    eb_ref[...] = (grp * float(EXPERTS_PER_GROUP) + bhi).astype(jnp.int32)
    nu_ref[...] = jnp.broadcast_to(total * (1.0 / MOE_BLK), (1, MAX_MOE_BLOCKS)).astype(jnp.int32)
    lane = lax.broadcasted_iota(jnp.int32, (1, MAX_MOE_BLOCKS), 1)
    ends_row = jnp.zeros((1, MAX_MOE_BLOCKS), F32)
    for c in range(N_CLASSES):
        ends_row = jnp.where(lane == c, starts_rows[c] + padded[c:c + 1, :], ends_row)
    ends_ref[...] = ends_row.astype(jnp.int32)


def _route(logits_t):
    t = logits_t.shape[1]
    rows = N_CLASSES
    i32 = jnp.int32
    return pl.pallas_call(
        functools.partial(_route_kernel, n_tokens=t),
        grid=(1,),
        in_specs=[pl.BlockSpec((rows, t), lambda i: (0, 0))],
        out_specs=[
            pl.BlockSpec((1, t), lambda i: (0, 0)),
        ] + [pl.BlockSpec((1, MAX_MOE_BLOCKS), lambda i: (0, 0))] * 4,
        out_shape=[jax.ShapeDtypeStruct((1, t), i32)] + [jax.ShapeDtypeStruct((1, MAX_MOE_BLOCKS), i32)] * 4,
        scratch_shapes=[pltpu.VMEM((CLASS_ROWS, t), BF16), pltpu.VMEM((1, t), F32)],
        compiler_params=_cparams(("arbitrary",)),
        name="route",
    )(logits_t)


ROW_UNROLL = 256
DMA_THREADS = 2


def _row_copy(src_ref, src_row, dst_ref, dst_row, sem):
    return pltpu.make_async_copy(src_ref.at[pl.ds(src_row, 1), :], dst_ref.at[pl.ds(dst_row, 1), :], sem)


def _start_row_copies(n_rows, make_copy):
    def body(i, carry):
        for u in range(ROW_UNROLL):
            make_copy(i * ROW_UNROLL + u).start(priority=u % DMA_THREADS)
        return carry

    if n_rows == ROW_UNROLL:
        body(0, 0)
    else:
        lax.fori_loop(0, n_rows // ROW_UNROLL, body, 0)


def _dispatch_kernel(pos_ref, ends_ref, h_ref, o_ref, zbuf, sem, zsem):
    @pl.when(pl.program_id(0) == 0)
    def _():
        zbuf[...] = jnp.zeros_like(zbuf)
        total = ends_ref[N_CLASSES - 1]
        n_free = (o_ref.shape[0] - total) // MOE_BLK

        def zero_block(first_row, phase):
            first = pl.multiple_of(first_row, MOE_BLK)
            getattr(pltpu.make_async_copy(zbuf, o_ref.at[pl.ds(first, MOE_BLK), :], zsem), phase)()

        for phase in ("start", "wait"):
            prev = 0
            for c in range(N_CLASSES):
                end = ends_ref[c]
                pl.when(end > prev)(functools.partial(zero_block, end - MOE_BLK, phase))
                prev = end

            def free_block(k, carry, phase=phase):
                zero_block(total + k * MOE_BLK, phase)
                return carry

            lax.fori_loop(0, n_free, free_block, 0)

    base = pl.program_id(0) * TM_OUT

    _start_row_copies(TM_OUT, lambda r: _row_copy(h_ref, r, o_ref, pos_ref[base + r], sem))
    pltpu.make_async_copy(h_ref, o_ref.at[pl.ds(0, TM_OUT), :], sem).wait()


def _dispatch(pos, ends, h2, n_rows):
    t, d = h2.shape
    return pl.pallas_call(
        _dispatch_kernel,
        grid_spec=pltpu.PrefetchScalarGridSpec(
            num_scalar_prefetch=2,
            grid=(t // TM_OUT,),
            in_specs=[pl.BlockSpec((TM_OUT, d), lambda i, pos, ends: (i, 0))],
            out_specs=pl.BlockSpec(memory_space=pl.ANY),
            scratch_shapes=[pltpu.VMEM((MOE_BLK, d), F32), pltpu.SemaphoreType.DMA(()), pltpu.SemaphoreType.DMA(())],
        ),
        out_shape=jax.ShapeDtypeStruct((n_rows, d), F32),
        compiler_params=_cparams(("arbitrary",)),
        name="moe_dispatch",
    )(pos, ends, h2)


def _expert_ffn(x, wg_ref, wu_ref, wd_ref):
    gate = jnp.dot(x, wg_ref[...], preferred_element_type=F32)
    up = jnp.dot(x, wu_ref[...], preferred_element_type=F32)
    hidden = (gate * jax.nn.sigmoid(gate) * up).astype(BF16)
    return jnp.dot(hidden, wd_ref[...], preferred_element_type=F32)


def _moe_kernel(ea_ref, eb_ref, nu_ref, x_ref, wr_ref, br_ref, wga, wua, wda, wgb, wub, wdb, y_ref):
    b = pl.program_id(0)

    @pl.when(b < nu_ref[0])
    def _():
        x = x_ref[...].astype(BF16)
        lg = jnp.dot(x, wr_ref[...], preferred_element_type=F32) + br_ref[...]
        lane = lax.broadcasted_iota(jnp.int32, lg.shape, 1)
        ea = ea_ref[b]
        eb = eb_ref[b]
        grp = ea // EXPERTS_PER_GROUP
        is_grp = lane < N_GROUPS_MOE
        gm = jnp.max(jnp.where(is_grp, lg, -jnp.inf), axis=-1, keepdims=True)
        ge = jnp.where(is_grp, jnp.exp(lg - gm), 0.0)
        p_g = jnp.sum(jnp.where(lane == grp, ge, 0.0), axis=-1, keepdims=True) / jnp.sum(ge, axis=-1, keepdims=True)
        la = jnp.sum(jnp.where(lane == N_GROUPS_MOE + ea, lg, 0.0), axis=-1, keepdims=True)
        lb = jnp.sum(jnp.where(lane == N_GROUPS_MOE + eb, lg, 0.0), axis=-1, keepdims=True)
        mm = jnp.maximum(la, lb)
        xa = jnp.exp(la - mm)
        xb = jnp.exp(lb - mm)
        ga = xa / (xa + xb) * p_g
        gb = xb / (xa + xb) * p_g
        ya = _expert_ffn(x, wga, wua, wda)
        yb = _expert_ffn(x, wgb, wub, wdb)
        y_ref[...] = ya * ga + yb * gb

    @pl.when(b >= nu_ref[0])
    def _():
        y_ref[...] = jnp.zeros_like(y_ref)


def _moe_experts(ea, eb, nused, hs, wr, br, w_gate, w_up, w_down, layer, n_blocks):
    d = hs.shape[1]
    de = w_gate.shape[3]

    def wa_idx(b, ea, eb, nu):
        return (layer, ea[b], 0, 0)

    def wb_idx(b, ea, eb, nu):
        return (layer, eb[b], 0, 0)

    return pl.pallas_call(
        _moe_kernel,
        grid_spec=pltpu.PrefetchScalarGridSpec(
            num_scalar_prefetch=3,
            grid=(n_blocks,),
            in_specs=[
                pl.BlockSpec((MOE_BLK, d), lambda b, ea, eb, nu: (jnp.minimum(b, nu[0] - 1), 0)),
                pl.BlockSpec((d, ROUTER_LANES), lambda b, ea, eb, nu: (0, 0)),
                pl.BlockSpec((1, ROUTER_LANES), lambda b, ea, eb, nu: (0, 0)),
                pl.BlockSpec((None, None, d, de), wa_idx),
                pl.BlockSpec((None, None, d, de), wa_idx),
                pl.BlockSpec((None, None, de, d), wa_idx),
                pl.BlockSpec((None, None, d, de), wb_idx),
                pl.BlockSpec((None, None, d, de), wb_idx),
                pl.BlockSpec((None, None, de, d), wb_idx),
            ],
            out_specs=pl.BlockSpec((MOE_BLK, d), lambda b, ea, eb, nu: (b, 0)),
        ),
        out_shape=jax.ShapeDtypeStruct((n_blocks * MOE_BLK, d), F32),
        compiler_params=_cparams(("arbitrary",)),
        name="moe_experts",
    )(ea, eb, nused, hs, wr, br, w_gate, w_up, w_down, w_gate, w_up, w_down)


def _combine_kernel(pos_ref, ys_ref, x1_ref, mod_ref, gf_ref, o_ref, ybuf, sem, *, block_off, final):
    base = (pl.program_id(0) + block_off) * TM_OUT

    _start_row_copies(TM_OUT, lambda r: _row_copy(ys_ref, pos_ref[base + r], ybuf, r, sem))
    pltpu.make_async_copy(ys_ref.at[pl.ds(0, TM_OUT), :], ybuf, sem).wait()
    x2 = x1_ref[...] + mod_ref[5:6, :] * ybuf[...]
    if final:
        ms = jnp.mean(x2 * x2, axis=-1, keepdims=True)
        x2 = x2 * lax.rsqrt(ms + EPS) * gf_ref[...]
    o_ref[...] = x2


def _combine(pos, ys, x1, mods, gain_f, n_prompt, dec_seq, row_off, n_rows, final):
    d = x1.shape[1]
    tm = TM_OUT
    npb = n_prompt // tm
    bps = dec_seq // tm
    boff = row_off // tm

    def mod_idx(i, pos):
        blk = i + boff
        return (jnp.where(blk < npb, 0, 1 + (blk - npb) // bps), 0, 0)

    return pl.pallas_call(
        functools.partial(_combine_kernel, block_off=boff, final=final),
        grid_spec=pltpu.PrefetchScalarGridSpec(
            num_scalar_prefetch=1,
            grid=(n_rows // tm,),
            in_specs=[
                pl.BlockSpec(memory_space=pl.ANY),
                pl.BlockSpec((tm, d), lambda i, pos: (i + boff, 0)),
                pl.BlockSpec((None, 6, d), mod_idx),
                pl.BlockSpec((1, d), lambda i, pos: (0, 0)),
            ],
            out_specs=pl.BlockSpec((tm, d), lambda i, pos: (i, 0)),
            scratch_shapes=[pltpu.VMEM((tm, d), F32), pltpu.SemaphoreType.DMA(())],
        ),
        out_shape=jax.ShapeDtypeStruct((n_rows, d), F32),
        compiler_params=_cparams(("arbitrary",)),
        name="moe_combine",
    )(pos, ys, x1, mods, gain_f)


def _proj_glu_kernel(x_ref, mod_ref, g_ref, w_ref, o_ref, h_scr):
    h_scr[...] = _rms_mod(x_ref[...], g_ref[...], mod_ref[1:2, :], mod_ref[0:1, :]).astype(BF16)
    width = w_ref.shape[1] // 2
    for j in range(width // TN_PROJ):
        cols = slice(j * TN_PROJ, (j + 1) * TN_PROJ)
        gcols = slice(width + j * TN_PROJ, width + (j + 1) * TN_PROJ)
        val = jnp.dot(h_scr[...], w_ref[:, cols], preferred_element_type=F32)
        gate = jnp.dot(h_scr[...], w_ref[:, gcols], preferred_element_type=F32)
        o_ref[:, cols] = (val * jax.nn.sigmoid(gate)).astype(BF16)


def _proj_glu(x, mods, gain, w, n_prompt, dec_seq):
    t, d = x.shape
    n = w.shape[1]
    tm = TM_PROJ
    npb = n_prompt // tm
    bps = dec_seq // tm

    def mod_idx(i):
        return (jnp.where(i < npb, 0, 1 + (i - npb) // bps), 0, 0)

    return pl.pallas_call(
        _proj_glu_kernel,
        grid=(t // tm,),
        in_specs=[
            pl.BlockSpec((tm, d), lambda i: (i, 0)),
            pl.BlockSpec((None, 6, d), mod_idx),
            pl.BlockSpec((1, d), lambda i: (0, 0)),
            pl.BlockSpec((d, n), lambda i: (0, 0), pipeline_mode=pl.Buffered(1)),
        ],
        out_specs=pl.BlockSpec((tm, n // 2), lambda i: (i, 0)),
        out_shape=jax.ShapeDtypeStruct((t, n // 2), BF16),
        scratch_shapes=[pltpu.VMEM((tm, d), BF16)],
        compiler_params=_cparams(("arbitrary",)),
        name="proj_glu",
    )(x, mods, gain, w)


CONV_ROWS = 64
CONV_COLS = 256
F32_SUBLANES = 8


def _conv_kernel(x_ref, prev_ref, next_ref, dw_ref, b_ref, g_ref, o_ref, xpad, shifted, ybuf, *,
                 n_prompt_blocks, blocks_per_seq):
    i = pl.program_id(0)
    in_sample = i >= n_prompt_blocks
    j = i - n_prompt_blocks
    is_start = jnp.logical_or(jnp.logical_not(in_sample), j % blocks_per_seq == 0)
    is_end = jnp.logical_or(jnp.logical_not(in_sample), j % blocks_per_seq == blocks_per_seq - 1)
    tm = TM_OUT
    xpad[0:CONV_HALO, :] = jnp.where(is_start, 0.0, prev_ref[...].astype(F32))
    xpad[CONV_HALO:CONV_HALO + tm, :] = x_ref[...].astype(F32)
    xpad[CONV_HALO + tm:, :] = jnp.where(is_end, 0.0, next_ref[...].astype(F32))
    first = CONV_HALO - CONV_K // 2
    span = shifted.shape[1]

    def col_chunk(c, ssq):
        c0 = pl.multiple_of(c * CONV_COLS, CONV_COLS)
        cols = pl.ds(c0, CONV_COLS)
        w = dw_ref[:, cols]
        bias = b_ref[:, cols]
        for s in range(1, F32_SUBLANES):
            shifted[s - 1] = xpad[s:s + span, cols]
        parts = []
        for rc in range(tm // CONV_ROWS):
            r0 = rc * CONV_ROWS
            acc = jnp.zeros((CONV_ROWS, CONV_COLS), F32)
            for k in range(CONV_K):
                whole, s = divmod(first + k, F32_SUBLANES)
                base = whole * F32_SUBLANES + r0
                if s == 0:
                    src = xpad[base:base + CONV_ROWS, cols]
                else:
                    src = shifted[s - 1, base:base + CONV_ROWS, :]
                acc = acc + w[k:k + 1, :] * src
            acc = acc + bias
            ybuf[r0:r0 + CONV_ROWS, cols] = acc
            parts.append(jnp.sum(acc * acc, axis=-1, keepdims=True))
        return ssq + jnp.concatenate(parts, axis=0)

    d = x_ref.shape[1]
    ssq = lax.fori_loop(0, d // CONV_COLS, col_chunk, jnp.zeros((tm, 1), F32))
    y = ybuf[...] * lax.rsqrt(ssq * (1.0 / d) + EPS) * g_ref[...]
    o_ref[...] = (y * jax.nn.sigmoid(y)).astype(BF16)


def _conv_mix(xg, dw, dw_b, norm_g, n_prompt, seq, dec_seq):
    t, d = xg.shape
    tm = TM_OUT
    assert seq == tm, "context sequences must be exactly one conv row block"
    npb = n_prompt // tm
    bps = dec_seq // tm
    hb = tm // CONV_HALO
    last = t // CONV_HALO - 1
    return pl.pallas_call(
        functools.partial(_conv_kernel, n_prompt_blocks=npb, blocks_per_seq=bps),
        grid=(t // tm,),
        in_specs=[
            pl.BlockSpec((tm, d), lambda i: (i, 0)),
            pl.BlockSpec((CONV_HALO, d), lambda i: (jnp.maximum(i * hb - 1, 0), 0)),
            pl.BlockSpec((CONV_HALO, d), lambda i: (jnp.minimum((i + 1) * hb, last), 0)),
            pl.BlockSpec((dw.shape[0], d), lambda i: (0, 0)),
            pl.BlockSpec((1, d), lambda i: (0, 0)),
            pl.BlockSpec((1, d), lambda i: (0, 0)),
        ],
        out_specs=pl.BlockSpec((tm, d), lambda i: (i, 0)),
        out_shape=jax.ShapeDtypeStruct((t, d), BF16),
        scratch_shapes=[
            pltpu.VMEM((tm + 2 * CONV_HALO, d), F32),
            pltpu.VMEM((F32_SUBLANES - 1, tm + 2 * CONV_HALO - F32_SUBLANES, CONV_COLS), F32),
            pltpu.VMEM((tm, d), F32),
        ],
        compiler_params=_cparams(("arbitrary",)),
        name="conv_mix",
    )(xg, xg, xg, dw, dw_b, norm_g)


def _router_operands(w_rg, b_rg, w_re, b_re):
    d = w_rg.shape[0]
    used = N_GROUPS_MOE + N_EXPERTS
    w = jnp.concatenate([w_rg, w_re, jnp.zeros((d, ROUTER_LANES - used), F32)], axis=1)
    b = jnp.concatenate([b_rg, b_re, jnp.zeros((ROUTER_LANES - used,), F32)])
    return w.astype(BF16), b.reshape(1, ROUTER_LANES), w.T.astype(BF16), b.reshape(ROUTER_LANES, 1)


def _moe_rows(h2, logits_t, wr, br, w_gate, w_up, w_down, layer):
    t = h2.shape[0]
    n_blocks = -(-(t + N_CLASSES * (MOE_BLK - 1)) // MOE_BLK)
    assert n_blocks <= MAX_MOE_BLOCKS
    pos, ea, eb, nused, ends = _route(logits_t)
    pos = pos.reshape(t)
    hs = _dispatch(pos, ends.reshape(-1), h2, n_blocks * MOE_BLK)
    ys = _moe_experts(ea.reshape(-1), eb.reshape(-1), nused.reshape(-1)[:1], hs, wr, br, w_gate, w_up, w_down,
                      layer, n_blocks)
    return pos, ys


def kernel(x_prompt, x_sample, cache_k, cache_v, c, c_ctx, w_in_even, attn_sink, sgu_norm, sgu_w, sgu_b,
           w_out_even, conv_w_in, conv_dw, conv_dw_b, conv_norm, conv_w_out, ada_w, ada_b, norm_mix, norm_ffn,
           router_group_w, router_group_b, router_expert_w, router_expert_b, expert_w_gate, expert_w_up,
           expert_w_down, final_norm):
    batch, seq, d = x_prompt.shape
    dec_batch, dec_seq, _ = x_sample.shape
    depth = ada_w.shape[0]
    n_prompt = batch * seq
    n_sample = dec_batch * dec_seq
    assert d == D_MODEL and depth == 2 and n_prompt % dec_seq == 0 and dec_seq % TM_PROJ == 0
    x_p = x_prompt.reshape(n_prompt, d)
    x_s = x_sample.reshape(n_sample, d)

    n_mod = -(-(1 + dec_batch) // 8) * 8
    cvec = jnp.concatenate([c_ctx[None, :], c, jnp.zeros((n_mod - 1 - dec_batch, d), F32)], axis=0)
    mods = _ada_params(cvec, ada_w, ada_b).reshape(depth, n_mod, 6, d)
    wg, wu, wd = expert_w_gate.astype(BF16), expert_w_up.astype(BF16), expert_w_down.astype(BF16)

    cos_t, sin_t = _rope_tables(dec_seq, TM_PROJ)
    proj, kv32 = _proj_even(x_p, x_s, mods[0], norm_mix[0][None, :], w_in_even[0].astype(BF16), cos_t, sin_t,
                            dec_seq // TM_PROJ)
    kv_col_block = ATTN_WIDTH // (2 * KV_WIDTH)
    sink = attn_sink[0]
    a_p = _ctx_attention(proj, sink, batch, seq, kv_col_block)
    past = cache_k.shape[2]
    ck = cache_k[:, 0].reshape(dec_batch, past, KV_WIDTH).astype(BF16)
    cv = cache_v[:, 0].reshape(dec_batch, past, KV_WIDTH).astype(BF16)
    a_s = _lat_attention(proj, sink, ck, cv, n_prompt, dec_batch, dec_seq, kv_col_block)
    z = _sgu(proj, sgu_norm[0][None, :], sgu_w[0].astype(BF16), sgu_b[0].T)

    wr0, br0, wrt0, brt0 = _router_operands(router_group_w[0], router_group_b[0], router_expert_w[0], router_expert_b[0])
    x1, h2, lg = _out_proj((x_p, x_s), (a_p, a_s, z), w_out_even[0].astype(BF16), mods[0], norm_ffn[0][None, :],
                           wrt0, brt0, n_prompt, dec_seq)
    pos, ys = _moe_rows(h2, lg, wr0, br0, wg, wu, wd, 0)
    gain_f = final_norm[None, :]
    x = _combine(pos, ys, x1, mods[0], gain_f, n_prompt, dec_seq, 0, n_prompt + n_sample, final=False)

    xg = _proj_glu(x, mods[1], norm_mix[1][None, :], conv_w_in[0].astype(BF16), n_prompt, dec_seq)
    dw = jnp.concatenate([conv_dw[0], jnp.zeros((1, d), F32)], axis=0)
    hc = _conv_mix(xg, dw, conv_dw_b[0][None, :], conv_norm[0][None, :], n_prompt, seq, dec_seq)
    wr1, br1, wrt1, brt1 = _router_operands(router_group_w[1], router_group_b[1], router_expert_w[1], router_expert_b[1])
    x1, h2, lg = _out_proj((x,), (hc,), conv_w_out[0].astype(BF16), mods[1], norm_ffn[1][None, :],
                           wrt1, brt1, n_prompt, dec_seq)
    pos, ys = _moe_rows(h2, lg, wr1, br1, wg, wu, wd, 1)
    y_prompt = _combine(pos, ys, x1, mods[1], gain_f, n_prompt, dec_seq, 0, n_prompt, final=True)
    y_sample = _combine(pos, ys, x1, mods[1], gain_f, n_prompt, dec_seq, n_prompt, n_sample, final=True)

    new_k = kv32[:n_prompt, :KV_WIDTH].reshape(batch, 1, seq, N_KV_HEADS, HEAD_DIM)
    new_v = kv32[:n_prompt, KV_WIDTH:].reshape(batch, 1, seq, N_KV_HEADS, HEAD_DIM)
    return (y_prompt.reshape(batch, seq, d), y_sample.reshape(dec_batch, dec_seq, d), new_k, new_v)
```

```python
import functools

import jax
import jax.numpy as jnp
import numpy as np
from jax import lax
from jax.experimental import pallas as pl
from jax.experimental.pallas import tpu as pltpu

F32 = jnp.float32
BF16 = jnp.bfloat16

D_MODEL = 2048
GRID_W = 64
ATTN_BLK = 128
N_HEADS = 16
N_KV_HEADS = 4
Q_GROUPS = N_HEADS // N_KV_HEADS
HEAD_DIM = 64
ATTN_WIDTH = N_HEADS * HEAD_DIM
KV_WIDTH = N_KV_HEADS * HEAD_DIM
ROPE_BASE = 10000.0
SGU_GROUPS = 4
SGU_WIDTH = D_MODEL // 2
SGU_GROUP_DIM = SGU_WIDTH // SGU_GROUPS
CHUNK = 128
IN_WIDTH_EVEN = ATTN_WIDTH + 2 * KV_WIDTH + 2 * SGU_WIDTH
CONV_K = 31
CONV_HALO = 16
N_GROUPS_MOE = 4
EXPERTS_PER_GROUP = 4
N_EXPERTS = N_GROUPS_MOE * EXPERTS_PER_GROUP
PAIRS_PER_GROUP = 6
N_CLASSES = N_GROUPS_MOE * PAIRS_PER_GROUP
D_EXPERT = 512
MOE_BLK = 256
EPS = 1e-6
NEG_INF = -1e30

LANES = 128
ROUTER_LANES = 128
CLASS_ROWS = 32
MAX_MOE_BLOCKS = 128
TM_PROJ = 512
TN_PROJ = 512
TM_OUT = 256
TM_SGU = 512
CUM_CHUNK = 256
VMEM_LIMIT = 56 * 1024 * 1024


def _cparams(sem):
    return pltpu.CompilerParams(dimension_semantics=sem, vmem_limit_bytes=VMEM_LIMIT)


def _rms_mod(x, gain, scale, shift):
    ms = jnp.mean(x * x, axis=-1, keepdims=True)
    y = x * lax.rsqrt(ms + EPS) * gain
    return y * (1.0 + scale) + shift


def _ada_kernel(c_ref, w_ref, b_ref, o_ref):
    c = c_ref[...]
    s = (c * jax.nn.sigmoid(c)).astype(BF16)
    o_ref[...] = jnp.dot(s, w_ref[...].astype(BF16), preferred_element_type=F32) + b_ref[...]


def _ada_params(cvec, ada_w, ada_b):
    depth, d, n = ada_w.shape
    mp = cvec.shape[0]
    tn = 1024
    return pl.pallas_call(
        _ada_kernel,
        grid=(depth, n // tn),
        in_specs=[
            pl.BlockSpec((mp, d), lambda l, j: (0, 0)),
            pl.BlockSpec((None, d, tn), lambda l, j: (l, 0, j)),
            pl.BlockSpec((None, 1, tn), lambda l, j: (l, 0, j)),
        ],
        out_specs=pl.BlockSpec((None, mp, tn), lambda l, j: (l, 0, j)),
        out_shape=jax.ShapeDtypeStruct((depth, mp, n), F32),
        compiler_params=_cparams(("arbitrary", "arbitrary")),
        name="ada_params",
    )(cvec, ada_w, ada_b.reshape(depth, 1, n))


def _swap16(x):
    width = x.shape[-1]
    lane = lax.broadcasted_iota(jnp.int32, x.shape, 1)
    return jnp.where((lane % 32) < 16, pltpu.roll(x, width - 16, 1), pltpu.roll(x, 16, 1))


def _proj_even_kernel(xp_ref, xs_ref, mod_ref, g_ref, w_ref, cos_ref, sin_ref, o_ref, kv_ref, h_scr, *,
                      n_prompt_blocks):
    i = pl.program_id(0)

    def normed(x_ref):
        h_scr[...] = _rms_mod(x_ref[...], g_ref[...], mod_ref[1:2, :], mod_ref[0:1, :]).astype(BF16)

    pl.when(i < n_prompt_blocks)(lambda: normed(xp_ref))
    pl.when(i >= n_prompt_blocks)(lambda: normed(xs_ref))
    kv_tile = ATTN_WIDTH // TN_PROJ
    for j in range(w_ref.shape[1] // TN_PROJ):
        cols = slice(j * TN_PROJ, (j + 1) * TN_PROJ)
        acc = jnp.dot(h_scr[...], w_ref[:, cols], preferred_element_type=F32)
        if j <= kv_tile:
            tab = slice(0, TN_PROJ) if j < kv_tile else slice(TN_PROJ, 2 * TN_PROJ)
            o_ref[:, cols] = (acc * cos_ref[:, tab] + _swap16(acc) * sin_ref[:, tab]).astype(BF16)
            if j == kv_tile:
                kv_ref[...] = acc
        else:
            o_ref[:, cols] = jax.nn.gelu(acc).astype(BF16)


def _proj_even(x_p, x_s, mods, gain, w, cos_t, sin_t, blocks_per_seq):
    d = x_p.shape[1]
    n = w.shape[1]
    tm, tn = TM_PROJ, TN_PROJ
    npb = x_p.shape[0] // tm
    nsb = x_s.shape[0] // tm
    t = x_p.shape[0] + x_s.shape[0]

    def mod_idx(i):
        return (jnp.where(i < npb, 0, 1 + (i - npb) // blocks_per_seq), 0, 0)

    def tab_idx(i):
        return (jnp.where(i < npb, blocks_per_seq, (i - npb) % blocks_per_seq), 0)

    return pl.pallas_call(
        functools.partial(_proj_even_kernel, n_prompt_blocks=npb),
        grid=(npb + nsb,),
        in_specs=[
            pl.BlockSpec((tm, d), lambda i: (jnp.minimum(i, npb - 1), 0)),
            pl.BlockSpec((tm, d), lambda i: (jnp.maximum(i - npb, 0), 0)),
            pl.BlockSpec((None, 6, d), mod_idx),
            pl.BlockSpec((1, d), lambda i: (0, 0)),
            pl.BlockSpec((d, n), lambda i: (0, 0), pipeline_mode=pl.Buffered(1)),
            pl.BlockSpec((tm, 2 * tn), tab_idx),
            pl.BlockSpec((tm, 2 * tn), tab_idx),
        ],
        out_specs=[
            pl.BlockSpec((tm, n), lambda i: (i, 0)),
            pl.BlockSpec((tm, tn), lambda i: (i, 0)),
        ],
        out_shape=[jax.ShapeDtypeStruct((t, n), BF16), jax.ShapeDtypeStruct((t, tn), F32)],
        scratch_shapes=[pltpu.VMEM((tm, d), BF16)],
        compiler_params=_cparams(("arbitrary",)),
        name="proj_even",
    )(x_p, x_s, mods, gain, w, cos_t, sin_t)


def _rope_tables(dec_seq, tm):
    f32 = np.float32
    n = np.arange(dec_seq)
    row = (n // GRID_W).astype(f32)
    col = (n % GRID_W).astype(f32)
    nf = HEAD_DIM // 4
    inv_freq = np.power(f32(ROPE_BASE), -np.arange(nf, dtype=f32) / f32(nf)).astype(f32)
    ar = row[:, None] * inv_freq[None, :]
    ac = col[:, None] * inv_freq[None, :]
    cos_h = np.concatenate([np.cos(ar), np.cos(ar), np.cos(ac), np.cos(ac)], axis=-1).astype(f32)
    sin_h = np.concatenate([-np.sin(ar), np.sin(ar), -np.sin(ac), np.sin(ac)], axis=-1).astype(f32)
    scale = f32(HEAD_DIM ** -0.5)
    ones = np.ones((dec_seq, KV_WIDTH), f32)
    zeros = np.zeros((dec_seq, KV_WIDTH), f32)
    q_heads = TN_PROJ // HEAD_DIM
    cos_t = np.concatenate([np.tile(cos_h, (1, q_heads)) * scale, np.tile(cos_h, (1, N_KV_HEADS)), ones], axis=-1)
    sin_t = np.concatenate([np.tile(sin_h, (1, q_heads)) * scale, np.tile(sin_h, (1, N_KV_HEADS)), zeros], axis=-1)
    id_cos = np.concatenate([np.full((tm, TN_PROJ), scale, f32), np.ones((tm, TN_PROJ), f32)], axis=-1)
    id_sin = np.zeros((tm, 2 * TN_PROJ), f32)
    return (jnp.asarray(np.concatenate([cos_t, id_cos], axis=0), F32),
            jnp.asarray(np.concatenate([sin_t, id_sin], axis=0), F32))


def _sink_attend(q, keys, vals, masks, sink):
    scores = []
    for k, mask in zip(keys, masks):
        s = lax.dot_general(q, k, (((1,), (1,)), ((), ())), preferred_element_type=F32)
        scores.append(s if mask is None else jnp.where(mask, s, NEG_INF))
    m = sink
    for s in scores:
        m = jnp.maximum(m, jnp.max(s, axis=-1, keepdims=True))
    es = [jnp.exp(s - m) for s in scores]
    den = jnp.exp(sink - m)
    for e in es:
        den = den + jnp.sum(e, axis=-1, keepdims=True)
    inv = 1.0 / den
    out = None
    for e, v in zip(es, vals):
        o = jnp.dot((e * inv).astype(BF16), v, preferred_element_type=F32)
        out = o if out is None else out + o
    return out


def _grouped_heads_attend(sink_ref, q_ref, o_ref, kh, keys, vals, masks):
    rows = q_ref.shape[0]
    heads = [kh * Q_GROUPS + g for g in range(Q_GROUPS)]
    q = jnp.concatenate([q_ref[:, h * HEAD_DIM:(h + 1) * HEAD_DIM] for h in heads], axis=0)
    ridx = lax.broadcasted_iota(jnp.int32, (Q_GROUPS * rows, 1), 0)
    sink = jnp.full((Q_GROUPS * rows, 1), sink_ref[heads[-1]], F32)
    for g in range(Q_GROUPS - 2, -1, -1):
        sink = jnp.where(ridx < (g + 1) * rows, sink_ref[heads[g]], sink)
    o = _sink_attend(q, keys, vals, masks, sink)
    for g, h in enumerate(heads):
        o_ref[:, h * HEAD_DIM:(h + 1) * HEAD_DIM] = o[g * rows:(g + 1) * rows, :].astype(BF16)


def _ctx_attn_kernel(sink_ref, q_ref, kv_ref, o_ref):
    for kh in range(N_KV_HEADS):
        k = kv_ref[:, kh * HEAD_DIM:(kh + 1) * HEAD_DIM]
        v = kv_ref[:, KV_WIDTH + kh * HEAD_DIM:KV_WIDTH + (kh + 1) * HEAD_DIM]
        _grouped_heads_attend(sink_ref, q_ref, o_ref, kh, [k], [v], [None])


def _ctx_attention(proj, sink, batch, seq, kv_col_block):
    return pl.pallas_call(
        _ctx_attn_kernel,
        grid=(batch,),
        in_specs=[
            pl.BlockSpec(memory_space=pltpu.SMEM),
            pl.BlockSpec((seq, ATTN_WIDTH), lambda b: (b, 0)),
            pl.BlockSpec((seq, 2 * KV_WIDTH), lambda b: (b, kv_col_block)),
        ],
        out_specs=pl.BlockSpec((seq, ATTN_WIDTH), lambda b: (b, 0)),
        out_shape=jax.ShapeDtypeStruct((batch * seq, ATTN_WIDTH), BF16),
        compiler_params=_cparams(("arbitrary",)),
        name="ctx_attention",
    )(sink, proj, proj)


def _lat_attn_kernel(sink_ref, q_ref, kv_ref, ck_ref, cv_ref, o_ref, *, seq):
    i = pl.program_id(1)
    win = 3 * ATTN_BLK
    ws = pl.multiple_of(jnp.clip((i - 1) * ATTN_BLK, 0, seq - win), ATTN_BLK)
    shape = (Q_GROUPS * ATTN_BLK, win)
    qpos = i * ATTN_BLK + (lax.broadcasted_iota(jnp.int32, shape, 0) % ATTN_BLK)
    kpos = ws + lax.broadcasted_iota(jnp.int32, shape, 1)
    valid = jnp.abs(qpos - kpos) <= ATTN_BLK
    kvw = kv_ref[pl.ds(ws, win), :]
    for kh in range(N_KV_HEADS):
        k_loc = kvw[:, kh * HEAD_DIM:(kh + 1) * HEAD_DIM]
        v_loc = kvw[:, KV_WIDTH + kh * HEAD_DIM:KV_WIDTH + (kh + 1) * HEAD_DIM]
        k_ctx = ck_ref[:, kh * HEAD_DIM:(kh + 1) * HEAD_DIM]
        v_ctx = cv_ref[:, kh * HEAD_DIM:(kh + 1) * HEAD_DIM]
        _grouped_heads_attend(sink_ref, q_ref, o_ref, kh, [k_ctx, k_loc], [v_ctx, v_loc], [None, valid])


def _lat_attention(proj, sink, cache_k, cache_v, n_prompt, dec_batch, dec_seq, kv_col_block):
    qb = dec_seq // ATTN_BLK
    q_off = n_prompt // ATTN_BLK
    s_off = n_prompt // dec_seq
    past = cache_k.shape[1]
    return pl.pallas_call(
        functools.partial(_lat_attn_kernel, seq=dec_seq),
        grid=(dec_batch, qb),
        in_specs=[
            pl.BlockSpec(memory_space=pltpu.SMEM),
            pl.BlockSpec((ATTN_BLK, ATTN_WIDTH), lambda b, i: (q_off + b * qb + i, 0)),
            pl.BlockSpec((dec_seq, 2 * KV_WIDTH), lambda b, i: (s_off + b, kv_col_block)),
            pl.BlockSpec((None, past, KV_WIDTH), lambda b, i: (b, 0, 0)),
            pl.BlockSpec((None, past, KV_WIDTH), lambda b, i: (b, 0, 0)),
        ],
        out_specs=pl.BlockSpec((ATTN_BLK, ATTN_WIDTH), lambda b, i: (b * qb + i, 0)),
        out_shape=jax.ShapeDtypeStruct((dec_batch * dec_seq, ATTN_WIDTH), BF16),
        compiler_params=_cparams(("arbitrary", "arbitrary")),
        name="lat_attention",
    )(sink, proj, proj, cache_k, cache_v)


def _sgu_kernel(ul_ref, uh_ref, vl_ref, vh_ref, g_ref, w_ref, bt_ref, z_ref):
    half = SGU_WIDTH // 2
    per_half = SGU_GROUPS // 2
    u_refs = (ul_ref, uh_ref)
    for c in range(TM_SGU // CHUNK):
        rows = slice(c * CHUNK, (c + 1) * CHUNK)
        vl = vl_ref[rows, :].astype(F32)
        vh = vh_ref[rows, :].astype(F32)
        ssq = jnp.sum(vl * vl, axis=-1, keepdims=True) + jnp.sum(vh * vh, axis=-1, keepdims=True)
        r = lax.rsqrt(ssq * (1.0 / SGU_WIDTH) + EPS)
        vn = ((vl * r * g_ref[:, :half]).astype(BF16), (vh * r * g_ref[:, half:]).astype(BF16))
        for g in range(SGU_GROUPS):
            src = g // per_half
            cols = slice((g % per_half) * SGU_GROUP_DIM, (g % per_half + 1) * SGU_GROUP_DIM)
            mixed = jnp.dot(w_ref[g], vn[src][:, cols], preferred_element_type=F32) + bt_ref[:, g:g + 1]
            z_ref[rows, g * SGU_GROUP_DIM:(g + 1) * SGU_GROUP_DIM] = (
                u_refs[src][rows, cols].astype(F32) * mixed).astype(BF16)


def _sgu(proj, norm_g, w_s, b_t):
    t = proj.shape[0]
    half = SGU_WIDTH // 2
    u0 = (ATTN_WIDTH + 2 * KV_WIDTH) // half
    blocks = [pl.BlockSpec((TM_SGU, half), functools.partial(lambda i, c: (i, c), c=u0 + k)) for k in range(4)]
    return pl.pallas_call(
        _sgu_kernel,
        grid=(t // TM_SGU,),
        in_specs=blocks + [
            pl.BlockSpec((1, SGU_WIDTH), lambda i: (0, 0)),
            pl.BlockSpec((SGU_GROUPS, CHUNK, CHUNK), lambda i: (0, 0, 0)),
            pl.BlockSpec((CHUNK, SGU_GROUPS), lambda i: (0, 0)),
        ],
        out_specs=pl.BlockSpec((TM_SGU, SGU_WIDTH), lambda i: (i, 0)),
        out_shape=jax.ShapeDtypeStruct((t, SGU_WIDTH), BF16),
        compiler_params=_cparams(("arbitrary",)),
        name="sgu",
    )(proj, proj, proj, proj, norm_g, w_s, b_t)


def _residual_router(x, acc, mod_ref, g2_ref, wrt_ref, brt_ref, x1_ref, h2_ref, lg_ref):
    x1 = x + mod_ref[2:3, :] * acc
    x1_ref[...] = x1
    h2 = _rms_mod(x1, g2_ref[...], mod_ref[4:5, :], mod_ref[3:4, :])
    h2_ref[...] = h2
    lg = lax.dot_general(wrt_ref[...], h2.astype(BF16), (((1,), (1,)), ((), ())), preferred_element_type=F32)
    lg_ref[...] = lg + brt_ref[...]


def _out_even_kernel(xp_ref, xs_ref, ap_ref, as_ref, z_ref, w_ref, mod_ref, g2_ref, wrt_ref, brt_ref,
                     x1_ref, h2_ref, lg_ref, *, n_prompt_blocks):
    is_prompt = pl.program_id(0) < n_prompt_blocks
    a = jnp.where(is_prompt, ap_ref[...].astype(F32), as_ref[...].astype(F32)).astype(BF16)
    acc = jnp.dot(a, w_ref[0:ATTN_WIDTH, :], preferred_element_type=F32)
    acc = acc + jnp.dot(z_ref[...], w_ref[ATTN_WIDTH:, :], preferred_element_type=F32)
    x = jnp.where(is_prompt, xp_ref[...], xs_ref[...])
    _residual_router(x, acc, mod_ref, g2_ref, wrt_ref, brt_ref, x1_ref, h2_ref, lg_ref)


def _out_odd_kernel(x_ref, hc_ref, w_ref, mod_ref, g2_ref, wrt_ref, brt_ref, x1_ref, h2_ref, lg_ref):
    acc = jnp.dot(hc_ref[...], w_ref[...], preferred_element_type=F32)
    _residual_router(x_ref[...], acc, mod_ref, g2_ref, wrt_ref, brt_ref, x1_ref, h2_ref, lg_ref)


def _out_proj(xs, lhs, w, mods, gain2, wrt, brt, n_prompt, dec_seq):
    d = xs[0].shape[1]
    t = sum(x.shape[0] for x in xs)
    tm = TM_OUT
    npb = n_prompt // tm
    bps = dec_seq // tm
    nsb = t // tm - npb
    if len(xs) == 2:
        x_specs = [pl.BlockSpec((tm, d), lambda i: (jnp.minimum(i, npb - 1), 0)),
                   pl.BlockSpec((tm, d), lambda i: (jnp.maximum(i - npb, 0), 0))]
    else:
        x_specs = [pl.BlockSpec((tm, d), lambda i: (i, 0))]

    def mod_idx(i):
        return (jnp.where(i < npb, 0, 1 + (i - npb) // bps), 0, 0)

    if len(lhs) == 3:
        a_p, a_s, z = lhs
        body = functools.partial(_out_even_kernel, n_prompt_blocks=npb)
        lhs_specs = [
            pl.BlockSpec((tm, ATTN_WIDTH), lambda i: (jnp.minimum(i, npb - 1), 0)),
            pl.BlockSpec((tm, ATTN_WIDTH), lambda i: (jnp.clip(i - npb, 0, nsb - 1), 0)),
            pl.BlockSpec((tm, SGU_WIDTH), lambda i: (i, 0)),
        ]
    else:
        body = _out_odd_kernel
        lhs_specs = [pl.BlockSpec((tm, d), lambda i: (i, 0))]
    return pl.pallas_call(
        body,
        grid=(t // tm,),
        in_specs=x_specs + lhs_specs + [
            pl.BlockSpec((w.shape[0], d), lambda i: (0, 0)),
            pl.BlockSpec((None, 6, d), mod_idx),
            pl.BlockSpec((1, d), lambda i: (0, 0)),
            pl.BlockSpec((ROUTER_LANES, d), lambda i: (0, 0)),
            pl.BlockSpec((ROUTER_LANES, 1), lambda i: (0, 0)),
        ],
        out_specs=[
            pl.BlockSpec((tm, d), lambda i: (i, 0)),
            pl.BlockSpec((tm, d), lambda i: (i, 0)),
            pl.BlockSpec((ROUTER_LANES, tm), lambda i: (0, i)),
        ],
        out_shape=[
            jax.ShapeDtypeStruct((t, d), F32),
            jax.ShapeDtypeStruct((t, d), F32),
            jax.ShapeDtypeStruct((ROUTER_LANES, t), F32),
        ],
        compiler_params=_cparams(("arbitrary",)),
        name="out_proj",
    )(*xs, *lhs, w, mods, gain2, wrt, brt)


def _first_argmax4(v0, v1, v2, v3):
    m = jnp.maximum(jnp.maximum(v0, v1), jnp.maximum(v2, v3))
    idx = jnp.where(v0 == m, 0.0, jnp.where(v1 == m, 1.0, jnp.where(v2 == m, 2.0, 3.0)))
    return m, idx


def _route_kernel(lg_ref, pos_ref, ea_ref, eb_ref, nu_ref, ends_ref, oh_scr, rank_scr, *, n_tokens):
    lg = lg_ref[...]
    rows = [lg[r:r + 1, :] for r in range(N_GROUPS_MOE + N_EXPERTS)]
    _, gidx = _first_argmax4(*rows[:N_GROUPS_MOE])
    e = []
    for k in range(EXPERTS_PER_GROUP):
        cand = [rows[N_GROUPS_MOE + g * EXPERTS_PER_GROUP + k] for g in range(N_GROUPS_MOE)]
        e.append(jnp.where(gidx == 0.0, cand[0], jnp.where(gidx == 1.0, cand[1], jnp.where(gidx == 2.0, cand[2], cand[3]))))
    _, l1 = _first_argmax4(*e)
    e2 = [jnp.where(l1 == float(k), -jnp.inf, e[k]) for k in range(EXPERTS_PER_GROUP)]
    _, l2 = _first_argmax4(*e2)
    lo = jnp.minimum(l1, l2)
    hi = jnp.maximum(l1, l2)
    pair = jnp.where(lo == 0.0, hi - 1.0, jnp.where(lo == 1.0, hi + 1.0, 5.0))
    cls = gidx * float(PAIRS_PER_GROUP) + pair

    crow = lax.broadcasted_iota(jnp.int32, (CLASS_ROWS, n_tokens), 0).astype(F32)
    oh_scr[...] = jnp.where(crow == cls, 1.0, 0.0).astype(BF16)

    jj = lax.broadcasted_iota(jnp.int32, (CUM_CHUNK, CUM_CHUNK), 0)
    ii = lax.broadcasted_iota(jnp.int32, (CUM_CHUNK, CUM_CHUNK), 1)
    tri = jnp.where(jj < ii, 1.0, 0.0).astype(BF16)

    def chunk(c, carry):
        off = pl.multiple_of(c * CUM_CHUNK, CUM_CHUNK)
        oh = oh_scr[:, pl.ds(off, CUM_CHUNK)]
        ohf = oh.astype(F32)
        cum = jnp.dot(oh, tri, preferred_element_type=F32) + carry
        rank_scr[:, pl.ds(off, CUM_CHUNK)] = jnp.sum(cum * ohf, axis=0, keepdims=True)
        return carry + jnp.sum(ohf, axis=1, keepdims=True)

    counts = lax.fori_loop(0, n_tokens // CUM_CHUNK, chunk, jnp.zeros((CLASS_ROWS, 1), F32))
    padded = jnp.floor((counts + float(MOE_BLK - 1)) * (1.0 / MOE_BLK)) * float(MOE_BLK)
    run = jnp.zeros((1, 1), F32)
    starts_rows = []
    for c in range(CLASS_ROWS):
        starts_rows.append(run)
        run = run + padded[c:c + 1, :]
    starts = jnp.concatenate(starts_rows, axis=0)
    ends = starts + padded
    total = run

    ohf = oh_scr[...].astype(F32)
    pos = rank_scr[...] + jnp.sum(ohf * starts, axis=0, keepdims=True)
    pos_ref[...] = pos.astype(jnp.int32)

    bstart = lax.broadcasted_iota(jnp.int32, (CLASS_ROWS, MAX_MOE_BLOCKS), 1).astype(F32) * float(MOE_BLK)
    brow = lax.broadcasted_iota(jnp.int32, (CLASS_ROWS, MAX_MOE_BLOCKS), 0)
    done = jnp.where((ends <= bstart) & (brow < N_CLASSES), 1.0, 0.0)
    bcls = jnp.minimum(jnp.sum(done, axis=0, keepdims=True), float(N_CLASSES - 1))
    grp = (jnp.where(bcls >= 6.0, 1.0, 0.0) + jnp.where(bcls >= 12.0, 1.0, 0.0) + jnp.where(bcls >= 18.0, 1.0, 0.0))
    bp = bcls - grp * float(PAIRS_PER_GROUP)
    ge3 = jnp.where(bp >= 3.0, 1.0, 0.0)
    ge5 = jnp.where(bp >= 5.0, 1.0, 0.0)
    blo = ge3 + ge5
    bhi = bp + 1.0 - 2.0 * ge3 - ge5
    ea_ref[...] = (grp * float(EXPERTS_PER_GROUP) + blo).astype(jnp.int32)
    eb_ref[...] = (grp * float(EXPERTS_PER_GROUP) + bhi).astype(jnp.int32)
    nu_ref[...] = jnp.broadcast_to(total * (1.0 / MOE_BLK), (1, MAX_MOE_BLOCKS)).astype(jnp.int32)
    lane = lax.broadcasted_iota(jnp.int32, (1, MAX_MOE_BLOCKS), 1)
    ends_row = jnp.zeros((1, MAX_MOE_BLOCKS), F32)
    for c in range(N_CLASSES):
        ends_row = jnp.where(lane == c, starts_rows[c] + padded[c:c + 1, :], ends_row)
    ends_ref[...] = ends_row.astype(jnp.int32)


def _route(logits_t):
    t = logits_t.shape[1]
    rows = N_CLASSES
    i32 = jnp.int32
    return pl.pallas_call(
        functools.partial(_route_kernel, n_tokens=t),
        grid=(1,),
        in_specs=[pl.BlockSpec((rows, t), lambda i: (0, 0))],
        out_specs=[
            pl.BlockSpec((1, t), lambda i: (0, 0)),
        ] + [pl.BlockSpec((1, MAX_MOE_BLOCKS), lambda i: (0, 0))] * 4,
        out_shape=[jax.ShapeDtypeStruct((1, t), i32)] + [jax.ShapeDtypeStruct((1, MAX_MOE_BLOCKS), i32)] * 4,
        scratch_shapes=[pltpu.VMEM((CLASS_ROWS, t), BF16), pltpu.VMEM((1, t), F32)],
        compiler_params=_cparams(("arbitrary",)),
        name="route",
    )(logits_t)


TOKEN_TILES = D_MODEL // LANES
TOKEN_PITCH = TOKEN_TILES + 4
ROW_UNROLL = 256
DMA_THREADS = 2


def _row_copy(src_ref, src_row, dst_ref, dst_row, sem):
    return pltpu.make_async_copy(src_ref.at[pl.ds(src_row, 1), :], dst_ref.at[pl.ds(dst_row, 1), :], sem)


def _start_row_copies(n_rows, make_copy):
    def body(i, carry):
        for u in range(ROW_UNROLL):
            make_copy(i * ROW_UNROLL + u).start(priority=u % DMA_THREADS)
        return carry

    if n_rows == ROW_UNROLL:
        body(0, 0)
    else:
        lax.fori_loop(0, n_rows // ROW_UNROLL, body, 0)


def _dispatch_kernel(pos_ref, ends_ref, h_ref, o_ref, zbuf, sem, zsem):
    @pl.when(pl.program_id(0) == 0)
    def _():
        zbuf[...] = jnp.zeros_like(zbuf)
        total = ends_ref[N_CLASSES - 1]
        n_free = (o_ref.shape[0] - total) // MOE_BLK

        def zero_block(first_row, phase):
            first = pl.multiple_of(first_row, MOE_BLK)
            getattr(pltpu.make_async_copy(zbuf, o_ref.at[pl.ds(first, MOE_BLK), :], zsem), phase)()

        for phase in ("start", "wait"):
            prev = 0
            for c in range(N_CLASSES):
                end = ends_ref[c]
                pl.when(end > prev)(functools.partial(zero_block, end - MOE_BLK, phase))
                prev = end

            def free_block(k, carry, phase=phase):
                zero_block(total + k * MOE_BLK, phase)
                return carry

            lax.fori_loop(0, n_free, free_block, 0)

    base = pl.program_id(0) * TM_OUT

    _start_row_copies(TM_OUT, lambda r: _row_copy(h_ref, r, o_ref, pos_ref[base + r], sem))
    pltpu.make_async_copy(h_ref, o_ref.at[pl.ds(0, TM_OUT), :], sem).wait()


def _dispatch(pos, ends, h2, n_rows):
    t, d = h2.shape
    return pl.pallas_call(
        _dispatch_kernel,
        grid_spec=pltpu.PrefetchScalarGridSpec(
            num_scalar_prefetch=2,
            grid=(t // TM_OUT,),
            in_specs=[pl.BlockSpec((TM_OUT, d), lambda i, pos, ends: (i, 0))],
            out_specs=pl.BlockSpec(memory_space=pl.ANY),
            scratch_shapes=[pltpu.VMEM((MOE_BLK, d), F32), pltpu.SemaphoreType.DMA(()), pltpu.SemaphoreType.DMA(())],
        ),
        out_shape=jax.ShapeDtypeStruct((n_rows, d), F32),
        compiler_params=_cparams(("arbitrary",)),
        name="moe_dispatch",
    )(pos, ends, h2)


def _expert_ffn(x, wg_ref, wu_ref, wd_ref):
    gate = jnp.dot(x, wg_ref[...], preferred_element_type=F32)
    up = jnp.dot(x, wu_ref[...], preferred_element_type=F32)
    hidden = (gate * jax.nn.sigmoid(gate) * up).astype(BF16)
    return jnp.dot(hidden, wd_ref[...], preferred_element_type=F32)


def _moe_kernel(ea_ref, eb_ref, nu_ref, x_ref, wr_ref, br_ref, wga, wua, wda, wgb, wub, wdb, y_ref):
    b = pl.program_id(0)

    @pl.when(b < nu_ref[0])
    def _():
        x = x_ref[...].astype(BF16)
        lg = jnp.dot(x, wr_ref[...], preferred_element_type=F32) + br_ref[...]
        lane = lax.broadcasted_iota(jnp.int32, lg.shape, 1)
        ea = ea_ref[b]
        eb = eb_ref[b]
        grp = ea // EXPERTS_PER_GROUP
        is_grp = lane < N_GROUPS_MOE
        gm = jnp.max(jnp.where(is_grp, lg, -jnp.inf), axis=-1, keepdims=True)
        ge = jnp.where(is_grp, jnp.exp(lg - gm), 0.0)
        p_g = jnp.sum(jnp.where(lane == grp, ge, 0.0), axis=-1, keepdims=True) / jnp.sum(ge, axis=-1, keepdims=True)
        la = jnp.sum(jnp.where(lane == N_GROUPS_MOE + ea, lg, 0.0), axis=-1, keepdims=True)
        lb = jnp.sum(jnp.where(lane == N_GROUPS_MOE + eb, lg, 0.0), axis=-1, keepdims=True)
        mm = jnp.maximum(la, lb)
        xa = jnp.exp(la - mm)
        xb = jnp.exp(lb - mm)
        ga = xa / (xa + xb) * p_g
        gb = xb / (xa + xb) * p_g
        ya = _expert_ffn(x, wga, wua, wda)
        yb = _expert_ffn(x, wgb, wub, wdb)
        _store_token_slabs(y_ref, ya * ga + yb * gb)

    @pl.when(b >= nu_ref[0])
    def _():
        y_ref[...] = jnp.zeros_like(y_ref)


def _store_token_slabs(slab_ref, rows):
    n = rows.shape[0]
    for c in range(TOKEN_TILES):
        slab_ref[pl.ds(c, n, stride=TOKEN_PITCH), :] = rows[:, c * LANES:(c + 1) * LANES]
    for c in range(TOKEN_TILES, TOKEN_PITCH):
        slab_ref[pl.ds(c, n, stride=TOKEN_PITCH), :] = jnp.zeros((n, LANES), rows.dtype)


def _moe_experts(ea, eb, nused, hs, wr, br, w_gate, w_up, w_down, layer, n_blocks):
    d = hs.shape[1]
    de = w_gate.shape[3]

    def wa_idx(b, ea, eb, nu):
        return (layer, ea[b], 0, 0)

    def wb_idx(b, ea, eb, nu):
        return (layer, eb[b], 0, 0)

    return pl.pallas_call(
        _moe_kernel,
        grid_spec=pltpu.PrefetchScalarGridSpec(
            num_scalar_prefetch=3,
            grid=(n_blocks,),
            in_specs=[
                pl.BlockSpec((MOE_BLK, d), lambda b, ea, eb, nu: (jnp.minimum(b, nu[0] - 1), 0)),
                pl.BlockSpec((d, ROUTER_LANES), lambda b, ea, eb, nu: (0, 0)),
                pl.BlockSpec((1, ROUTER_LANES), lambda b, ea, eb, nu: (0, 0)),
                pl.BlockSpec((None, None, d, de), wa_idx),
                pl.BlockSpec((None, None, d, de), wa_idx),
                pl.BlockSpec((None, None, de, d), wa_idx),
                pl.BlockSpec((None, None, d, de), wb_idx),
                pl.BlockSpec((None, None, d, de), wb_idx),
                pl.BlockSpec((None, None, de, d), wb_idx),
            ],
            out_specs=pl.BlockSpec((MOE_BLK * TOKEN_PITCH, LANES), lambda b, ea, eb, nu: (b, 0)),
        ),
        out_shape=jax.ShapeDtypeStruct((n_blocks * MOE_BLK * TOKEN_PITCH, LANES), F32),
        compiler_params=_cparams(("arbitrary",)),
        name="moe_experts",
    )(ea, eb, nused, hs, wr, br, w_gate, w_up, w_down, w_gate, w_up, w_down)


def _combine_kernel(pos_ref, ys_ref, x1_ref, mod_ref, gf_ref, o_ref, ybuf, sem, *, block_off, final):
    base = (pl.program_id(0) + block_off) * TM_OUT

    def slab_copy(r):
        src = ys_ref.at[pl.ds(pos_ref[base + r] * TOKEN_PITCH, TOKEN_TILES), :]
        return pltpu.make_async_copy(src, ybuf.at[pl.ds(r * TOKEN_PITCH, TOKEN_TILES), :], sem)

    _start_row_copies(TM_OUT, slab_copy)
    whole = pl.ds(0, TM_OUT * TOKEN_TILES)
    pltpu.make_async_copy(ys_ref.at[whole, :], ybuf.at[whole, :], sem).wait()
    for c in range(TOKEN_TILES):
        cols = slice(c * LANES, (c + 1) * LANES)
        y = ybuf[pl.ds(c, TM_OUT, stride=TOKEN_PITCH), :]
        o_ref[:, cols] = x1_ref[:, cols] + mod_ref[5:6, cols] * y
    if final:
        x2 = o_ref[...]
        ms = jnp.mean(x2 * x2, axis=-1, keepdims=True)
        o_ref[...] = x2 * lax.rsqrt(ms + EPS) * gf_ref[...]


def _combine(pos, ys, x1, mods, gain_f, n_prompt, dec_seq, row_off, n_rows, final):
    d = x1.shape[1]
    tm = TM_OUT
    npb = n_prompt // tm
    bps = dec_seq // tm
    boff = row_off // tm

    def mod_idx(i, pos):
        blk = i + boff
        return (jnp.where(blk < npb, 0, 1 + (blk - npb) // bps), 0, 0)

    return pl.pallas_call(
        functools.partial(_combine_kernel, block_off=boff, final=final),
        grid_spec=pltpu.PrefetchScalarGridSpec(
            num_scalar_prefetch=1,
            grid=(n_rows // tm,),
            in_specs=[
                pl.BlockSpec(memory_space=pl.ANY),
                pl.BlockSpec((tm, d), lambda i, pos: (i + boff, 0)),
                pl.BlockSpec((None, 6, d), mod_idx),
                pl.BlockSpec((1, d), lambda i, pos: (0, 0)),
            ],
            out_specs=pl.BlockSpec((tm, d), lambda i, pos: (i, 0)),
            scratch_shapes=[pltpu.VMEM((tm * TOKEN_PITCH, LANES), F32), pltpu.SemaphoreType.DMA(())],
        ),
        out_shape=jax.ShapeDtypeStruct((n_rows, d), F32),
        compiler_params=_cparams(("arbitrary",)),
        name="moe_combine",
    )(pos, ys, x1, mods, gain_f)


def _proj_glu_kernel(x_ref, mod_ref, g_ref, w_ref, o_ref, h_scr):
    h_scr[...] = _rms_mod(x_ref[...], g_ref[...], mod_ref[1:2, :], mod_ref[0:1, :]).astype(BF16)
    width = w_ref.shape[1] // 2
    for j in range(width // TN_PROJ):
        cols = slice(j * TN_PROJ, (j + 1) * TN_PROJ)
        gcols = slice(width + j * TN_PROJ, width + (j + 1) * TN_PROJ)
        val = jnp.dot(h_scr[...], w_ref[:, cols], preferred_element_type=F32)
        gate = jnp.dot(h_scr[...], w_ref[:, gcols], preferred_element_type=F32)
        o_ref[:, cols] = (val * jax.nn.sigmoid(gate)).astype(BF16)


def _proj_glu(x, mods, gain, w, n_prompt, dec_seq):
    t, d = x.shape
    n = w.shape[1]
    tm = TM_PROJ
    npb = n_prompt // tm
    bps = dec_seq // tm

    def mod_idx(i):
        return (jnp.where(i < npb, 0, 1 + (i - npb) // bps), 0, 0)

    return pl.pallas_call(
        _proj_glu_kernel,
        grid=(t // tm,),
        in_specs=[
            pl.BlockSpec((tm, d), lambda i: (i, 0)),
            pl.BlockSpec((None, 6, d), mod_idx),
            pl.BlockSpec((1, d), lambda i: (0, 0)),
            pl.BlockSpec((d, n), lambda i: (0, 0), pipeline_mode=pl.Buffered(1)),
        ],
        out_specs=pl.BlockSpec((tm, n // 2), lambda i: (i, 0)),
        out_shape=jax.ShapeDtypeStruct((t, n // 2), BF16),
        scratch_shapes=[pltpu.VMEM((tm, d), BF16)],
        compiler_params=_cparams(("arbitrary",)),
        name="proj_glu",
    )(x, mods, gain, w)


CONV_ROWS = 64
CONV_COLS = 256
F32_SUBLANES = 8


def _conv_kernel(x_ref, prev_ref, next_ref, dw_ref, b_ref, g_ref, o_ref, xpad, shifted, ybuf, *,
                 n_prompt_blocks, blocks_per_seq):
    i = pl.program_id(0)
    in_sample = i >= n_prompt_blocks
    j = i - n_prompt_blocks
    is_start = jnp.logical_or(jnp.logical_not(in_sample), j % blocks_per_seq == 0)
    is_end = jnp.logical_or(jnp.logical_not(in_sample), j % blocks_per_seq == blocks_per_seq - 1)
    tm = TM_OUT
    xpad[0:CONV_HALO, :] = jnp.where(is_start, 0.0, prev_ref[...].astype(F32))
    xpad[CONV_HALO:CONV_HALO + tm, :] = x_ref[...].astype(F32)
    xpad[CONV_HALO + tm:, :] = jnp.where(is_end, 0.0, next_ref[...].astype(F32))
    first = CONV_HALO - CONV_K // 2
    span = shifted.shape[1]

    def col_chunk(c, ssq):
        c0 = pl.multiple_of(c * CONV_COLS, CONV_COLS)
        cols = pl.ds(c0, CONV_COLS)
        w = dw_ref[:, cols]
        bias = b_ref[:, cols]
        for s in range(1, F32_SUBLANES):
            shifted[s - 1] = xpad[s:s + span, cols]
        parts = []
        for rc in range(tm // CONV_ROWS):
            r0 = rc * CONV_ROWS
            acc = jnp.zeros((CONV_ROWS, CONV_COLS), F32)
            for k in range(CONV_K):
                whole, s = divmod(first + k, F32_SUBLANES)
                base = whole * F32_SUBLANES + r0
                if s == 0:
                    src = xpad[base:base + CONV_ROWS, cols]
                else:
                    src = shifted[s - 1, base:base + CONV_ROWS, :]
                acc = acc + w[k:k + 1, :] * src
            acc = acc + bias
            ybuf[r0:r0 + CONV_ROWS, cols] = acc
            parts.append(jnp.sum(acc * acc, axis=-1, keepdims=True))
        return ssq + jnp.concatenate(parts, axis=0)

    d = x_ref.shape[1]
    ssq = lax.fori_loop(0, d // CONV_COLS, col_chunk, jnp.zeros((tm, 1), F32))
    y = ybuf[...] * lax.rsqrt(ssq * (1.0 / d) + EPS) * g_ref[...]
    o_ref[...] = (y * jax.nn.sigmoid(y)).astype(BF16)


def _conv_mix(xg, dw, dw_b, norm_g, n_prompt, seq, dec_seq):
    t, d = xg.shape
    tm = TM_OUT
    assert seq == tm, "context sequences must be exactly one conv row block"
    npb = n_prompt // tm
    bps = dec_seq // tm
    hb = tm // CONV_HALO
    last = t // CONV_HALO - 1
    return pl.pallas_call(
        functools.partial(_conv_kernel, n_prompt_blocks=npb, blocks_per_seq=bps),
        grid=(t // tm,),
        in_specs=[
            pl.BlockSpec((tm, d), lambda i: (i, 0)),
            pl.BlockSpec((CONV_HALO, d), lambda i: (jnp.maximum(i * hb - 1, 0), 0)),
            pl.BlockSpec((CONV_HALO, d), lambda i: (jnp.minimum((i + 1) * hb, last), 0)),
            pl.BlockSpec((dw.shape[0], d), lambda i: (0, 0)),
            pl.BlockSpec((1, d), lambda i: (0, 0)),
            pl.BlockSpec((1, d), lambda i: (0, 0)),
        ],
        out_specs=pl.BlockSpec((tm, d), lambda i: (i, 0)),
        out_shape=jax.ShapeDtypeStruct((t, d), BF16),
        scratch_shapes=[
            pltpu.VMEM((tm + 2 * CONV_HALO, d), F32),
            pltpu.VMEM((F32_SUBLANES - 1, tm + 2 * CONV_HALO - F32_SUBLANES, CONV_COLS), F32),
            pltpu.VMEM((tm, d), F32),
        ],
        compiler_params=_cparams(("arbitrary",)),
        name="conv_mix",
    )(xg, xg, xg, dw, dw_b, norm_g)


def _router_operands(w_rg, b_rg, w_re, b_re):
    d = w_rg.shape[0]
    used = N_GROUPS_MOE + N_EXPERTS
    w = jnp.concatenate([w_rg, w_re, jnp.zeros((d, ROUTER_LANES - used), F32)], axis=1)
    b = jnp.concatenate([b_rg, b_re, jnp.zeros((ROUTER_LANES - used,), F32)])
    return w.astype(BF16), b.reshape(1, ROUTER_LANES), w.T.astype(BF16), b.reshape(ROUTER_LANES, 1)


def _moe_rows(h2, logits_t, wr, br, w_gate, w_up, w_down, layer):
    t = h2.shape[0]
    n_blocks = -(-(t + N_CLASSES * (MOE_BLK - 1)) // MOE_BLK)
    assert n_blocks <= MAX_MOE_BLOCKS
    pos, ea, eb, nused, ends = _route(logits_t)
    pos = pos.reshape(t)
    hs = _dispatch(pos, ends.reshape(-1), h2, n_blocks * MOE_BLK)
    ys = _moe_experts(ea.reshape(-1), eb.reshape(-1), nused.reshape(-1)[:1], hs, wr, br, w_gate, w_up, w_down,
                      layer, n_blocks)
    return pos, ys


def kernel(x_prompt, x_sample, cache_k, cache_v, c, c_ctx, w_in_even, attn_sink, sgu_norm, sgu_w, sgu_b,
           w_out_even, conv_w_in, conv_dw, conv_dw_b, conv_norm, conv_w_out, ada_w, ada_b, norm_mix, norm_ffn,
           router_group_w, router_group_b, router_expert_w, router_expert_b, expert_w_gate, expert_w_up,
           expert_w_down, final_norm):
    batch, seq, d = x_prompt.shape
    dec_batch, dec_seq, _ = x_sample.shape
    depth = ada_w.shape[0]
    n_prompt = batch * seq
    n_sample = dec_batch * dec_seq
    assert d == D_MODEL and depth == 2 and n_prompt % dec_seq == 0 and dec_seq % TM_PROJ == 0
    x_p = x_prompt.reshape(n_prompt, d)
    x_s = x_sample.reshape(n_sample, d)

    n_mod = -(-(1 + dec_batch) // 8) * 8
    cvec = jnp.concatenate([c_ctx[None, :], c, jnp.zeros((n_mod - 1 - dec_batch, d), F32)], axis=0)
    mods = _ada_params(cvec, ada_w, ada_b).reshape(depth, n_mod, 6, d)
    wg, wu, wd = expert_w_gate.astype(BF16), expert_w_up.astype(BF16), expert_w_down.astype(BF16)

    cos_t, sin_t = _rope_tables(dec_seq, TM_PROJ)
    proj, kv32 = _proj_even(x_p, x_s, mods[0], norm_mix[0][None, :], w_in_even[0].astype(BF16), cos_t, sin_t,
                            dec_seq // TM_PROJ)
    kv_col_block = ATTN_WIDTH // (2 * KV_WIDTH)
    sink = attn_sink[0]
    a_p = _ctx_attention(proj, sink, batch, seq, kv_col_block)
    past = cache_k.shape[2]
    ck = cache_k[:, 0].reshape(dec_batch, past, KV_WIDTH).astype(BF16)
    cv = cache_v[:, 0].reshape(dec_batch, past, KV_WIDTH).astype(BF16)
    a_s = _lat_attention(proj, sink, ck, cv, n_prompt, dec_batch, dec_seq, kv_col_block)
    z = _sgu(proj, sgu_norm[0][None, :], sgu_w[0].astype(BF16), sgu_b[0].T)

    wr0, br0, wrt0, brt0 = _router_operands(router_group_w[0], router_group_b[0], router_expert_w[0], router_expert_b[0])
    x1, h2, lg = _out_proj((x_p, x_s), (a_p, a_s, z), w_out_even[0].astype(BF16), mods[0], norm_ffn[0][None, :],
                           wrt0, brt0, n_prompt, dec_seq)
    pos, ys = _moe_rows(h2, lg, wr0, br0, wg, wu, wd, 0)
    gain_f = final_norm[None, :]
    x = _combine(pos, ys, x1, mods[0], gain_f, n_prompt, dec_seq, 0, n_prompt + n_sample, final=False)

    xg = _proj_glu(x, mods[1], norm_mix[1][None, :], conv_w_in[0].astype(BF16), n_prompt, dec_seq)
    dw = jnp.concatenate([conv_dw[0], jnp.zeros((1, d), F32)], axis=0)
    hc = _conv_mix(xg, dw, conv_dw_b[0][None, :], conv_norm[0][None, :], n_prompt, seq, dec_seq)
    wr1, br1, wrt1, brt1 = _router_operands(router_group_w[1], router_group_b[1], router_expert_w[1], router_expert_b[1])
    x1, h2, lg = _out_proj((x,), (hc,), conv_w_out[0].astype(BF16), mods[1], norm_ffn[1][None, :],
                           wrt1, brt1, n_prompt, dec_seq)
    pos, ys = _moe_rows(h2, lg, wr1, br1, wg, wu, wd, 1)
    y_prompt = _combine(pos, ys, x1, mods[1], gain_f, n_prompt, dec_seq, 0, n_prompt, final=True)
    y_sample = _combine(pos, ys, x1, mods[1], gain_f, n_prompt, dec_seq, n_prompt, n_sample, final=True)

    new_k = kv32[:n_prompt, :KV_WIDTH].reshape(batch, 1, seq, N_KV_HEADS, HEAD_DIM)
    new_v = kv32[:n_prompt, KV_WIDTH:].reshape(batch, 1, seq, N_KV_HEADS, HEAD_DIM)
    return (y_prompt.reshape(batch, seq, d), y_sample.reshape(dec_batch, dec_seq, d), new_k, new_v)
```

```python
import functools

import jax
import jax.numpy as jnp
import numpy as np
from jax import lax
from jax.experimental import pallas as pl
from jax.experimental.pallas import tpu as pltpu

F32 = jnp.float32
BF16 = jnp.bfloat16

D_MODEL = 2048
GRID_W = 64
ATTN_BLK = 128
N_HEADS = 16
N_KV_HEADS = 4
Q_GROUPS = N_HEADS // N_KV_HEADS
HEAD_DIM = 64
ATTN_WIDTH = N_HEADS * HEAD_DIM
KV_WIDTH = N_KV_HEADS * HEAD_DIM
ROPE_BASE = 10000.0
SGU_GROUPS = 4
SGU_WIDTH = D_MODEL // 2
SGU_GROUP_DIM = SGU_WIDTH // SGU_GROUPS
CHUNK = 128
IN_WIDTH_EVEN = ATTN_WIDTH + 2 * KV_WIDTH + 2 * SGU_WIDTH
CONV_K = 31
CONV_HALO = 16
N_GROUPS_MOE = 4
EXPERTS_PER_GROUP = 4
N_EXPERTS = N_GROUPS_MOE * EXPERTS_PER_GROUP
PAIRS_PER_GROUP = 6
N_CLASSES = N_GROUPS_MOE * PAIRS_PER_GROUP
D_EXPERT = 512
MOE_BLK = 256
EPS = 1e-6
NEG_INF = -1e30

LANES = 128
ROUTER_LANES = 128
CLASS_ROWS = 32
MAX_MOE_BLOCKS = 128
TM_PROJ = 512
TN_PROJ = 512
TM_OUT = 256
TM_OUTPROJ = 512
TM_SGU = 512
CUM_CHUNK = 256
VMEM_LIMIT = 56 * 1024 * 1024


def _cparams(sem):
    return pltpu.CompilerParams(dimension_semantics=sem, vmem_limit_bytes=VMEM_LIMIT)


def _rms_mod(x, gain, scale, shift):
    ms = jnp.mean(x * x, axis=-1, keepdims=True)
    y = x * lax.rsqrt(ms + EPS) * gain
    return y * (1.0 + scale) + shift


def _ada_kernel(c_ref, w_ref, b_ref, o_ref):
    c = c_ref[...]
    s = (c * jax.nn.sigmoid(c)).astype(BF16)
    o_ref[...] = jnp.dot(s, w_ref[...].astype(BF16), preferred_element_type=F32) + b_ref[...]


def _ada_params(cvec, ada_w, ada_b):
    depth, d, n = ada_w.shape
    mp = cvec.shape[0]
    tn = 1024
    return pl.pallas_call(
        _ada_kernel,
        grid=(depth, n // tn),
        in_specs=[
            pl.BlockSpec((mp, d), lambda l, j: (0, 0)),
            pl.BlockSpec((None, d, tn), lambda l, j: (l, 0, j)),
            pl.BlockSpec((None, 1, tn), lambda l, j: (l, 0, j)),
        ],
        out_specs=pl.BlockSpec((None, mp, tn), lambda l, j: (l, 0, j)),
        out_shape=jax.ShapeDtypeStruct((depth, mp, n), F32),
        compiler_params=_cparams(("arbitrary", "arbitrary")),
        name="ada_params",
    )(cvec, ada_w, ada_b.reshape(depth, 1, n))


def _swap16(x):
    width = x.shape[-1]
    lane = lax.broadcasted_iota(jnp.int32, x.shape, 1)
    return jnp.where((lane % 32) < 16, pltpu.roll(x, width - 16, 1), pltpu.roll(x, 16, 1))


def _proj_even_kernel(xp_ref, xs_ref, mod_ref, g_ref, w_ref, cos_ref, sin_ref, o_ref, kv_ref, h_scr, *,
                      n_prompt_blocks):
    i = pl.program_id(0)

    def normed(x_ref):
        h_scr[...] = _rms_mod(x_ref[...], g_ref[...], mod_ref[1:2, :], mod_ref[0:1, :]).astype(BF16)

    pl.when(i < n_prompt_blocks)(lambda: normed(xp_ref))
    pl.when(i >= n_prompt_blocks)(lambda: normed(xs_ref))
    kv_tile = ATTN_WIDTH // TN_PROJ
    for j in range(w_ref.shape[1] // TN_PROJ):
        cols = slice(j * TN_PROJ, (j + 1) * TN_PROJ)
        acc = jnp.dot(h_scr[...], w_ref[:, cols], preferred_element_type=F32)
        if j <= kv_tile:
            tab = slice(0, TN_PROJ) if j < kv_tile else slice(TN_PROJ, 2 * TN_PROJ)
            o_ref[:, cols] = (acc * cos_ref[:, tab] + _swap16(acc) * sin_ref[:, tab]).astype(BF16)
            if j == kv_tile:
                kv_ref[...] = acc
        else:
            o_ref[:, cols] = jax.nn.gelu(acc).astype(BF16)


def _proj_even(x_p, x_s, mods, gain, w, cos_t, sin_t, blocks_per_seq):
    d = x_p.shape[1]
    n = w.shape[1]
    tm, tn = TM_PROJ, TN_PROJ
    npb = x_p.shape[0] // tm
    nsb = x_s.shape[0] // tm
    t = x_p.shape[0] + x_s.shape[0]

    def mod_idx(i):
        return (jnp.where(i < npb, 0, 1 + (i - npb) // blocks_per_seq), 0, 0)

    def tab_idx(i):
        return (jnp.where(i < npb, blocks_per_seq, (i - npb) % blocks_per_seq), 0)

    return pl.pallas_call(
        functools.partial(_proj_even_kernel, n_prompt_blocks=npb),
        grid=(npb + nsb,),
        in_specs=[
            pl.BlockSpec((tm, d), lambda i: (jnp.minimum(i, npb - 1), 0)),
            pl.BlockSpec((tm, d), lambda i: (jnp.maximum(i - npb, 0), 0)),
            pl.BlockSpec((None, 6, d), mod_idx),
            pl.BlockSpec((1, d), lambda i: (0, 0)),
            pl.BlockSpec((d, n), lambda i: (0, 0), pipeline_mode=pl.Buffered(1)),
            pl.BlockSpec((tm, 2 * tn), tab_idx),
            pl.BlockSpec((tm, 2 * tn), tab_idx),
        ],
        out_specs=[
            pl.BlockSpec((tm, n), lambda i: (i, 0)),
            pl.BlockSpec((tm, tn), lambda i: (i, 0)),
        ],
        out_shape=[jax.ShapeDtypeStruct((t, n), BF16), jax.ShapeDtypeStruct((t, tn), F32)],
        scratch_shapes=[pltpu.VMEM((tm, d), BF16)],
        compiler_params=_cparams(("arbitrary",)),
        name="proj_even",
    )(x_p, x_s, mods, gain, w, cos_t, sin_t)


def _rope_tables(dec_seq, tm):
    f32 = np.float32
    n = np.arange(dec_seq)
    row = (n // GRID_W).astype(f32)
    col = (n % GRID_W).astype(f32)
    nf = HEAD_DIM // 4
    inv_freq = np.power(f32(ROPE_BASE), -np.arange(nf, dtype=f32) / f32(nf)).astype(f32)
    ar = row[:, None] * inv_freq[None, :]
    ac = col[:, None] * inv_freq[None, :]
    cos_h = np.concatenate([np.cos(ar), np.cos(ar), np.cos(ac), np.cos(ac)], axis=-1).astype(f32)
    sin_h = np.concatenate([-np.sin(ar), np.sin(ar), -np.sin(ac), np.sin(ac)], axis=-1).astype(f32)
    scale = f32(HEAD_DIM ** -0.5)
    ones = np.ones((dec_seq, KV_WIDTH), f32)
    zeros = np.zeros((dec_seq, KV_WIDTH), f32)
    q_heads = TN_PROJ // HEAD_DIM
    cos_t = np.concatenate([np.tile(cos_h, (1, q_heads)) * scale, np.tile(cos_h, (1, N_KV_HEADS)), ones], axis=-1)
    sin_t = np.concatenate([np.tile(sin_h, (1, q_heads)) * scale, np.tile(sin_h, (1, N_KV_HEADS)), zeros], axis=-1)
    id_cos = np.concatenate([np.full((tm, TN_PROJ), scale, f32), np.ones((tm, TN_PROJ), f32)], axis=-1)
    id_sin = np.zeros((tm, 2 * TN_PROJ), f32)
    return (jnp.asarray(np.concatenate([cos_t, id_cos], axis=0), F32),
            jnp.asarray(np.concatenate([sin_t, id_sin], axis=0), F32))


def _sink_attend(q, keys, vals, masks, sink):
    scores = []
    for k, mask in zip(keys, masks):
        s = lax.dot_general(q, k, (((1,), (1,)), ((), ())), preferred_element_type=F32)
        scores.append(s if mask is None else jnp.where(mask, s, NEG_INF))
    m = sink
    for s in scores:
        m = jnp.maximum(m, jnp.max(s, axis=-1, keepdims=True))
    es = [jnp.exp(s - m) for s in scores]
    den = jnp.exp(sink - m)
    for e in es:
        den = den + jnp.sum(e, axis=-1, keepdims=True)
    inv = 1.0 / den
    out = None
    for e, v in zip(es, vals):
        o = jnp.dot((e * inv).astype(BF16), v, preferred_element_type=F32)
        out = o if out is None else out + o
    return out


def _grouped_heads_attend(sink_ref, q_ref, o_ref, kh, keys, vals, masks):
    rows = q_ref.shape[0]
    heads = [kh * Q_GROUPS + g for g in range(Q_GROUPS)]
    q = jnp.concatenate([q_ref[:, h * HEAD_DIM:(h + 1) * HEAD_DIM] for h in heads], axis=0)
    ridx = lax.broadcasted_iota(jnp.int32, (Q_GROUPS * rows, 1), 0)
    sink = jnp.full((Q_GROUPS * rows, 1), sink_ref[heads[-1]], F32)
    for g in range(Q_GROUPS - 2, -1, -1):
        sink = jnp.where(ridx < (g + 1) * rows, sink_ref[heads[g]], sink)
    o = _sink_attend(q, keys, vals, masks, sink)
    for g, h in enumerate(heads):
        o_ref[:, h * HEAD_DIM:(h + 1) * HEAD_DIM] = o[g * rows:(g + 1) * rows, :].astype(BF16)


def _ctx_attn_kernel(sink_ref, q_ref, kv_ref, o_ref):
    for kh in range(N_KV_HEADS):
        k = kv_ref[:, kh * HEAD_DIM:(kh + 1) * HEAD_DIM]
        v = kv_ref[:, KV_WIDTH + kh * HEAD_DIM:KV_WIDTH + (kh + 1) * HEAD_DIM]
        _grouped_heads_attend(sink_ref, q_ref, o_ref, kh, [k], [v], [None])


def _ctx_attention(proj, sink, batch, seq, kv_col_block):
    return pl.pallas_call(
        _ctx_attn_kernel,
        grid=(batch,),
        in_specs=[
            pl.BlockSpec(memory_space=pltpu.SMEM),
            pl.BlockSpec((seq, ATTN_WIDTH), lambda b: (b, 0)),
            pl.BlockSpec((seq, 2 * KV_WIDTH), lambda b: (b, kv_col_block)),
        ],
        out_specs=pl.BlockSpec((seq, ATTN_WIDTH), lambda b: (b, 0)),
        out_shape=jax.ShapeDtypeStruct((batch * seq, ATTN_WIDTH), BF16),
        compiler_params=_cparams(("arbitrary",)),
        name="ctx_attention",
    )(sink, proj, proj)


def _lat_attn_kernel(sink_ref, q_ref, kv_ref, ck_ref, cv_ref, o_ref, *, seq):
    i = pl.program_id(1)
    win = 3 * ATTN_BLK
    ws = pl.multiple_of(jnp.clip((i - 1) * ATTN_BLK, 0, seq - win), ATTN_BLK)
    shape = (Q_GROUPS * ATTN_BLK, win)
    qpos = i * ATTN_BLK + (lax.broadcasted_iota(jnp.int32, shape, 0) % ATTN_BLK)
    kpos = ws + lax.broadcasted_iota(jnp.int32, shape, 1)
    valid = jnp.abs(qpos - kpos) <= ATTN_BLK
    kvw = kv_ref[pl.ds(ws, win), :]
    for kh in range(N_KV_HEADS):
        k_loc = kvw[:, kh * HEAD_DIM:(kh + 1) * HEAD_DIM]
        v_loc = kvw[:, KV_WIDTH + kh * HEAD_DIM:KV_WIDTH + (kh + 1) * HEAD_DIM]
        k_ctx = ck_ref[:, kh * HEAD_DIM:(kh + 1) * HEAD_DIM]
        v_ctx = cv_ref[:, kh * HEAD_DIM:(kh + 1) * HEAD_DIM]
        _grouped_heads_attend(sink_ref, q_ref, o_ref, kh, [k_ctx, k_loc], [v_ctx, v_loc], [None, valid])


def _lat_attention(proj, sink, cache_k, cache_v, n_prompt, dec_batch, dec_seq, kv_col_block):
    qb = dec_seq // ATTN_BLK
    q_off = n_prompt // ATTN_BLK
    s_off = n_prompt // dec_seq
    past = cache_k.shape[1]
    return pl.pallas_call(
        functools.partial(_lat_attn_kernel, seq=dec_seq),
        grid=(dec_batch, qb),
        in_specs=[
            pl.BlockSpec(memory_space=pltpu.SMEM),
            pl.BlockSpec((ATTN_BLK, ATTN_WIDTH), lambda b, i: (q_off + b * qb + i, 0)),
            pl.BlockSpec((dec_seq, 2 * KV_WIDTH), lambda b, i: (s_off + b, kv_col_block)),
            pl.BlockSpec((None, past, KV_WIDTH), lambda b, i: (b, 0, 0)),
            pl.BlockSpec((None, past, KV_WIDTH), lambda b, i: (b, 0, 0)),
        ],
        out_specs=pl.BlockSpec((ATTN_BLK, ATTN_WIDTH), lambda b, i: (b * qb + i, 0)),
        out_shape=jax.ShapeDtypeStruct((dec_batch * dec_seq, ATTN_WIDTH), BF16),
        compiler_params=_cparams(("arbitrary", "arbitrary")),
        name="lat_attention",
    )(sink, proj, proj, cache_k, cache_v)


def _sgu_kernel(ul_ref, uh_ref, vl_ref, vh_ref, g_ref, w_ref, bt_ref, z_ref):
    half = SGU_WIDTH // 2
    per_half = SGU_GROUPS // 2
    u_refs = (ul_ref, uh_ref)
    for c in range(TM_SGU // CHUNK):
        rows = slice(c * CHUNK, (c + 1) * CHUNK)
        vl = vl_ref[rows, :].astype(F32)
        vh = vh_ref[rows, :].astype(F32)
        ssq = jnp.sum(vl * vl, axis=-1, keepdims=True) + jnp.sum(vh * vh, axis=-1, keepdims=True)
        r = lax.rsqrt(ssq * (1.0 / SGU_WIDTH) + EPS)
        vn = ((vl * r * g_ref[:, :half]).astype(BF16), (vh * r * g_ref[:, half:]).astype(BF16))
        for g in range(SGU_GROUPS):
            src = g // per_half
            cols = slice((g % per_half) * SGU_GROUP_DIM, (g % per_half + 1) * SGU_GROUP_DIM)
            mixed = jnp.dot(w_ref[g], vn[src][:, cols], preferred_element_type=F32) + bt_ref[:, g:g + 1]
            z_ref[rows, g * SGU_GROUP_DIM:(g + 1) * SGU_GROUP_DIM] = (
                u_refs[src][rows, cols].astype(F32) * mixed).astype(BF16)


def _sgu(proj, norm_g, w_s, b_t):
    t = proj.shape[0]
    half = SGU_WIDTH // 2
    u0 = (ATTN_WIDTH + 2 * KV_WIDTH) // half
    blocks = [pl.BlockSpec((TM_SGU, half), functools.partial(lambda i, c: (i, c), c=u0 + k)) for k in range(4)]
    return pl.pallas_call(
        _sgu_kernel,
        grid=(t // TM_SGU,),
        in_specs=blocks + [
            pl.BlockSpec((1, SGU_WIDTH), lambda i: (0, 0)),
            pl.BlockSpec((SGU_GROUPS, CHUNK, CHUNK), lambda i: (0, 0, 0)),
            pl.BlockSpec((CHUNK, SGU_GROUPS), lambda i: (0, 0)),
        ],
        out_specs=pl.BlockSpec((TM_SGU, SGU_WIDTH), lambda i: (i, 0)),
        out_shape=jax.ShapeDtypeStruct((t, SGU_WIDTH), BF16),
        compiler_params=_cparams(("arbitrary",)),
        name="sgu",
    )(proj, proj, proj, proj, norm_g, w_s, b_t)


def _residual_router(rows, x, acc, mod_ref, g2_ref, wrt_ref, brt_ref, x1_ref, h2_ref, lg_ref):
    x1 = x + mod_ref[2:3, :] * acc
    x1_ref[rows, :] = x1
    h2 = _rms_mod(x1, g2_ref[...], mod_ref[4:5, :], mod_ref[3:4, :])
    h2_ref[rows, :] = h2
    lg = lax.dot_general(wrt_ref[...], h2.astype(BF16), (((1,), (1,)), ((), ())), preferred_element_type=F32)
    lg_ref[:, rows] = lg + brt_ref[...]


def _sub_blocks(ref):
    return [slice(r, r + TM_OUT) for r in range(0, ref.shape[0], TM_OUT)]


def _out_even_kernel(xp_ref, xs_ref, ap_ref, as_ref, z_ref, w_ref, mod_ref, g2_ref, wrt_ref, brt_ref,
                     x1_ref, h2_ref, lg_ref, *, n_prompt_blocks):
    is_prompt = pl.program_id(0) < n_prompt_blocks
    for rows in _sub_blocks(x1_ref):
        a = jnp.where(is_prompt, ap_ref[rows, :].astype(F32), as_ref[rows, :].astype(F32)).astype(BF16)
        acc = jnp.dot(a, w_ref[0:ATTN_WIDTH, :], preferred_element_type=F32)
        acc = acc + jnp.dot(z_ref[rows, :], w_ref[ATTN_WIDTH:, :], preferred_element_type=F32)
        x = jnp.where(is_prompt, xp_ref[rows, :], xs_ref[rows, :])
        _residual_router(rows, x, acc, mod_ref, g2_ref, wrt_ref, brt_ref, x1_ref, h2_ref, lg_ref)


def _out_odd_kernel(x_ref, hc_ref, w_ref, mod_ref, g2_ref, wrt_ref, brt_ref, x1_ref, h2_ref, lg_ref):
    for rows in _sub_blocks(x1_ref):
        acc = jnp.dot(hc_ref[rows, :], w_ref[...], preferred_element_type=F32)
        _residual_router(rows, x_ref[rows, :], acc, mod_ref, g2_ref, wrt_ref, brt_ref, x1_ref, h2_ref, lg_ref)


def _out_proj(xs, lhs, w, mods, gain2, wrt, brt, n_prompt, dec_seq):
    d = xs[0].shape[1]
    t = sum(x.shape[0] for x in xs)
    tm = TM_OUTPROJ
    npb = n_prompt // tm
    bps = dec_seq // tm
    nsb = t // tm - npb
    if len(xs) == 2:
        x_specs = [pl.BlockSpec((tm, d), lambda i: (jnp.minimum(i, npb - 1), 0)),
                   pl.BlockSpec((tm, d), lambda i: (jnp.maximum(i - npb, 0), 0))]
    else:
        x_specs = [pl.BlockSpec((tm, d), lambda i: (i, 0))]

    def mod_idx(i):
        return (jnp.where(i < npb, 0, 1 + (i - npb) // bps), 0, 0)

    if len(lhs) == 3:
        a_p, a_s, z = lhs
        body = functools.partial(_out_even_kernel, n_prompt_blocks=npb)
        lhs_specs = [
            pl.BlockSpec((tm, ATTN_WIDTH), lambda i: (jnp.minimum(i, npb - 1), 0)),
            pl.BlockSpec((tm, ATTN_WIDTH), lambda i: (jnp.clip(i - npb, 0, nsb - 1), 0)),
            pl.BlockSpec((tm, SGU_WIDTH), lambda i: (i, 0)),
        ]
    else:
        body = _out_odd_kernel
        lhs_specs = [pl.BlockSpec((tm, d), lambda i: (i, 0))]
    return pl.pallas_call(
        body,
        grid=(t // tm,),
        in_specs=x_specs + lhs_specs + [
            pl.BlockSpec((w.shape[0], d), lambda i: (0, 0), pipeline_mode=pl.Buffered(1)),
            pl.BlockSpec((None, 6, d), mod_idx),
            pl.BlockSpec((1, d), lambda i: (0, 0)),
            pl.BlockSpec((ROUTER_LANES, d), lambda i: (0, 0)),
            pl.BlockSpec((ROUTER_LANES, 1), lambda i: (0, 0)),
        ],
        out_specs=[
            pl.BlockSpec((tm, d), lambda i: (i, 0)),
            pl.BlockSpec((tm, d), lambda i: (i, 0)),
            pl.BlockSpec((ROUTER_LANES, tm), lambda i: (0, i)),
        ],
        out_shape=[
            jax.ShapeDtypeStruct((t, d), F32),
            jax.ShapeDtypeStruct((t, d), F32),
            jax.ShapeDtypeStruct((ROUTER_LANES, t), F32),
        ],
        compiler_params=_cparams(("arbitrary",)),
        name="out_proj",
    )(*xs, *lhs, w, mods, gain2, wrt, brt)


def _first_argmax4(v0, v1, v2, v3):
    m = jnp.maximum(jnp.maximum(v0, v1), jnp.maximum(v2, v3))
    idx = jnp.where(v0 == m, 0.0, jnp.where(v1 == m, 1.0, jnp.where(v2 == m, 2.0, 3.0)))
    return m, idx


def _route_kernel(lg_ref, pos_ref, ea_ref, eb_ref, nu_ref, ends_ref, oh_scr, rank_scr, *, n_tokens):
    lg = lg_ref[...]
    rows = [lg[r:r + 1, :] for r in range(N_GROUPS_MOE + N_EXPERTS)]
    _, gidx = _first_argmax4(*rows[:N_GROUPS_MOE])
    e = []
    for k in range(EXPERTS_PER_GROUP):
        cand = [rows[N_GROUPS_MOE + g * EXPERTS_PER_GROUP + k] for g in range(N_GROUPS_MOE)]
        e.append(jnp.where(gidx == 0.0, cand[0], jnp.where(gidx == 1.0, cand[1], jnp.where(gidx == 2.0, cand[2], cand[3]))))
    _, l1 = _first_argmax4(*e)
    e2 = [jnp.where(l1 == float(k), -jnp.inf, e[k]) for k in range(EXPERTS_PER_GROUP)]
    _, l2 = _first_argmax4(*e2)
    lo = jnp.minimum(l1, l2)
    hi = jnp.maximum(l1, l2)
    pair = jnp.where(lo == 0.0, hi - 1.0, jnp.where(lo == 1.0, hi + 1.0, 5.0))
    cls = gidx * float(PAIRS_PER_GROUP) + pair

    crow = lax.broadcasted_iota(jnp.int32, (CLASS_ROWS, n_tokens), 0).astype(F32)
    oh_scr[...] = jnp.where(crow == cls, 1.0, 0.0).astype(BF16)

    jj = lax.broadcasted_iota(jnp.int32, (CUM_CHUNK, CUM_CHUNK), 0)
    ii = lax.broadcasted_iota(jnp.int32, (CUM_CHUNK, CUM_CHUNK), 1)
    tri = jnp.where(jj < ii, 1.0, 0.0).astype(BF16)

    def chunk(c, carry):
        off = pl.multiple_of(c * CUM_CHUNK, CUM_CHUNK)
        oh = oh_scr[:, pl.ds(off, CUM_CHUNK)]
        ohf = oh.astype(F32)
        cum = jnp.dot(oh, tri, preferred_element_type=F32) + carry
        rank_scr[:, pl.ds(off, CUM_CHUNK)] = jnp.sum(cum * ohf, axis=0, keepdims=True)
        return carry + jnp.sum(ohf, axis=1, keepdims=True)

    counts = lax.fori_loop(0, n_tokens // CUM_CHUNK, chunk, jnp.zeros((CLASS_ROWS, 1), F32))
    padded = jnp.floor((counts + float(MOE_BLK - 1)) * (1.0 / MOE_BLK)) * float(MOE_BLK)
    run = jnp.zeros((1, 1), F32)
    starts_rows = []
    for c in range(CLASS_ROWS):
        starts_rows.append(run)
        run = run + padded[c:c + 1, :]
    starts = jnp.concatenate(starts_rows, axis=0)
    ends = starts + padded
    total = run

    ohf = oh_scr[...].astype(F32)
    pos = rank_scr[...] + jnp.sum(ohf * starts, axis=0, keepdims=True)
    pos_ref[...] = pos.astype(jnp.int32)

    bstart = lax.broadcasted_iota(jnp.int32, (CLASS_ROWS, MAX_MOE_BLOCKS), 1).astype(F32) * float(MOE_BLK)
    brow = lax.broadcasted_iota(jnp.int32, (CLASS_ROWS, MAX_MOE_BLOCKS), 0)
    done = jnp.where((ends <= bstart) & (brow < N_CLASSES), 1.0, 0.0)
    bcls = jnp.minimum(jnp.sum(done, axis=0, keepdims=True), float(N_CLASSES - 1))
    grp = (jnp.where(bcls >= 6.0, 1.0, 0.0) + jnp.where(bcls >= 12.0, 1.0, 0.0) + jnp.where(bcls >= 18.0, 1.0, 0.0))
    bp = bcls - grp * float(PAIRS_PER_GROUP)
    ge3 = jnp.where(bp >= 3.0, 1.0, 0.0)
    ge5 = jnp.where(bp >= 5.0, 1.0, 0.0)
    blo = ge3 + ge5
    bhi = bp + 1.0 - 2.0 * ge3 - ge5
    ea_ref[...] = (grp * float(EXPERTS_PER_GROUP) + blo).astype(jnp.int32)
    eb_ref[...] = (grp * float(EXPERTS_PER_GROUP) + bhi).astype(jnp.int32)
    nu_ref[...] = jnp.broadcast_to(total * (1.0 / MOE_BLK), (1, MAX_MOE_BLOCKS)).astype(jnp.int32)
    lane = lax.broadcasted_iota(jnp.int32, (1, MAX_MOE_BLOCKS), 1)
    ends_row = jnp.zeros((1, MAX_MOE_BLOCKS), F32)
    for c in range(N_CLASSES):
        ends_row = jnp.where(lane == c, starts_rows[c] + padded[c:c + 1, :], ends_row)
    ends_ref[...] = ends_row.astype(jnp.int32)


def _route(logits_t):
    t = logits_t.shape[1]
    rows = N_CLASSES
    i32 = jnp.int32
    return pl.pallas_call(
        functools.partial(_route_kernel, n_tokens=t),
        grid=(1,),
        in_specs=[pl.BlockSpec((rows, t), lambda i: (0, 0))],
        out_specs=[
            pl.BlockSpec((1, t), lambda i: (0, 0)),
        ] + [pl.BlockSpec((1, MAX_MOE_BLOCKS), lambda i: (0, 0))] * 4,
        out_shape=[jax.ShapeDtypeStruct((1, t), i32)] + [jax.ShapeDtypeStruct((1, MAX_MOE_BLOCKS), i32)] * 4,
        scratch_shapes=[pltpu.VMEM((CLASS_ROWS, t), BF16), pltpu.VMEM((1, t), F32)],
        compiler_params=_cparams(("arbitrary",)),
        name="route",
    )(logits_t)


ROW_UNROLL = 256
DMA_THREADS = 2


def _row_copy(src_ref, src_row, dst_ref, dst_row, sem):
    return pltpu.make_async_copy(src_ref.at[pl.ds(src_row, 1), :], dst_ref.at[pl.ds(dst_row, 1), :], sem)


def _start_row_copies(n_rows, make_copy):
    def body(i, carry):
        for u in range(ROW_UNROLL):
            make_copy(i * ROW_UNROLL + u).start(priority=u % DMA_THREADS)
        return carry

    if n_rows == ROW_UNROLL:
        body(0, 0)
    else:
        lax.fori_loop(0, n_rows // ROW_UNROLL, body, 0)


def _dispatch_kernel(pos_ref, ends_ref, h_ref, wg_ref, wu_ref, wd_ref, o_ref, wg_out, wu_out, wd_out,
                     zbuf, sem, zsem, *, steps_per_matrix):
    step = pl.program_id(0)

    @pl.when(step == 0)
    def _():
        zbuf[...] = jnp.zeros_like(zbuf)
        total = ends_ref[N_CLASSES - 1]
        n_free = (o_ref.shape[0] - total) // MOE_BLK

        def zero_block(first_row, phase):
            first = pl.multiple_of(first_row, MOE_BLK)
            getattr(pltpu.make_async_copy(zbuf, o_ref.at[pl.ds(first, MOE_BLK), :], zsem), phase)()

        for phase in ("start", "wait"):
            prev = 0
            for c in range(N_CLASSES):
                end = ends_ref[c]
                pl.when(end > prev)(functools.partial(zero_block, end - MOE_BLK, phase))
                prev = end

            def free_block(k, carry, phase=phase):
                zero_block(total + k * MOE_BLK, phase)
                return carry

            lax.fori_loop(0, n_free, free_block, 0)

    base = step * TM_OUT

    _start_row_copies(TM_OUT, lambda r: _row_copy(h_ref, r, o_ref, pos_ref[base + r], sem))
    for k, (src, dst) in enumerate(((wg_ref, wg_out), (wu_ref, wu_out), (wd_ref, wd_out))):
        @pl.when((step >= k * steps_per_matrix) & (step < (k + 1) * steps_per_matrix))
        def _(src=src, dst=dst):
            dst[...] = src[...].astype(BF16)
    pltpu.make_async_copy(h_ref, o_ref.at[pl.ds(0, TM_OUT), :], sem).wait()


def _dispatch(pos, ends, h2, n_rows, w_gate, w_up, w_down, layer):
    t, d = h2.shape
    n_steps = t // TM_OUT
    n_exp = w_gate.shape[1]
    steps_per_matrix = min(n_steps // 3, n_exp)
    assert steps_per_matrix >= 1 and n_exp % steps_per_matrix == 0
    e_blk = n_exp // steps_per_matrix

    def w_spec(k, w):
        def idx(i, pos, ends):
            return (layer, jnp.clip(i - k * steps_per_matrix, 0, steps_per_matrix - 1), 0, 0)
        return pl.BlockSpec((None, e_blk) + w.shape[2:], idx)

    def w_out_spec(k, w):
        def idx(i, pos, ends):
            return (jnp.clip(i - k * steps_per_matrix, 0, steps_per_matrix - 1), 0, 0)
        return pl.BlockSpec((e_blk,) + w.shape[2:], idx)

    weights = (w_gate, w_up, w_down)
    return pl.pallas_call(
        functools.partial(_dispatch_kernel, steps_per_matrix=steps_per_matrix),
        grid_spec=pltpu.PrefetchScalarGridSpec(
            num_scalar_prefetch=2,
            grid=(n_steps,),
            in_specs=[pl.BlockSpec((TM_OUT, d), lambda i, pos, ends: (i, 0))]
            + [w_spec(k, w) for k, w in enumerate(weights)],
            out_specs=[pl.BlockSpec(memory_space=pl.ANY)] + [w_out_spec(k, w) for k, w in enumerate(weights)],
            scratch_shapes=[pltpu.VMEM((MOE_BLK, d), F32), pltpu.SemaphoreType.DMA(()), pltpu.SemaphoreType.DMA(())],
        ),
        out_shape=[jax.ShapeDtypeStruct((n_rows, d), F32)]
        + [jax.ShapeDtypeStruct(w.shape[1:], BF16) for w in weights],
        compiler_params=_cparams(("arbitrary",)),
        name="moe_dispatch",
    )(pos, ends, h2, *weights)


def _expert_ffn(x, wg_ref, wu_ref, wd_ref):
    gate = jnp.dot(x, wg_ref[...], preferred_element_type=F32)
    up = jnp.dot(x, wu_ref[...], preferred_element_type=F32)
    hidden = (gate * jax.nn.sigmoid(gate) * up).astype(BF16)
    return jnp.dot(hidden, wd_ref[...], preferred_element_type=F32)


def _moe_kernel(ea_ref, eb_ref, nu_ref, x_ref, wr_ref, br_ref, wga, wua, wda, wgb, wub, wdb, y_ref):
    b = pl.program_id(0)

    @pl.when(b < nu_ref[0])
    def _():
        x = x_ref[...].astype(BF16)
        lg = jnp.dot(x, wr_ref[...], preferred_element_type=F32) + br_ref[...]
        lane = lax.broadcasted_iota(jnp.int32, lg.shape, 1)
        ea = ea_ref[b]
        eb = eb_ref[b]
        grp = ea // EXPERTS_PER_GROUP
        is_grp = lane < N_GROUPS_MOE
        gm = jnp.max(jnp.where(is_grp, lg, -jnp.inf), axis=-1, keepdims=True)
        ge = jnp.where(is_grp, jnp.exp(lg - gm), 0.0)
        p_g = jnp.sum(jnp.where(lane == grp, ge, 0.0), axis=-1, keepdims=True) / jnp.sum(ge, axis=-1, keepdims=True)
        la = jnp.sum(jnp.where(lane == N_GROUPS_MOE + ea, lg, 0.0), axis=-1, keepdims=True)
        lb = jnp.sum(jnp.where(lane == N_GROUPS_MOE + eb, lg, 0.0), axis=-1, keepdims=True)
        mm = jnp.maximum(la, lb)
        xa = jnp.exp(la - mm)
        xb = jnp.exp(lb - mm)
        ga = xa / (xa + xb) * p_g
        gb = xb / (xa + xb) * p_g
        ya = _expert_ffn(x, wga, wua, wda)
        yb = _expert_ffn(x, wgb, wub, wdb)
        y_ref[...] = ya * ga + yb * gb

    @pl.when(b >= nu_ref[0])
    def _():
        y_ref[...] = jnp.zeros_like(y_ref)


def _moe_experts(ea, eb, nused, hs, wr, br, w_gate, w_up, w_down, n_blocks):
    d = hs.shape[1]
    de = w_gate.shape[2]

    def wa_idx(b, ea, eb, nu):
        return (ea[b], 0, 0)

    def wb_idx(b, ea, eb, nu):
        return (eb[b], 0, 0)

    return pl.pallas_call(
        _moe_kernel,
        grid_spec=pltpu.PrefetchScalarGridSpec(
            num_scalar_prefetch=3,
            grid=(n_blocks,),
            in_specs=[
                pl.BlockSpec((MOE_BLK, d), lambda b, ea, eb, nu: (jnp.minimum(b, nu[0] - 1), 0)),
                pl.BlockSpec((d, ROUTER_LANES), lambda b, ea, eb, nu: (0, 0)),
                pl.BlockSpec((1, ROUTER_LANES), lambda b, ea, eb, nu: (0, 0)),
                pl.BlockSpec((None, d, de), wa_idx),
                pl.BlockSpec((None, d, de), wa_idx),
                pl.BlockSpec((None, de, d), wa_idx),
                pl.BlockSpec((None, d, de), wb_idx),
                pl.BlockSpec((None, d, de), wb_idx),
                pl.BlockSpec((None, de, d), wb_idx),
            ],
            out_specs=pl.BlockSpec((MOE_BLK, d), lambda b, ea, eb, nu: (b, 0)),
        ),
        out_shape=jax.ShapeDtypeStruct((n_blocks * MOE_BLK, d), F32),
        compiler_params=_cparams(("arbitrary",)),
        name="moe_experts",
    )(ea, eb, nused, hs, wr, br, w_gate, w_up, w_down, w_gate, w_up, w_down)


def _combine_kernel(pos_ref, ys_ref, x1_ref, mod_ref, gf_ref, o_ref, ybuf, sem, *, block_off, n_steps, final):
    i = pl.program_id(0)
    slot = i % 2

    def start_gather(step, into):
        base = (step + block_off) * TM_OUT
        _start_row_copies(
            TM_OUT, lambda r: _row_copy(ys_ref, pos_ref[base + r], ybuf.at[into], r, sem.at[into]))

    pl.when(i == 0)(lambda: start_gather(0, 0))
    pl.when(i + 1 < n_steps)(lambda: start_gather(i + 1, 1 - slot))
    pltpu.make_async_copy(ys_ref.at[pl.ds(0, TM_OUT), :], ybuf.at[slot], sem.at[slot]).wait()
    x2 = x1_ref[...] + mod_ref[5:6, :] * ybuf[slot]
    if final:
        ms = jnp.mean(x2 * x2, axis=-1, keepdims=True)
        x2 = x2 * lax.rsqrt(ms + EPS) * gf_ref[...]
    o_ref[...] = x2


def _combine(pos, ys, x1, mods, gain_f, n_prompt, dec_seq, row_off, n_rows, final):
    d = x1.shape[1]
    tm = TM_OUT
    npb = n_prompt // tm
    bps = dec_seq // tm
    boff = row_off // tm

    def mod_idx(i, pos):
        blk = i + boff
        return (jnp.where(blk < npb, 0, 1 + (blk - npb) // bps), 0, 0)

    return pl.pallas_call(
        functools.partial(_combine_kernel, block_off=boff, n_steps=n_rows // tm, final=final),
        grid_spec=pltpu.PrefetchScalarGridSpec(
            num_scalar_prefetch=1,
            grid=(n_rows // tm,),
            in_specs=[
                pl.BlockSpec(memory_space=pl.ANY),
                pl.BlockSpec((tm, d), lambda i, pos: (i + boff, 0)),
                pl.BlockSpec((None, 6, d), mod_idx),
                pl.BlockSpec((1, d), lambda i, pos: (0, 0)),
            ],
            out_specs=pl.BlockSpec((tm, d), lambda i, pos: (i, 0)),
            scratch_shapes=[pltpu.VMEM((2, tm, d), F32), pltpu.SemaphoreType.DMA((2,))],
        ),
        out_shape=jax.ShapeDtypeStruct((n_rows, d), F32),
        compiler_params=_cparams(("arbitrary",)),
        name="moe_combine",
    )(pos, ys, x1, mods, gain_f)


def _proj_glu_kernel(x_ref, mod_ref, g_ref, w_ref, o_ref, h_scr):
    h_scr[...] = _rms_mod(x_ref[...], g_ref[...], mod_ref[1:2, :], mod_ref[0:1, :]).astype(BF16)
    width = w_ref.shape[1] // 2
    for j in range(width // TN_PROJ):
        cols = slice(j * TN_PROJ, (j + 1) * TN_PROJ)
        gcols = slice(width + j * TN_PROJ, width + (j + 1) * TN_PROJ)
        val = jnp.dot(h_scr[...], w_ref[:, cols], preferred_element_type=F32)
        gate = jnp.dot(h_scr[...], w_ref[:, gcols], preferred_element_type=F32)
        o_ref[:, cols] = (val * jax.nn.sigmoid(gate)).astype(BF16)


def _proj_glu(x, mods, gain, w, n_prompt, dec_seq):
    t, d = x.shape
    n = w.shape[1]
    tm = TM_PROJ
    npb = n_prompt // tm
    bps = dec_seq // tm

    def mod_idx(i):
        return (jnp.where(i < npb, 0, 1 + (i - npb) // bps), 0, 0)

    return pl.pallas_call(
        _proj_glu_kernel,
        grid=(t // tm,),
        in_specs=[
            pl.BlockSpec((tm, d), lambda i: (i, 0)),
            pl.BlockSpec((None, 6, d), mod_idx),
            pl.BlockSpec((1, d), lambda i: (0, 0)),
            pl.BlockSpec((d, n), lambda i: (0, 0), pipeline_mode=pl.Buffered(1)),
        ],
        out_specs=pl.BlockSpec((tm, n // 2), lambda i: (i, 0)),
        out_shape=jax.ShapeDtypeStruct((t, n // 2), BF16),
        scratch_shapes=[pltpu.VMEM((tm, d), BF16)],
        compiler_params=_cparams(("arbitrary",)),
        name="proj_glu",
    )(x, mods, gain, w)


CONV_ROWS = 64
CONV_COLS = 256
F32_SUBLANES = 8


def _conv_kernel(x_ref, prev_ref, next_ref, dw_ref, b_ref, g_ref, o_ref, xpad, shifted, ybuf, *,
                 n_prompt_blocks, blocks_per_seq):
    i = pl.program_id(0)
    in_sample = i >= n_prompt_blocks
    j = i - n_prompt_blocks
    is_start = jnp.logical_or(jnp.logical_not(in_sample), j % blocks_per_seq == 0)
    is_end = jnp.logical_or(jnp.logical_not(in_sample), j % blocks_per_seq == blocks_per_seq - 1)
    tm = TM_OUT
    xpad[0:CONV_HALO, :] = jnp.where(is_start, 0.0, prev_ref[...].astype(F32))
    xpad[CONV_HALO:CONV_HALO + tm, :] = x_ref[...].astype(F32)
    xpad[CONV_HALO + tm:, :] = jnp.where(is_end, 0.0, next_ref[...].astype(F32))
    first = CONV_HALO - CONV_K // 2
    span = shifted.shape[1]

    def col_chunk(c, ssq):
        c0 = pl.multiple_of(c * CONV_COLS, CONV_COLS)
        cols = pl.ds(c0, CONV_COLS)
        w = dw_ref[:, cols]
        bias = b_ref[:, cols]
        for s in range(1, F32_SUBLANES):
            shifted[s - 1] = xpad[s:s + span, cols]
        parts = []
        for rc in range(tm // CONV_ROWS):
            r0 = rc * CONV_ROWS
            acc = jnp.zeros((CONV_ROWS, CONV_COLS), F32)
            for k in range(CONV_K):
                whole, s = divmod(first + k, F32_SUBLANES)
                base = whole * F32_SUBLANES + r0
                if s == 0:
                    src = xpad[base:base + CONV_ROWS, cols]
                else:
                    src = shifted[s - 1, base:base + CONV_ROWS, :]
                acc = acc + w[k:k + 1, :] * src
            acc = acc + bias
            ybuf[r0:r0 + CONV_ROWS, cols] = acc
            parts.append(jnp.sum(acc * acc, axis=-1, keepdims=True))
        return ssq + jnp.concatenate(parts, axis=0)

    d = x_ref.shape[1]
    ssq = lax.fori_loop(0, d // CONV_COLS, col_chunk, jnp.zeros((tm, 1), F32))
    y = ybuf[...] * lax.rsqrt(ssq * (1.0 / d) + EPS) * g_ref[...]
    o_ref[...] = (y * jax.nn.sigmoid(y)).astype(BF16)


def _conv_mix(xg, dw, dw_b, norm_g, n_prompt, seq, dec_seq):
    t, d = xg.shape
    tm = TM_OUT
    assert seq == tm, "context sequences must be exactly one conv row block"
    npb = n_prompt // tm
    bps = dec_seq // tm
    hb = tm // CONV_HALO
    last = t // CONV_HALO - 1
    return pl.pallas_call(
        functools.partial(_conv_kernel, n_prompt_blocks=npb, blocks_per_seq=bps),
        grid=(t // tm,),
        in_specs=[
            pl.BlockSpec((tm, d), lambda i: (i, 0)),
            pl.BlockSpec((CONV_HALO, d), lambda i: (jnp.maximum(i * hb - 1, 0), 0)),
            pl.BlockSpec((CONV_HALO, d), lambda i: (jnp.minimum((i + 1) * hb, last), 0)),
            pl.BlockSpec((dw.shape[0], d), lambda i: (0, 0)),
            pl.BlockSpec((1, d), lambda i: (0, 0)),
            pl.BlockSpec((1, d), lambda i: (0, 0)),
        ],
        out_specs=pl.BlockSpec((tm, d), lambda i: (i, 0)),
        out_shape=jax.ShapeDtypeStruct((t, d), BF16),
        scratch_shapes=[
            pltpu.VMEM((tm + 2 * CONV_HALO, d), F32),
            pltpu.VMEM((F32_SUBLANES - 1, tm + 2 * CONV_HALO - F32_SUBLANES, CONV_COLS), F32),
            pltpu.VMEM((tm, d), F32),
        ],
        compiler_params=_cparams(("arbitrary",)),
        name="conv_mix",
    )(xg, xg, xg, dw, dw_b, norm_g)


def _router_operands(w_rg, b_rg, w_re, b_re):
    d = w_rg.shape[0]
    used = N_GROUPS_MOE + N_EXPERTS
    w = jnp.concatenate([w_rg, w_re, jnp.zeros((d, ROUTER_LANES - used), F32)], axis=1)
    b = jnp.concatenate([b_rg, b_re, jnp.zeros((ROUTER_LANES - used,), F32)])
    return w.astype(BF16), b.reshape(1, ROUTER_LANES), w.T.astype(BF16), b.reshape(ROUTER_LANES, 1)


def _moe_rows(h2, logits_t, wr, br, w_gate, w_up, w_down, layer):
    t = h2.shape[0]
    n_blocks = -(-(t + N_CLASSES * (MOE_BLK - 1)) // MOE_BLK)
    assert n_blocks <= MAX_MOE_BLOCKS
    pos, ea, eb, nused, ends = _route(logits_t)
    pos = pos.reshape(t)
    hs, wg, wu, wd = _dispatch(pos, ends.reshape(-1), h2, n_blocks * MOE_BLK, w_gate, w_up, w_down, layer)
    ys = _moe_experts(ea.reshape(-1), eb.reshape(-1), nused.reshape(-1)[:1], hs, wr, br, wg, wu, wd, n_blocks)
    return pos, ys


def kernel(x_prompt, x_sample, cache_k, cache_v, c, c_ctx, w_in_even, attn_sink, sgu_norm, sgu_w, sgu_b,
           w_out_even, conv_w_in, conv_dw, conv_dw_b, conv_norm, conv_w_out, ada_w, ada_b, norm_mix, norm_ffn,
           router_group_w, router_group_b, router_expert_w, router_expert_b, expert_w_gate, expert_w_up,
           expert_w_down, final_norm):
    batch, seq, d = x_prompt.shape
    dec_batch, dec_seq, _ = x_sample.shape
    depth = ada_w.shape[0]
    n_prompt = batch * seq
    n_sample = dec_batch * dec_seq
    assert d == D_MODEL and depth == 2 and n_prompt % dec_seq == 0 and dec_seq % TM_PROJ == 0
    x_p = x_prompt.reshape(n_prompt, d)
    x_s = x_sample.reshape(n_sample, d)

    n_mod = -(-(1 + dec_batch) // 8) * 8
    cvec = jnp.concatenate([c_ctx[None, :], c, jnp.zeros((n_mod - 1 - dec_batch, d), F32)], axis=0)
    mods = _ada_params(cvec, ada_w, ada_b).reshape(depth, n_mod, 6, d)
    wg, wu, wd = expert_w_gate, expert_w_up, expert_w_down

    cos_t, sin_t = _rope_tables(dec_seq, TM_PROJ)
    proj, kv32 = _proj_even(x_p, x_s, mods[0], norm_mix[0][None, :], w_in_even[0].astype(BF16), cos_t, sin_t,
                            dec_seq // TM_PROJ)
    kv_col_block = ATTN_WIDTH // (2 * KV_WIDTH)
    sink = attn_sink[0]
    a_p = _ctx_attention(proj, sink, batch, seq, kv_col_block)
    past = cache_k.shape[2]
    ck = cache_k[:, 0].reshape(dec_batch, past, KV_WIDTH).astype(BF16)
    cv = cache_v[:, 0].reshape(dec_batch, past, KV_WIDTH).astype(BF16)
    a_s = _lat_attention(proj, sink, ck, cv, n_prompt, dec_batch, dec_seq, kv_col_block)
    z = _sgu(proj, sgu_norm[0][None, :], sgu_w[0].astype(BF16), sgu_b[0].T)

    wr0, br0, wrt0, brt0 = _router_operands(router_group_w[0], router_group_b[0], router_expert_w[0], router_expert_b[0])
    x1, h2, lg = _out_proj((x_p, x_s), (a_p, a_s, z), w_out_even[0].astype(BF16), mods[0], norm_ffn[0][None, :],
                           wrt0, brt0, n_prompt, dec_seq)
    pos, ys = _moe_rows(h2, lg, wr0, br0, wg, wu, wd, 0)
    gain_f = final_norm[None, :]
    x = _combine(pos, ys, x1, mods[0], gain_f, n_prompt, dec_seq, 0, n_prompt + n_sample, final=False)

    xg = _proj_glu(x, mods[1], norm_mix[1][None, :], conv_w_in[0].astype(BF16), n_prompt, dec_seq)
    dw = jnp.concatenate([conv_dw[0], jnp.zeros((1, d), F32)], axis=0)
    hc = _conv_mix(xg, dw, conv_dw_b[0][None, :], conv_norm[0][None, :], n_prompt, seq, dec_seq)
    wr1, br1, wrt1, brt1 = _router_operands(router_group_w[1], router_group_b[1], router_expert_w[1], router_expert_b[1])
    x1, h2, lg = _out_proj((x,), (hc,), conv_w_out[0].astype(BF16), mods[1], norm_ffn[1][None, :],
                           wrt1, brt1, n_prompt, dec_seq)
    pos, ys = _moe_rows(h2, lg, wr1, br1, wg, wu, wd, 1)
    y_prompt = _combine(pos, ys, x1, mods[1], gain_f, n_prompt, dec_seq, 0, n_prompt, final=True)
    y_sample = _combine(pos, ys, x1, mods[1], gain_f, n_prompt, dec_seq, n_prompt, n_sample, final=True)

    new_k = kv32[:n_prompt, :KV_WIDTH].reshape(batch, 1, seq, N_KV_HEADS, HEAD_DIM)
    new_v = kv32[:n_prompt, KV_WIDTH:].reshape(batch, 1, seq, N_KV_HEADS, HEAD_DIM)
    return (y_prompt.reshape(batch, seq, d), y_sample.reshape(dec_batch, dec_seq, d), new_k, new_v)
```

```python
import functools

import jax
import jax.numpy as jnp
import numpy as np
from jax import lax
from jax.experimental import pallas as pl
from jax.experimental.pallas import tpu as pltpu

F32 = jnp.float32
BF16 = jnp.bfloat16

D_MODEL = 2048
GRID_W = 64
ATTN_BLK = 128
N_HEADS = 16
N_KV_HEADS = 4
Q_GROUPS = N_HEADS // N_KV_HEADS
HEAD_DIM = 64
ATTN_WIDTH = N_HEADS * HEAD_DIM
KV_WIDTH = N_KV_HEADS * HEAD_DIM
ROPE_BASE = 10000.0
SGU_GROUPS = 4
SGU_WIDTH = D_MODEL // 2
SGU_GROUP_DIM = SGU_WIDTH // SGU_GROUPS
CHUNK = 128
IN_WIDTH_EVEN = ATTN_WIDTH + 2 * KV_WIDTH + 2 * SGU_WIDTH
CONV_K = 31
CONV_HALO = 16
N_GROUPS_MOE = 4
EXPERTS_PER_GROUP = 4
N_EXPERTS = N_GROUPS_MOE * EXPERTS_PER_GROUP
PAIRS_PER_GROUP = 6
N_CLASSES = N_GROUPS_MOE * PAIRS_PER_GROUP
D_EXPERT = 512
MOE_BLK = 256
EPS = 1e-6
NEG_INF = -1e30

LANES = 128
ROUTER_LANES = 128
CLASS_ROWS = 32
MAX_MOE_BLOCKS = 128
TM_PROJ = 512
TN_PROJ = 512
TM_OUT = 256
TM_OUTPROJ = 512
TM_SGU = 512
CUM_CHUNK = 256
VMEM_LIMIT = 56 * 1024 * 1024


def _cparams(sem):
    return pltpu.CompilerParams(dimension_semantics=sem, vmem_limit_bytes=VMEM_LIMIT)


def _pack_bf16_pairs(h):
    w = h.shape[1] // 2
    lo = pltpu.bitcast(h[:, :w].astype(BF16).astype(F32), jnp.uint32) >> 16
    hi = pltpu.bitcast(h[:, w:].astype(BF16).astype(F32), jnp.uint32) & jnp.uint32(0xFFFF0000)
    return hi | lo


def _unpack_bf16_pairs(u):
    lo = pltpu.bitcast(u << 16, F32)
    hi = pltpu.bitcast(u & jnp.uint32(0xFFFF0000), F32)
    return jnp.concatenate([lo, hi], axis=1).astype(BF16)


def _rms_mod(x, gain, scale, shift):
    ms = jnp.mean(x * x, axis=-1, keepdims=True)
    y = x * lax.rsqrt(ms + EPS) * gain
    return y * (1.0 + scale) + shift


def _ada_kernel(c_ref, w_ref, b_ref, o_ref):
    c = c_ref[...]
    s = (c * jax.nn.sigmoid(c)).astype(BF16)
    o_ref[...] = jnp.dot(s, w_ref[...].astype(BF16), preferred_element_type=F32) + b_ref[...]


def _ada_params(cvec, ada_w, ada_b):
    depth, d, n = ada_w.shape
    mp = cvec.shape[0]
    tn = 1024
    return pl.pallas_call(
        _ada_kernel,
        grid=(depth, n // tn),
        in_specs=[
            pl.BlockSpec((mp, d), lambda l, j: (0, 0)),
            pl.BlockSpec((None, d, tn), lambda l, j: (l, 0, j)),
            pl.BlockSpec((None, 1, tn), lambda l, j: (l, 0, j)),
        ],
        out_specs=pl.BlockSpec((None, mp, tn), lambda l, j: (l, 0, j)),
        out_shape=jax.ShapeDtypeStruct((depth, mp, n), F32),
        compiler_params=_cparams(("arbitrary", "arbitrary")),
        name="ada_params",
    )(cvec, ada_w, ada_b.reshape(depth, 1, n))


def _swap16(x):
    width = x.shape[-1]
    lane = lax.broadcasted_iota(jnp.int32, x.shape, 1)
    return jnp.where((lane % 32) < 16, pltpu.roll(x, width - 16, 1), pltpu.roll(x, 16, 1))


def _proj_even_kernel(xp_ref, xs_ref, mod_ref, g_ref, w_ref, cos_ref, sin_ref, o_ref, kv_ref, h_scr, *,
                      n_prompt_blocks):
    x = jnp.where(pl.program_id(0) < n_prompt_blocks, xp_ref[...], xs_ref[...])
    h_scr[...] = _rms_mod(x, g_ref[...], mod_ref[1:2, :], mod_ref[0:1, :]).astype(BF16)
    kv_tile = ATTN_WIDTH // TN_PROJ
    for j in range(w_ref.shape[1] // TN_PROJ):
        cols = slice(j * TN_PROJ, (j + 1) * TN_PROJ)
        acc = jnp.dot(h_scr[...], w_ref[:, cols], preferred_element_type=F32)
        if j <= kv_tile:
            tab = slice(0, TN_PROJ) if j < kv_tile else slice(TN_PROJ, 2 * TN_PROJ)
            o_ref[:, cols] = (acc * cos_ref[:, tab] + _swap16(acc) * sin_ref[:, tab]).astype(BF16)
            if j == kv_tile:
                kv_ref[...] = acc
        else:
            o_ref[:, cols] = jax.nn.gelu(acc).astype(BF16)


def _proj_even(x_p, x_s, mods, gain, w, cos_t, sin_t, blocks_per_seq):
    d = x_p.shape[1]
    n = w.shape[1]
    tm, tn = TM_PROJ, TN_PROJ
    npb = x_p.shape[0] // tm
    nsb = x_s.shape[0] // tm
    t = x_p.shape[0] + x_s.shape[0]

    def mod_idx(i):
        return (jnp.where(i < npb, 0, 1 + (i - npb) // blocks_per_seq), 0, 0)

    def tab_idx(i):
        return (jnp.where(i < npb, blocks_per_seq, (i - npb) % blocks_per_seq), 0)

    return pl.pallas_call(
        functools.partial(_proj_even_kernel, n_prompt_blocks=npb),
        grid=(npb + nsb,),
        in_specs=[
            pl.BlockSpec((tm, d), lambda i: (jnp.minimum(i, npb - 1), 0)),
            pl.BlockSpec((tm, d), lambda i: (jnp.maximum(i - npb, 0), 0)),
            pl.BlockSpec((None, 6, d), mod_idx),
            pl.BlockSpec((1, d), lambda i: (0, 0)),
            pl.BlockSpec((d, n), lambda i: (0, 0), pipeline_mode=pl.Buffered(1)),
            pl.BlockSpec((tm, 2 * tn), tab_idx),
            pl.BlockSpec((tm, 2 * tn), tab_idx),
        ],
        out_specs=[
            pl.BlockSpec((tm, n), lambda i: (i, 0)),
            pl.BlockSpec((tm, tn), lambda i: (i, 0)),
        ],
        out_shape=[jax.ShapeDtypeStruct((t, n), BF16), jax.ShapeDtypeStruct((t, tn), F32)],
        scratch_shapes=[pltpu.VMEM((tm, d), BF16)],
        compiler_params=_cparams(("arbitrary",)),
        name="proj_even",
    )(x_p, x_s, mods, gain, w, cos_t, sin_t)


def _rope_tables(dec_seq, tm):
    f32 = np.float32
    n = np.arange(dec_seq)
    row = (n // GRID_W).astype(f32)
    col = (n % GRID_W).astype(f32)
    nf = HEAD_DIM // 4
    inv_freq = np.power(f32(ROPE_BASE), -np.arange(nf, dtype=f32) / f32(nf)).astype(f32)
    ar = row[:, None] * inv_freq[None, :]
    ac = col[:, None] * inv_freq[None, :]
    cos_h = np.concatenate([np.cos(ar), np.cos(ar), np.cos(ac), np.cos(ac)], axis=-1).astype(f32)
    sin_h = np.concatenate([-np.sin(ar), np.sin(ar), -np.sin(ac), np.sin(ac)], axis=-1).astype(f32)
    scale = f32(HEAD_DIM ** -0.5)
    ones = np.ones((dec_seq, KV_WIDTH), f32)
    zeros = np.zeros((dec_seq, KV_WIDTH), f32)
    q_heads = TN_PROJ // HEAD_DIM
    cos_t = np.concatenate([np.tile(cos_h, (1, q_heads)) * scale, np.tile(cos_h, (1, N_KV_HEADS)), ones], axis=-1)
    sin_t = np.concatenate([np.tile(sin_h, (1, q_heads)) * scale, np.tile(sin_h, (1, N_KV_HEADS)), zeros], axis=-1)
    id_cos = np.concatenate([np.full((tm, TN_PROJ), scale, f32), np.ones((tm, TN_PROJ), f32)], axis=-1)
    id_sin = np.zeros((tm, 2 * TN_PROJ), f32)
    return (jnp.asarray(np.concatenate([cos_t, id_cos], axis=0), F32),
            jnp.asarray(np.concatenate([sin_t, id_sin], axis=0), F32))


def _sink_attend(q, keys, vals, masks, sink):
    scores = []
    for k, mask in zip(keys, masks):
        s = lax.dot_general(q, k, (((1,), (1,)), ((), ())), preferred_element_type=F32)
        scores.append(s if mask is None else jnp.where(mask, s, NEG_INF))
    m = sink
    for s in scores:
        m = jnp.maximum(m, jnp.max(s, axis=-1, keepdims=True))
    es = [jnp.exp(s - m) for s in scores]
    den = jnp.exp(sink - m)
    for e in es:
        den = den + jnp.sum(e, axis=-1, keepdims=True)
    inv = 1.0 / den
    out = None
    for e, v in zip(es, vals):
        o = jnp.dot((e * inv).astype(BF16), v, preferred_element_type=F32)
        out = o if out is None else out + o
    return out


def _grouped_heads_attend(sink_ref, q_ref, o_ref, kh, keys, vals, masks):
    rows = q_ref.shape[0]
    heads = [kh * Q_GROUPS + g for g in range(Q_GROUPS)]
    q = jnp.concatenate([q_ref[:, h * HEAD_DIM:(h + 1) * HEAD_DIM] for h in heads], axis=0)
    ridx = lax.broadcasted_iota(jnp.int32, (Q_GROUPS * rows, 1), 0)
    sink = jnp.full((Q_GROUPS * rows, 1), sink_ref[heads[-1]], F32)
    for g in range(Q_GROUPS - 2, -1, -1):
        sink = jnp.where(ridx < (g + 1) * rows, sink_ref[heads[g]], sink)
    o = _sink_attend(q, keys, vals, masks, sink)
    for g, h in enumerate(heads):
        o_ref[:, h * HEAD_DIM:(h + 1) * HEAD_DIM] = o[g * rows:(g + 1) * rows, :].astype(BF16)


def _ctx_attn_kernel(sink_ref, q_ref, kv_ref, o_ref):
    for kh in range(N_KV_HEADS):
        k = kv_ref[:, kh * HEAD_DIM:(kh + 1) * HEAD_DIM]
        v = kv_ref[:, KV_WIDTH + kh * HEAD_DIM:KV_WIDTH + (kh + 1) * HEAD_DIM]
        _grouped_heads_attend(sink_ref, q_ref, o_ref, kh, [k], [v], [None])


def _ctx_attention(proj, sink, batch, seq, kv_col_block):
    return pl.pallas_call(
        _ctx_attn_kernel,
        grid=(batch,),
        in_specs=[
            pl.BlockSpec(memory_space=pltpu.SMEM),
            pl.BlockSpec((seq, ATTN_WIDTH), lambda b: (b, 0)),
            pl.BlockSpec((seq, 2 * KV_WIDTH), lambda b: (b, kv_col_block)),
        ],
        out_specs=pl.BlockSpec((seq, ATTN_WIDTH), lambda b: (b, 0)),
        out_shape=jax.ShapeDtypeStruct((batch * seq, ATTN_WIDTH), BF16),
        compiler_params=_cparams(("arbitrary",)),
        name="ctx_attention",
    )(sink, proj, proj)


def _lat_attn_kernel(sink_ref, q_ref, kv_ref, ck_ref, cv_ref, o_ref, *, seq):
    i = pl.program_id(1)
    win = 3 * ATTN_BLK
    ws = pl.multiple_of(jnp.clip((i - 1) * ATTN_BLK, 0, seq - win), ATTN_BLK)
    shape = (Q_GROUPS * ATTN_BLK, win)
    qpos = i * ATTN_BLK + (lax.broadcasted_iota(jnp.int32, shape, 0) % ATTN_BLK)
    kpos = ws + lax.broadcasted_iota(jnp.int32, shape, 1)
    valid = jnp.abs(qpos - kpos) <= ATTN_BLK
    kvw = kv_ref[pl.ds(ws, win), :]
    for kh in range(N_KV_HEADS):
        k_loc = kvw[:, kh * HEAD_DIM:(kh + 1) * HEAD_DIM]
        v_loc = kvw[:, KV_WIDTH + kh * HEAD_DIM:KV_WIDTH + (kh + 1) * HEAD_DIM]
        k_ctx = ck_ref[:, kh * HEAD_DIM:(kh + 1) * HEAD_DIM]
        v_ctx = cv_ref[:, kh * HEAD_DIM:(kh + 1) * HEAD_DIM]
        _grouped_heads_attend(sink_ref, q_ref, o_ref, kh, [k_ctx, k_loc], [v_ctx, v_loc], [None, valid])


def _lat_attention(proj, sink, cache_k, cache_v, n_prompt, dec_batch, dec_seq, kv_col_block):
    qb = dec_seq // ATTN_BLK
    q_off = n_prompt // ATTN_BLK
    s_off = n_prompt // dec_seq
    past = cache_k.shape[1]
    return pl.pallas_call(
        functools.partial(_lat_attn_kernel, seq=dec_seq),
        grid=(dec_batch, qb),
        in_specs=[
            pl.BlockSpec(memory_space=pltpu.SMEM),
            pl.BlockSpec((ATTN_BLK, ATTN_WIDTH), lambda b, i: (q_off + b * qb + i, 0)),
            pl.BlockSpec((dec_seq, 2 * KV_WIDTH), lambda b, i: (s_off + b, kv_col_block)),
            pl.BlockSpec((None, past, KV_WIDTH), lambda b, i: (b, 0, 0)),
            pl.BlockSpec((None, past, KV_WIDTH), lambda b, i: (b, 0, 0)),
        ],
        out_specs=pl.BlockSpec((ATTN_BLK, ATTN_WIDTH), lambda b, i: (b * qb + i, 0)),
        out_shape=jax.ShapeDtypeStruct((dec_batch * dec_seq, ATTN_WIDTH), BF16),
        compiler_params=_cparams(("arbitrary", "arbitrary")),
        name="lat_attention",
    )(sink, proj, proj, cache_k, cache_v)


def _sgu_kernel(ul_ref, uh_ref, vl_ref, vh_ref, g_ref, w_ref, bt_ref, z_ref):
    half = SGU_WIDTH // 2
    per_half = SGU_GROUPS // 2
    u_refs = (ul_ref, uh_ref)
    for c in range(TM_SGU // CHUNK):
        rows = slice(c * CHUNK, (c + 1) * CHUNK)
        vl = vl_ref[rows, :].astype(F32)
        vh = vh_ref[rows, :].astype(F32)
        ssq = jnp.sum(vl * vl, axis=-1, keepdims=True) + jnp.sum(vh * vh, axis=-1, keepdims=True)
        r = lax.rsqrt(ssq * (1.0 / SGU_WIDTH) + EPS)
        vn = ((vl * r * g_ref[:, :half]).astype(BF16), (vh * r * g_ref[:, half:]).astype(BF16))
        for g in range(SGU_GROUPS):
            src = g // per_half
            cols = slice((g % per_half) * SGU_GROUP_DIM, (g % per_half + 1) * SGU_GROUP_DIM)
            mixed = jnp.dot(w_ref[g], vn[src][:, cols], preferred_element_type=F32) + bt_ref[:, g:g + 1]
            z_ref[rows, g * SGU_GROUP_DIM:(g + 1) * SGU_GROUP_DIM] = (
                u_refs[src][rows, cols].astype(F32) * mixed).astype(BF16)


def _sgu(proj, norm_g, w_s, b_t):
    t = proj.shape[0]
    half = SGU_WIDTH // 2
    u0 = (ATTN_WIDTH + 2 * KV_WIDTH) // half
    blocks = [pl.BlockSpec((TM_SGU, half), functools.partial(lambda i, c: (i, c), c=u0 + k)) for k in range(4)]
    return pl.pallas_call(
        _sgu_kernel,
        grid=(t // TM_SGU,),
        in_specs=blocks + [
            pl.BlockSpec((1, SGU_WIDTH), lambda i: (0, 0)),
            pl.BlockSpec((SGU_GROUPS, CHUNK, CHUNK), lambda i: (0, 0, 0)),
            pl.BlockSpec((CHUNK, SGU_GROUPS), lambda i: (0, 0)),
        ],
        out_specs=pl.BlockSpec((TM_SGU, SGU_WIDTH), lambda i: (i, 0)),
        out_shape=jax.ShapeDtypeStruct((t, SGU_WIDTH), BF16),
        compiler_params=_cparams(("arbitrary",)),
        name="sgu",
    )(proj, proj, proj, proj, norm_g, w_s, b_t)


def _residual_router(rows, x, acc, mod_ref, g2_ref, wrt_ref, brt_ref, x1_ref, h2_ref, lg_ref):
    x1 = x + mod_ref[2:3, :] * acc
    x1_ref[rows, :] = x1
    h2 = _rms_mod(x1, g2_ref[...], mod_ref[4:5, :], mod_ref[3:4, :])
    h2_ref[rows, :] = _pack_bf16_pairs(h2)
    lg = lax.dot_general(wrt_ref[...], h2.astype(BF16), (((1,), (1,)), ((), ())), preferred_element_type=F32)
    lg_ref[:, rows] = lg + brt_ref[...]


def _sub_blocks(ref):
    return [slice(r, r + TM_OUT) for r in range(0, ref.shape[0], TM_OUT)]


def _out_even_kernel(xp_ref, xs_ref, ap_ref, as_ref, z_ref, w_ref, mod_ref, g2_ref, wrt_ref, brt_ref,
                     x1_ref, h2_ref, lg_ref, *, n_prompt_blocks):
    is_prompt = pl.program_id(0) < n_prompt_blocks
    for rows in _sub_blocks(x1_ref):
        a = jnp.where(is_prompt, ap_ref[rows, :].astype(F32), as_ref[rows, :].astype(F32)).astype(BF16)
        acc = jnp.dot(a, w_ref[0:ATTN_WIDTH, :], preferred_element_type=F32)
        acc = acc + jnp.dot(z_ref[rows, :], w_ref[ATTN_WIDTH:, :], preferred_element_type=F32)
        x = jnp.where(is_prompt, xp_ref[rows, :], xs_ref[rows, :])
        _residual_router(rows, x, acc, mod_ref, g2_ref, wrt_ref, brt_ref, x1_ref, h2_ref, lg_ref)


def _out_odd_kernel(x_ref, hc_ref, w_ref, mod_ref, g2_ref, wrt_ref, brt_ref, x1_ref, h2_ref, lg_ref):
    for rows in _sub_blocks(x1_ref):
        acc = jnp.dot(hc_ref[rows, :], w_ref[...], preferred_element_type=F32)
        _residual_router(rows, x_ref[rows, :], acc, mod_ref, g2_ref, wrt_ref, brt_ref, x1_ref, h2_ref, lg_ref)


def _out_proj(xs, lhs, w, mods, gain2, wrt, brt, n_prompt, dec_seq):
    d = xs[0].shape[1]
    t = sum(x.shape[0] for x in xs)
    tm = TM_OUTPROJ
    npb = n_prompt // tm
    bps = dec_seq // tm
    nsb = t // tm - npb
    if len(xs) == 2:
        x_specs = [pl.BlockSpec((tm, d), lambda i: (jnp.minimum(i, npb - 1), 0)),
                   pl.BlockSpec((tm, d), lambda i: (jnp.maximum(i - npb, 0), 0))]
    else:
        x_specs = [pl.BlockSpec((tm, d), lambda i: (i, 0))]

    def mod_idx(i):
        return (jnp.where(i < npb, 0, 1 + (i - npb) // bps), 0, 0)

    if len(lhs) == 3:
        a_p, a_s, z = lhs
        body = functools.partial(_out_even_kernel, n_prompt_blocks=npb)
        lhs_specs = [
            pl.BlockSpec((tm, ATTN_WIDTH), lambda i: (jnp.minimum(i, npb - 1), 0)),
            pl.BlockSpec((tm, ATTN_WIDTH), lambda i: (jnp.clip(i - npb, 0, nsb - 1), 0)),
            pl.BlockSpec((tm, SGU_WIDTH), lambda i: (i, 0)),
        ]
    else:
        body = _out_odd_kernel
        lhs_specs = [pl.BlockSpec((tm, d), lambda i: (i, 0))]
    return pl.pallas_call(
        body,
        grid=(t // tm,),
        in_specs=x_specs + lhs_specs + [
            pl.BlockSpec((w.shape[0], d), lambda i: (0, 0), pipeline_mode=pl.Buffered(1)),
            pl.BlockSpec((None, 6, d), mod_idx),
            pl.BlockSpec((1, d), lambda i: (0, 0)),
            pl.BlockSpec((ROUTER_LANES, d), lambda i: (0, 0)),
            pl.BlockSpec((ROUTER_LANES, 1), lambda i: (0, 0)),
        ],
        out_specs=[
            pl.BlockSpec((tm, d), lambda i: (i, 0)),
            pl.BlockSpec((tm, d // 2), lambda i: (i, 0)),
            pl.BlockSpec((ROUTER_LANES, tm), lambda i: (0, i)),
        ],
        out_shape=[
            jax.ShapeDtypeStruct((t, d), F32),
            jax.ShapeDtypeStruct((t, d // 2), jnp.uint32),
            jax.ShapeDtypeStruct((ROUTER_LANES, t), F32),
        ],
        compiler_params=_cparams(("arbitrary",)),
        name="out_proj",
    )(*xs, *lhs, w, mods, gain2, wrt, brt)


def _first_argmax4(v0, v1, v2, v3):
    m = jnp.maximum(jnp.maximum(v0, v1), jnp.maximum(v2, v3))
    idx = jnp.where(v0 == m, 0.0, jnp.where(v1 == m, 1.0, jnp.where(v2 == m, 2.0, 3.0)))
    return m, idx


def _route_kernel(lg_ref, pos_ref, ea_ref, eb_ref, nu_ref, ends_ref, oh_scr, rank_scr, *, n_tokens):
    lg = lg_ref[...]
    rows = [lg[r:r + 1, :] for r in range(N_GROUPS_MOE + N_EXPERTS)]
    _, gidx = _first_argmax4(*rows[:N_GROUPS_MOE])
    e = []
    for k in range(EXPERTS_PER_GROUP):
        cand = [rows[N_GROUPS_MOE + g * EXPERTS_PER_GROUP + k] for g in range(N_GROUPS_MOE)]
        e.append(jnp.where(gidx == 0.0, cand[0], jnp.where(gidx == 1.0, cand[1], jnp.where(gidx == 2.0, cand[2], cand[3]))))
    _, l1 = _first_argmax4(*e)
    e2 = [jnp.where(l1 == float(k), -jnp.inf, e[k]) for k in range(EXPERTS_PER_GROUP)]
    _, l2 = _first_argmax4(*e2)
    lo = jnp.minimum(l1, l2)
    hi = jnp.maximum(l1, l2)
    pair = jnp.where(lo == 0.0, hi - 1.0, jnp.where(lo == 1.0, hi + 1.0, 5.0))
    cls = gidx * float(PAIRS_PER_GROUP) + pair

    crow = lax.broadcasted_iota(jnp.int32, (CLASS_ROWS, n_tokens), 0).astype(F32)
    oh_scr[...] = jnp.where(crow == cls, 1.0, 0.0).astype(BF16)

    jj = lax.broadcasted_iota(jnp.int32, (CUM_CHUNK, CUM_CHUNK), 0)
    ii = lax.broadcasted_iota(jnp.int32, (CUM_CHUNK, CUM_CHUNK), 1)
    tri = jnp.where(jj < ii, 1.0, 0.0).astype(BF16)

    def chunk(c, carry):
        off = pl.multiple_of(c * CUM_CHUNK, CUM_CHUNK)
        oh = oh_scr[:, pl.ds(off, CUM_CHUNK)]
        ohf = oh.astype(F32)
        cum = jnp.dot(oh, tri, preferred_element_type=F32) + carry
        rank_scr[:, pl.ds(off, CUM_CHUNK)] = jnp.sum(cum * ohf, axis=0, keepdims=True)
        return carry + jnp.sum(ohf, axis=1, keepdims=True)

    counts = lax.fori_loop(0, n_tokens // CUM_CHUNK, chunk, jnp.zeros((CLASS_ROWS, 1), F32))
    padded = jnp.floor((counts + float(MOE_BLK - 1)) * (1.0 / MOE_BLK)) * float(MOE_BLK)
    run = jnp.zeros((1, 1), F32)
    starts_rows = []
    for c in range(CLASS_ROWS):
        starts_rows.append(run)
        run = run + padded[c:c + 1, :]
    starts = jnp.concatenate(starts_rows, axis=0)
    ends = starts + padded
    total = run

    ohf = oh_scr[...].astype(F32)
    pos = rank_scr[...] + jnp.sum(ohf * starts, axis=0, keepdims=True)
    pos_ref[...] = pos.astype(jnp.int32)

    bstart = lax.broadcasted_iota(jnp.int32, (CLASS_ROWS, MAX_MOE_BLOCKS), 1).astype(F32) * float(MOE_BLK)
    brow = lax.broadcasted_iota(jnp.int32, (CLASS_ROWS, MAX_MOE_BLOCKS), 0)
    done = jnp.where((ends <= bstart) & (brow < N_CLASSES), 1.0, 0.0)
    bcls = jnp.minimum(jnp.sum(done, axis=0, keepdims=True), float(N_CLASSES - 1))
    grp = (jnp.where(bcls >= 6.0, 1.0, 0.0) + jnp.where(bcls >= 12.0, 1.0, 0.0) + jnp.where(bcls >= 18.0, 1.0, 0.0))
    bp = bcls - grp * float(PAIRS_PER_GROUP)
    ge3 = jnp.where(bp >= 3.0, 1.0, 0.0)
    ge5 = jnp.where(bp >= 5.0, 1.0, 0.0)
    blo = ge3 + ge5
    bhi = bp + 1.0 - 2.0 * ge3 - ge5
    ea_ref[...] = (grp * float(EXPERTS_PER_GROUP) + blo).astype(jnp.int32)
    eb_ref[...] = (grp * float(EXPERTS_PER_GROUP) + bhi).astype(jnp.int32)
    nu_ref[...] = jnp.broadcast_to(total * (1.0 / MOE_BLK), (1, MAX_MOE_BLOCKS)).astype(jnp.int32)
    lane = lax.broadcasted_iota(jnp.int32, (1, MAX_MOE_BLOCKS), 1)
    ends_row = jnp.zeros((1, MAX_MOE_BLOCKS), F32)
    for c in range(N_CLASSES):
        ends_row = jnp.where(lane == c, starts_rows[c] + padded[c:c + 1, :], ends_row)
    ends_ref[...] = ends_row.astype(jnp.int32)


def _route(logits_t):
    t = logits_t.shape[1]
    rows = N_CLASSES
    i32 = jnp.int32
    return pl.pallas_call(
        functools.partial(_route_kernel, n_tokens=t),
        grid=(1,),
        in_specs=[pl.BlockSpec((rows, t), lambda i: (0, 0))],
        out_specs=[
            pl.BlockSpec((1, t), lambda i: (0, 0)),
        ] + [pl.BlockSpec((1, MAX_MOE_BLOCKS), lambda i: (0, 0))] * 4,
        out_shape=[jax.ShapeDtypeStruct((1, t), i32)] + [jax.ShapeDtypeStruct((1, MAX_MOE_BLOCKS), i32)] * 4,
        scratch_shapes=[pltpu.VMEM((CLASS_ROWS, t), BF16), pltpu.VMEM((1, t), F32)],
        compiler_params=_cparams(("arbitrary",)),
        name="route",
    )(logits_t)


ROW_UNROLL = 256
DMA_THREADS = 2


def _row_copy(src_ref, src_row, dst_ref, dst_row, sem):
    return pltpu.make_async_copy(src_ref.at[pl.ds(src_row, 1), :], dst_ref.at[pl.ds(dst_row, 1), :], sem)


def _start_row_copies(n_rows, make_copy):
    def body(i, carry):
        for u in range(ROW_UNROLL):
            make_copy(i * ROW_UNROLL + u).start(priority=u % DMA_THREADS)
        return carry

    if n_rows == ROW_UNROLL:
        body(0, 0)
    else:
        lax.fori_loop(0, n_rows // ROW_UNROLL, body, 0)


def _dispatch_kernel(pos_ref, ends_ref, h_ref, wg_ref, wu_ref, wd_ref, o_ref, wg_out, wu_out, wd_out,
                     zbuf, sem, zsem, *, steps_per_matrix):
    step = pl.program_id(0)

    @pl.when(step == 0)
    def _():
        zbuf[...] = jnp.zeros_like(zbuf)
        total = ends_ref[N_CLASSES - 1]
        n_free = (o_ref.shape[0] - total) // MOE_BLK

        def zero_block(first_row, phase):
            first = pl.multiple_of(first_row, MOE_BLK)
            getattr(pltpu.make_async_copy(zbuf, o_ref.at[pl.ds(first, MOE_BLK), :], zsem), phase)()

        for phase in ("start", "wait"):
            prev = 0
            for c in range(N_CLASSES):
                end = ends_ref[c]
                pl.when(end > prev)(functools.partial(zero_block, end - MOE_BLK, phase))
                prev = end

            def free_block(k, carry, phase=phase):
                zero_block(total + k * MOE_BLK, phase)
                return carry

            lax.fori_loop(0, n_free, free_block, 0)

    base = step * TM_OUT

    _start_row_copies(TM_OUT, lambda r: _row_copy(h_ref, r, o_ref, pos_ref[base + r], sem))
    for k, (src, dst) in enumerate(((wg_ref, wg_out), (wu_ref, wu_out), (wd_ref, wd_out))):
        @pl.when((step >= k * steps_per_matrix) & (step < (k + 1) * steps_per_matrix))
        def _(src=src, dst=dst):
            dst[...] = src[...].astype(BF16)
    pltpu.make_async_copy(h_ref, o_ref.at[pl.ds(0, TM_OUT), :], sem).wait()


def _dispatch(pos, ends, h2, n_rows, w_gate, w_up, w_down, layer):
    t, d = h2.shape
    n_steps = t // TM_OUT
    n_exp = w_gate.shape[1]
    steps_per_matrix = min(n_steps // 3, n_exp)
    assert steps_per_matrix >= 1 and n_exp % steps_per_matrix == 0
    e_blk = n_exp // steps_per_matrix

    def w_spec(k, w):
        def idx(i, pos, ends):
            return (layer, jnp.clip(i - k * steps_per_matrix, 0, steps_per_matrix - 1), 0, 0)
        return pl.BlockSpec((None, e_blk) + w.shape[2:], idx)

    def w_out_spec(k, w):
        def idx(i, pos, ends):
            return (jnp.clip(i - k * steps_per_matrix, 0, steps_per_matrix - 1), 0, 0)
        return pl.BlockSpec((e_blk,) + w.shape[2:], idx)

    weights = (w_gate, w_up, w_down)
    return pl.pallas_call(
        functools.partial(_dispatch_kernel, steps_per_matrix=steps_per_matrix),
        grid_spec=pltpu.PrefetchScalarGridSpec(
            num_scalar_prefetch=2,
            grid=(n_steps,),
            in_specs=[pl.BlockSpec((TM_OUT, d), lambda i, pos, ends: (i, 0))]
            + [w_spec(k, w) for k, w in enumerate(weights)],
            out_specs=[pl.BlockSpec(memory_space=pl.ANY)] + [w_out_spec(k, w) for k, w in enumerate(weights)],
            scratch_shapes=[pltpu.VMEM((MOE_BLK, d), h2.dtype), pltpu.SemaphoreType.DMA(()),
                            pltpu.SemaphoreType.DMA(())],
        ),
        out_shape=[jax.ShapeDtypeStruct((n_rows, d), h2.dtype)]
        + [jax.ShapeDtypeStruct(w.shape[1:], BF16) for w in weights],
        compiler_params=_cparams(("arbitrary",)),
        name="moe_dispatch",
    )(pos, ends, h2, *weights)


def _expert_ffn(x, wg_ref, wu_ref, wd_ref):
    gate = jnp.dot(x, wg_ref[...], preferred_element_type=F32)
    up = jnp.dot(x, wu_ref[...], preferred_element_type=F32)
    hidden = (gate * jax.nn.sigmoid(gate) * up).astype(BF16)
    return jnp.dot(hidden, wd_ref[...], preferred_element_type=F32)


def _moe_kernel(ea_ref, eb_ref, nu_ref, x_ref, wr_ref, br_ref, wga, wua, wda, wgb, wub, wdb, y_ref):
    b = pl.program_id(0)

    @pl.when(b < nu_ref[0])
    def _():
        x = _unpack_bf16_pairs(x_ref[...])
        lg = jnp.dot(x, wr_ref[...], preferred_element_type=F32) + br_ref[...]
        lane = lax.broadcasted_iota(jnp.int32, lg.shape, 1)
        ea = ea_ref[b]
        eb = eb_ref[b]
        grp = ea // EXPERTS_PER_GROUP
        is_grp = lane < N_GROUPS_MOE
        gm = jnp.max(jnp.where(is_grp, lg, -jnp.inf), axis=-1, keepdims=True)
        ge = jnp.where(is_grp, jnp.exp(lg - gm), 0.0)
        p_g = jnp.sum(jnp.where(lane == grp, ge, 0.0), axis=-1, keepdims=True) / jnp.sum(ge, axis=-1, keepdims=True)
        la = jnp.sum(jnp.where(lane == N_GROUPS_MOE + ea, lg, 0.0), axis=-1, keepdims=True)
        lb = jnp.sum(jnp.where(lane == N_GROUPS_MOE + eb, lg, 0.0), axis=-1, keepdims=True)
        mm = jnp.maximum(la, lb)
        xa = jnp.exp(la - mm)
        xb = jnp.exp(lb - mm)
        ga = xa / (xa + xb) * p_g
        gb = xb / (xa + xb) * p_g
        ya = _expert_ffn(x, wga, wua, wda)
        yb = _expert_ffn(x, wgb, wub, wdb)
        y_ref[...] = ya * ga + yb * gb

    @pl.when(b >= nu_ref[0])
    def _():
        y_ref[...] = jnp.zeros_like(y_ref)


def _moe_experts(ea, eb, nused, hs, wr, br, w_gate, w_up, w_down, n_blocks):
    d = w_gate.shape[1]
    de = w_gate.shape[2]

    def wa_idx(b, ea, eb, nu):
        return (ea[b], 0, 0)

    def wb_idx(b, ea, eb, nu):
        return (eb[b], 0, 0)

    return pl.pallas_call(
        _moe_kernel,
        grid_spec=pltpu.PrefetchScalarGridSpec(
            num_scalar_prefetch=3,
            grid=(n_blocks,),
            in_specs=[
                pl.BlockSpec((MOE_BLK, hs.shape[1]), lambda b, ea, eb, nu: (jnp.minimum(b, nu[0] - 1), 0)),
                pl.BlockSpec((d, ROUTER_LANES), lambda b, ea, eb, nu: (0, 0)),
                pl.BlockSpec((1, ROUTER_LANES), lambda b, ea, eb, nu: (0, 0)),
                pl.BlockSpec((None, d, de), wa_idx),
                pl.BlockSpec((None, d, de), wa_idx),
                pl.BlockSpec((None, de, d), wa_idx),
                pl.BlockSpec((None, d, de), wb_idx),
                pl.BlockSpec((None, d, de), wb_idx),
                pl.BlockSpec((None, de, d), wb_idx),
            ],
            out_specs=pl.BlockSpec((MOE_BLK, d), lambda b, ea, eb, nu: (b, 0)),
        ),
        out_shape=jax.ShapeDtypeStruct((n_blocks * MOE_BLK, d), F32),
        compiler_params=_cparams(("arbitrary",)),
        name="moe_experts",
    )(ea, eb, nused, hs, wr, br, w_gate, w_up, w_down, w_gate, w_up, w_down)


def _combine_kernel(pos_ref, ys_ref, x1_ref, mod_ref, gf_ref, o_ref, ybuf, sem, *, block_off, n_steps, final):
    i = pl.program_id(0)
    slot = i % 2

    def start_gather(step, into):
        base = (step + block_off) * TM_OUT
        _start_row_copies(
            TM_OUT, lambda r: _row_copy(ys_ref, pos_ref[base + r], ybuf.at[into], r, sem.at[into]))

    pl.when(i == 0)(lambda: start_gather(0, 0))
    pl.when(i + 1 < n_steps)(lambda: start_gather(i + 1, 1 - slot))
    pltpu.make_async_copy(ys_ref.at[pl.ds(0, TM_OUT), :], ybuf.at[slot], sem.at[slot]).wait()
    x2 = x1_ref[...] + mod_ref[5:6, :] * ybuf[slot]
    if final:
        ms = jnp.mean(x2 * x2, axis=-1, keepdims=True)
        x2 = x2 * lax.rsqrt(ms + EPS) * gf_ref[...]
    o_ref[...] = x2


def _combine(pos, ys, x1, mods, gain_f, n_prompt, dec_seq, row_off, n_rows, final):
    d = x1.shape[1]
    tm = TM_OUT
    npb = n_prompt // tm
    bps = dec_seq // tm
    boff = row_off // tm

    def mod_idx(i, pos):
        blk = i + boff
        return (jnp.where(blk < npb, 0, 1 + (blk - npb) // bps), 0, 0)

    return pl.pallas_call(
        functools.partial(_combine_kernel, block_off=boff, n_steps=n_rows // tm, final=final),
        grid_spec=pltpu.PrefetchScalarGridSpec(
            num_scalar_prefetch=1,
            grid=(n_rows // tm,),
            in_specs=[
                pl.BlockSpec(memory_space=pl.ANY),
                pl.BlockSpec((tm, d), lambda i, pos: (i + boff, 0)),
                pl.BlockSpec((None, 6, d), mod_idx),
                pl.BlockSpec((1, d), lambda i, pos: (0, 0)),
            ],
            out_specs=pl.BlockSpec((tm, d), lambda i, pos: (i, 0)),
            scratch_shapes=[pltpu.VMEM((2, tm, d), F32), pltpu.SemaphoreType.DMA((2,))],
        ),
        out_shape=jax.ShapeDtypeStruct((n_rows, d), F32),
        compiler_params=_cparams(("arbitrary",)),
        name="moe_combine",
    )(pos, ys, x1, mods, gain_f)


def _proj_glu_kernel(x_ref, mod_ref, g_ref, w_ref, o_ref, h_scr):
    h_scr[...] = _rms_mod(x_ref[...], g_ref[...], mod_ref[1:2, :], mod_ref[0:1, :]).astype(BF16)
    width = w_ref.shape[1] // 2
    for j in range(width // TN_PROJ):
        cols = slice(j * TN_PROJ, (j + 1) * TN_PROJ)
        gcols = slice(width + j * TN_PROJ, width + (j + 1) * TN_PROJ)
        val = jnp.dot(h_scr[...], w_ref[:, cols], preferred_element_type=F32)
        gate = jnp.dot(h_scr[...], w_ref[:, gcols], preferred_element_type=F32)
        o_ref[:, cols] = (val * jax.nn.sigmoid(gate)).astype(BF16)


def _proj_glu(x, mods, gain, w, n_prompt, dec_seq):
    t, d = x.shape
    n = w.shape[1]
    tm = TM_PROJ
    npb = n_prompt // tm
    bps = dec_seq // tm

    def mod_idx(i):
        return (jnp.where(i < npb, 0, 1 + (i - npb) // bps), 0, 0)

    return pl.pallas_call(
        _proj_glu_kernel,
        grid=(t // tm,),
        in_specs=[
            pl.BlockSpec((tm, d), lambda i: (i, 0)),
            pl.BlockSpec((None, 6, d), mod_idx),
            pl.BlockSpec((1, d), lambda i: (0, 0)),
            pl.BlockSpec((d, n), lambda i: (0, 0), pipeline_mode=pl.Buffered(1)),
        ],
        out_specs=pl.BlockSpec((tm, n // 2), lambda i: (i, 0)),
        out_shape=jax.ShapeDtypeStruct((t, n // 2), BF16),
        scratch_shapes=[pltpu.VMEM((tm, d), BF16)],
        compiler_params=_cparams(("arbitrary",)),
        name="proj_glu",
    )(x, mods, gain, w)


CONV_ROWS = 64
CONV_COLS = 256
F32_SUBLANES = 8


def _conv_kernel(x_ref, prev_ref, next_ref, dw_ref, b_ref, g_ref, o_ref, xpad, shifted, ybuf, *,
                 n_prompt_blocks, blocks_per_seq):
    i = pl.program_id(0)
    in_sample = i >= n_prompt_blocks
    j = i - n_prompt_blocks
    is_start = jnp.logical_or(jnp.logical_not(in_sample), j % blocks_per_seq == 0)
    is_end = jnp.logical_or(jnp.logical_not(in_sample), j % blocks_per_seq == blocks_per_seq - 1)
    tm = TM_OUT
    xpad[0:CONV_HALO, :] = jnp.where(is_start, 0.0, prev_ref[...].astype(F32))
    xpad[CONV_HALO:CONV_HALO + tm, :] = x_ref[...].astype(F32)
    xpad[CONV_HALO + tm:, :] = jnp.where(is_end, 0.0, next_ref[...].astype(F32))
    first = CONV_HALO - CONV_K // 2
    span = shifted.shape[1]

    def col_chunk(c, ssq):
        c0 = pl.multiple_of(c * CONV_COLS, CONV_COLS)
        cols = pl.ds(c0, CONV_COLS)
        w = dw_ref[:, cols]
        bias = b_ref[:, cols]
        for s in range(1, F32_SUBLANES):
            shifted[s - 1] = xpad[s:s + span, cols]
        parts = []
        for rc in range(tm // CONV_ROWS):
            r0 = rc * CONV_ROWS
            acc = jnp.zeros((CONV_ROWS, CONV_COLS), F32)
            for k in range(CONV_K):
                whole, s = divmod(first + k, F32_SUBLANES)
                base = whole * F32_SUBLANES + r0
                if s == 0:
                    src = xpad[base:base + CONV_ROWS, cols]
                else:
                    src = shifted[s - 1, base:base + CONV_ROWS, :]
                acc = acc + w[k:k + 1, :] * src
            acc = acc + bias
            ybuf[r0:r0 + CONV_ROWS, cols] = acc
            parts.append(jnp.sum(acc * acc, axis=-1, keepdims=True))
        return ssq + jnp.concatenate(parts, axis=0)

    d = x_ref.shape[1]
    ssq = lax.fori_loop(0, d // CONV_COLS, col_chunk, jnp.zeros((tm, 1), F32))
    y = ybuf[...] * lax.rsqrt(ssq * (1.0 / d) + EPS) * g_ref[...]
    o_ref[...] = (y * jax.nn.sigmoid(y)).astype(BF16)


def _conv_mix(xg, dw, dw_b, norm_g, n_prompt, seq, dec_seq):
    t, d = xg.shape
    tm = TM_OUT
    assert seq == tm, "context sequences must be exactly one conv row block"
    npb = n_prompt // tm
    bps = dec_seq // tm
    hb = tm // CONV_HALO
    last = t // CONV_HALO - 1
    return pl.pallas_call(
        functools.partial(_conv_kernel, n_prompt_blocks=npb, blocks_per_seq=bps),
        grid=(t // tm,),
        in_specs=[
            pl.BlockSpec((tm, d), lambda i: (i, 0)),
            pl.BlockSpec((CONV_HALO, d), lambda i: (jnp.maximum(i * hb - 1, 0), 0)),
            pl.BlockSpec((CONV_HALO, d), lambda i: (jnp.minimum((i + 1) * hb, last), 0)),
            pl.BlockSpec((dw.shape[0], d), lambda i: (0, 0)),
            pl.BlockSpec((1, d), lambda i: (0, 0)),
            pl.BlockSpec((1, d), lambda i: (0, 0)),
        ],
        out_specs=pl.BlockSpec((tm, d), lambda i: (i, 0)),
        out_shape=jax.ShapeDtypeStruct((t, d), BF16),
        scratch_shapes=[
            pltpu.VMEM((tm + 2 * CONV_HALO, d), F32),
            pltpu.VMEM((F32_SUBLANES - 1, tm + 2 * CONV_HALO - F32_SUBLANES, CONV_COLS), F32),
            pltpu.VMEM((tm, d), F32),
        ],
        compiler_params=_cparams(("arbitrary",)),
        name="conv_mix",
    )(xg, xg, xg, dw, dw_b, norm_g)


def _router_operands(w_rg, b_rg, w_re, b_re):
    d = w_rg.shape[0]
    used = N_GROUPS_MOE + N_EXPERTS
    w = jnp.concatenate([w_rg, w_re, jnp.zeros((d, ROUTER_LANES - used), F32)], axis=1)
    b = jnp.concatenate([b_rg, b_re, jnp.zeros((ROUTER_LANES - used,), F32)])
    return w.astype(BF16), b.reshape(1, ROUTER_LANES), w.T.astype(BF16), b.reshape(ROUTER_LANES, 1)


def _moe_rows(h2, logits_t, wr, br, w_gate, w_up, w_down, layer):
    t = h2.shape[0]
    n_blocks = -(-(t + N_CLASSES * (MOE_BLK - 1)) // MOE_BLK)
    assert n_blocks <= MAX_MOE_BLOCKS
    pos, ea, eb, nused, ends = _route(logits_t)
    pos = pos.reshape(t)
    hs, wg, wu, wd = _dispatch(pos, ends.reshape(-1), h2, n_blocks * MOE_BLK, w_gate, w_up, w_down, layer)
    ys = _moe_experts(ea.reshape(-1), eb.reshape(-1), nused.reshape(-1)[:1], hs, wr, br, wg, wu, wd, n_blocks)
    return pos, ys


def kernel(x_prompt, x_sample, cache_k, cache_v, c, c_ctx, w_in_even, attn_sink, sgu_norm, sgu_w, sgu_b,
           w_out_even, conv_w_in, conv_dw, conv_dw_b, conv_norm, conv_w_out, ada_w, ada_b, norm_mix, norm_ffn,
           router_group_w, router_group_b, router_expert_w, router_expert_b, expert_w_gate, expert_w_up,
           expert_w_down, final_norm):
    batch, seq, d = x_prompt.shape
    dec_batch, dec_seq, _ = x_sample.shape
    depth = ada_w.shape[0]
    n_prompt = batch * seq
    n_sample = dec_batch * dec_seq
    assert d == D_MODEL and depth == 2 and n_prompt % dec_seq == 0 and dec_seq % TM_PROJ == 0
    x_p = x_prompt.reshape(n_prompt, d)
    x_s = x_sample.reshape(n_sample, d)

    n_mod = -(-(1 + dec_batch) // 8) * 8
    cvec = jnp.concatenate([c_ctx[None, :], c, jnp.zeros((n_mod - 1 - dec_batch, d), F32)], axis=0)
    mods = _ada_params(cvec, ada_w, ada_b).reshape(depth, n_mod, 6, d)
    wg, wu, wd = expert_w_gate, expert_w_up, expert_w_down

    cos_t, sin_t = _rope_tables(dec_seq, TM_PROJ)
    proj, kv32 = _proj_even(x_p, x_s, mods[0], norm_mix[0][None, :], w_in_even[0].astype(BF16), cos_t, sin_t,
                            dec_seq // TM_PROJ)
    kv_col_block = ATTN_WIDTH // (2 * KV_WIDTH)
    sink = attn_sink[0]
    a_p = _ctx_attention(proj, sink, batch, seq, kv_col_block)
    past = cache_k.shape[2]
    ck = cache_k[:, 0].reshape(dec_batch, past, KV_WIDTH).astype(BF16)
    cv = cache_v[:, 0].reshape(dec_batch, past, KV_WIDTH).astype(BF16)
    a_s = _lat_attention(proj, sink, ck, cv, n_prompt, dec_batch, dec_seq, kv_col_block)
    z = _sgu(proj, sgu_norm[0][None, :], sgu_w[0].astype(BF16), sgu_b[0].T)

    wr0, br0, wrt0, brt0 = _router_operands(router_group_w[0], router_group_b[0], router_expert_w[0], router_expert_b[0])
    x1, h2, lg = _out_proj((x_p, x_s), (a_p, a_s, z), w_out_even[0].astype(BF16), mods[0], norm_ffn[0][None, :],
                           wrt0, brt0, n_prompt, dec_seq)
    pos, ys = _moe_rows(h2, lg, wr0, br0, wg, wu, wd, 0)
    gain_f = final_norm[None, :]
    x = _combine(pos, ys, x1, mods[0], gain_f, n_prompt, dec_seq, 0, n_prompt + n_sample, final=False)

    xg = _proj_glu(x, mods[1], norm_mix[1][None, :], conv_w_in[0].astype(BF16), n_prompt, dec_seq)
    dw = jnp.concatenate([conv_dw[0], jnp.zeros((1, d), F32)], axis=0)
    hc = _conv_mix(xg, dw, conv_dw_b[0][None, :], conv_norm[0][None, :], n_prompt, seq, dec_seq)
    wr1, br1, wrt1, brt1 = _router_operands(router_group_w[1], router_group_b[1], router_expert_w[1], router_expert_b[1])
    x1, h2, lg = _out_proj((x,), (hc,), conv_w_out[0].astype(BF16), mods[1], norm_ffn[1][None, :],
                           wrt1, brt1, n_prompt, dec_seq)
    pos, ys = _moe_rows(h2, lg, wr1, br1, wg, wu, wd, 1)
    y_prompt = _combine(pos, ys, x1, mods[1], gain_f, n_prompt, dec_seq, 0, n_prompt, final=True)
    y_sample = _combine(pos, ys, x1, mods[1], gain_f, n_prompt, dec_seq, n_prompt, n_sample, final=True)

    new_k = kv32[:n_prompt, :KV_WIDTH].reshape(batch, 1, seq, N_KV_HEADS, HEAD_DIM)
    new_v = kv32[:n_prompt, KV_WIDTH:].reshape(batch, 1, seq, N_KV_HEADS, HEAD_DIM)
    return (y_prompt.reshape(batch, seq, d), y_sample.reshape(dec_batch, dec_seq, d), new_k, new_v)
```

```python
import functools

import jax
import jax.numpy as jnp
import numpy as np
from jax import lax
from jax.experimental import pallas as pl
from jax.experimental.pallas import tpu as pltpu

F32 = jnp.float32
BF16 = jnp.bfloat16

D_MODEL = 2048
GRID_W = 64
ATTN_BLK = 128
N_HEADS = 16
N_KV_HEADS = 4
Q_GROUPS = N_HEADS // N_KV_HEADS
HEAD_DIM = 64
ATTN_WIDTH = N_HEADS * HEAD_DIM
KV_WIDTH = N_KV_HEADS * HEAD_DIM
ROPE_BASE = 10000.0
SGU_GROUPS = 4
SGU_WIDTH = D_MODEL // 2
SGU_GROUP_DIM = SGU_WIDTH // SGU_GROUPS
CHUNK = 128
IN_WIDTH_EVEN = ATTN_WIDTH + 2 * KV_WIDTH + 2 * SGU_WIDTH
CONV_K = 31
CONV_HALO = 16
N_GROUPS_MOE = 4
EXPERTS_PER_GROUP = 4
N_EXPERTS = N_GROUPS_MOE * EXPERTS_PER_GROUP
PAIRS_PER_GROUP = 6
N_CLASSES = N_GROUPS_MOE * PAIRS_PER_GROUP
D_EXPERT = 512
MOE_BLK = 256
EPS = 1e-6
NEG_INF = -1e30

LANES = 128
ROUTER_LANES = 128
CLASS_ROWS = 32
MAX_MOE_BLOCKS = 128
TM_PROJ = 512
TN_PROJ = 512
TM_OUT = 256
TM_OUTPROJ = 512
TM_SGU = 512
CUM_CHUNK = 256
VMEM_LIMIT = 56 * 1024 * 1024


def _cparams(sem):
    return pltpu.CompilerParams(dimension_semantics=sem, vmem_limit_bytes=VMEM_LIMIT)


def _pack_bf16_pairs(h):
    w = h.shape[1] // 2
    lo = pltpu.bitcast(h[:, :w].astype(BF16).astype(F32), jnp.uint32) >> 16
    hi = pltpu.bitcast(h[:, w:].astype(BF16).astype(F32), jnp.uint32) & jnp.uint32(0xFFFF0000)
    return hi | lo


def _unpack_bf16_pairs(u):
    lo = pltpu.bitcast(u << 16, F32)
    hi = pltpu.bitcast(u & jnp.uint32(0xFFFF0000), F32)
    return jnp.concatenate([lo, hi], axis=1).astype(BF16)


def _rms_mod(x, gain, scale, shift):
    ms = jnp.mean(x * x, axis=-1, keepdims=True)
    y = x * lax.rsqrt(ms + EPS) * gain
    return y * (1.0 + scale) + shift


def _ada_kernel(c_ref, w_ref, b_ref, o_ref):
    c = c_ref[...]
    s = (c * jax.nn.sigmoid(c)).astype(BF16)
    o_ref[...] = jnp.dot(s, w_ref[...].astype(BF16), preferred_element_type=F32) + b_ref[...]


def _ada_params(cvec, ada_w, ada_b):
    depth, d, n = ada_w.shape
    mp = cvec.shape[0]
    tn = 1024
    return pl.pallas_call(
        _ada_kernel,
        grid=(depth, n // tn),
        in_specs=[
            pl.BlockSpec((mp, d), lambda l, j: (0, 0)),
            pl.BlockSpec((None, d, tn), lambda l, j: (l, 0, j)),
            pl.BlockSpec((None, 1, tn), lambda l, j: (l, 0, j)),
        ],
        out_specs=pl.BlockSpec((None, mp, tn), lambda l, j: (l, 0, j)),
        out_shape=jax.ShapeDtypeStruct((depth, mp, n), F32),
        compiler_params=_cparams(("arbitrary", "arbitrary")),
        name="ada_params",
    )(cvec, ada_w, ada_b.reshape(depth, 1, n))


def _swap16(x):
    width = x.shape[-1]
    lane = lax.broadcasted_iota(jnp.int32, x.shape, 1)
    return jnp.where((lane % 32) < 16, pltpu.roll(x, width - 16, 1), pltpu.roll(x, 16, 1))


def _proj_even_kernel(xp_ref, xs_ref, mod_ref, g_ref, w_ref, cos_ref, sin_ref, o_ref, kv_ref, h_scr, *,
                      n_prompt_blocks):
    x = jnp.where(pl.program_id(0) < n_prompt_blocks, xp_ref[...], xs_ref[...])
    h_scr[...] = _rms_mod(x, g_ref[...], mod_ref[1:2, :], mod_ref[0:1, :]).astype(BF16)
    kv_tile = ATTN_WIDTH // TN_PROJ
    for j in range(w_ref.shape[1] // TN_PROJ):
        cols = slice(j * TN_PROJ, (j + 1) * TN_PROJ)
        acc = jnp.dot(h_scr[...], w_ref[:, cols], preferred_element_type=F32)
        if j <= kv_tile:
            tab = slice(0, TN_PROJ) if j < kv_tile else slice(TN_PROJ, 2 * TN_PROJ)
            o_ref[:, cols] = (acc * cos_ref[:, tab] + _swap16(acc) * sin_ref[:, tab]).astype(BF16)
            if j == kv_tile:
                kv_ref[...] = acc
        else:
            o_ref[:, cols] = jax.nn.gelu(acc).astype(BF16)


def _proj_even(x_p, x_s, mods, gain, w, cos_t, sin_t, blocks_per_seq):
    d = x_p.shape[1]
    n = w.shape[1]
    tm, tn = TM_PROJ, TN_PROJ
    npb = x_p.shape[0] // tm
    nsb = x_s.shape[0] // tm
    t = x_p.shape[0] + x_s.shape[0]

    def mod_idx(i):
        return (jnp.where(i < npb, 0, 1 + (i - npb) // blocks_per_seq), 0, 0)

    def tab_idx(i):
        return (jnp.where(i < npb, blocks_per_seq, (i - npb) % blocks_per_seq), 0)

    return pl.pallas_call(
        functools.partial(_proj_even_kernel, n_prompt_blocks=npb),
        grid=(npb + nsb,),
        in_specs=[
            pl.BlockSpec((tm, d), lambda i: (jnp.minimum(i, npb - 1), 0)),
            pl.BlockSpec((tm, d), lambda i: (jnp.maximum(i - npb, 0), 0)),
            pl.BlockSpec((None, 6, d), mod_idx),
            pl.BlockSpec((1, d), lambda i: (0, 0)),
            pl.BlockSpec((d, n), lambda i: (0, 0), pipeline_mode=pl.Buffered(1)),
            pl.BlockSpec((tm, 2 * tn), tab_idx),
            pl.BlockSpec((tm, 2 * tn), tab_idx),
        ],
        out_specs=[
            pl.BlockSpec((tm, n), lambda i: (i, 0)),
            pl.BlockSpec((tm, tn), lambda i: (i, 0)),
        ],
        out_shape=[jax.ShapeDtypeStruct((t, n), BF16), jax.ShapeDtypeStruct((t, tn), F32)],
        scratch_shapes=[pltpu.VMEM((tm, d), BF16)],
        compiler_params=_cparams(("arbitrary",)),
        name="proj_even",
    )(x_p, x_s, mods, gain, w, cos_t, sin_t)


def _rope_tables(dec_seq, tm):
    f32 = np.float32
    n = np.arange(dec_seq)
    row = (n // GRID_W).astype(f32)
    col = (n % GRID_W).astype(f32)
    nf = HEAD_DIM // 4
    inv_freq = np.power(f32(ROPE_BASE), -np.arange(nf, dtype=f32) / f32(nf)).astype(f32)
    ar = row[:, None] * inv_freq[None, :]
    ac = col[:, None] * inv_freq[None, :]
    cos_h = np.concatenate([np.cos(ar), np.cos(ar), np.cos(ac), np.cos(ac)], axis=-1).astype(f32)
    sin_h = np.concatenate([-np.sin(ar), np.sin(ar), -np.sin(ac), np.sin(ac)], axis=-1).astype(f32)
    scale = f32(HEAD_DIM ** -0.5)
    ones = np.ones((dec_seq, KV_WIDTH), f32)
    zeros = np.zeros((dec_seq, KV_WIDTH), f32)
    q_heads = TN_PROJ // HEAD_DIM
    cos_t = np.concatenate([np.tile(cos_h, (1, q_heads)) * scale, np.tile(cos_h, (1, N_KV_HEADS)), ones], axis=-1)
    sin_t = np.concatenate([np.tile(sin_h, (1, q_heads)) * scale, np.tile(sin_h, (1, N_KV_HEADS)), zeros], axis=-1)
    id_cos = np.concatenate([np.full((tm, TN_PROJ), scale, f32), np.ones((tm, TN_PROJ), f32)], axis=-1)
    id_sin = np.zeros((tm, 2 * TN_PROJ), f32)
    return (jnp.asarray(np.concatenate([cos_t, id_cos], axis=0), F32),
            jnp.asarray(np.concatenate([sin_t, id_sin], axis=0), F32))


def _sink_attend(q, keys, vals, masks, sink):
    scores = []
    for k, mask in zip(keys, masks):
        s = lax.dot_general(q, k, (((1,), (1,)), ((), ())), preferred_element_type=F32)
        if mask is not None:
            stacked = s.reshape(s.shape[0] // mask.shape[0], *mask.shape)
            s = jnp.where(mask[None], stacked, NEG_INF).reshape(s.shape)
        scores.append(s)
    m = sink
    for s in scores:
        m = jnp.maximum(m, jnp.max(s, axis=-1, keepdims=True))
    es = [jnp.exp(s - m) for s in scores]
    den = jnp.exp(sink - m)
    for e in es:
        den = den + jnp.sum(e, axis=-1, keepdims=True)
    out = None
    for e, v in zip(es, vals):
        o = jnp.dot(e.astype(BF16), v, preferred_element_type=F32)
        out = o if out is None else out + o
    return out * (1.0 / den)


def _grouped_heads_attend(sink_ref, q_ref, o_ref, kh, keys, vals, masks):
    rows = q_ref.shape[0]
    heads = [kh * Q_GROUPS + g for g in range(Q_GROUPS)]
    q = jnp.concatenate([q_ref[:, h * HEAD_DIM:(h + 1) * HEAD_DIM] for h in heads], axis=0)
    ridx = lax.broadcasted_iota(jnp.int32, (Q_GROUPS * rows, 1), 0)
    sink = jnp.full((Q_GROUPS * rows, 1), sink_ref[heads[-1]], F32)
    for g in range(Q_GROUPS - 2, -1, -1):
        sink = jnp.where(ridx < (g + 1) * rows, sink_ref[heads[g]], sink)
    o = _sink_attend(q, keys, vals, masks, sink)
    for g, h in enumerate(heads):
        o_ref[:, h * HEAD_DIM:(h + 1) * HEAD_DIM] = o[g * rows:(g + 1) * rows, :].astype(BF16)


def _ctx_attn_kernel(sink_ref, q_ref, kv_ref, o_ref):
    for kh in range(N_KV_HEADS):
        k = kv_ref[:, kh * HEAD_DIM:(kh + 1) * HEAD_DIM]
        v = kv_ref[:, KV_WIDTH + kh * HEAD_DIM:KV_WIDTH + (kh + 1) * HEAD_DIM]
        _grouped_heads_attend(sink_ref, q_ref, o_ref, kh, [k], [v], [None])


def _ctx_attention(proj, sink, batch, seq, kv_col_block):
    return pl.pallas_call(
        _ctx_attn_kernel,
        grid=(batch,),
        in_specs=[
            pl.BlockSpec(memory_space=pltpu.SMEM),
            pl.BlockSpec((seq, ATTN_WIDTH), lambda b: (b, 0)),
            pl.BlockSpec((seq, 2 * KV_WIDTH), lambda b: (b, kv_col_block)),
        ],
        out_specs=pl.BlockSpec((seq, ATTN_WIDTH), lambda b: (b, 0)),
        out_shape=jax.ShapeDtypeStruct((batch * seq, ATTN_WIDTH), BF16),
        compiler_params=_cparams(("arbitrary",)),
        name="ctx_attention",
    )(sink, proj, proj)


def _lat_attn_kernel(sink_ref, q_ref, kv_ref, ck_ref, cv_ref, o_ref, *, seq):
    i = pl.program_id(1)
    win = 3 * ATTN_BLK
    ws = pl.multiple_of(jnp.clip((i - 1) * ATTN_BLK, 0, seq - win), ATTN_BLK)
    shape = (ATTN_BLK, win)
    qpos = i * ATTN_BLK + lax.broadcasted_iota(jnp.int32, shape, 0)
    kpos = ws + lax.broadcasted_iota(jnp.int32, shape, 1)
    valid = jnp.abs(qpos - kpos) <= ATTN_BLK
    kvw = kv_ref[pl.ds(ws, win), :]
    for kh in range(N_KV_HEADS):
        k_loc = kvw[:, kh * HEAD_DIM:(kh + 1) * HEAD_DIM]
        v_loc = kvw[:, KV_WIDTH + kh * HEAD_DIM:KV_WIDTH + (kh + 1) * HEAD_DIM]
        k_ctx = ck_ref[:, kh * HEAD_DIM:(kh + 1) * HEAD_DIM]
        v_ctx = cv_ref[:, kh * HEAD_DIM:(kh + 1) * HEAD_DIM]
        _grouped_heads_attend(sink_ref, q_ref, o_ref, kh, [k_ctx, k_loc], [v_ctx, v_loc], [None, valid])


def _lat_attention(proj, sink, cache_k, cache_v, n_prompt, dec_batch, dec_seq, kv_col_block):
    qb = dec_seq // ATTN_BLK
    q_off = n_prompt // ATTN_BLK
    s_off = n_prompt // dec_seq
    past = cache_k.shape[1]
    return pl.pallas_call(
        functools.partial(_lat_attn_kernel, seq=dec_seq),
        grid=(dec_batch, qb),
        in_specs=[
            pl.BlockSpec(memory_space=pltpu.SMEM),
            pl.BlockSpec((ATTN_BLK, ATTN_WIDTH), lambda b, i: (q_off + b * qb + i, 0)),
            pl.BlockSpec((dec_seq, 2 * KV_WIDTH), lambda b, i: (s_off + b, kv_col_block)),
            pl.BlockSpec((None, past, KV_WIDTH), lambda b, i: (b, 0, 0)),
            pl.BlockSpec((None, past, KV_WIDTH), lambda b, i: (b, 0, 0)),
        ],
        out_specs=pl.BlockSpec((ATTN_BLK, ATTN_WIDTH), lambda b, i: (b * qb + i, 0)),
        out_shape=jax.ShapeDtypeStruct((dec_batch * dec_seq, ATTN_WIDTH), BF16),
        compiler_params=_cparams(("arbitrary", "arbitrary")),
        name="lat_attention",
    )(sink, proj, proj, cache_k, cache_v)


def _sgu_kernel(ul_ref, uh_ref, vl_ref, vh_ref, g_ref, w_ref, bt_ref, z_ref):
    half = SGU_WIDTH // 2
    per_half = SGU_GROUPS // 2
    u_refs = (ul_ref, uh_ref)
    for c in range(TM_SGU // CHUNK):
        rows = slice(c * CHUNK, (c + 1) * CHUNK)
        vl = vl_ref[rows, :].astype(F32)
        vh = vh_ref[rows, :].astype(F32)
        ssq = jnp.sum(vl * vl, axis=-1, keepdims=True) + jnp.sum(vh * vh, axis=-1, keepdims=True)
        r = lax.rsqrt(ssq * (1.0 / SGU_WIDTH) + EPS)
        vn = ((vl * r * g_ref[:, :half]).astype(BF16), (vh * r * g_ref[:, half:]).astype(BF16))
        for g in range(SGU_GROUPS):
            src = g // per_half
            cols = slice((g % per_half) * SGU_GROUP_DIM, (g % per_half + 1) * SGU_GROUP_DIM)
            mixed = jnp.dot(w_ref[g], vn[src][:, cols], preferred_element_type=F32) + bt_ref[:, g:g + 1]
            z_ref[rows, g * SGU_GROUP_DIM:(g + 1) * SGU_GROUP_DIM] = (
                u_refs[src][rows, cols].astype(F32) * mixed).astype(BF16)


def _sgu(proj, norm_g, w_s, b_t):
    t = proj.shape[0]
    half = SGU_WIDTH // 2
    u0 = (ATTN_WIDTH + 2 * KV_WIDTH) // half
    blocks = [pl.BlockSpec((TM_SGU, half), functools.partial(lambda i, c: (i, c), c=u0 + k)) for k in range(4)]
    return pl.pallas_call(
        _sgu_kernel,
        grid=(t // TM_SGU,),
        in_specs=blocks + [
            pl.BlockSpec((1, SGU_WIDTH), lambda i: (0, 0)),
            pl.BlockSpec((SGU_GROUPS, CHUNK, CHUNK), lambda i: (0, 0, 0)),
            pl.BlockSpec((CHUNK, SGU_GROUPS), lambda i: (0, 0)),
        ],
        out_specs=pl.BlockSpec((TM_SGU, SGU_WIDTH), lambda i: (i, 0)),
        out_shape=jax.ShapeDtypeStruct((t, SGU_WIDTH), BF16),
        compiler_params=_cparams(("arbitrary",)),
        name="sgu",
    )(proj, proj, proj, proj, norm_g, w_s, b_t)


def _residual_router(rows, x, acc, mod_ref, g2_ref, wrt_ref, brt_ref, x1_ref, h2_ref, lg_ref):
    x1 = x + mod_ref[2:3, :] * acc
    x1_ref[rows, :] = x1
    h2 = _rms_mod(x1, g2_ref[...], mod_ref[4:5, :], mod_ref[3:4, :])
    h2_ref[rows, :] = _pack_bf16_pairs(h2)
    lg = lax.dot_general(wrt_ref[...], h2.astype(BF16), (((1,), (1,)), ((), ())), preferred_element_type=F32)
    lg_ref[:, rows] = lg + brt_ref[...]


def _sub_blocks(ref):
    return [slice(r, r + TM_OUT) for r in range(0, ref.shape[0], TM_OUT)]


def _out_even_kernel(xp_ref, xs_ref, ap_ref, as_ref, z_ref, w_ref, mod_ref, g2_ref, wrt_ref, brt_ref,
                     x1_ref, h2_ref, lg_ref, *, n_prompt_blocks):
    is_prompt = pl.program_id(0) < n_prompt_blocks
    for rows in _sub_blocks(x1_ref):
        a = jnp.where(is_prompt, ap_ref[rows, :].astype(F32), as_ref[rows, :].astype(F32)).astype(BF16)
        acc = jnp.dot(a, w_ref[0:ATTN_WIDTH, :], preferred_element_type=F32)
        acc = acc + jnp.dot(z_ref[rows, :], w_ref[ATTN_WIDTH:, :], preferred_element_type=F32)
        x = jnp.where(is_prompt, xp_ref[rows, :], xs_ref[rows, :])
        _residual_router(rows, x, acc, mod_ref, g2_ref, wrt_ref, brt_ref, x1_ref, h2_ref, lg_ref)


def _out_odd_kernel(x_ref, hc_ref, w_ref, mod_ref, g2_ref, wrt_ref, brt_ref, x1_ref, h2_ref, lg_ref):
    for rows in _sub_blocks(x1_ref):
        acc = jnp.dot(hc_ref[rows, :], w_ref[...], preferred_element_type=F32)
        _residual_router(rows, x_ref[rows, :], acc, mod_ref, g2_ref, wrt_ref, brt_ref, x1_ref, h2_ref, lg_ref)


def _out_proj(xs, lhs, w, mods, gain2, wrt, brt, n_prompt, dec_seq):
    d = xs[0].shape[1]
    t = sum(x.shape[0] for x in xs)
    tm = TM_OUTPROJ
    npb = n_prompt // tm
    bps = dec_seq // tm
    nsb = t // tm - npb
    if len(xs) == 2:
        x_specs = [pl.BlockSpec((tm, d), lambda i: (jnp.minimum(i, npb - 1), 0)),
                   pl.BlockSpec((tm, d), lambda i: (jnp.maximum(i - npb, 0), 0))]
    else:
        x_specs = [pl.BlockSpec((tm, d), lambda i: (i, 0))]

    def mod_idx(i):
        return (jnp.where(i < npb, 0, 1 + (i - npb) // bps), 0, 0)

    if len(lhs) == 3:
        a_p, a_s, z = lhs
        body = functools.partial(_out_even_kernel, n_prompt_blocks=npb)
        lhs_specs = [
            pl.BlockSpec((tm, ATTN_WIDTH), lambda i: (jnp.minimum(i, npb - 1), 0)),
            pl.BlockSpec((tm, ATTN_WIDTH), lambda i: (jnp.clip(i - npb, 0, nsb - 1), 0)),
            pl.BlockSpec((tm, SGU_WIDTH), lambda i: (i, 0)),
        ]
    else:
        body = _out_odd_kernel
        lhs_specs = [pl.BlockSpec((tm, d), lambda i: (i, 0))]
    return pl.pallas_call(
        body,
        grid=(t // tm,),
        in_specs=x_specs + lhs_specs + [
            pl.BlockSpec((w.shape[0], d), lambda i: (0, 0), pipeline_mode=pl.Buffered(1)),
            pl.BlockSpec((None, 6, d), mod_idx),
            pl.BlockSpec((1, d), lambda i: (0, 0)),
            pl.BlockSpec((ROUTER_LANES, d), lambda i: (0, 0)),
            pl.BlockSpec((ROUTER_LANES, 1), lambda i: (0, 0)),
        ],
        out_specs=[
            pl.BlockSpec((tm, d), lambda i: (i, 0)),
            pl.BlockSpec((tm, d // 2), lambda i: (i, 0)),
            pl.BlockSpec((ROUTER_LANES, tm), lambda i: (0, i)),
        ],
        out_shape=[
            jax.ShapeDtypeStruct((t, d), F32),
            jax.ShapeDtypeStruct((t, d // 2), jnp.uint32),
            jax.ShapeDtypeStruct((ROUTER_LANES, t), F32),
        ],
        compiler_params=_cparams(("arbitrary",)),
        name="out_proj",
    )(*xs, *lhs, w, mods, gain2, wrt, brt)


def _first_argmax4(v0, v1, v2, v3):
    m = jnp.maximum(jnp.maximum(v0, v1), jnp.maximum(v2, v3))
    idx = jnp.where(v0 == m, 0.0, jnp.where(v1 == m, 1.0, jnp.where(v2 == m, 2.0, 3.0)))
    return m, idx


def _route_kernel(lg_ref, pos_ref, ea_ref, eb_ref, nv_ref, ends_ref, oh_scr, rank_scr, *, n_tokens):
    lg = lg_ref[...]
    rows = [lg[r:r + 1, :] for r in range(N_GROUPS_MOE + N_EXPERTS)]
    _, gidx = _first_argmax4(*rows[:N_GROUPS_MOE])
    e = []
    for k in range(EXPERTS_PER_GROUP):
        cand = [rows[N_GROUPS_MOE + g * EXPERTS_PER_GROUP + k] for g in range(N_GROUPS_MOE)]
        e.append(jnp.where(gidx == 0.0, cand[0], jnp.where(gidx == 1.0, cand[1], jnp.where(gidx == 2.0, cand[2], cand[3]))))
    _, l1 = _first_argmax4(*e)
    e2 = [jnp.where(l1 == float(k), -jnp.inf, e[k]) for k in range(EXPERTS_PER_GROUP)]
    _, l2 = _first_argmax4(*e2)
    lo = jnp.minimum(l1, l2)
    hi = jnp.maximum(l1, l2)
    pair = jnp.where(lo == 0.0, hi - 1.0, jnp.where(lo == 1.0, hi + 1.0, 5.0))
    cls = gidx * float(PAIRS_PER_GROUP) + pair

    crow = lax.broadcasted_iota(jnp.int32, (CLASS_ROWS, n_tokens), 0).astype(F32)
    oh_scr[...] = jnp.where(crow == cls, 1.0, 0.0).astype(BF16)

    jj = lax.broadcasted_iota(jnp.int32, (CUM_CHUNK, CUM_CHUNK), 0)
    ii = lax.broadcasted_iota(jnp.int32, (CUM_CHUNK, CUM_CHUNK), 1)
    tri = jnp.where(jj < ii, 1.0, 0.0).astype(BF16)

    def chunk(c, carry):
        off = pl.multiple_of(c * CUM_CHUNK, CUM_CHUNK)
        oh = oh_scr[:, pl.ds(off, CUM_CHUNK)]
        ohf = oh.astype(F32)
        cum = jnp.dot(oh, tri, preferred_element_type=F32) + carry
        rank_scr[:, pl.ds(off, CUM_CHUNK)] = jnp.sum(cum * ohf, axis=0, keepdims=True)
        return carry + jnp.sum(ohf, axis=1, keepdims=True)

    counts = lax.fori_loop(0, n_tokens // CUM_CHUNK, chunk, jnp.zeros((CLASS_ROWS, 1), F32))
    padded = jnp.floor((counts + float(MOE_BLK - 1)) * (1.0 / MOE_BLK)) * float(MOE_BLK)
    run = jnp.zeros((1, 1), F32)
    starts_rows = []
    for c in range(CLASS_ROWS):
        starts_rows.append(run)
        run = run + padded[c:c + 1, :]
    starts = jnp.concatenate(starts_rows, axis=0)
    ends = starts + padded
    total = run

    ohf = oh_scr[...].astype(F32)
    pos = rank_scr[...] + jnp.sum(ohf * starts, axis=0, keepdims=True)
    pos_ref[...] = pos.astype(jnp.int32)

    bstart = lax.broadcasted_iota(jnp.int32, (CLASS_ROWS, MAX_MOE_BLOCKS), 1).astype(F32) * float(MOE_BLK)
    brow = lax.broadcasted_iota(jnp.int32, (CLASS_ROWS, MAX_MOE_BLOCKS), 0)
    done = jnp.where((ends <= bstart) & (brow < N_CLASSES), 1.0, 0.0)
    bcls = jnp.minimum(jnp.sum(done, axis=0, keepdims=True), float(N_CLASSES - 1))
    grp = (jnp.where(bcls >= 6.0, 1.0, 0.0) + jnp.where(bcls >= 12.0, 1.0, 0.0) + jnp.where(bcls >= 18.0, 1.0, 0.0))
    bp = bcls - grp * float(PAIRS_PER_GROUP)
    ge3 = jnp.where(bp >= 3.0, 1.0, 0.0)
    ge5 = jnp.where(bp >= 5.0, 1.0, 0.0)
    blo = ge3 + ge5
    bhi = bp + 1.0 - 2.0 * ge3 - ge5
    ea_ref[...] = (grp * float(EXPERTS_PER_GROUP) + blo).astype(jnp.int32)
    eb_ref[...] = (grp * float(EXPERTS_PER_GROUP) + bhi).astype(jnp.int32)
    own = jnp.where(brow.astype(F32) == bcls, starts + counts, 0.0)
    filled = jnp.sum(own, axis=0, keepdims=True) - bstart[0:1, :]
    nv_ref[...] = jnp.clip(filled, 0.0, float(MOE_BLK)).astype(jnp.int32)
    lane = lax.broadcasted_iota(jnp.int32, (1, MAX_MOE_BLOCKS), 1)
    ends_row = jnp.zeros((1, MAX_MOE_BLOCKS), F32)
    for c in range(N_CLASSES):
        ends_row = jnp.where(lane == c, starts_rows[c] + padded[c:c + 1, :], ends_row)
    ends_ref[...] = ends_row.astype(jnp.int32)


def _route(logits_t):
    t = logits_t.shape[1]
    rows = N_CLASSES
    i32 = jnp.int32
    return pl.pallas_call(
        functools.partial(_route_kernel, n_tokens=t),
        grid=(1,),
        in_specs=[pl.BlockSpec((rows, t), lambda i: (0, 0))],
        out_specs=[
            pl.BlockSpec((1, t), lambda i: (0, 0)),
        ] + [pl.BlockSpec((1, MAX_MOE_BLOCKS), lambda i: (0, 0))] * 4,
        out_shape=[jax.ShapeDtypeStruct((1, t), i32)] + [jax.ShapeDtypeStruct((1, MAX_MOE_BLOCKS), i32)] * 4,
        scratch_shapes=[pltpu.VMEM((CLASS_ROWS, t), BF16), pltpu.VMEM((1, t), F32)],
        compiler_params=_cparams(("arbitrary",)),
        name="route",
    )(logits_t)


ROW_UNROLL = 256
DMA_THREADS = 2


def _row_copy(src_ref, src_row, dst_ref, dst_row, sem):
    return pltpu.make_async_copy(src_ref.at[pl.ds(src_row, 1), :], dst_ref.at[pl.ds(dst_row, 1), :], sem)


def _start_row_copies(n_rows, make_copy):
    def body(i, carry):
        for u in range(ROW_UNROLL):
            make_copy(i * ROW_UNROLL + u).start(priority=u % DMA_THREADS)
        return carry

    if n_rows == ROW_UNROLL:
        body(0, 0)
    else:
        lax.fori_loop(0, n_rows // ROW_UNROLL, body, 0)


def _dispatch_kernel(pos_ref, ends_ref, h_ref, wg_ref, wu_ref, wd_ref, o_ref, wg_out, wu_out, wd_out,
                     zbuf, sem, zsem, *, steps_per_matrix):
    step = pl.program_id(0)

    @pl.when(step == 0)
    def _():
        zbuf[...] = jnp.zeros_like(zbuf)
        total = ends_ref[N_CLASSES - 1]
        n_free = (o_ref.shape[0] - total) // MOE_BLK

        def zero_block(first_row, phase):
            first = pl.multiple_of(first_row, MOE_BLK)
            getattr(pltpu.make_async_copy(zbuf, o_ref.at[pl.ds(first, MOE_BLK), :], zsem), phase)()

        for phase in ("start", "wait"):
            prev = 0
            for c in range(N_CLASSES):
                end = ends_ref[c]
                pl.when(end > prev)(functools.partial(zero_block, end - MOE_BLK, phase))
                prev = end

            def free_block(k, carry, phase=phase):
                zero_block(total + k * MOE_BLK, phase)
                return carry

            lax.fori_loop(0, n_free, free_block, 0)

    base = step * TM_OUT

    _start_row_copies(TM_OUT, lambda r: _row_copy(h_ref, r, o_ref, pos_ref[base + r], sem))
    for k, (src, dst) in enumerate(((wg_ref, wg_out), (wu_ref, wu_out), (wd_ref, wd_out))):
        @pl.when((step >= k * steps_per_matrix) & (step < (k + 1) * steps_per_matrix))
        def _(src=src, dst=dst):
            dst[...] = src[...].astype(BF16)
    pltpu.make_async_copy(h_ref, o_ref.at[pl.ds(0, TM_OUT), :], sem).wait()


def _dispatch(pos, ends, h2, n_rows, w_gate, w_up, w_down, layer):
    t, d = h2.shape
    n_steps = t // TM_OUT
    n_exp = w_gate.shape[1]
    steps_per_matrix = min(n_steps // 3, n_exp)
    assert steps_per_matrix >= 1 and n_exp % steps_per_matrix == 0
    e_blk = n_exp // steps_per_matrix

    def w_spec(k, w):
        def idx(i, pos, ends):
            return (layer, jnp.clip(i - k * steps_per_matrix, 0, steps_per_matrix - 1), 0, 0)
        return pl.BlockSpec((None, e_blk) + w.shape[2:], idx)

    def w_out_spec(k, w):
        def idx(i, pos, ends):
            return (jnp.clip(i - k * steps_per_matrix, 0, steps_per_matrix - 1), 0, 0)
        return pl.BlockSpec((e_blk,) + w.shape[2:], idx)

    weights = (w_gate, w_up, w_down)
    return pl.pallas_call(
        functools.partial(_dispatch_kernel, steps_per_matrix=steps_per_matrix),
        grid_spec=pltpu.PrefetchScalarGridSpec(
            num_scalar_prefetch=2,
            grid=(n_steps,),
            in_specs=[pl.BlockSpec((TM_OUT, d), lambda i, pos, ends: (i, 0))]
            + [w_spec(k, w) for k, w in enumerate(weights)],
            out_specs=[pl.BlockSpec(memory_space=pl.ANY)] + [w_out_spec(k, w) for k, w in enumerate(weights)],
            scratch_shapes=[pltpu.VMEM((MOE_BLK, d), h2.dtype), pltpu.SemaphoreType.DMA(()),
                            pltpu.SemaphoreType.DMA(())],
        ),
        out_shape=[jax.ShapeDtypeStruct((n_rows, d), h2.dtype)]
        + [jax.ShapeDtypeStruct(w.shape[1:], BF16) for w in weights],
        compiler_params=_cparams(("arbitrary",)),
        name="moe_dispatch",
    )(pos, ends, h2, *weights)


def _expert_ffn(x, wg_ref, wu_ref, wd_ref):
    gate = jnp.dot(x, wg_ref[...], preferred_element_type=F32)
    up = jnp.dot(x, wu_ref[...], preferred_element_type=F32)
    hidden = (gate * jax.nn.sigmoid(gate) * up).astype(BF16)
    return jnp.dot(hidden, wd_ref[...], preferred_element_type=F32)


def _moe_kernel(ea_ref, eb_ref, nv_ref, x_ref, wr_ref, br_ref, wga, wua, wda, wgb, wub, wdb, y_ref):
    b = pl.program_id(0)
    n_valid = nv_ref[b]

    def run(rows):
        x = _unpack_bf16_pairs(x_ref[rows, :])
        lg = jnp.dot(x, wr_ref[...], preferred_element_type=F32) + br_ref[...]
        lane = lax.broadcasted_iota(jnp.int32, lg.shape, 1)
        ea = ea_ref[b]
        eb = eb_ref[b]
        grp = ea // EXPERTS_PER_GROUP
        is_grp = lane < N_GROUPS_MOE
        gm = jnp.max(jnp.where(is_grp, lg, -jnp.inf), axis=-1, keepdims=True)
        ge = jnp.where(is_grp, jnp.exp(lg - gm), 0.0)
        p_g = jnp.sum(jnp.where(lane == grp, ge, 0.0), axis=-1, keepdims=True) / jnp.sum(ge, axis=-1, keepdims=True)
        la = jnp.sum(jnp.where(lane == N_GROUPS_MOE + ea, lg, 0.0), axis=-1, keepdims=True)
        lb = jnp.sum(jnp.where(lane == N_GROUPS_MOE + eb, lg, 0.0), axis=-1, keepdims=True)
        mm = jnp.maximum(la, lb)
        xa = jnp.exp(la - mm)
        xb = jnp.exp(lb - mm)
        ga = xa / (xa + xb) * p_g
        gb = xb / (xa + xb) * p_g
        ya = _expert_ffn(x, wga, wua, wda)
        yb = _expert_ffn(x, wgb, wub, wdb)
        y_ref[rows, :] = ya * ga + yb * gb

    half = MOE_BLK // 2

    @pl.when(n_valid > half)
    def _():
        run(slice(0, MOE_BLK))

    @pl.when((n_valid > 0) & (n_valid <= half))
    def _():
        run(slice(0, half))
        y_ref[half:, :] = jnp.zeros((MOE_BLK - half, y_ref.shape[1]), y_ref.dtype)

    @pl.when(n_valid == 0)
    def _():
        y_ref[...] = jnp.zeros_like(y_ref)


def _moe_experts(ea, eb, n_valid, hs, wr, br, w_gate, w_up, w_down, n_blocks):
    d = w_gate.shape[1]
    de = w_gate.shape[2]

    def wa_idx(b, ea, eb, nv):
        return (ea[b], 0, 0)

    def wb_idx(b, ea, eb, nv):
        return (eb[b], 0, 0)

    return pl.pallas_call(
        _moe_kernel,
        grid_spec=pltpu.PrefetchScalarGridSpec(
            num_scalar_prefetch=3,
            grid=(n_blocks,),
            in_specs=[
                pl.BlockSpec((MOE_BLK, hs.shape[1]), lambda b, ea, eb, nv: (jnp.where(nv[b] > 0, b, 0), 0)),
                pl.BlockSpec((d, ROUTER_LANES), lambda b, ea, eb, nv: (0, 0)),
                pl.BlockSpec((1, ROUTER_LANES), lambda b, ea, eb, nv: (0, 0)),
                pl.BlockSpec((None, d, de), wa_idx),
                pl.BlockSpec((None, d, de), wa_idx),
                pl.BlockSpec((None, de, d), wa_idx),
                pl.BlockSpec((None, d, de), wb_idx),
                pl.BlockSpec((None, d, de), wb_idx),
                pl.BlockSpec((None, de, d), wb_idx),
            ],
            out_specs=pl.BlockSpec((MOE_BLK, d), lambda b, ea, eb, nv: (b, 0)),
        ),
        out_shape=jax.ShapeDtypeStruct((n_blocks * MOE_BLK, d), F32),
        compiler_params=_cparams(("arbitrary",)),
        name="moe_experts",
    )(ea, eb, n_valid, hs, wr, br, w_gate, w_up, w_down, w_gate, w_up, w_down)


def _combine_kernel(pos_ref, ys_ref, x1_ref, mod_ref, gf_ref, o_ref, ybuf, sem, *, block_off, n_steps, final):
    i = pl.program_id(0)
    slot = i % 2

    def start_gather(step, into):
        base = (step + block_off) * TM_OUT
        _start_row_copies(
            TM_OUT, lambda r: _row_copy(ys_ref, pos_ref[base + r], ybuf.at[into], r, sem.at[into]))

    pl.when(i == 0)(lambda: start_gather(0, 0))
    pl.when(i + 1 < n_steps)(lambda: start_gather(i + 1, 1 - slot))
    pltpu.make_async_copy(ys_ref.at[pl.ds(0, TM_OUT), :], ybuf.at[slot], sem.at[slot]).wait()
    x2 = x1_ref[...] + mod_ref[5:6, :] * ybuf[slot]
    if final:
        ms = jnp.mean(x2 * x2, axis=-1, keepdims=True)
        x2 = x2 * lax.rsqrt(ms + EPS) * gf_ref[...]
    o_ref[...] = x2


def _combine(pos, ys, x1, mods, gain_f, n_prompt, dec_seq, row_off, n_rows, final):
    d = x1.shape[1]
    tm = TM_OUT
    npb = n_prompt // tm
    bps = dec_seq // tm
    boff = row_off // tm

    def mod_idx(i, pos):
        blk = i + boff
        return (jnp.where(blk < npb, 0, 1 + (blk - npb) // bps), 0, 0)

    return pl.pallas_call(
        functools.partial(_combine_kernel, block_off=boff, n_steps=n_rows // tm, final=final),
        grid_spec=pltpu.PrefetchScalarGridSpec(
            num_scalar_prefetch=1,
            grid=(n_rows // tm,),
            in_specs=[
                pl.BlockSpec(memory_space=pl.ANY),
                pl.BlockSpec((tm, d), lambda i, pos: (i + boff, 0)),
                pl.BlockSpec((None, 6, d), mod_idx),
                pl.BlockSpec((1, d), lambda i, pos: (0, 0)),
            ],
            out_specs=pl.BlockSpec((tm, d), lambda i, pos: (i, 0)),
            scratch_shapes=[pltpu.VMEM((2, tm, d), F32), pltpu.SemaphoreType.DMA((2,))],
        ),
        out_shape=jax.ShapeDtypeStruct((n_rows, d), F32),
        compiler_params=_cparams(("arbitrary",)),
        name="moe_combine",
    )(pos, ys, x1, mods, gain_f)


def _proj_glu_kernel(x_ref, mod_ref, g_ref, w_ref, o_ref, h_scr):
    h_scr[...] = _rms_mod(x_ref[...], g_ref[...], mod_ref[1:2, :], mod_ref[0:1, :]).astype(BF16)
    width = w_ref.shape[1] // 2
    for j in range(width // TN_PROJ):
        cols = slice(j * TN_PROJ, (j + 1) * TN_PROJ)
        gcols = slice(width + j * TN_PROJ, width + (j + 1) * TN_PROJ)
        val = jnp.dot(h_scr[...], w_ref[:, cols], preferred_element_type=F32)
        gate = jnp.dot(h_scr[...], w_ref[:, gcols], preferred_element_type=F32)
        o_ref[:, cols] = (val * jax.nn.sigmoid(gate)).astype(BF16)


def _proj_glu(x, mods, gain, w, n_prompt, dec_seq):
    t, d = x.shape
    n = w.shape[1]
    tm = TM_PROJ
    npb = n_prompt // tm
    bps = dec_seq // tm

    def mod_idx(i):
        return (jnp.where(i < npb, 0, 1 + (i - npb) // bps), 0, 0)

    return pl.pallas_call(
        _proj_glu_kernel,
        grid=(t // tm,),
        in_specs=[
            pl.BlockSpec((tm, d), lambda i: (i, 0)),
            pl.BlockSpec((None, 6, d), mod_idx),
            pl.BlockSpec((1, d), lambda i: (0, 0)),
            pl.BlockSpec((d, n), lambda i: (0, 0), pipeline_mode=pl.Buffered(1)),
        ],
        out_specs=pl.BlockSpec((tm, n // 2), lambda i: (i, 0)),
        out_shape=jax.ShapeDtypeStruct((t, n // 2), BF16),
        scratch_shapes=[pltpu.VMEM((tm, d), BF16)],
        compiler_params=_cparams(("arbitrary",)),
        name="proj_glu",
    )(x, mods, gain, w)


CONV_ROWS = 64
CONV_COLS = 256
F32_SUBLANES = 8


def _conv_kernel(x_ref, prev_ref, next_ref, dw_ref, b_ref, g_ref, o_ref, xpad, shifted, ybuf, *,
                 n_prompt_blocks, blocks_per_seq):
    i = pl.program_id(0)
    in_sample = i >= n_prompt_blocks
    j = i - n_prompt_blocks
    is_start = jnp.logical_or(jnp.logical_not(in_sample), j % blocks_per_seq == 0)
    is_end = jnp.logical_or(jnp.logical_not(in_sample), j % blocks_per_seq == blocks_per_seq - 1)
    tm = TM_OUT
    xpad[0:CONV_HALO, :] = jnp.where(is_start, 0.0, prev_ref[...].astype(F32))
    xpad[CONV_HALO:CONV_HALO + tm, :] = x_ref[...].astype(F32)
    xpad[CONV_HALO + tm:, :] = jnp.where(is_end, 0.0, next_ref[...].astype(F32))
    first = CONV_HALO - CONV_K // 2
    span = shifted.shape[1]

    def col_chunk(c, ssq):
        c0 = pl.multiple_of(c * CONV_COLS, CONV_COLS)
        cols = pl.ds(c0, CONV_COLS)
        w = dw_ref[:, cols]
        bias = b_ref[:, cols]
        for s in range(1, F32_SUBLANES):
            shifted[s - 1] = xpad[s:s + span, cols]
        parts = []
        for rc in range(tm // CONV_ROWS):
            r0 = rc * CONV_ROWS
            acc = jnp.zeros((CONV_ROWS, CONV_COLS), F32)
            for k in range(CONV_K):
                whole, s = divmod(first + k, F32_SUBLANES)
                base = whole * F32_SUBLANES + r0
                if s == 0:
                    src = xpad[base:base + CONV_ROWS, cols]
                else:
                    src = shifted[s - 1, base:base + CONV_ROWS, :]
                acc = acc + w[k:k + 1, :] * src
            acc = acc + bias
            ybuf[r0:r0 + CONV_ROWS, cols] = acc
            parts.append(jnp.sum(acc * acc, axis=-1, keepdims=True))
        return ssq + jnp.concatenate(parts, axis=0)

    d = x_ref.shape[1]
    ssq = lax.fori_loop(0, d // CONV_COLS, col_chunk, jnp.zeros((tm, 1), F32))
    y = ybuf[...] * lax.rsqrt(ssq * (1.0 / d) + EPS) * g_ref[...]
    o_ref[...] = (y * jax.nn.sigmoid(y)).astype(BF16)


def _conv_mix(xg, dw, dw_b, norm_g, n_prompt, seq, dec_seq):
    t, d = xg.shape
    tm = TM_OUT
    assert seq == tm, "context sequences must be exactly one conv row block"
    npb = n_prompt // tm
    bps = dec_seq // tm
    hb = tm // CONV_HALO
    last = t // CONV_HALO - 1
    return pl.pallas_call(
        functools.partial(_conv_kernel, n_prompt_blocks=npb, blocks_per_seq=bps),
        grid=(t // tm,),
        in_specs=[
            pl.BlockSpec((tm, d), lambda i: (i, 0)),
            pl.BlockSpec((CONV_HALO, d), lambda i: (jnp.maximum(i * hb - 1, 0), 0)),
            pl.BlockSpec((CONV_HALO, d), lambda i: (jnp.minimum((i + 1) * hb, last), 0)),
            pl.BlockSpec((dw.shape[0], d), lambda i: (0, 0)),
            pl.BlockSpec((1, d), lambda i: (0, 0)),
            pl.BlockSpec((1, d), lambda i: (0, 0)),
        ],
        out_specs=pl.BlockSpec((tm, d), lambda i: (i, 0)),
        out_shape=jax.ShapeDtypeStruct((t, d), BF16),
        scratch_shapes=[
            pltpu.VMEM((tm + 2 * CONV_HALO, d), F32),
            pltpu.VMEM((F32_SUBLANES - 1, tm + 2 * CONV_HALO - F32_SUBLANES, CONV_COLS), F32),
            pltpu.VMEM((tm, d), F32),
        ],
        compiler_params=_cparams(("arbitrary",)),
        name="conv_mix",
    )(xg, xg, xg, dw, dw_b, norm_g)


def _router_operands(w_rg, b_rg, w_re, b_re):
    d = w_rg.shape[0]
    used = N_GROUPS_MOE + N_EXPERTS
    w = jnp.concatenate([w_rg, w_re, jnp.zeros((d, ROUTER_LANES - used), F32)], axis=1)
    b = jnp.concatenate([b_rg, b_re, jnp.zeros((ROUTER_LANES - used,), F32)])
    return w.astype(BF16), b.reshape(1, ROUTER_LANES), w.T.astype(BF16), b.reshape(ROUTER_LANES, 1)


def _moe_rows(h2, logits_t, wr, br, w_gate, w_up, w_down, layer):
    t = h2.shape[0]
    n_blocks = -(-(t + N_CLASSES * (MOE_BLK - 1)) // MOE_BLK)
    assert n_blocks <= MAX_MOE_BLOCKS
    pos, ea, eb, n_valid, ends = _route(logits_t)
    pos = pos.reshape(t)
    hs, wg, wu, wd = _dispatch(pos, ends.reshape(-1), h2, n_blocks * MOE_BLK, w_gate, w_up, w_down, layer)
    ys = _moe_experts(ea.reshape(-1), eb.reshape(-1), n_valid.reshape(-1), hs, wr, br, wg, wu, wd, n_blocks)
    return pos, ys


def kernel(x_prompt, x_sample, cache_k, cache_v, c, c_ctx, w_in_even, attn_sink, sgu_norm, sgu_w, sgu_b,
           w_out_even, conv_w_in, conv_dw, conv_dw_b, conv_norm, conv_w_out, ada_w, ada_b, norm_mix, norm_ffn,
           router_group_w, router_group_b, router_expert_w, router_expert_b, expert_w_gate, expert_w_up,
           expert_w_down, final_norm):
    batch, seq, d = x_prompt.shape
    dec_batch, dec_seq, _ = x_sample.shape
    depth = ada_w.shape[0]
    n_prompt = batch * seq
    n_sample = dec_batch * dec_seq
    assert d == D_MODEL and depth == 2 and n_prompt % dec_seq == 0 and dec_seq % TM_PROJ == 0
    x_p = x_prompt.reshape(n_prompt, d)
    x_s = x_sample.reshape(n_sample, d)

    n_mod = -(-(1 + dec_batch) // 8) * 8
    cvec = jnp.concatenate([c_ctx[None, :], c, jnp.zeros((n_mod - 1 - dec_batch, d), F32)], axis=0)
    mods = _ada_params(cvec, ada_w, ada_b).reshape(depth, n_mod, 6, d)
    wg, wu, wd = expert_w_gate, expert_w_up, expert_w_down

    cos_t, sin_t = _rope_tables(dec_seq, TM_PROJ)
    proj, kv32 = _proj_even(x_p, x_s, mods[0], norm_mix[0][None, :], w_in_even[0].astype(BF16), cos_t, sin_t,
                            dec_seq // TM_PROJ)
    kv_col_block = ATTN_WIDTH // (2 * KV_WIDTH)
    sink = attn_sink[0]
    a_p = _ctx_attention(proj, sink, batch, seq, kv_col_block)
    past = cache_k.shape[2]
    ck = cache_k[:, 0].reshape(dec_batch, past, KV_WIDTH).astype(BF16)
    cv = cache_v[:, 0].reshape(dec_batch, past, KV_WIDTH).astype(BF16)
    a_s = _lat_attention(proj, sink, ck, cv, n_prompt, dec_batch, dec_seq, kv_col_block)
    z = _sgu(proj, sgu_norm[0][None, :], sgu_w[0].astype(BF16), sgu_b[0].T)

    wr0, br0, wrt0, brt0 = _router_operands(router_group_w[0], router_group_b[0], router_expert_w[0], router_expert_b[0])
    x1, h2, lg = _out_proj((x_p, x_s), (a_p, a_s, z), w_out_even[0].astype(BF16), mods[0], norm_ffn[0][None, :],
                           wrt0, brt0, n_prompt, dec_seq)
    pos, ys = _moe_rows(h2, lg, wr0, br0, wg, wu, wd, 0)
    gain_f = final_norm[None, :]
    x = _combine(pos, ys, x1, mods[0], gain_f, n_prompt, dec_seq, 0, n_prompt + n_sample, final=False)

    xg = _proj_glu(x, mods[1], norm_mix[1][None, :], conv_w_in[0].astype(BF16), n_prompt, dec_seq)
    dw = jnp.concatenate([conv_dw[0], jnp.zeros((1, d), F32)], axis=0)
    hc = _conv_mix(xg, dw, conv_dw_b[0][None, :], conv_norm[0][None, :], n_prompt, seq, dec_seq)
    wr1, br1, wrt1, brt1 = _router_operands(router_group_w[1], router_group_b[1], router_expert_w[1], router_expert_b[1])
    x1, h2, lg = _out_proj((x,), (hc,), conv_w_out[0].astype(BF16), mods[1], norm_ffn[1][None, :],
                           wrt1, brt1, n_prompt, dec_seq)
    pos, ys = _moe_rows(h2, lg, wr1, br1, wg, wu, wd, 1)
    y_prompt = _combine(pos, ys, x1, mods[1], gain_f, n_prompt, dec_seq, 0, n_prompt, final=True)
    y_sample = _combine(pos, ys, x1, mods[1], gain_f, n_prompt, dec_seq, n_prompt, n_sample, final=True)

    new_k = kv32[:n_prompt, :KV_WIDTH].reshape(batch, 1, seq, N_KV_HEADS, HEAD_DIM)
    new_v = kv32[:n_prompt, KV_WIDTH:].reshape(batch, 1, seq, N_KV_HEADS, HEAD_DIM)
    return (y_prompt.reshape(batch, seq, d), y_sample.reshape(dec_batch, dec_seq, d), new_k, new_v)
```

```python
import functools

import jax
import jax.numpy as jnp
import numpy as np
from jax import lax
from jax.experimental import pallas as pl
from jax.experimental.pallas import tpu as pltpu

F32 = jnp.float32
BF16 = jnp.bfloat16

D_MODEL = 2048
GRID_W = 64
ATTN_BLK = 128
N_HEADS = 16
N_KV_HEADS = 4
Q_GROUPS = N_HEADS // N_KV_HEADS
HEAD_DIM = 64
ATTN_WIDTH = N_HEADS * HEAD_DIM
KV_WIDTH = N_KV_HEADS * HEAD_DIM
ROPE_BASE = 10000.0
SGU_GROUPS = 4
SGU_WIDTH = D_MODEL // 2
SGU_GROUP_DIM = SGU_WIDTH // SGU_GROUPS
CHUNK = 128
IN_WIDTH_EVEN = ATTN_WIDTH + 2 * KV_WIDTH + 2 * SGU_WIDTH
CONV_K = 31
CONV_HALO = 16
N_GROUPS_MOE = 4
EXPERTS_PER_GROUP = 4
N_EXPERTS = N_GROUPS_MOE * EXPERTS_PER_GROUP
PAIRS_PER_GROUP = 6
N_CLASSES = N_GROUPS_MOE * PAIRS_PER_GROUP
D_EXPERT = 512
MOE_BLK = 256
EPS = 1e-6
NEG_INF = -1e30

LANES = 128
ROUTER_LANES = 128
CLASS_ROWS = 32
MAX_MOE_BLOCKS = 128
TM_PROJ = 512
TN_PROJ = 512
TM_OUT = 256
TM_OUTPROJ = 512
TM_SGU = 512
CUM_CHUNK = 256
VMEM_LIMIT = 56 * 1024 * 1024


def _cparams(sem):
    return pltpu.CompilerParams(dimension_semantics=sem, vmem_limit_bytes=VMEM_LIMIT)


def _pack_bf16_pairs(h):
    w = h.shape[1] // 2
    lo = pltpu.bitcast(h[:, :w].astype(BF16).astype(F32), jnp.uint32) >> 16
    hi = pltpu.bitcast(h[:, w:].astype(BF16).astype(F32), jnp.uint32) & jnp.uint32(0xFFFF0000)
    return hi | lo


def _unpack_bf16_pairs(u):
    lo = pltpu.bitcast(u << 16, F32)
    hi = pltpu.bitcast(u & jnp.uint32(0xFFFF0000), F32)
    return jnp.concatenate([lo, hi], axis=1).astype(BF16)


def _rms_mod(x, gain, scale, shift):
    ms = jnp.mean(x * x, axis=-1, keepdims=True)
    y = x * lax.rsqrt(ms + EPS) * gain
    return y * (1.0 + scale) + shift


def _ada_kernel(c_ref, w_ref, b_ref, o_ref):
    c = c_ref[...]
    s = (c * jax.nn.sigmoid(c)).astype(BF16)
    o_ref[...] = jnp.dot(s, w_ref[...].astype(BF16), preferred_element_type=F32) + b_ref[...]


def _ada_params(cvec, ada_w, ada_b):
    depth, d, n = ada_w.shape
    mp = cvec.shape[0]
    tn = 1024
    return pl.pallas_call(
        _ada_kernel,
        grid=(depth, n // tn),
        in_specs=[
            pl.BlockSpec((mp, d), lambda l, j: (0, 0)),
            pl.BlockSpec((None, d, tn), lambda l, j: (l, 0, j)),
            pl.BlockSpec((None, 1, tn), lambda l, j: (l, 0, j)),
        ],
        out_specs=pl.BlockSpec((None, mp, tn), lambda l, j: (l, 0, j)),
        out_shape=jax.ShapeDtypeStruct((depth, mp, n), F32),
        compiler_params=_cparams(("arbitrary", "arbitrary")),
        name="ada_params",
    )(cvec, ada_w, ada_b.reshape(depth, 1, n))


def _swap16(x):
    width = x.shape[-1]
    lane = lax.broadcasted_iota(jnp.int32, x.shape, 1)
    return jnp.where((lane % 32) < 16, pltpu.roll(x, width - 16, 1), pltpu.roll(x, 16, 1))


def _proj_even_kernel(xp_ref, xs_ref, mod_ref, g_ref, w_ref, cos_ref, sin_ref, o_ref, kv_ref, h_scr, *,
                      n_prompt_blocks):
    x = jnp.where(pl.program_id(0) < n_prompt_blocks, xp_ref[...], xs_ref[...])
    h_scr[...] = _rms_mod(x, g_ref[...], mod_ref[1:2, :], mod_ref[0:1, :]).astype(BF16)
    kv_tile = ATTN_WIDTH // TN_PROJ
    for j in range(w_ref.shape[1] // TN_PROJ):
        cols = slice(j * TN_PROJ, (j + 1) * TN_PROJ)
        acc = jnp.dot(h_scr[...], w_ref[:, cols], preferred_element_type=F32)
        if j <= kv_tile:
            tab = slice(0, TN_PROJ) if j < kv_tile else slice(TN_PROJ, 2 * TN_PROJ)
            o_ref[:, cols] = (acc * cos_ref[:, tab] + _swap16(acc) * sin_ref[:, tab]).astype(BF16)
            if j == kv_tile:
                kv_ref[...] = acc
        else:
            o_ref[:, cols] = jax.nn.gelu(acc).astype(BF16)


def _proj_even(x_p, x_s, mods, gain, w, cos_t, sin_t, blocks_per_seq):
    d = x_p.shape[1]
    n = w.shape[1]
    tm, tn = TM_PROJ, TN_PROJ
    npb = x_p.shape[0] // tm
    nsb = x_s.shape[0] // tm
    t = x_p.shape[0] + x_s.shape[0]

    def mod_idx(i):
        return (jnp.where(i < npb, 0, 1 + (i - npb) // blocks_per_seq), 0, 0)

    def tab_idx(i):
        return (jnp.where(i < npb, blocks_per_seq, (i - npb) % blocks_per_seq), 0)

    return pl.pallas_call(
        functools.partial(_proj_even_kernel, n_prompt_blocks=npb),
        grid=(npb + nsb,),
        in_specs=[
            pl.BlockSpec((tm, d), lambda i: (jnp.minimum(i, npb - 1), 0)),
            pl.BlockSpec((tm, d), lambda i: (jnp.maximum(i - npb, 0), 0)),
            pl.BlockSpec((None, 6, d), mod_idx),
            pl.BlockSpec((1, d), lambda i: (0, 0)),
            pl.BlockSpec((d, n), lambda i: (0, 0), pipeline_mode=pl.Buffered(1)),
            pl.BlockSpec((tm, 2 * tn), tab_idx),
            pl.BlockSpec((tm, 2 * tn), tab_idx),
        ],
        out_specs=[
            pl.BlockSpec((tm, n), lambda i: (i, 0)),
            pl.BlockSpec((tm, tn), lambda i: (i, 0)),
        ],
        out_shape=[jax.ShapeDtypeStruct((t, n), BF16), jax.ShapeDtypeStruct((t, tn), F32)],
        scratch_shapes=[pltpu.VMEM((tm, d), BF16)],
        compiler_params=_cparams(("arbitrary",)),
        name="proj_even",
    )(x_p, x_s, mods, gain, w, cos_t, sin_t)


def _rope_tables(dec_seq, tm):
    f32 = np.float32
    n = np.arange(dec_seq)
    row = (n // GRID_W).astype(f32)
    col = (n % GRID_W).astype(f32)
    nf = HEAD_DIM // 4
    inv_freq = np.power(f32(ROPE_BASE), -np.arange(nf, dtype=f32) / f32(nf)).astype(f32)
    ar = row[:, None] * inv_freq[None, :]
    ac = col[:, None] * inv_freq[None, :]
    cos_h = np.concatenate([np.cos(ar), np.cos(ar), np.cos(ac), np.cos(ac)], axis=-1).astype(f32)
    sin_h = np.concatenate([-np.sin(ar), np.sin(ar), -np.sin(ac), np.sin(ac)], axis=-1).astype(f32)
    scale = f32(HEAD_DIM ** -0.5)
    ones = np.ones((dec_seq, KV_WIDTH), f32)
    zeros = np.zeros((dec_seq, KV_WIDTH), f32)
    q_heads = TN_PROJ // HEAD_DIM
    cos_t = np.concatenate([np.tile(cos_h, (1, q_heads)) * scale, np.tile(cos_h, (1, N_KV_HEADS)), ones], axis=-1)
    sin_t = np.concatenate([np.tile(sin_h, (1, q_heads)) * scale, np.tile(sin_h, (1, N_KV_HEADS)), zeros], axis=-1)
    id_cos = np.concatenate([np.full((tm, TN_PROJ), scale, f32), np.ones((tm, TN_PROJ), f32)], axis=-1)
    id_sin = np.zeros((tm, 2 * TN_PROJ), f32)
    return (jnp.asarray(np.concatenate([cos_t, id_cos], axis=0), F32),
            jnp.asarray(np.concatenate([sin_t, id_sin], axis=0), F32))


def _cast_plan(weights, layer, n_steps, step_of):
    n_exp = weights[0].shape[1]
    steps_per_matrix = min(n_steps // len(weights), n_exp)
    assert steps_per_matrix >= 1 and n_exp % steps_per_matrix == 0
    e_blk = n_exp // steps_per_matrix

    def slice_of(k, idx):
        return jnp.clip(step_of(*idx) - k * steps_per_matrix, 0, steps_per_matrix - 1)

    in_specs = [pl.BlockSpec((None, e_blk) + w.shape[2:], lambda *idx, k=k: (layer, slice_of(k, idx), 0, 0))
                for k, w in enumerate(weights)]
    out_specs = [pl.BlockSpec((e_blk,) + w.shape[2:], lambda *idx, k=k: (slice_of(k, idx), 0, 0))
                 for k, w in enumerate(weights)]
    out_shapes = [jax.ShapeDtypeStruct(w.shape[1:], BF16) for w in weights]
    return steps_per_matrix, in_specs, out_specs, out_shapes


def _cast_slices(step, steps_per_matrix, srcs, dsts):
    for k, (src, dst) in enumerate(zip(srcs, dsts)):
        @pl.when((step >= k * steps_per_matrix) & (step < (k + 1) * steps_per_matrix))
        def _(src=src, dst=dst):
            dst[...] = src[...].astype(BF16)


def _sink_attend(q, keys, vals, masks, sink):
    scores = []
    for k, mask in zip(keys, masks):
        s = lax.dot_general(q, k, (((1,), (1,)), ((), ())), preferred_element_type=F32)
        if mask is not None:
            stacked = s.reshape(s.shape[0] // mask.shape[0], *mask.shape)
            s = jnp.where(mask[None], stacked, NEG_INF).reshape(s.shape)
        scores.append(s)
    m = sink
    for s in scores:
        m = jnp.maximum(m, jnp.max(s, axis=-1, keepdims=True))
    es = [jnp.exp(s - m) for s in scores]
    den = jnp.exp(sink - m)
    for e in es:
        den = den + jnp.sum(e, axis=-1, keepdims=True)
    out = None
    for e, v in zip(es, vals):
        o = jnp.dot(e.astype(BF16), v, preferred_element_type=F32)
        out = o if out is None else out + o
    return out * (1.0 / den)


def _grouped_heads_attend(sink_ref, q_ref, o_ref, kh, keys, vals, masks):
    rows = q_ref.shape[0]
    heads = [kh * Q_GROUPS + g for g in range(Q_GROUPS)]
    q = jnp.concatenate([q_ref[:, h * HEAD_DIM:(h + 1) * HEAD_DIM] for h in heads], axis=0)
    ridx = lax.broadcasted_iota(jnp.int32, (Q_GROUPS * rows, 1), 0)
    sink = jnp.full((Q_GROUPS * rows, 1), sink_ref[heads[-1]], F32)
    for g in range(Q_GROUPS - 2, -1, -1):
        sink = jnp.where(ridx < (g + 1) * rows, sink_ref[heads[g]], sink)
    o = _sink_attend(q, keys, vals, masks, sink)
    for g, h in enumerate(heads):
        o_ref[:, h * HEAD_DIM:(h + 1) * HEAD_DIM] = o[g * rows:(g + 1) * rows, :].astype(BF16)


def _ctx_attn_kernel(sink_ref, q_ref, kv_ref, o_ref):
    for kh in range(N_KV_HEADS):
        k = kv_ref[:, kh * HEAD_DIM:(kh + 1) * HEAD_DIM]
        v = kv_ref[:, KV_WIDTH + kh * HEAD_DIM:KV_WIDTH + (kh + 1) * HEAD_DIM]
        _grouped_heads_attend(sink_ref, q_ref, o_ref, kh, [k], [v], [None])


def _ctx_attention(proj, sink, batch, seq, kv_col_block):
    return pl.pallas_call(
        _ctx_attn_kernel,
        grid=(batch,),
        in_specs=[
            pl.BlockSpec(memory_space=pltpu.SMEM),
            pl.BlockSpec((seq, ATTN_WIDTH), lambda b: (b, 0)),
            pl.BlockSpec((seq, 2 * KV_WIDTH), lambda b: (b, kv_col_block)),
        ],
        out_specs=pl.BlockSpec((seq, ATTN_WIDTH), lambda b: (b, 0)),
        out_shape=jax.ShapeDtypeStruct((batch * seq, ATTN_WIDTH), BF16),
        compiler_params=_cparams(("arbitrary",)),
        name="ctx_attention",
    )(sink, proj, proj)


def _lat_attn_kernel(sink_ref, q_ref, kv_ref, ck_ref, cv_ref, wg_ref, wu_ref, wd_ref, o_ref, wg_out, wu_out, wd_out,
                     *, seq, steps_per_matrix):
    i = pl.program_id(1)
    _cast_slices(pl.program_id(0) * pl.num_programs(1) + i, steps_per_matrix,
                 (wg_ref, wu_ref, wd_ref), (wg_out, wu_out, wd_out))
    win = 3 * ATTN_BLK
    ws = pl.multiple_of(jnp.clip((i - 1) * ATTN_BLK, 0, seq - win), ATTN_BLK)
    shape = (ATTN_BLK, win)
    qpos = i * ATTN_BLK + lax.broadcasted_iota(jnp.int32, shape, 0)
    kpos = ws + lax.broadcasted_iota(jnp.int32, shape, 1)
    valid = jnp.abs(qpos - kpos) <= ATTN_BLK
    kvw = kv_ref[pl.ds(ws, win), :]
    for kh in range(N_KV_HEADS):
        k_loc = kvw[:, kh * HEAD_DIM:(kh + 1) * HEAD_DIM]
        v_loc = kvw[:, KV_WIDTH + kh * HEAD_DIM:KV_WIDTH + (kh + 1) * HEAD_DIM]
        k_ctx = ck_ref[:, kh * HEAD_DIM:(kh + 1) * HEAD_DIM]
        v_ctx = cv_ref[:, kh * HEAD_DIM:(kh + 1) * HEAD_DIM]
        _grouped_heads_attend(sink_ref, q_ref, o_ref, kh, [k_ctx, k_loc], [v_ctx, v_loc], [None, valid])


def _lat_attention(proj, sink, cache_k, cache_v, n_prompt, dec_batch, dec_seq, kv_col_block, weights, layer):
    qb = dec_seq // ATTN_BLK
    q_off = n_prompt // ATTN_BLK
    s_off = n_prompt // dec_seq
    past = cache_k.shape[1]
    spm, w_in, w_out, w_shapes = _cast_plan(weights, layer, dec_batch * qb, lambda b, i: b * qb + i)
    return pl.pallas_call(
        functools.partial(_lat_attn_kernel, seq=dec_seq, steps_per_matrix=spm),
        grid=(dec_batch, qb),
        in_specs=[
            pl.BlockSpec(memory_space=pltpu.SMEM),
            pl.BlockSpec((ATTN_BLK, ATTN_WIDTH), lambda b, i: (q_off + b * qb + i, 0)),
            pl.BlockSpec((dec_seq, 2 * KV_WIDTH), lambda b, i: (s_off + b, kv_col_block)),
            pl.BlockSpec((None, past, KV_WIDTH), lambda b, i: (b, 0, 0)),
            pl.BlockSpec((None, past, KV_WIDTH), lambda b, i: (b, 0, 0)),
        ] + w_in,
        out_specs=[pl.BlockSpec((ATTN_BLK, ATTN_WIDTH), lambda b, i: (b * qb + i, 0))] + w_out,
        out_shape=[jax.ShapeDtypeStruct((dec_batch * dec_seq, ATTN_WIDTH), BF16)] + w_shapes,
        compiler_params=_cparams(("arbitrary", "arbitrary")),
        name="lat_attention",
    )(sink, proj, proj, cache_k, cache_v, *weights)


def _sgu_kernel(ul_ref, uh_ref, vl_ref, vh_ref, g_ref, w_ref, bt_ref, z_ref):
    half = SGU_WIDTH // 2
    per_half = SGU_GROUPS // 2
    u_refs = (ul_ref, uh_ref)
    for c in range(TM_SGU // CHUNK):
        rows = slice(c * CHUNK, (c + 1) * CHUNK)
        vl = vl_ref[rows, :].astype(F32)
        vh = vh_ref[rows, :].astype(F32)
        ssq = jnp.sum(vl * vl, axis=-1, keepdims=True) + jnp.sum(vh * vh, axis=-1, keepdims=True)
        r = lax.rsqrt(ssq * (1.0 / SGU_WIDTH) + EPS)
        vn = ((vl * r * g_ref[:, :half]).astype(BF16), (vh * r * g_ref[:, half:]).astype(BF16))
        for g in range(SGU_GROUPS):
            src = g // per_half
            cols = slice((g % per_half) * SGU_GROUP_DIM, (g % per_half + 1) * SGU_GROUP_DIM)
            mixed = jnp.dot(w_ref[g], vn[src][:, cols], preferred_element_type=F32) + bt_ref[:, g:g + 1]
            z_ref[rows, g * SGU_GROUP_DIM:(g + 1) * SGU_GROUP_DIM] = (
                u_refs[src][rows, cols].astype(F32) * mixed).astype(BF16)


def _sgu(proj, norm_g, w_s, b_t):
    t = proj.shape[0]
    half = SGU_WIDTH // 2
    u0 = (ATTN_WIDTH + 2 * KV_WIDTH) // half
    blocks = [pl.BlockSpec((TM_SGU, half), functools.partial(lambda i, c: (i, c), c=u0 + k)) for k in range(4)]
    return pl.pallas_call(
        _sgu_kernel,
        grid=(t // TM_SGU,),
        in_specs=blocks + [
            pl.BlockSpec((1, SGU_WIDTH), lambda i: (0, 0)),
            pl.BlockSpec((SGU_GROUPS, CHUNK, CHUNK), lambda i: (0, 0, 0)),
            pl.BlockSpec((CHUNK, SGU_GROUPS), lambda i: (0, 0)),
        ],
        out_specs=pl.BlockSpec((TM_SGU, SGU_WIDTH), lambda i: (i, 0)),
        out_shape=jax.ShapeDtypeStruct((t, SGU_WIDTH), BF16),
        compiler_params=_cparams(("arbitrary",)),
        name="sgu",
    )(proj, proj, proj, proj, norm_g, w_s, b_t)


def _residual_router(rows, x, acc, mod_ref, g2_ref, wrt_ref, brt_ref, x1_ref, h2_ref, lg_ref):
    x1 = x + mod_ref[2:3, :] * acc
    x1_ref[rows, :] = x1
    h2 = _rms_mod(x1, g2_ref[...], mod_ref[4:5, :], mod_ref[3:4, :])
    h2_ref[rows, :] = _pack_bf16_pairs(h2)
    lg = lax.dot_general(wrt_ref[...], h2.astype(BF16), (((1,), (1,)), ((), ())), preferred_element_type=F32)
    lg_ref[:, rows] = lg + brt_ref[...]


def _sub_blocks(ref):
    return [slice(r, r + TM_OUT) for r in range(0, ref.shape[0], TM_OUT)]


def _out_even_kernel(xp_ref, xs_ref, ap_ref, as_ref, z_ref, w_ref, mod_ref, g2_ref, wrt_ref, brt_ref,
                     x1_ref, h2_ref, lg_ref, *, n_prompt_blocks):
    is_prompt = pl.program_id(0) < n_prompt_blocks
    for rows in _sub_blocks(x1_ref):
        a = jnp.where(is_prompt, ap_ref[rows, :].astype(F32), as_ref[rows, :].astype(F32)).astype(BF16)
        acc = jnp.dot(a, w_ref[0:ATTN_WIDTH, :], preferred_element_type=F32)
        acc = acc + jnp.dot(z_ref[rows, :], w_ref[ATTN_WIDTH:, :], preferred_element_type=F32)
        x = jnp.where(is_prompt, xp_ref[rows, :], xs_ref[rows, :])
        _residual_router(rows, x, acc, mod_ref, g2_ref, wrt_ref, brt_ref, x1_ref, h2_ref, lg_ref)


def _out_odd_kernel(x_ref, hc_ref, w_ref, mod_ref, g2_ref, wrt_ref, brt_ref, x1_ref, h2_ref, lg_ref):
    for rows in _sub_blocks(x1_ref):
        acc = jnp.dot(hc_ref[rows, :], w_ref[...], preferred_element_type=F32)
        _residual_router(rows, x_ref[rows, :], acc, mod_ref, g2_ref, wrt_ref, brt_ref, x1_ref, h2_ref, lg_ref)


def _out_proj(xs, lhs, w, mods, gain2, wrt, brt, n_prompt, dec_seq):
    d = xs[0].shape[1]
    t = sum(x.shape[0] for x in xs)
    tm = TM_OUTPROJ
    npb = n_prompt // tm
    bps = dec_seq // tm
    nsb = t // tm - npb
    if len(xs) == 2:
        x_specs = [pl.BlockSpec((tm, d), lambda i: (jnp.minimum(i, npb - 1), 0)),
                   pl.BlockSpec((tm, d), lambda i: (jnp.maximum(i - npb, 0), 0))]
    else:
        x_specs = [pl.BlockSpec((tm, d), lambda i: (i, 0))]

    def mod_idx(i):
        return (jnp.where(i < npb, 0, 1 + (i - npb) // bps), 0, 0)

    if len(lhs) == 3:
        a_p, a_s, z = lhs
        body = functools.partial(_out_even_kernel, n_prompt_blocks=npb)
        lhs_specs = [
            pl.BlockSpec((tm, ATTN_WIDTH), lambda i: (jnp.minimum(i, npb - 1), 0)),
            pl.BlockSpec((tm, ATTN_WIDTH), lambda i: (jnp.clip(i - npb, 0, nsb - 1), 0)),
            pl.BlockSpec((tm, SGU_WIDTH), lambda i: (i, 0)),
        ]
    else:
        body = _out_odd_kernel
        lhs_specs = [pl.BlockSpec((tm, d), lambda i: (i, 0))]
    return pl.pallas_call(
        body,
        grid=(t // tm,),
        in_specs=x_specs + lhs_specs + [
            pl.BlockSpec((w.shape[0], d), lambda i: (0, 0), pipeline_mode=pl.Buffered(1)),
            pl.BlockSpec((None, 6, d), mod_idx),
            pl.BlockSpec((1, d), lambda i: (0, 0)),
            pl.BlockSpec((ROUTER_LANES, d), lambda i: (0, 0)),
            pl.BlockSpec((ROUTER_LANES, 1), lambda i: (0, 0)),
        ],
        out_specs=[
            pl.BlockSpec((tm, d), lambda i: (i, 0)),
            pl.BlockSpec((tm, d // 2), lambda i: (i, 0)),
            pl.BlockSpec((ROUTER_LANES, tm), lambda i: (0, i)),
        ],
        out_shape=[
            jax.ShapeDtypeStruct((t, d), F32),
            jax.ShapeDtypeStruct((t, d // 2), jnp.uint32),
            jax.ShapeDtypeStruct((ROUTER_LANES, t), F32),
        ],
        compiler_params=_cparams(("arbitrary",)),
        name="out_proj",
    )(*xs, *lhs, w, mods, gain2, wrt, brt)


def _first_argmax4(v0, v1, v2, v3):
    m = jnp.maximum(jnp.maximum(v0, v1), jnp.maximum(v2, v3))
    idx = jnp.where(v0 == m, 0.0, jnp.where(v1 == m, 1.0, jnp.where(v2 == m, 2.0, 3.0)))
    return m, idx


def _route_kernel(lg_ref, pos_ref, ea_ref, eb_ref, nv_ref, ends_ref, oh_scr, rank_scr, *, n_tokens):
    lg = lg_ref[...]
    rows = [lg[r:r + 1, :] for r in range(N_GROUPS_MOE + N_EXPERTS)]
    _, gidx = _first_argmax4(*rows[:N_GROUPS_MOE])
    e = []
    for k in range(EXPERTS_PER_GROUP):
        cand = [rows[N_GROUPS_MOE + g * EXPERTS_PER_GROUP + k] for g in range(N_GROUPS_MOE)]
        e.append(jnp.where(gidx == 0.0, cand[0], jnp.where(gidx == 1.0, cand[1], jnp.where(gidx == 2.0, cand[2], cand[3]))))
    _, l1 = _first_argmax4(*e)
    e2 = [jnp.where(l1 == float(k), -jnp.inf, e[k]) for k in range(EXPERTS_PER_GROUP)]
    _, l2 = _first_argmax4(*e2)
    lo = jnp.minimum(l1, l2)
    hi = jnp.maximum(l1, l2)
    pair = jnp.where(lo == 0.0, hi - 1.0, jnp.where(lo == 1.0, hi + 1.0, 5.0))
    cls = gidx * float(PAIRS_PER_GROUP) + pair

    crow = lax.broadcasted_iota(jnp.int32, (CLASS_ROWS, n_tokens), 0).astype(F32)
    oh_scr[...] = jnp.where(crow == cls, 1.0, 0.0).astype(BF16)

    jj = lax.broadcasted_iota(jnp.int32, (CUM_CHUNK, CUM_CHUNK), 0)
    ii = lax.broadcasted_iota(jnp.int32, (CUM_CHUNK, CUM_CHUNK), 1)
    tri = jnp.where(jj < ii, 1.0, 0.0).astype(BF16)

    def chunk(c, carry):
        off = pl.multiple_of(c * CUM_CHUNK, CUM_CHUNK)
        oh = oh_scr[:, pl.ds(off, CUM_CHUNK)]
        ohf = oh.astype(F32)
        cum = jnp.dot(oh, tri, preferred_element_type=F32) + carry
        rank_scr[:, pl.ds(off, CUM_CHUNK)] = jnp.sum(cum * ohf, axis=0, keepdims=True)
        return carry + jnp.sum(ohf, axis=1, keepdims=True)

    counts = lax.fori_loop(0, n_tokens // CUM_CHUNK, chunk, jnp.zeros((CLASS_ROWS, 1), F32))
    padded = jnp.floor((counts + float(MOE_BLK - 1)) * (1.0 / MOE_BLK)) * float(MOE_BLK)
    run = jnp.zeros((1, 1), F32)
    starts_rows = []
    for c in range(CLASS_ROWS):
        starts_rows.append(run)
        run = run + padded[c:c + 1, :]
    starts = jnp.concatenate(starts_rows, axis=0)
    ends = starts + padded
    total = run

    ohf = oh_scr[...].astype(F32)
    pos = rank_scr[...] + jnp.sum(ohf * starts, axis=0, keepdims=True)
    pos_ref[...] = pos.astype(jnp.int32)

    bstart = lax.broadcasted_iota(jnp.int32, (CLASS_ROWS, MAX_MOE_BLOCKS), 1).astype(F32) * float(MOE_BLK)
    brow = lax.broadcasted_iota(jnp.int32, (CLASS_ROWS, MAX_MOE_BLOCKS), 0)
    done = jnp.where((ends <= bstart) & (brow < N_CLASSES), 1.0, 0.0)
    bcls = jnp.minimum(jnp.sum(done, axis=0, keepdims=True), float(N_CLASSES - 1))
    grp = (jnp.where(bcls >= 6.0, 1.0, 0.0) + jnp.where(bcls >= 12.0, 1.0, 0.0) + jnp.where(bcls >= 18.0, 1.0, 0.0))
    bp = bcls - grp * float(PAIRS_PER_GROUP)
    ge3 = jnp.where(bp >= 3.0, 1.0, 0.0)
    ge5 = jnp.where(bp >= 5.0, 1.0, 0.0)
    blo = ge3 + ge5
    bhi = bp + 1.0 - 2.0 * ge3 - ge5
    ea_ref[...] = (grp * float(EXPERTS_PER_GROUP) + blo).astype(jnp.int32)
    eb_ref[...] = (grp * float(EXPERTS_PER_GROUP) + bhi).astype(jnp.int32)
    own = jnp.where(brow.astype(F32) == bcls, starts + counts, 0.0)
    filled = jnp.sum(own, axis=0, keepdims=True) - bstart[0:1, :]
    nv_ref[...] = jnp.clip(filled, 0.0, float(MOE_BLK)).astype(jnp.int32)
    lane = lax.broadcasted_iota(jnp.int32, (1, MAX_MOE_BLOCKS), 1)
    ends_row = jnp.zeros((1, MAX_MOE_BLOCKS), F32)
    for c in range(N_CLASSES):
        ends_row = jnp.where(lane == c, starts_rows[c] + padded[c:c + 1, :], ends_row)
    ends_ref[...] = ends_row.astype(jnp.int32)


def _route(logits_t):
    t = logits_t.shape[1]
    rows = N_CLASSES
    i32 = jnp.int32
    return pl.pallas_call(
        functools.partial(_route_kernel, n_tokens=t),
        grid=(1,),
        in_specs=[pl.BlockSpec((rows, t), lambda i: (0, 0))],
        out_specs=[
            pl.BlockSpec((1, t), lambda i: (0, 0)),
        ] + [pl.BlockSpec((1, MAX_MOE_BLOCKS), lambda i: (0, 0))] * 4,
        out_shape=[jax.ShapeDtypeStruct((1, t), i32)] + [jax.ShapeDtypeStruct((1, MAX_MOE_BLOCKS), i32)] * 4,
        scratch_shapes=[pltpu.VMEM((CLASS_ROWS, t), BF16), pltpu.VMEM((1, t), F32)],
        compiler_params=_cparams(("arbitrary",)),
        name="route",
    )(logits_t)


ROW_UNROLL = 256
DMA_THREADS = 2


def _row_copy(src_ref, src_row, dst_ref, dst_row, sem):
    return pltpu.make_async_copy(src_ref.at[pl.ds(src_row, 1), :], dst_ref.at[pl.ds(dst_row, 1), :], sem)


def _start_row_copies(n_rows, make_copy):
    def body(i, carry):
        for u in range(ROW_UNROLL):
            make_copy(i * ROW_UNROLL + u).start(priority=u % DMA_THREADS)
        return carry

    if n_rows == ROW_UNROLL:
        body(0, 0)
    else:
        lax.fori_loop(0, n_rows // ROW_UNROLL, body, 0)


def _dispatch_kernel(pos_ref, ends_ref, h_ref, o_ref, zbuf, sem, zsem):
    step = pl.program_id(0)

    @pl.when(step == 0)
    def _():
        zbuf[...] = jnp.zeros_like(zbuf)
        total = ends_ref[N_CLASSES - 1]
        n_free = (o_ref.shape[0] - total) // MOE_BLK

        def zero_block(first_row, phase):
            first = pl.multiple_of(first_row, MOE_BLK)
            getattr(pltpu.make_async_copy(zbuf, o_ref.at[pl.ds(first, MOE_BLK), :], zsem), phase)()

        for phase in ("start", "wait"):
            prev = 0
            for c in range(N_CLASSES):
                end = ends_ref[c]
                pl.when(end > prev)(functools.partial(zero_block, end - MOE_BLK, phase))
                prev = end

            def free_block(k, carry, phase=phase):
                zero_block(total + k * MOE_BLK, phase)
                return carry

            lax.fori_loop(0, n_free, free_block, 0)

    base = step * TM_OUT

    _start_row_copies(TM_OUT, lambda r: _row_copy(h_ref, r, o_ref, pos_ref[base + r], sem))
    pltpu.make_async_copy(h_ref, o_ref.at[pl.ds(0, TM_OUT), :], sem).wait()


def _dispatch(pos, ends, h2, n_rows):
    t, d = h2.shape
    return pl.pallas_call(
        _dispatch_kernel,
        grid_spec=pltpu.PrefetchScalarGridSpec(
            num_scalar_prefetch=2,
            grid=(t // TM_OUT,),
            in_specs=[pl.BlockSpec((TM_OUT, d), lambda i, pos, ends: (i, 0))],
            out_specs=pl.BlockSpec(memory_space=pl.ANY),
            scratch_shapes=[pltpu.VMEM((MOE_BLK, d), h2.dtype), pltpu.SemaphoreType.DMA(()),
                            pltpu.SemaphoreType.DMA(())],
        ),
        out_shape=jax.ShapeDtypeStruct((n_rows, d), h2.dtype),
        compiler_params=_cparams(("arbitrary",)),
        name="moe_dispatch",
    )(pos, ends, h2)


def _expert_ffn(x, wg_ref, wu_ref, wd_ref):
    gate = jnp.dot(x, wg_ref[...], preferred_element_type=F32)
    up = jnp.dot(x, wu_ref[...], preferred_element_type=F32)
    hidden = (gate * jax.nn.sigmoid(gate) * up).astype(BF16)
    return jnp.dot(hidden, wd_ref[...], preferred_element_type=F32)


def _moe_kernel(ea_ref, eb_ref, nv_ref, x_ref, wr_ref, br_ref, wga, wua, wda, wgb, wub, wdb, y_ref):
    b = pl.program_id(0)
    n_valid = nv_ref[b]

    def run(rows):
        x = _unpack_bf16_pairs(x_ref[rows, :])
        lg = jnp.dot(x, wr_ref[...], preferred_element_type=F32) + br_ref[...]
        lane = lax.broadcasted_iota(jnp.int32, lg.shape, 1)
        ea = ea_ref[b]
        eb = eb_ref[b]
        grp = ea // EXPERTS_PER_GROUP
        is_grp = lane < N_GROUPS_MOE
        gm = jnp.max(jnp.where(is_grp, lg, -jnp.inf), axis=-1, keepdims=True)
        ge = jnp.where(is_grp, jnp.exp(lg - gm), 0.0)
        p_g = jnp.sum(jnp.where(lane == grp, ge, 0.0), axis=-1, keepdims=True) / jnp.sum(ge, axis=-1, keepdims=True)
        la = jnp.sum(jnp.where(lane == N_GROUPS_MOE + ea, lg, 0.0), axis=-1, keepdims=True)
        lb = jnp.sum(jnp.where(lane == N_GROUPS_MOE + eb, lg, 0.0), axis=-1, keepdims=True)
        mm = jnp.maximum(la, lb)
        xa = jnp.exp(la - mm)
        xb = jnp.exp(lb - mm)
        ga = xa / (xa + xb) * p_g
        gb = xb / (xa + xb) * p_g
        ya = _expert_ffn(x, wga, wua, wda)
        yb = _expert_ffn(x, wgb, wub, wdb)
        y_ref[rows, :] = ya * ga + yb * gb

    half = MOE_BLK // 2

    @pl.when(n_valid > half)
    def _():
        run(slice(0, MOE_BLK))

    @pl.when((n_valid > 0) & (n_valid <= half))
    def _():
        run(slice(0, half))
        y_ref[half:, :] = jnp.zeros((MOE_BLK - half, y_ref.shape[1]), y_ref.dtype)

    @pl.when(n_valid == 0)
    def _():
        y_ref[...] = jnp.zeros_like(y_ref)


def _moe_experts(ea, eb, n_valid, hs, wr, br, w_gate, w_up, w_down, n_blocks):
    d = w_gate.shape[1]
    de = w_gate.shape[2]

    def wa_idx(b, ea, eb, nv):
        return (ea[b], 0, 0)

    def wb_idx(b, ea, eb, nv):
        return (eb[b], 0, 0)

    return pl.pallas_call(
        _moe_kernel,
        grid_spec=pltpu.PrefetchScalarGridSpec(
            num_scalar_prefetch=3,
            grid=(n_blocks,),
            in_specs=[
                pl.BlockSpec((MOE_BLK, hs.shape[1]), lambda b, ea, eb, nv: (jnp.where(nv[b] > 0, b, 0), 0)),
                pl.BlockSpec((d, ROUTER_LANES), lambda b, ea, eb, nv: (0, 0)),
                pl.BlockSpec((1, ROUTER_LANES), lambda b, ea, eb, nv: (0, 0)),
                pl.BlockSpec((None, d, de), wa_idx),
                pl.BlockSpec((None, d, de), wa_idx),
                pl.BlockSpec((None, de, d), wa_idx),
                pl.BlockSpec((None, d, de), wb_idx),
                pl.BlockSpec((None, d, de), wb_idx),
                pl.BlockSpec((None, de, d), wb_idx),
            ],
            out_specs=pl.BlockSpec((MOE_BLK, d), lambda b, ea, eb, nv: (b, 0)),
        ),
        out_shape=jax.ShapeDtypeStruct((n_blocks * MOE_BLK, d), F32),
        compiler_params=_cparams(("arbitrary",)),
        name="moe_experts",
    )(ea, eb, n_valid, hs, wr, br, w_gate, w_up, w_down, w_gate, w_up, w_down)


def _combine_kernel(pos_ref, ys_ref, x1_ref, mod_ref, gf_ref, o_ref, ybuf, sem, *, block_off, n_steps, final):
    i = pl.program_id(0)
    slot = i % 2

    def start_gather(step, into):
        base = (step + block_off) * TM_OUT
        _start_row_copies(
            TM_OUT, lambda r: _row_copy(ys_ref, pos_ref[base + r], ybuf.at[into], r, sem.at[into]))

    pl.when(i == 0)(lambda: start_gather(0, 0))
    pl.when(i + 1 < n_steps)(lambda: start_gather(i + 1, 1 - slot))
    pltpu.make_async_copy(ys_ref.at[pl.ds(0, TM_OUT), :], ybuf.at[slot], sem.at[slot]).wait()
    x2 = x1_ref[...] + mod_ref[5:6, :] * ybuf[slot]
    if final:
        ms = jnp.mean(x2 * x2, axis=-1, keepdims=True)
        x2 = x2 * lax.rsqrt(ms + EPS) * gf_ref[...]
    o_ref[...] = x2


def _combine(pos, ys, x1, mods, gain_f, n_prompt, dec_seq, row_off, n_rows, final):
    d = x1.shape[1]
    tm = TM_OUT
    npb = n_prompt // tm
    bps = dec_seq // tm
    boff = row_off // tm

    def mod_idx(i, pos):
        blk = i + boff
        return (jnp.where(blk < npb, 0, 1 + (blk - npb) // bps), 0, 0)

    return pl.pallas_call(
        functools.partial(_combine_kernel, block_off=boff, n_steps=n_rows // tm, final=final),
        grid_spec=pltpu.PrefetchScalarGridSpec(
            num_scalar_prefetch=1,
            grid=(n_rows // tm,),
            in_specs=[
                pl.BlockSpec(memory_space=pl.ANY),
                pl.BlockSpec((tm, d), lambda i, pos: (i + boff, 0)),
                pl.BlockSpec((None, 6, d), mod_idx),
                pl.BlockSpec((1, d), lambda i, pos: (0, 0)),
            ],
            out_specs=pl.BlockSpec((tm, d), lambda i, pos: (i, 0)),
            scratch_shapes=[pltpu.VMEM((2, tm, d), F32), pltpu.SemaphoreType.DMA((2,))],
        ),
        out_shape=jax.ShapeDtypeStruct((n_rows, d), F32),
        compiler_params=_cparams(("arbitrary",)),
        name="moe_combine",
    )(pos, ys, x1, mods, gain_f)


def _proj_glu_kernel(x_ref, mod_ref, g_ref, w_ref, o_ref, h_scr):
    h_scr[...] = _rms_mod(x_ref[...], g_ref[...], mod_ref[1:2, :], mod_ref[0:1, :]).astype(BF16)
    width = w_ref.shape[1] // 2
    for j in range(width // TN_PROJ):
        cols = slice(j * TN_PROJ, (j + 1) * TN_PROJ)
        gcols = slice(width + j * TN_PROJ, width + (j + 1) * TN_PROJ)
        val = jnp.dot(h_scr[...], w_ref[:, cols], preferred_element_type=F32)
        gate = jnp.dot(h_scr[...], w_ref[:, gcols], preferred_element_type=F32)
        o_ref[:, cols] = (val * jax.nn.sigmoid(gate)).astype(BF16)


def _proj_glu(x, mods, gain, w, n_prompt, dec_seq):
    t, d = x.shape
    n = w.shape[1]
    tm = TM_PROJ
    npb = n_prompt // tm
    bps = dec_seq // tm

    def mod_idx(i):
        return (jnp.where(i < npb, 0, 1 + (i - npb) // bps), 0, 0)

    return pl.pallas_call(
        _proj_glu_kernel,
        grid=(t // tm,),
        in_specs=[
            pl.BlockSpec((tm, d), lambda i: (i, 0)),
            pl.BlockSpec((None, 6, d), mod_idx),
            pl.BlockSpec((1, d), lambda i: (0, 0)),
            pl.BlockSpec((d, n), lambda i: (0, 0), pipeline_mode=pl.Buffered(1)),
        ],
        out_specs=pl.BlockSpec((tm, n // 2), lambda i: (i, 0)),
        out_shape=jax.ShapeDtypeStruct((t, n // 2), BF16),
        scratch_shapes=[pltpu.VMEM((tm, d), BF16)],
        compiler_params=_cparams(("arbitrary",)),
        name="proj_glu",
    )(x, mods, gain, w)


CONV_ROWS = 64
CONV_COLS = 256
F32_SUBLANES = 8


def _conv_kernel(x_ref, prev_ref, next_ref, dw_ref, b_ref, g_ref, wg_ref, wu_ref, wd_ref, o_ref, wg_out, wu_out, wd_out,
                 xpad, shifted, ybuf, *, n_prompt_blocks, blocks_per_seq, steps_per_matrix):
    i = pl.program_id(0)
    _cast_slices(i, steps_per_matrix, (wg_ref, wu_ref, wd_ref), (wg_out, wu_out, wd_out))
    in_sample = i >= n_prompt_blocks
    j = i - n_prompt_blocks
    is_start = jnp.logical_or(jnp.logical_not(in_sample), j % blocks_per_seq == 0)
    is_end = jnp.logical_or(jnp.logical_not(in_sample), j % blocks_per_seq == blocks_per_seq - 1)
    tm = TM_OUT
    xpad[0:CONV_HALO, :] = jnp.where(is_start, 0.0, prev_ref[...].astype(F32))
    xpad[CONV_HALO:CONV_HALO + tm, :] = x_ref[...].astype(F32)
    xpad[CONV_HALO + tm:, :] = jnp.where(is_end, 0.0, next_ref[...].astype(F32))
    first = CONV_HALO - CONV_K // 2
    span = shifted.shape[1]

    def col_chunk(c, ssq):
        c0 = pl.multiple_of(c * CONV_COLS, CONV_COLS)
        cols = pl.ds(c0, CONV_COLS)
        w = dw_ref[:, cols]
        bias = b_ref[:, cols]
        for s in range(1, F32_SUBLANES):
            shifted[s - 1] = xpad[s:s + span, cols]
        parts = []
        for rc in range(tm // CONV_ROWS):
            r0 = rc * CONV_ROWS
            acc = jnp.zeros((CONV_ROWS, CONV_COLS), F32)
            for k in range(CONV_K):
                whole, s = divmod(first + k, F32_SUBLANES)
                base = whole * F32_SUBLANES + r0
                if s == 0:
                    src = xpad[base:base + CONV_ROWS, cols]
                else:
                    src = shifted[s - 1, base:base + CONV_ROWS, :]
                acc = acc + w[k:k + 1, :] * src
            acc = acc + bias
            ybuf[r0:r0 + CONV_ROWS, cols] = acc
            parts.append(jnp.sum(acc * acc, axis=-1, keepdims=True))
        return ssq + jnp.concatenate(parts, axis=0)

    d = x_ref.shape[1]
    ssq = lax.fori_loop(0, d // CONV_COLS, col_chunk, jnp.zeros((tm, 1), F32))
    y = ybuf[...] * lax.rsqrt(ssq * (1.0 / d) + EPS) * g_ref[...]
    o_ref[...] = (y * jax.nn.sigmoid(y)).astype(BF16)


def _conv_mix(xg, dw, dw_b, norm_g, n_prompt, seq, dec_seq, weights, layer):
    t, d = xg.shape
    tm = TM_OUT
    assert seq == tm, "context sequences must be exactly one conv row block"
    npb = n_prompt // tm
    bps = dec_seq // tm
    hb = tm // CONV_HALO
    last = t // CONV_HALO - 1
    spm, w_in, w_out, w_shapes = _cast_plan(weights, layer, t // tm, lambda i: i)
    return pl.pallas_call(
        functools.partial(_conv_kernel, n_prompt_blocks=npb, blocks_per_seq=bps, steps_per_matrix=spm),
        grid=(t // tm,),
        in_specs=[
            pl.BlockSpec((tm, d), lambda i: (i, 0)),
            pl.BlockSpec((CONV_HALO, d), lambda i: (jnp.maximum(i * hb - 1, 0), 0)),
            pl.BlockSpec((CONV_HALO, d), lambda i: (jnp.minimum((i + 1) * hb, last), 0)),
            pl.BlockSpec((dw.shape[0], d), lambda i: (0, 0)),
            pl.BlockSpec((1, d), lambda i: (0, 0)),
            pl.BlockSpec((1, d), lambda i: (0, 0)),
        ] + w_in,
        out_specs=[pl.BlockSpec((tm, d), lambda i: (i, 0))] + w_out,
        out_shape=[jax.ShapeDtypeStruct((t, d), BF16)] + w_shapes,
        scratch_shapes=[
            pltpu.VMEM((tm + 2 * CONV_HALO, d), F32),
            pltpu.VMEM((F32_SUBLANES - 1, tm + 2 * CONV_HALO - F32_SUBLANES, CONV_COLS), F32),
            pltpu.VMEM((tm, d), F32),
        ],
        compiler_params=_cparams(("arbitrary",)),
        name="conv_mix",
    )(xg, xg, xg, dw, dw_b, norm_g, *weights)


def _router_operands(w_rg, b_rg, w_re, b_re):
    d = w_rg.shape[0]
    used = N_GROUPS_MOE + N_EXPERTS
    w = jnp.concatenate([w_rg, w_re, jnp.zeros((d, ROUTER_LANES - used), F32)], axis=1)
    b = jnp.concatenate([b_rg, b_re, jnp.zeros((ROUTER_LANES - used,), F32)])
    return w.astype(BF16), b.reshape(1, ROUTER_LANES), w.T.astype(BF16), b.reshape(ROUTER_LANES, 1)


def _moe_rows(h2, logits_t, wr, br, wg, wu, wd):
    t = h2.shape[0]
    n_blocks = -(-(t + N_CLASSES * (MOE_BLK - 1)) // MOE_BLK)
    assert n_blocks <= MAX_MOE_BLOCKS
    pos, ea, eb, n_valid, ends = _route(logits_t)
    pos = pos.reshape(t)
    hs = _dispatch(pos, ends.reshape(-1), h2, n_blocks * MOE_BLK)
    ys = _moe_experts(ea.reshape(-1), eb.reshape(-1), n_valid.reshape(-1), hs, wr, br, wg, wu, wd, n_blocks)
    return pos, ys


def kernel(x_prompt, x_sample, cache_k, cache_v, c, c_ctx, w_in_even, attn_sink, sgu_norm, sgu_w, sgu_b,
           w_out_even, conv_w_in, conv_dw, conv_dw_b, conv_norm, conv_w_out, ada_w, ada_b, norm_mix, norm_ffn,
           router_group_w, router_group_b, router_expert_w, router_expert_b, expert_w_gate, expert_w_up,
           expert_w_down, final_norm):
    batch, seq, d = x_prompt.shape
    dec_batch, dec_seq, _ = x_sample.shape
    depth = ada_w.shape[0]
    n_prompt = batch * seq
    n_sample = dec_batch * dec_seq
    assert d == D_MODEL and depth == 2 and n_prompt % dec_seq == 0 and dec_seq % TM_PROJ == 0
    x_p = x_prompt.reshape(n_prompt, d)
    x_s = x_sample.reshape(n_sample, d)

    n_mod = -(-(1 + dec_batch) // 8) * 8
    cvec = jnp.concatenate([c_ctx[None, :], c, jnp.zeros((n_mod - 1 - dec_batch, d), F32)], axis=0)
    mods = _ada_params(cvec, ada_w, ada_b).reshape(depth, n_mod, 6, d)
    expert_w = (expert_w_gate, expert_w_up, expert_w_down)

    cos_t, sin_t = _rope_tables(dec_seq, TM_PROJ)
    proj, kv32 = _proj_even(x_p, x_s, mods[0], norm_mix[0][None, :], w_in_even[0].astype(BF16), cos_t, sin_t,
                            dec_seq // TM_PROJ)
    kv_col_block = ATTN_WIDTH // (2 * KV_WIDTH)
    sink = attn_sink[0]
    a_p = _ctx_attention(proj, sink, batch, seq, kv_col_block)
    past = cache_k.shape[2]
    ck = cache_k[:, 0].reshape(dec_batch, past, KV_WIDTH).astype(BF16)
    cv = cache_v[:, 0].reshape(dec_batch, past, KV_WIDTH).astype(BF16)
    a_s, *experts0 = _lat_attention(proj, sink, ck, cv, n_prompt, dec_batch, dec_seq, kv_col_block, expert_w, 0)
    z = _sgu(proj, sgu_norm[0][None, :], sgu_w[0].astype(BF16), sgu_b[0].T)

    wr0, br0, wrt0, brt0 = _router_operands(router_group_w[0], router_group_b[0], router_expert_w[0], router_expert_b[0])
    x1, h2, lg = _out_proj((x_p, x_s), (a_p, a_s, z), w_out_even[0].astype(BF16), mods[0], norm_ffn[0][None, :],
                           wrt0, brt0, n_prompt, dec_seq)
    pos, ys = _moe_rows(h2, lg, wr0, br0, *experts0)
    gain_f = final_norm[None, :]
    x = _combine(pos, ys, x1, mods[0], gain_f, n_prompt, dec_seq, 0, n_prompt + n_sample, final=False)

    xg = _proj_glu(x, mods[1], norm_mix[1][None, :], conv_w_in[0].astype(BF16), n_prompt, dec_seq)
    dw = jnp.concatenate([conv_dw[0], jnp.zeros((1, d), F32)], axis=0)
    hc, *experts1 = _conv_mix(xg, dw, conv_dw_b[0][None, :], conv_norm[0][None, :], n_prompt, seq, dec_seq,
                              expert_w, 1)
    wr1, br1, wrt1, brt1 = _router_operands(router_group_w[1], router_group_b[1], router_expert_w[1], router_expert_b[1])
    x1, h2, lg = _out_proj((x,), (hc,), conv_w_out[0].astype(BF16), mods[1], norm_ffn[1][None, :],
                           wrt1, brt1, n_prompt, dec_seq)
    pos, ys = _moe_rows(h2, lg, wr1, br1, *experts1)
    y_prompt = _combine(pos, ys, x1, mods[1], gain_f, n_prompt, dec_seq, 0, n_prompt, final=True)
    y_sample = _combine(pos, ys, x1, mods[1], gain_f, n_prompt, dec_seq, n_prompt, n_sample, final=True)

    new_k = kv32[:n_prompt, :KV_WIDTH].reshape(batch, 1, seq, N_KV_HEADS, HEAD_DIM)
    new_v = kv32[:n_prompt, KV_WIDTH:].reshape(batch, 1, seq, N_KV_HEADS, HEAD_DIM)
    return (y_prompt.reshape(batch, seq, d), y_sample.reshape(dec_batch, dec_seq, d), new_k, new_v)
```

```python
import functools

import jax
import jax.numpy as jnp
import numpy as np
from jax import lax
from jax.experimental import pallas as pl
from jax.experimental.pallas import tpu as pltpu

F32 = jnp.float32
BF16 = jnp.bfloat16

D_MODEL = 2048
GRID_W = 64
ATTN_BLK = 128
N_HEADS = 16
N_KV_HEADS = 4
Q_GROUPS = N_HEADS // N_KV_HEADS
HEAD_DIM = 64
ATTN_WIDTH = N_HEADS * HEAD_DIM
KV_WIDTH = N_KV_HEADS * HEAD_DIM
ROPE_BASE = 10000.0
SGU_GROUPS = 4
SGU_WIDTH = D_MODEL // 2
SGU_GROUP_DIM = SGU_WIDTH // SGU_GROUPS
CHUNK = 128
IN_WIDTH_EVEN = ATTN_WIDTH + 2 * KV_WIDTH + 2 * SGU_WIDTH
CONV_K = 31
CONV_HALO = 16
N_GROUPS_MOE = 4
EXPERTS_PER_GROUP = 4
N_EXPERTS = N_GROUPS_MOE * EXPERTS_PER_GROUP
PAIRS_PER_GROUP = 6
N_CLASSES = N_GROUPS_MOE * PAIRS_PER_GROUP
D_EXPERT = 512
MOE_BLK = 256
EPS = 1e-6
NEG_INF = -1e30

LANES = 128
ROUTER_LANES = 128
CLASS_ROWS = 32
MAX_MOE_BLOCKS = 128
TM_PROJ = 512
TN_PROJ = 512
TM_OUT = 256
TM_OUTPROJ = 512
TM_SGU = 512
CUM_CHUNK = 256
VMEM_LIMIT = 56 * 1024 * 1024


def _cparams(sem):
    return pltpu.CompilerParams(dimension_semantics=sem, vmem_limit_bytes=VMEM_LIMIT)


def _pack_bf16_pairs(h):
    w = h.shape[1] // 2
    lo = pltpu.bitcast(h[:, :w].astype(BF16).astype(F32), jnp.uint32) >> 16
    hi = pltpu.bitcast(h[:, w:].astype(BF16).astype(F32), jnp.uint32) & jnp.uint32(0xFFFF0000)
    return hi | lo


def _unpack_bf16_pairs(u):
    lo = pltpu.bitcast(u << 16, F32)
    hi = pltpu.bitcast(u & jnp.uint32(0xFFFF0000), F32)
    return jnp.concatenate([lo, hi], axis=1).astype(BF16)


def _rms_mod(x, gain, scale, shift):
    ms = jnp.mean(x * x, axis=-1, keepdims=True)
    y = x * lax.rsqrt(ms + EPS) * gain
    return y * (1.0 + scale) + shift


def _ada_kernel(c_ref, w_ref, b_ref, o_ref):
    c = c_ref[...]
    s = (c * jax.nn.sigmoid(c)).astype(BF16)
    o_ref[...] = jnp.dot(s, w_ref[...].astype(BF16), preferred_element_type=F32) + b_ref[...]


def _ada_params(cvec, ada_w, ada_b):
    depth, d, n = ada_w.shape
    mp = cvec.shape[0]
    tn = 1024
    return pl.pallas_call(
        _ada_kernel,
        grid=(depth, n // tn),
        in_specs=[
            pl.BlockSpec((mp, d), lambda l, j: (0, 0)),
            pl.BlockSpec((None, d, tn), lambda l, j: (l, 0, j)),
            pl.BlockSpec((None, 1, tn), lambda l, j: (l, 0, j)),
        ],
        out_specs=pl.BlockSpec((None, mp, tn), lambda l, j: (l, 0, j)),
        out_shape=jax.ShapeDtypeStruct((depth, mp, n), F32),
        compiler_params=_cparams(("arbitrary", "arbitrary")),
        name="ada_params",
    )(cvec, ada_w, ada_b.reshape(depth, 1, n))


def _swap16(x):
    width = x.shape[-1]
    lane = lax.broadcasted_iota(jnp.int32, x.shape, 1)
    return jnp.where((lane % 32) < 16, pltpu.roll(x, width - 16, 1), pltpu.roll(x, 16, 1))


def _proj_even_kernel(xp_ref, xs_ref, mod_ref, g_ref, w_ref, cos_ref, sin_ref, o_ref, kv_ref, h_scr, *,
                      n_prompt_blocks):
    x = jnp.where(pl.program_id(0) < n_prompt_blocks, xp_ref[...], xs_ref[...])
    h_scr[...] = _rms_mod(x, g_ref[...], mod_ref[1:2, :], mod_ref[0:1, :]).astype(BF16)
    kv_tile = ATTN_WIDTH // TN_PROJ
    for j in range(w_ref.shape[1] // TN_PROJ):
        cols = slice(j * TN_PROJ, (j + 1) * TN_PROJ)
        acc = jnp.dot(h_scr[...], w_ref[:, cols], preferred_element_type=F32)
        if j <= kv_tile:
            tab = slice(0, TN_PROJ) if j < kv_tile else slice(TN_PROJ, 2 * TN_PROJ)
            o_ref[:, cols] = (acc * cos_ref[:, tab] + _swap16(acc) * sin_ref[:, tab]).astype(BF16)
            if j == kv_tile:
                kv_ref[...] = acc
        else:
            o_ref[:, cols] = jax.nn.gelu(acc).astype(BF16)


def _proj_even(x_p, x_s, mods, gain, w, cos_t, sin_t, blocks_per_seq):
    d = x_p.shape[1]
    n = w.shape[1]
    tm, tn = TM_PROJ, TN_PROJ
    npb = x_p.shape[0] // tm
    nsb = x_s.shape[0] // tm
    t = x_p.shape[0] + x_s.shape[0]

    def mod_idx(i):
        return (jnp.where(i < npb, 0, 1 + (i - npb) // blocks_per_seq), 0, 0)

    def tab_idx(i):
        return (jnp.where(i < npb, blocks_per_seq, (i - npb) % blocks_per_seq), 0)

    return pl.pallas_call(
        functools.partial(_proj_even_kernel, n_prompt_blocks=npb),
        grid=(npb + nsb,),
        in_specs=[
            pl.BlockSpec((tm, d), lambda i: (jnp.minimum(i, npb - 1), 0)),
            pl.BlockSpec((tm, d), lambda i: (jnp.maximum(i - npb, 0), 0)),
            pl.BlockSpec((None, 6, d), mod_idx),
            pl.BlockSpec((1, d), lambda i: (0, 0)),
            pl.BlockSpec((d, n), lambda i: (0, 0), pipeline_mode=pl.Buffered(1)),
            pl.BlockSpec((tm, 2 * tn), tab_idx),
            pl.BlockSpec((tm, 2 * tn), tab_idx),
        ],
        out_specs=[
            pl.BlockSpec((tm, n), lambda i: (i, 0)),
            pl.BlockSpec((tm, tn), lambda i: (i, 0)),
        ],
        out_shape=[jax.ShapeDtypeStruct((t, n), BF16), jax.ShapeDtypeStruct((t, tn), F32)],
        scratch_shapes=[pltpu.VMEM((tm, d), BF16)],
        compiler_params=_cparams(("arbitrary",)),
        name="proj_even",
    )(x_p, x_s, mods, gain, w, cos_t, sin_t)


def _rope_tables(dec_seq, tm):
    f32 = np.float32
    n = np.arange(dec_seq)
    row = (n // GRID_W).astype(f32)
    col = (n % GRID_W).astype(f32)
    nf = HEAD_DIM // 4
    inv_freq = np.power(f32(ROPE_BASE), -np.arange(nf, dtype=f32) / f32(nf)).astype(f32)
    ar = row[:, None] * inv_freq[None, :]
    ac = col[:, None] * inv_freq[None, :]
    cos_h = np.concatenate([np.cos(ar), np.cos(ar), np.cos(ac), np.cos(ac)], axis=-1).astype(f32)
    sin_h = np.concatenate([-np.sin(ar), np.sin(ar), -np.sin(ac), np.sin(ac)], axis=-1).astype(f32)
    scale = f32(HEAD_DIM ** -0.5)
    ones = np.ones((dec_seq, KV_WIDTH), f32)
    zeros = np.zeros((dec_seq, KV_WIDTH), f32)
    q_heads = TN_PROJ // HEAD_DIM
    cos_t = np.concatenate([np.tile(cos_h, (1, q_heads)) * scale, np.tile(cos_h, (1, N_KV_HEADS)), ones], axis=-1)
    sin_t = np.concatenate([np.tile(sin_h, (1, q_heads)) * scale, np.tile(sin_h, (1, N_KV_HEADS)), zeros], axis=-1)
    id_cos = np.concatenate([np.full((tm, TN_PROJ), scale, f32), np.ones((tm, TN_PROJ), f32)], axis=-1)
    id_sin = np.zeros((tm, 2 * TN_PROJ), f32)
    return (jnp.asarray(np.concatenate([cos_t, id_cos], axis=0), F32),
            jnp.asarray(np.concatenate([sin_t, id_sin], axis=0), F32))


def _cast_plan(weights, layer, n_steps, step_of):
    n_exp = weights[0].shape[1]
    steps_per_matrix = min(n_steps // len(weights), n_exp)
    assert steps_per_matrix >= 1 and n_exp % steps_per_matrix == 0
    e_blk = n_exp // steps_per_matrix

    def slice_of(k, idx):
        return jnp.clip(step_of(*idx) - k * steps_per_matrix, 0, steps_per_matrix - 1)

    in_specs = [pl.BlockSpec((None, e_blk) + w.shape[2:], lambda *idx, k=k: (layer, slice_of(k, idx), 0, 0))
                for k, w in enumerate(weights)]
    out_specs = [pl.BlockSpec((e_blk,) + w.shape[2:], lambda *idx, k=k: (slice_of(k, idx), 0, 0))
                 for k, w in enumerate(weights)]
    out_shapes = [jax.ShapeDtypeStruct(w.shape[1:], BF16) for w in weights]
    return steps_per_matrix, in_specs, out_specs, out_shapes


def _cast_slices(step, steps_per_matrix, srcs, dsts):
    for k, (src, dst) in enumerate(zip(srcs, dsts)):
        @pl.when((step >= k * steps_per_matrix) & (step < (k + 1) * steps_per_matrix))
        def _(src=src, dst=dst):
            dst[...] = src[...].astype(BF16)


def _sink_attend(q, keys, vals, masks, sink):
    scores = []
    for k, mask in zip(keys, masks):
        s = lax.dot_general(q, k, (((1,), (1,)), ((), ())), preferred_element_type=F32)
        if mask is not None:
            stacked = s.reshape(s.shape[0] // mask.shape[0], *mask.shape)
            s = jnp.where(mask[None], stacked, NEG_INF).reshape(s.shape)
        scores.append(s)
    m = sink
    for s in scores:
        m = jnp.maximum(m, jnp.max(s, axis=-1, keepdims=True))
    es = [jnp.exp(s - m) for s in scores]
    den = jnp.exp(sink - m)
    for e in es:
        den = den + jnp.sum(e, axis=-1, keepdims=True)
    out = None
    for e, v in zip(es, vals):
        o = jnp.dot(e.astype(BF16), v, preferred_element_type=F32)
        out = o if out is None else out + o
    return out * (1.0 / den)


def _grouped_heads_attend(sink_ref, q_ref, o_ref, kh, keys, vals, masks):
    rows = q_ref.shape[0]
    heads = [kh * Q_GROUPS + g for g in range(Q_GROUPS)]
    q = jnp.concatenate([q_ref[:, h * HEAD_DIM:(h + 1) * HEAD_DIM] for h in heads], axis=0)
    ridx = lax.broadcasted_iota(jnp.int32, (Q_GROUPS * rows, 1), 0)
    sink = jnp.full((Q_GROUPS * rows, 1), sink_ref[heads[-1]], F32)
    for g in range(Q_GROUPS - 2, -1, -1):
        sink = jnp.where(ridx < (g + 1) * rows, sink_ref[heads[g]], sink)
    o = _sink_attend(q, keys, vals, masks, sink)
    for g, h in enumerate(heads):
        o_ref[:, h * HEAD_DIM:(h + 1) * HEAD_DIM] = o[g * rows:(g + 1) * rows, :].astype(BF16)


def _ctx_attn_kernel(sink_ref, q_ref, kv_ref, o_ref):
    for kh in range(N_KV_HEADS):
        k = kv_ref[:, kh * HEAD_DIM:(kh + 1) * HEAD_DIM]
        v = kv_ref[:, KV_WIDTH + kh * HEAD_DIM:KV_WIDTH + (kh + 1) * HEAD_DIM]
        _grouped_heads_attend(sink_ref, q_ref, o_ref, kh, [k], [v], [None])


def _ctx_attention(proj, sink, batch, seq, kv_col_block):
    return pl.pallas_call(
        _ctx_attn_kernel,
        grid=(batch,),
        in_specs=[
            pl.BlockSpec(memory_space=pltpu.SMEM),
            pl.BlockSpec((seq, ATTN_WIDTH), lambda b: (b, 0)),
            pl.BlockSpec((seq, 2 * KV_WIDTH), lambda b: (b, kv_col_block)),
        ],
        out_specs=pl.BlockSpec((seq, ATTN_WIDTH), lambda b: (b, 0)),
        out_shape=jax.ShapeDtypeStruct((batch * seq, ATTN_WIDTH), BF16),
        compiler_params=_cparams(("arbitrary",)),
        name="ctx_attention",
    )(sink, proj, proj)


def _lat_attn_kernel(sink_ref, q_ref, kv_ref, ck_ref, cv_ref, wg_ref, wu_ref, wd_ref, o_ref, wg_out, wu_out, wd_out,
                     *, seq, steps_per_matrix):
    i = pl.program_id(1)
    _cast_slices(pl.program_id(0) * pl.num_programs(1) + i, steps_per_matrix,
                 (wg_ref, wu_ref, wd_ref), (wg_out, wu_out, wd_out))
    win = 3 * ATTN_BLK
    ws = pl.multiple_of(jnp.clip((i - 1) * ATTN_BLK, 0, seq - win), ATTN_BLK)
    shape = (ATTN_BLK, win)
    qpos = i * ATTN_BLK + lax.broadcasted_iota(jnp.int32, shape, 0)
    kpos = ws + lax.broadcasted_iota(jnp.int32, shape, 1)
    valid = jnp.abs(qpos - kpos) <= ATTN_BLK
    kvw = kv_ref[pl.ds(ws, win), :]
    for kh in range(N_KV_HEADS):
        k_loc = kvw[:, kh * HEAD_DIM:(kh + 1) * HEAD_DIM]
        v_loc = kvw[:, KV_WIDTH + kh * HEAD_DIM:KV_WIDTH + (kh + 1) * HEAD_DIM]
        k_ctx = ck_ref[:, kh * HEAD_DIM:(kh + 1) * HEAD_DIM]
        v_ctx = cv_ref[:, kh * HEAD_DIM:(kh + 1) * HEAD_DIM]
        _grouped_heads_attend(sink_ref, q_ref, o_ref, kh, [k_ctx, k_loc], [v_ctx, v_loc], [None, valid])


def _lat_attention(proj, sink, cache_k, cache_v, n_prompt, dec_batch, dec_seq, kv_col_block, weights, layer):
    qb = dec_seq // ATTN_BLK
    q_off = n_prompt // ATTN_BLK
    s_off = n_prompt // dec_seq
    past = cache_k.shape[1]
    spm, w_in, w_out, w_shapes = _cast_plan(weights, layer, dec_batch * qb, lambda b, i: b * qb + i)
    return pl.pallas_call(
        functools.partial(_lat_attn_kernel, seq=dec_seq, steps_per_matrix=spm),
        grid=(dec_batch, qb),
        in_specs=[
            pl.BlockSpec(memory_space=pltpu.SMEM),
            pl.BlockSpec((ATTN_BLK, ATTN_WIDTH), lambda b, i: (q_off + b * qb + i, 0)),
            pl.BlockSpec((dec_seq, 2 * KV_WIDTH), lambda b, i: (s_off + b, kv_col_block)),
            pl.BlockSpec((None, past, KV_WIDTH), lambda b, i: (b, 0, 0)),
            pl.BlockSpec((None, past, KV_WIDTH), lambda b, i: (b, 0, 0)),
        ] + w_in,
        out_specs=[pl.BlockSpec((ATTN_BLK, ATTN_WIDTH), lambda b, i: (b * qb + i, 0))] + w_out,
        out_shape=[jax.ShapeDtypeStruct((dec_batch * dec_seq, ATTN_WIDTH), BF16)] + w_shapes,
        compiler_params=_cparams(("arbitrary", "arbitrary")),
        name="lat_attention",
    )(sink, proj, proj, cache_k, cache_v, *weights)


def _sgu_kernel(ul_ref, uh_ref, vl_ref, vh_ref, g_ref, w_ref, bt_ref, z_ref):
    half = SGU_WIDTH // 2
    per_half = SGU_GROUPS // 2
    u_refs = (ul_ref, uh_ref)
    for c in range(TM_SGU // CHUNK):
        rows = slice(c * CHUNK, (c + 1) * CHUNK)
        vl = vl_ref[rows, :].astype(F32)
        vh = vh_ref[rows, :].astype(F32)
        ssq = jnp.sum(vl * vl, axis=-1, keepdims=True) + jnp.sum(vh * vh, axis=-1, keepdims=True)
        r = lax.rsqrt(ssq * (1.0 / SGU_WIDTH) + EPS)
        vn = ((vl * r * g_ref[:, :half]).astype(BF16), (vh * r * g_ref[:, half:]).astype(BF16))
        for g in range(SGU_GROUPS):
            src = g // per_half
            cols = slice((g % per_half) * SGU_GROUP_DIM, (g % per_half + 1) * SGU_GROUP_DIM)
            mixed = jnp.dot(w_ref[g], vn[src][:, cols], preferred_element_type=F32) + bt_ref[:, g:g + 1]
            z_ref[rows, g * SGU_GROUP_DIM:(g + 1) * SGU_GROUP_DIM] = (
                u_refs[src][rows, cols].astype(F32) * mixed).astype(BF16)


def _sgu(proj, norm_g, w_s, b_t):
    t = proj.shape[0]
    half = SGU_WIDTH // 2
    u0 = (ATTN_WIDTH + 2 * KV_WIDTH) // half
    blocks = [pl.BlockSpec((TM_SGU, half), functools.partial(lambda i, c: (i, c), c=u0 + k)) for k in range(4)]
    return pl.pallas_call(
        _sgu_kernel,
        grid=(t // TM_SGU,),
        in_specs=blocks + [
            pl.BlockSpec((1, SGU_WIDTH), lambda i: (0, 0)),
            pl.BlockSpec((SGU_GROUPS, CHUNK, CHUNK), lambda i: (0, 0, 0)),
            pl.BlockSpec((CHUNK, SGU_GROUPS), lambda i: (0, 0)),
        ],
        out_specs=pl.BlockSpec((TM_SGU, SGU_WIDTH), lambda i: (i, 0)),
        out_shape=jax.ShapeDtypeStruct((t, SGU_WIDTH), BF16),
        compiler_params=_cparams(("arbitrary",)),
        name="sgu",
    )(proj, proj, proj, proj, norm_g, w_s, b_t)


def _residual_router(rows, x, acc, mod_ref, g2_ref, wrt_ref, brt_ref, x1_ref, h2_ref, lg_ref):
    x1 = x + mod_ref[2:3, :] * acc
    x1_ref[rows, :] = x1
    h2 = _rms_mod(x1, g2_ref[...], mod_ref[4:5, :], mod_ref[3:4, :])
    h2_ref[rows, :] = _pack_bf16_pairs(h2)
    lg = lax.dot_general(wrt_ref[...], h2.astype(BF16), (((1,), (1,)), ((), ())), preferred_element_type=F32)
    lg_ref[:, rows] = lg + brt_ref[...]


def _sub_blocks(ref):
    return [slice(r, r + TM_OUT) for r in range(0, ref.shape[0], TM_OUT)]


def _out_even_kernel(xp_ref, xs_ref, ap_ref, as_ref, z_ref, w_ref, mod_ref, g2_ref, wrt_ref, brt_ref,
                     x1_ref, h2_ref, lg_ref, *, n_prompt_blocks):
    is_prompt = pl.program_id(0) < n_prompt_blocks
    for rows in _sub_blocks(x1_ref):
        a = jnp.where(is_prompt, ap_ref[rows, :].astype(F32), as_ref[rows, :].astype(F32)).astype(BF16)
        acc = jnp.dot(a, w_ref[0:ATTN_WIDTH, :], preferred_element_type=F32)
        acc = acc + jnp.dot(z_ref[rows, :], w_ref[ATTN_WIDTH:, :], preferred_element_type=F32)
        x = jnp.where(is_prompt, xp_ref[rows, :], xs_ref[rows, :])
        _residual_router(rows, x, acc, mod_ref, g2_ref, wrt_ref, brt_ref, x1_ref, h2_ref, lg_ref)


def _out_odd_kernel(x_ref, hc_ref, w_ref, mod_ref, g2_ref, wrt_ref, brt_ref, x1_ref, h2_ref, lg_ref):
    for rows in _sub_blocks(x1_ref):
        acc = jnp.dot(hc_ref[rows, :], w_ref[...], preferred_element_type=F32)
        _residual_router(rows, x_ref[rows, :], acc, mod_ref, g2_ref, wrt_ref, brt_ref, x1_ref, h2_ref, lg_ref)


def _out_proj(xs, lhs, w, mods, gain2, wrt, brt, n_prompt, dec_seq):
    d = xs[0].shape[1]
    t = sum(x.shape[0] for x in xs)
    tm = TM_OUTPROJ
    npb = n_prompt // tm
    bps = dec_seq // tm
    nsb = t // tm - npb
    if len(xs) == 2:
        x_specs = [pl.BlockSpec((tm, d), lambda i: (jnp.minimum(i, npb - 1), 0)),
                   pl.BlockSpec((tm, d), lambda i: (jnp.maximum(i - npb, 0), 0))]
    else:
        x_specs = [pl.BlockSpec((tm, d), lambda i: (i, 0))]

    def mod_idx(i):
        return (jnp.where(i < npb, 0, 1 + (i - npb) // bps), 0, 0)

    if len(lhs) == 3:
        a_p, a_s, z = lhs
        body = functools.partial(_out_even_kernel, n_prompt_blocks=npb)
        lhs_specs = [
            pl.BlockSpec((tm, ATTN_WIDTH), lambda i: (jnp.minimum(i, npb - 1), 0)),
            pl.BlockSpec((tm, ATTN_WIDTH), lambda i: (jnp.clip(i - npb, 0, nsb - 1), 0)),
            pl.BlockSpec((tm, SGU_WIDTH), lambda i: (i, 0)),
        ]
    else:
        body = _out_odd_kernel
        lhs_specs = [pl.BlockSpec((tm, d), lambda i: (i, 0))]
    return pl.pallas_call(
        body,
        grid=(t // tm,),
        in_specs=x_specs + lhs_specs + [
            pl.BlockSpec((w.shape[0], d), lambda i: (0, 0), pipeline_mode=pl.Buffered(1)),
            pl.BlockSpec((None, 6, d), mod_idx),
            pl.BlockSpec((1, d), lambda i: (0, 0)),
            pl.BlockSpec((ROUTER_LANES, d), lambda i: (0, 0)),
            pl.BlockSpec((ROUTER_LANES, 1), lambda i: (0, 0)),
        ],
        out_specs=[
            pl.BlockSpec((tm, d), lambda i: (i, 0)),
            pl.BlockSpec((tm, d // 2), lambda i: (i, 0)),
            pl.BlockSpec((ROUTER_LANES, tm), lambda i: (0, i)),
        ],
        out_shape=[
            jax.ShapeDtypeStruct((t, d), F32),
            jax.ShapeDtypeStruct((t, d // 2), jnp.uint32),
            jax.ShapeDtypeStruct((ROUTER_LANES, t), F32),
        ],
        compiler_params=_cparams(("arbitrary",)),
        name="out_proj",
    )(*xs, *lhs, w, mods, gain2, wrt, brt)


def _first_argmax4(v0, v1, v2, v3):
    m = jnp.maximum(jnp.maximum(v0, v1), jnp.maximum(v2, v3))
    idx = jnp.where(v0 == m, 0.0, jnp.where(v1 == m, 1.0, jnp.where(v2 == m, 2.0, 3.0)))
    return m, idx


def _route_kernel(lg_ref, pos_ref, ea_ref, eb_ref, nv_ref, ends_ref, oh_scr, rank_scr, *, n_tokens):
    lg = lg_ref[...]
    rows = [lg[r:r + 1, :] for r in range(N_GROUPS_MOE + N_EXPERTS)]
    _, gidx = _first_argmax4(*rows[:N_GROUPS_MOE])
    e = []
    for k in range(EXPERTS_PER_GROUP):
        cand = [rows[N_GROUPS_MOE + g * EXPERTS_PER_GROUP + k] for g in range(N_GROUPS_MOE)]
        e.append(jnp.where(gidx == 0.0, cand[0], jnp.where(gidx == 1.0, cand[1], jnp.where(gidx == 2.0, cand[2], cand[3]))))
    _, l1 = _first_argmax4(*e)
    e2 = [jnp.where(l1 == float(k), -jnp.inf, e[k]) for k in range(EXPERTS_PER_GROUP)]
    _, l2 = _first_argmax4(*e2)
    lo = jnp.minimum(l1, l2)
    hi = jnp.maximum(l1, l2)
    pair = jnp.where(lo == 0.0, hi - 1.0, jnp.where(lo == 1.0, hi + 1.0, 5.0))
    cls = gidx * float(PAIRS_PER_GROUP) + pair

    crow = lax.broadcasted_iota(jnp.int32, (CLASS_ROWS, n_tokens), 0).astype(F32)
    oh_scr[...] = jnp.where(crow == cls, 1.0, 0.0).astype(BF16)

    jj = lax.broadcasted_iota(jnp.int32, (CUM_CHUNK, CUM_CHUNK), 0)
    ii = lax.broadcasted_iota(jnp.int32, (CUM_CHUNK, CUM_CHUNK), 1)
    tri = jnp.where(jj < ii, 1.0, 0.0).astype(BF16)

    def chunk(c, carry):
        off = pl.multiple_of(c * CUM_CHUNK, CUM_CHUNK)
        oh = oh_scr[:, pl.ds(off, CUM_CHUNK)]
        ohf = oh.astype(F32)
        cum = jnp.dot(oh, tri, preferred_element_type=F32) + carry
        rank_scr[:, pl.ds(off, CUM_CHUNK)] = jnp.sum(cum * ohf, axis=0, keepdims=True)
        return carry + jnp.sum(ohf, axis=1, keepdims=True)

    counts = lax.fori_loop(0, n_tokens // CUM_CHUNK, chunk, jnp.zeros((CLASS_ROWS, 1), F32))
    padded = jnp.floor((counts + float(MOE_BLK - 1)) * (1.0 / MOE_BLK)) * float(MOE_BLK)
    run = jnp.zeros((1, 1), F32)
    starts_rows = []
    for c in range(CLASS_ROWS):
        starts_rows.append(run)
        run = run + padded[c:c + 1, :]
    starts = jnp.concatenate(starts_rows, axis=0)
    ends = starts + padded
    total = run

    ohf = oh_scr[...].astype(F32)
    pos = rank_scr[...] + jnp.sum(ohf * starts, axis=0, keepdims=True)
    pos_ref[...] = pos.astype(jnp.int32)

    bstart = lax.broadcasted_iota(jnp.int32, (CLASS_ROWS, MAX_MOE_BLOCKS), 1).astype(F32) * float(MOE_BLK)
    brow = lax.broadcasted_iota(jnp.int32, (CLASS_ROWS, MAX_MOE_BLOCKS), 0)
    done = jnp.where((ends <= bstart) & (brow < N_CLASSES), 1.0, 0.0)
    bcls = jnp.minimum(jnp.sum(done, axis=0, keepdims=True), float(N_CLASSES - 1))
    grp = (jnp.where(bcls >= 6.0, 1.0, 0.0) + jnp.where(bcls >= 12.0, 1.0, 0.0) + jnp.where(bcls >= 18.0, 1.0, 0.0))
    bp = bcls - grp * float(PAIRS_PER_GROUP)
    ge3 = jnp.where(bp >= 3.0, 1.0, 0.0)
    ge5 = jnp.where(bp >= 5.0, 1.0, 0.0)
    blo = ge3 + ge5
    bhi = bp + 1.0 - 2.0 * ge3 - ge5
    ea_ref[...] = (grp * float(EXPERTS_PER_GROUP) + blo).astype(jnp.int32)
    eb_ref[...] = (grp * float(EXPERTS_PER_GROUP) + bhi).astype(jnp.int32)
    own = jnp.where(brow.astype(F32) == bcls, starts + counts, 0.0)
    filled = jnp.sum(own, axis=0, keepdims=True) - bstart[0:1, :]
    nv_ref[...] = jnp.clip(filled, 0.0, float(MOE_BLK)).astype(jnp.int32)
    lane = lax.broadcasted_iota(jnp.int32, (1, MAX_MOE_BLOCKS), 1)
    ends_row = jnp.zeros((1, MAX_MOE_BLOCKS), F32)
    for c in range(N_CLASSES):
        ends_row = jnp.where(lane == c, starts_rows[c] + padded[c:c + 1, :], ends_row)
    ends_ref[...] = ends_row.astype(jnp.int32)


def _route(logits_t):
    t = logits_t.shape[1]
    rows = N_CLASSES
    i32 = jnp.int32
    return pl.pallas_call(
        functools.partial(_route_kernel, n_tokens=t),
        grid=(1,),
        in_specs=[pl.BlockSpec((rows, t), lambda i: (0, 0))],
        out_specs=[
            pl.BlockSpec((1, t), lambda i: (0, 0)),
        ] + [pl.BlockSpec((1, MAX_MOE_BLOCKS), lambda i: (0, 0))] * 4,
        out_shape=[jax.ShapeDtypeStruct((1, t), i32)] + [jax.ShapeDtypeStruct((1, MAX_MOE_BLOCKS), i32)] * 4,
        scratch_shapes=[pltpu.VMEM((CLASS_ROWS, t), BF16), pltpu.VMEM((1, t), F32)],
        compiler_params=_cparams(("arbitrary",)),
        name="route",
    )(logits_t)


DMA_THREADS = 2


def _row_copy(src_ref, src_row, dst_ref, dst_row, sem):
    return pltpu.make_async_copy(src_ref.at[pl.ds(src_row, 1), :], dst_ref.at[pl.ds(dst_row, 1), :], sem)


def _start_row_copies(n_rows, make_copy):
    for r in range(n_rows):
        make_copy(r).start(priority=r % DMA_THREADS)


def _dispatch_kernel(pos_ref, ends_ref, h_ref, o_ref, zbuf, sem, zsem):
    step = pl.program_id(0)

    @pl.when(step == 0)
    def _():
        zbuf[...] = jnp.zeros_like(zbuf)
        total = ends_ref[N_CLASSES - 1]
        n_free = (o_ref.shape[0] - total) // MOE_BLK

        def zero_block(first_row, phase):
            first = pl.multiple_of(first_row, MOE_BLK)
            getattr(pltpu.make_async_copy(zbuf, o_ref.at[pl.ds(first, MOE_BLK), :], zsem), phase)()

        for phase in ("start", "wait"):
            prev = 0
            for c in range(N_CLASSES):
                end = ends_ref[c]
                pl.when(end > prev)(functools.partial(zero_block, end - MOE_BLK, phase))
                prev = end

            def free_block(k, carry, phase=phase):
                zero_block(total + k * MOE_BLK, phase)
                return carry

            lax.fori_loop(0, n_free, free_block, 0)

    base = step * TM_OUT

    _start_row_copies(TM_OUT, lambda r: _row_copy(h_ref, r, o_ref, pos_ref[base + r], sem))
    pltpu.make_async_copy(h_ref, o_ref.at[pl.ds(0, TM_OUT), :], sem).wait()


def _dispatch(pos, ends, h2, n_rows):
    t, d = h2.shape
    return pl.pallas_call(
        _dispatch_kernel,
        grid_spec=pltpu.PrefetchScalarGridSpec(
            num_scalar_prefetch=2,
            grid=(t // TM_OUT,),
            in_specs=[pl.BlockSpec((TM_OUT, d), lambda i, pos, ends: (i, 0))],
            out_specs=pl.BlockSpec(memory_space=pl.ANY),
            scratch_shapes=[pltpu.VMEM((MOE_BLK, d), h2.dtype), pltpu.SemaphoreType.DMA(()),
                            pltpu.SemaphoreType.DMA(())],
        ),
        out_shape=jax.ShapeDtypeStruct((n_rows, d), h2.dtype),
        compiler_params=_cparams(("arbitrary",)),
        name="moe_dispatch",
    )(pos, ends, h2)


def _expert_ffn(x, wg_ref, wu_ref, wd_ref):
    gate = jnp.dot(x, wg_ref[...], preferred_element_type=F32)
    up = jnp.dot(x, wu_ref[...], preferred_element_type=F32)
    hidden = (gate * jax.nn.sigmoid(gate) * up).astype(BF16)
    return jnp.dot(hidden, wd_ref[...], preferred_element_type=F32)


def _moe_kernel(ea_ref, eb_ref, nv_ref, x_ref, wr_ref, br_ref, wga, wua, wda, wgb, wub, wdb, y_ref):
    b = pl.program_id(0)
    n_valid = nv_ref[b]

    def run(rows):
        x = _unpack_bf16_pairs(x_ref[rows, :])
        lg = jnp.dot(x, wr_ref[...], preferred_element_type=F32) + br_ref[...]
        lane = lax.broadcasted_iota(jnp.int32, lg.shape, 1)
        ea = ea_ref[b]
        eb = eb_ref[b]
        grp = ea // EXPERTS_PER_GROUP
        is_grp = lane < N_GROUPS_MOE
        gm = jnp.max(jnp.where(is_grp, lg, -jnp.inf), axis=-1, keepdims=True)
        ge = jnp.where(is_grp, jnp.exp(lg - gm), 0.0)
        p_g = jnp.sum(jnp.where(lane == grp, ge, 0.0), axis=-1, keepdims=True) / jnp.sum(ge, axis=-1, keepdims=True)
        la = jnp.sum(jnp.where(lane == N_GROUPS_MOE + ea, lg, 0.0), axis=-1, keepdims=True)
        lb = jnp.sum(jnp.where(lane == N_GROUPS_MOE + eb, lg, 0.0), axis=-1, keepdims=True)
        mm = jnp.maximum(la, lb)
        xa = jnp.exp(la - mm)
        xb = jnp.exp(lb - mm)
        ga = xa / (xa + xb) * p_g
        gb = xb / (xa + xb) * p_g
        ya = _expert_ffn(x, wga, wua, wda)
        yb = _expert_ffn(x, wgb, wub, wdb)
        y_ref[rows, :] = ya * ga + yb * gb

    half = MOE_BLK // 2

    @pl.when(n_valid > half)
    def _():
        run(slice(0, MOE_BLK))

    @pl.when((n_valid > 0) & (n_valid <= half))
    def _():
        run(slice(0, half))
        y_ref[half:, :] = jnp.zeros((MOE_BLK - half, y_ref.shape[1]), y_ref.dtype)

    @pl.when(n_valid == 0)
    def _():
        y_ref[...] = jnp.zeros_like(y_ref)


def _moe_experts(ea, eb, n_valid, hs, wr, br, w_gate, w_up, w_down, n_blocks):
    d = w_gate.shape[1]
    de = w_gate.shape[2]

    def wa_idx(b, ea, eb, nv):
        return (ea[b], 0, 0)

    def wb_idx(b, ea, eb, nv):
        return (eb[b], 0, 0)

    return pl.pallas_call(
        _moe_kernel,
        grid_spec=pltpu.PrefetchScalarGridSpec(
            num_scalar_prefetch=3,
            grid=(n_blocks,),
            in_specs=[
                pl.BlockSpec((MOE_BLK, hs.shape[1]), lambda b, ea, eb, nv: (jnp.where(nv[b] > 0, b, 0), 0)),
                pl.BlockSpec((d, ROUTER_LANES), lambda b, ea, eb, nv: (0, 0)),
                pl.BlockSpec((1, ROUTER_LANES), lambda b, ea, eb, nv: (0, 0)),
                pl.BlockSpec((None, d, de), wa_idx),
                pl.BlockSpec((None, d, de), wa_idx),
                pl.BlockSpec((None, de, d), wa_idx),
                pl.BlockSpec((None, d, de), wb_idx),
                pl.BlockSpec((None, d, de), wb_idx),
                pl.BlockSpec((None, de, d), wb_idx),
            ],
            out_specs=pl.BlockSpec((MOE_BLK, d), lambda b, ea, eb, nv: (b, 0)),
        ),
        out_shape=jax.ShapeDtypeStruct((n_blocks * MOE_BLK, d), F32),
        compiler_params=_cparams(("arbitrary",)),
        name="moe_experts",
    )(ea, eb, n_valid, hs, wr, br, w_gate, w_up, w_down, w_gate, w_up, w_down)


def _combine_kernel(pos_ref, ys_ref, x1_ref, mod_ref, gf_ref, o_ref, ybuf, sem, *, block_off, n_steps, final):
    i = pl.program_id(0)
    slot = i % 2

    def start_gather(step, into):
        base = (step + block_off) * TM_OUT
        _start_row_copies(
            TM_OUT, lambda r: _row_copy(ys_ref, pos_ref[base + r], ybuf.at[into], r, sem.at[into]))

    pl.when(i == 0)(lambda: start_gather(0, 0))
    pl.when(i + 1 < n_steps)(lambda: start_gather(i + 1, 1 - slot))
    pltpu.make_async_copy(ys_ref.at[pl.ds(0, TM_OUT), :], ybuf.at[slot], sem.at[slot]).wait()
    x2 = x1_ref[...] + mod_ref[5:6, :] * ybuf[slot]
    if final:
        ms = jnp.mean(x2 * x2, axis=-1, keepdims=True)
        x2 = x2 * lax.rsqrt(ms + EPS) * gf_ref[...]
    o_ref[...] = x2


def _combine(pos, ys, x1, mods, gain_f, n_prompt, dec_seq, row_off, n_rows, final):
    d = x1.shape[1]
    tm = TM_OUT
    npb = n_prompt // tm
    bps = dec_seq // tm
    boff = row_off // tm

    def mod_idx(i, pos):
        blk = i + boff
        return (jnp.where(blk < npb, 0, 1 + (blk - npb) // bps), 0, 0)

    return pl.pallas_call(
        functools.partial(_combine_kernel, block_off=boff, n_steps=n_rows // tm, final=final),
        grid_spec=pltpu.PrefetchScalarGridSpec(
            num_scalar_prefetch=1,
            grid=(n_rows // tm,),
            in_specs=[
                pl.BlockSpec(memory_space=pl.ANY),
                pl.BlockSpec((tm, d), lambda i, pos: (i + boff, 0)),
                pl.BlockSpec((None, 6, d), mod_idx),
                pl.BlockSpec((1, d), lambda i, pos: (0, 0)),
            ],
            out_specs=pl.BlockSpec((tm, d), lambda i, pos: (i, 0)),
            scratch_shapes=[pltpu.VMEM((2, tm, d), F32), pltpu.SemaphoreType.DMA((2,))],
        ),
        out_shape=jax.ShapeDtypeStruct((n_rows, d), F32),
        compiler_params=_cparams(("arbitrary",)),
        name="moe_combine",
    )(pos, ys, x1, mods, gain_f)


def _proj_glu_kernel(pos_ref, ys_ref, x1_ref, mod_prev_ref, mod_ref, g_ref, w_ref, o_ref, x2_ref, h_scr, ybuf, sem,
                     *, n_steps):
    i = pl.program_id(0)
    slot = i % 2
    tm = x1_ref.shape[0]

    def start_gather(step, into):
        base = step * tm
        _start_row_copies(tm, lambda r: _row_copy(ys_ref, pos_ref[base + r], ybuf.at[into], r, sem.at[into]))

    pl.when(i == 0)(lambda: start_gather(0, 0))
    pl.when(i + 1 < n_steps)(lambda: start_gather(i + 1, 1 - slot))
    pltpu.make_async_copy(ys_ref.at[pl.ds(0, tm), :], ybuf.at[slot], sem.at[slot]).wait()
    x2 = x1_ref[...] + mod_prev_ref[5:6, :] * ybuf[slot]
    x2_ref[...] = x2
    h_scr[...] = _rms_mod(x2, g_ref[...], mod_ref[1:2, :], mod_ref[0:1, :]).astype(BF16)
    width = w_ref.shape[1] // 2
    for j in range(width // TN_PROJ):
        cols = slice(j * TN_PROJ, (j + 1) * TN_PROJ)
        gcols = slice(width + j * TN_PROJ, width + (j + 1) * TN_PROJ)
        val = jnp.dot(h_scr[...], w_ref[:, cols], preferred_element_type=F32)
        gate = jnp.dot(h_scr[...], w_ref[:, gcols], preferred_element_type=F32)
        o_ref[:, cols] = (val * jax.nn.sigmoid(gate)).astype(BF16)


def _proj_glu(pos, ys, x1, mods_prev, mods, gain, w, n_prompt, dec_seq):
    t, d = x1.shape
    n = w.shape[1]
    tm = TM_PROJ
    npb = n_prompt // tm
    bps = dec_seq // tm

    def mod_idx(i, pos):
        return (jnp.where(i < npb, 0, 1 + (i - npb) // bps), 0, 0)

    return pl.pallas_call(
        functools.partial(_proj_glu_kernel, n_steps=t // tm),
        grid_spec=pltpu.PrefetchScalarGridSpec(
            num_scalar_prefetch=1,
            grid=(t // tm,),
            in_specs=[
                pl.BlockSpec(memory_space=pl.ANY),
                pl.BlockSpec((tm, d), lambda i, pos: (i, 0)),
                pl.BlockSpec((None, 6, d), mod_idx),
                pl.BlockSpec((None, 6, d), mod_idx),
                pl.BlockSpec((1, d), lambda i, pos: (0, 0)),
                pl.BlockSpec((d, n), lambda i, pos: (0, 0), pipeline_mode=pl.Buffered(1)),
            ],
            out_specs=[
                pl.BlockSpec((tm, n // 2), lambda i, pos: (i, 0)),
                pl.BlockSpec((tm, d), lambda i, pos: (i, 0)),
            ],
            scratch_shapes=[pltpu.VMEM((tm, d), BF16), pltpu.VMEM((2, tm, d), F32), pltpu.SemaphoreType.DMA((2,))],
        ),
        out_shape=[jax.ShapeDtypeStruct((t, n // 2), BF16), jax.ShapeDtypeStruct((t, d), F32)],
        compiler_params=_cparams(("arbitrary",)),
        name="proj_glu",
    )(pos, ys, x1, mods_prev, mods, gain, w)


CONV_ROWS = 64
CONV_COLS = 256
F32_SUBLANES = 8


def _conv_kernel(x_ref, prev_ref, next_ref, dw_ref, b_ref, g_ref, wg_ref, wu_ref, wd_ref, o_ref, wg_out, wu_out, wd_out,
                 xpad, shifted, ybuf, *, n_prompt_blocks, blocks_per_seq, steps_per_matrix):
    i = pl.program_id(0)
    _cast_slices(i, steps_per_matrix, (wg_ref, wu_ref, wd_ref), (wg_out, wu_out, wd_out))
    in_sample = i >= n_prompt_blocks
    j = i - n_prompt_blocks
    is_start = jnp.logical_or(jnp.logical_not(in_sample), j % blocks_per_seq == 0)
    is_end = jnp.logical_or(jnp.logical_not(in_sample), j % blocks_per_seq == blocks_per_seq - 1)
    tm = TM_OUT
    xpad[0:CONV_HALO, :] = jnp.where(is_start, 0.0, prev_ref[...].astype(F32))
    xpad[CONV_HALO:CONV_HALO + tm, :] = x_ref[...].astype(F32)
    xpad[CONV_HALO + tm:, :] = jnp.where(is_end, 0.0, next_ref[...].astype(F32))
    first = CONV_HALO - CONV_K // 2
    span = shifted.shape[1]

    def col_chunk(c, ssq):
        c0 = pl.multiple_of(c * CONV_COLS, CONV_COLS)
        cols = pl.ds(c0, CONV_COLS)
        w = dw_ref[:, cols]
        bias = b_ref[:, cols]
        for s in range(1, F32_SUBLANES):
            shifted[s - 1] = xpad[s:s + span, cols]
        parts = []
        for rc in range(tm // CONV_ROWS):
            r0 = rc * CONV_ROWS
            acc = jnp.zeros((CONV_ROWS, CONV_COLS), F32)
            for k in range(CONV_K):
                whole, s = divmod(first + k, F32_SUBLANES)
                base = whole * F32_SUBLANES + r0
                if s == 0:
                    src = xpad[base:base + CONV_ROWS, cols]
                else:
                    src = shifted[s - 1, base:base + CONV_ROWS, :]
                acc = acc + w[k:k + 1, :] * src
            acc = acc + bias
            ybuf[r0:r0 + CONV_ROWS, cols] = acc
            parts.append(jnp.sum(acc * acc, axis=-1, keepdims=True))
        return ssq + jnp.concatenate(parts, axis=0)

    d = x_ref.shape[1]
    ssq = lax.fori_loop(0, d // CONV_COLS, col_chunk, jnp.zeros((tm, 1), F32))
    y = ybuf[...] * lax.rsqrt(ssq * (1.0 / d) + EPS) * g_ref[...]
    o_ref[...] = (y * jax.nn.sigmoid(y)).astype(BF16)


def _conv_mix(xg, dw, dw_b, norm_g, n_prompt, seq, dec_seq, weights, layer):
    t, d = xg.shape
    tm = TM_OUT
    assert seq == tm, "context sequences must be exactly one conv row block"
    npb = n_prompt // tm
    bps = dec_seq // tm
    hb = tm // CONV_HALO
    last = t // CONV_HALO - 1
    spm, w_in, w_out, w_shapes = _cast_plan(weights, layer, t // tm, lambda i: i)
    return pl.pallas_call(
        functools.partial(_conv_kernel, n_prompt_blocks=npb, blocks_per_seq=bps, steps_per_matrix=spm),
        grid=(t // tm,),
        in_specs=[
            pl.BlockSpec((tm, d), lambda i: (i, 0)),
            pl.BlockSpec((CONV_HALO, d), lambda i: (jnp.maximum(i * hb - 1, 0), 0)),
            pl.BlockSpec((CONV_HALO, d), lambda i: (jnp.minimum((i + 1) * hb, last), 0)),
            pl.BlockSpec((dw.shape[0], d), lambda i: (0, 0)),
            pl.BlockSpec((1, d), lambda i: (0, 0)),
            pl.BlockSpec((1, d), lambda i: (0, 0)),
        ] + w_in,
        out_specs=[pl.BlockSpec((tm, d), lambda i: (i, 0))] + w_out,
        out_shape=[jax.ShapeDtypeStruct((t, d), BF16)] + w_shapes,
        scratch_shapes=[
            pltpu.VMEM((tm + 2 * CONV_HALO, d), F32),
            pltpu.VMEM((F32_SUBLANES - 1, tm + 2 * CONV_HALO - F32_SUBLANES, CONV_COLS), F32),
            pltpu.VMEM((tm, d), F32),
        ],
        compiler_params=_cparams(("arbitrary",)),
        name="conv_mix",
    )(xg, xg, xg, dw, dw_b, norm_g, *weights)


def _router_operands(w_rg, b_rg, w_re, b_re):
    d = w_rg.shape[0]
    used = N_GROUPS_MOE + N_EXPERTS
    w = jnp.concatenate([w_rg, w_re, jnp.zeros((d, ROUTER_LANES - used), F32)], axis=1)
    b = jnp.concatenate([b_rg, b_re, jnp.zeros((ROUTER_LANES - used,), F32)])
    return w.astype(BF16), b.reshape(1, ROUTER_LANES), w.T.astype(BF16), b.reshape(ROUTER_LANES, 1)


def _moe_rows(h2, logits_t, wr, br, wg, wu, wd):
    t = h2.shape[0]
    n_blocks = -(-(t + N_CLASSES * (MOE_BLK - 1)) // MOE_BLK)
    assert n_blocks <= MAX_MOE_BLOCKS
    pos, ea, eb, n_valid, ends = _route(logits_t)
    pos = pos.reshape(t)
    hs = _dispatch(pos, ends.reshape(-1), h2, n_blocks * MOE_BLK)
    ys = _moe_experts(ea.reshape(-1), eb.reshape(-1), n_valid.reshape(-1), hs, wr, br, wg, wu, wd, n_blocks)
    return pos, ys


def kernel(x_prompt, x_sample, cache_k, cache_v, c, c_ctx, w_in_even, attn_sink, sgu_norm, sgu_w, sgu_b,
           w_out_even, conv_w_in, conv_dw, conv_dw_b, conv_norm, conv_w_out, ada_w, ada_b, norm_mix, norm_ffn,
           router_group_w, router_group_b, router_expert_w, router_expert_b, expert_w_gate, expert_w_up,
           expert_w_down, final_norm):
    batch, seq, d = x_prompt.shape
    dec_batch, dec_seq, _ = x_sample.shape
    depth = ada_w.shape[0]
    n_prompt = batch * seq
    n_sample = dec_batch * dec_seq
    assert d == D_MODEL and depth == 2 and n_prompt % dec_seq == 0 and dec_seq % TM_PROJ == 0
    x_p = x_prompt.reshape(n_prompt, d)
    x_s = x_sample.reshape(n_sample, d)

    n_mod = -(-(1 + dec_batch) // 8) * 8
    cvec = jnp.concatenate([c_ctx[None, :], c, jnp.zeros((n_mod - 1 - dec_batch, d), F32)], axis=0)
    mods = _ada_params(cvec, ada_w, ada_b).reshape(depth, n_mod, 6, d)
    expert_w = (expert_w_gate, expert_w_up, expert_w_down)

    cos_t, sin_t = _rope_tables(dec_seq, TM_PROJ)
    proj, kv32 = _proj_even(x_p, x_s, mods[0], norm_mix[0][None, :], w_in_even[0].astype(BF16), cos_t, sin_t,
                            dec_seq // TM_PROJ)
    kv_col_block = ATTN_WIDTH // (2 * KV_WIDTH)
    sink = attn_sink[0]
    a_p = _ctx_attention(proj, sink, batch, seq, kv_col_block)
    past = cache_k.shape[2]
    ck = cache_k[:, 0].reshape(dec_batch, past, KV_WIDTH).astype(BF16)
    cv = cache_v[:, 0].reshape(dec_batch, past, KV_WIDTH).astype(BF16)
    a_s, *experts0 = _lat_attention(proj, sink, ck, cv, n_prompt, dec_batch, dec_seq, kv_col_block, expert_w, 0)
    z = _sgu(proj, sgu_norm[0][None, :], sgu_w[0].astype(BF16), sgu_b[0].T)

    wr0, br0, wrt0, brt0 = _router_operands(router_group_w[0], router_group_b[0], router_expert_w[0], router_expert_b[0])
    x1, h2, lg = _out_proj((x_p, x_s), (a_p, a_s, z), w_out_even[0].astype(BF16), mods[0], norm_ffn[0][None, :],
                           wrt0, brt0, n_prompt, dec_seq)
    pos, ys = _moe_rows(h2, lg, wr0, br0, *experts0)
    gain_f = final_norm[None, :]

    xg, x = _proj_glu(pos, ys, x1, mods[0], mods[1], norm_mix[1][None, :], conv_w_in[0].astype(BF16),
                      n_prompt, dec_seq)
    dw = jnp.concatenate([conv_dw[0], jnp.zeros((1, d), F32)], axis=0)
    hc, *experts1 = _conv_mix(xg, dw, conv_dw_b[0][None, :], conv_norm[0][None, :], n_prompt, seq, dec_seq,
                              expert_w, 1)
    wr1, br1, wrt1, brt1 = _router_operands(router_group_w[1], router_group_b[1], router_expert_w[1], router_expert_b[1])
    x1, h2, lg = _out_proj((x,), (hc,), conv_w_out[0].astype(BF16), mods[1], norm_ffn[1][None, :],
                           wrt1, brt1, n_prompt, dec_seq)
    pos, ys = _moe_rows(h2, lg, wr1, br1, *experts1)
    y_prompt = _combine(pos, ys, x1, mods[1], gain_f, n_prompt, dec_seq, 0, n_prompt, final=True)
    y_sample = _combine(pos, ys, x1, mods[1], gain_f, n_prompt, dec_seq, n_prompt, n_sample, final=True)

    new_k = kv32[:n_prompt, :KV_WIDTH].reshape(batch, 1, seq, N_KV_HEADS, HEAD_DIM)
    new_v = kv32[:n_prompt, KV_WIDTH:].reshape(batch, 1, seq, N_KV_HEADS, HEAD_DIM)
    return (y_prompt.reshape(batch, seq, d), y_sample.reshape(dec_batch, dec_seq, d), new_k, new_v)
```

```python
import functools

import jax
import jax.numpy as jnp
import numpy as np
from jax import lax
from jax.experimental import pallas as pl
from jax.experimental.pallas import tpu as pltpu

F32 = jnp.float32
BF16 = jnp.bfloat16

D_MODEL = 2048
GRID_W = 64
ATTN_BLK = 128
N_HEADS = 16
N_KV_HEADS = 4
Q_GROUPS = N_HEADS // N_KV_HEADS
HEAD_DIM = 64
ATTN_WIDTH = N_HEADS * HEAD_DIM
KV_WIDTH = N_KV_HEADS * HEAD_DIM
ROPE_BASE = 10000.0
SGU_GROUPS = 4
SGU_WIDTH = D_MODEL // 2
SGU_GROUP_DIM = SGU_WIDTH // SGU_GROUPS
CHUNK = 128
IN_WIDTH_EVEN = ATTN_WIDTH + 2 * KV_WIDTH + 2 * SGU_WIDTH
CONV_K = 31
CONV_HALO = 16
N_GROUPS_MOE = 4
EXPERTS_PER_GROUP = 4
N_EXPERTS = N_GROUPS_MOE * EXPERTS_PER_GROUP
PAIRS_PER_GROUP = 6
N_CLASSES = N_GROUPS_MOE * PAIRS_PER_GROUP
D_EXPERT = 512
MOE_BLK = 256
EPS = 1e-6
NEG_INF = -1e30

LANES = 128
ROUTER_LANES = 128
CLASS_ROWS = 32
MAX_MOE_BLOCKS = 128
TM_PROJ = 512
TN_PROJ = 512
TM_OUT = 256
TM_OUTPROJ = 512
TM_SGU = 512
CUM_CHUNK = 256
VMEM_LIMIT = 56 * 1024 * 1024


def _cparams(sem):
    return pltpu.CompilerParams(dimension_semantics=sem, vmem_limit_bytes=VMEM_LIMIT)


def _pack_bf16_pairs(h):
    w = h.shape[1] // 2
    lo = pltpu.bitcast(h[:, :w].astype(BF16).astype(F32), jnp.uint32) >> 16
    hi = pltpu.bitcast(h[:, w:].astype(BF16).astype(F32), jnp.uint32) & jnp.uint32(0xFFFF0000)
    return hi | lo


def _unpack_bf16_pairs(u):
    lo = pltpu.bitcast(u << 16, F32)
    hi = pltpu.bitcast(u & jnp.uint32(0xFFFF0000), F32)
    return jnp.concatenate([lo, hi], axis=1).astype(BF16)


def _rms_mod(x, gain, scale, shift):
    ms = jnp.mean(x * x, axis=-1, keepdims=True)
    y = x * lax.rsqrt(ms + EPS) * gain
    return y * (1.0 + scale) + shift


def _ada_kernel(c_ref, w_ref, b_ref, o_ref):
    c = c_ref[...]
    s = (c * jax.nn.sigmoid(c)).astype(BF16)
    o_ref[...] = jnp.dot(s, w_ref[...].astype(BF16), preferred_element_type=F32) + b_ref[...]


def _ada_params(cvec, ada_w, ada_b):
    depth, d, n = ada_w.shape
    mp = cvec.shape[0]
    tn = 1024
    return pl.pallas_call(
        _ada_kernel,
        grid=(depth, n // tn),
        in_specs=[
            pl.BlockSpec((mp, d), lambda l, j: (0, 0)),
            pl.BlockSpec((None, d, tn), lambda l, j: (l, 0, j)),
            pl.BlockSpec((None, 1, tn), lambda l, j: (l, 0, j)),
        ],
        out_specs=pl.BlockSpec((None, mp, tn), lambda l, j: (l, 0, j)),
        out_shape=jax.ShapeDtypeStruct((depth, mp, n), F32),
        compiler_params=_cparams(("arbitrary", "arbitrary")),
        name="ada_params",
    )(cvec, ada_w, ada_b.reshape(depth, 1, n))


def _swap16(x):
    width = x.shape[-1]
    lane = lax.broadcasted_iota(jnp.int32, x.shape, 1)
    return jnp.where((lane % 32) < 16, pltpu.roll(x, width - 16, 1), pltpu.roll(x, 16, 1))


def _proj_even_kernel(xp_ref, xs_ref, mod_ref, g_ref, w_ref, cos_ref, sin_ref, o_ref, kv_ref, h_scr, *,
                      n_prompt_blocks):
    x = jnp.where(pl.program_id(0) < n_prompt_blocks, xp_ref[...], xs_ref[...])
    h_scr[...] = _rms_mod(x, g_ref[...], mod_ref[1:2, :], mod_ref[0:1, :]).astype(BF16)
    kv_tile = ATTN_WIDTH // TN_PROJ
    for j in range(w_ref.shape[1] // TN_PROJ):
        cols = slice(j * TN_PROJ, (j + 1) * TN_PROJ)
        acc = jnp.dot(h_scr[...], w_ref[:, cols], preferred_element_type=F32)
        if j <= kv_tile:
            tab = slice(0, TN_PROJ) if j < kv_tile else slice(TN_PROJ, 2 * TN_PROJ)
            o_ref[:, cols] = (acc * cos_ref[:, tab] + _swap16(acc) * sin_ref[:, tab]).astype(BF16)
            if j == kv_tile:
                kv_ref[...] = acc
        else:
            o_ref[:, cols] = jax.nn.gelu(acc).astype(BF16)


def _proj_even(x_p, x_s, mods, gain, w, cos_t, sin_t, blocks_per_seq):
    d = x_p.shape[1]
    n = w.shape[1]
    tm, tn = TM_PROJ, TN_PROJ
    npb = x_p.shape[0] // tm
    nsb = x_s.shape[0] // tm
    t = x_p.shape[0] + x_s.shape[0]

    def mod_idx(i):
        return (jnp.where(i < npb, 0, 1 + (i - npb) // blocks_per_seq), 0, 0)

    def tab_idx(i):
        return (jnp.where(i < npb, blocks_per_seq, (i - npb) % blocks_per_seq), 0)

    return pl.pallas_call(
        functools.partial(_proj_even_kernel, n_prompt_blocks=npb),
        grid=(npb + nsb,),
        in_specs=[
            pl.BlockSpec((tm, d), lambda i: (jnp.minimum(i, npb - 1), 0)),
            pl.BlockSpec((tm, d), lambda i: (jnp.maximum(i - npb, 0), 0)),
            pl.BlockSpec((None, 6, d), mod_idx),
            pl.BlockSpec((1, d), lambda i: (0, 0)),
            pl.BlockSpec((d, n), lambda i: (0, 0), pipeline_mode=pl.Buffered(1)),
            pl.BlockSpec((tm, 2 * tn), tab_idx),
            pl.BlockSpec((tm, 2 * tn), tab_idx),
        ],
        out_specs=[
            pl.BlockSpec((tm, n), lambda i: (i, 0)),
            pl.BlockSpec((tm, tn), lambda i: (i, 0)),
        ],
        out_shape=[jax.ShapeDtypeStruct((t, n), BF16), jax.ShapeDtypeStruct((t, tn), F32)],
        scratch_shapes=[pltpu.VMEM((tm, d), BF16)],
        compiler_params=_cparams(("arbitrary",)),
        name="proj_even",
    )(x_p, x_s, mods, gain, w, cos_t, sin_t)


def _rope_tables(dec_seq, tm):
    f32 = np.float32
    n = np.arange(dec_seq)
    row = (n // GRID_W).astype(f32)
    col = (n % GRID_W).astype(f32)
    nf = HEAD_DIM // 4
    inv_freq = np.power(f32(ROPE_BASE), -np.arange(nf, dtype=f32) / f32(nf)).astype(f32)
    ar = row[:, None] * inv_freq[None, :]
    ac = col[:, None] * inv_freq[None, :]
    cos_h = np.concatenate([np.cos(ar), np.cos(ar), np.cos(ac), np.cos(ac)], axis=-1).astype(f32)
    sin_h = np.concatenate([-np.sin(ar), np.sin(ar), -np.sin(ac), np.sin(ac)], axis=-1).astype(f32)
    scale = f32(HEAD_DIM ** -0.5)
    ones = np.ones((dec_seq, KV_WIDTH), f32)
    zeros = np.zeros((dec_seq, KV_WIDTH), f32)
    q_heads = TN_PROJ // HEAD_DIM
    cos_t = np.concatenate([np.tile(cos_h, (1, q_heads)) * scale, np.tile(cos_h, (1, N_KV_HEADS)), ones], axis=-1)
    sin_t = np.concatenate([np.tile(sin_h, (1, q_heads)) * scale, np.tile(sin_h, (1, N_KV_HEADS)), zeros], axis=-1)
    id_cos = np.concatenate([np.full((tm, TN_PROJ), scale, f32), np.ones((tm, TN_PROJ), f32)], axis=-1)
    id_sin = np.zeros((tm, 2 * TN_PROJ), f32)
    return (jnp.asarray(np.concatenate([cos_t, id_cos], axis=0), F32),
            jnp.asarray(np.concatenate([sin_t, id_sin], axis=0), F32))


def _cast_plan(weights, layer, n_steps, step_of):
    n_exp = weights[0].shape[1]
    steps_per_matrix = min(n_steps // len(weights), n_exp)
    assert steps_per_matrix >= 1 and n_exp % steps_per_matrix == 0
    e_blk = n_exp // steps_per_matrix

    def slice_of(k, idx):
        return jnp.clip(step_of(*idx) - k * steps_per_matrix, 0, steps_per_matrix - 1)

    in_specs = [pl.BlockSpec((None, e_blk) + w.shape[2:], lambda *idx, k=k: (layer, slice_of(k, idx), 0, 0))
                for k, w in enumerate(weights)]
    out_specs = [pl.BlockSpec((e_blk,) + w.shape[2:], lambda *idx, k=k: (slice_of(k, idx), 0, 0))
                 for k, w in enumerate(weights)]
    out_shapes = [jax.ShapeDtypeStruct(w.shape[1:], BF16) for w in weights]
    return steps_per_matrix, in_specs, out_specs, out_shapes


def _cast_slices(step, steps_per_matrix, srcs, dsts):
    for k, (src, dst) in enumerate(zip(srcs, dsts)):
        @pl.when((step >= k * steps_per_matrix) & (step < (k + 1) * steps_per_matrix))
        def _(src=src, dst=dst):
            dst[...] = src[...].astype(BF16)


def _sink_attend(q, keys, vals, masks, sink):
    scores = []
    for k, mask in zip(keys, masks):
        s = lax.dot_general(q, k, (((1,), (1,)), ((), ())), preferred_element_type=F32)
        if mask is not None:
            stacked = s.reshape(s.shape[0] // mask.shape[0], *mask.shape)
            s = jnp.where(mask[None], stacked, NEG_INF).reshape(s.shape)
        scores.append(s)
    m = sink
    for s in scores:
        m = jnp.maximum(m, jnp.max(s, axis=-1, keepdims=True))
    es = [jnp.exp(s - m) for s in scores]
    den = jnp.exp(sink - m)
    for e in es:
        den = den + jnp.sum(e, axis=-1, keepdims=True)
    out = None
    for e, v in zip(es, vals):
        o = jnp.dot(e.astype(BF16), v, preferred_element_type=F32)
        out = o if out is None else out + o
    return out * (1.0 / den)


def _grouped_heads_attend(sink_ref, q_ref, o_ref, kh, keys, vals, masks):
    rows = q_ref.shape[0]
    heads = [kh * Q_GROUPS + g for g in range(Q_GROUPS)]
    q = jnp.concatenate([q_ref[:, h * HEAD_DIM:(h + 1) * HEAD_DIM] for h in heads], axis=0)
    ridx = lax.broadcasted_iota(jnp.int32, (Q_GROUPS * rows, 1), 0)
    sink = jnp.full((Q_GROUPS * rows, 1), sink_ref[heads[-1]], F32)
    for g in range(Q_GROUPS - 2, -1, -1):
        sink = jnp.where(ridx < (g + 1) * rows, sink_ref[heads[g]], sink)
    o = _sink_attend(q, keys, vals, masks, sink)
    for g, h in enumerate(heads):
        o_ref[:, h * HEAD_DIM:(h + 1) * HEAD_DIM] = o[g * rows:(g + 1) * rows, :].astype(BF16)


def _ctx_attn_kernel(sink_ref, q_ref, kv_ref, o_ref):
    for kh in range(N_KV_HEADS):
        k = kv_ref[:, kh * HEAD_DIM:(kh + 1) * HEAD_DIM]
        v = kv_ref[:, KV_WIDTH + kh * HEAD_DIM:KV_WIDTH + (kh + 1) * HEAD_DIM]
        _grouped_heads_attend(sink_ref, q_ref, o_ref, kh, [k], [v], [None])


def _ctx_attention(proj, sink, batch, seq, kv_col_block):
    return pl.pallas_call(
        _ctx_attn_kernel,
        grid=(batch,),
        in_specs=[
            pl.BlockSpec(memory_space=pltpu.SMEM),
            pl.BlockSpec((seq, ATTN_WIDTH), lambda b: (b, 0)),
            pl.BlockSpec((seq, 2 * KV_WIDTH), lambda b: (b, kv_col_block)),
        ],
        out_specs=pl.BlockSpec((seq, ATTN_WIDTH), lambda b: (b, 0)),
        out_shape=jax.ShapeDtypeStruct((batch * seq, ATTN_WIDTH), BF16),
        compiler_params=_cparams(("arbitrary",)),
        name="ctx_attention",
    )(sink, proj, proj)


def _lat_attn_kernel(sink_ref, q_ref, kv_ref, ck_ref, cv_ref, wg_ref, wu_ref, wd_ref, o_ref, wg_out, wu_out, wd_out,
                     *, seq, steps_per_matrix):
    i = pl.program_id(1)
    _cast_slices(pl.program_id(0) * pl.num_programs(1) + i, steps_per_matrix,
                 (wg_ref, wu_ref, wd_ref), (wg_out, wu_out, wd_out))
    win = 3 * ATTN_BLK
    ws = pl.multiple_of(jnp.clip((i - 1) * ATTN_BLK, 0, seq - win), ATTN_BLK)
    shape = (ATTN_BLK, win)
    qpos = i * ATTN_BLK + lax.broadcasted_iota(jnp.int32, shape, 0)
    kpos = ws + lax.broadcasted_iota(jnp.int32, shape, 1)
    valid = jnp.abs(qpos - kpos) <= ATTN_BLK
    kvw = kv_ref[pl.ds(ws, win), :]
    for kh in range(N_KV_HEADS):
        k_loc = kvw[:, kh * HEAD_DIM:(kh + 1) * HEAD_DIM]
        v_loc = kvw[:, KV_WIDTH + kh * HEAD_DIM:KV_WIDTH + (kh + 1) * HEAD_DIM]
        k_ctx = ck_ref[:, kh * HEAD_DIM:(kh + 1) * HEAD_DIM]
        v_ctx = cv_ref[:, kh * HEAD_DIM:(kh + 1) * HEAD_DIM]
        _grouped_heads_attend(sink_ref, q_ref, o_ref, kh, [k_ctx, k_loc], [v_ctx, v_loc], [None, valid])


def _lat_attention(proj, sink, cache_k, cache_v, n_prompt, dec_batch, dec_seq, kv_col_block, weights, layer):
    qb = dec_seq // ATTN_BLK
    q_off = n_prompt // ATTN_BLK
    s_off = n_prompt // dec_seq
    past = cache_k.shape[1]
    spm, w_in, w_out, w_shapes = _cast_plan(weights, layer, dec_batch * qb, lambda b, i: b * qb + i)
    return pl.pallas_call(
        functools.partial(_lat_attn_kernel, seq=dec_seq, steps_per_matrix=spm),
        grid=(dec_batch, qb),
        in_specs=[
            pl.BlockSpec(memory_space=pltpu.SMEM),
            pl.BlockSpec((ATTN_BLK, ATTN_WIDTH), lambda b, i: (q_off + b * qb + i, 0)),
            pl.BlockSpec((dec_seq, 2 * KV_WIDTH), lambda b, i: (s_off + b, kv_col_block)),
            pl.BlockSpec((None, past, KV_WIDTH), lambda b, i: (b, 0, 0)),
            pl.BlockSpec((None, past, KV_WIDTH), lambda b, i: (b, 0, 0)),
        ] + w_in,
        out_specs=[pl.BlockSpec((ATTN_BLK, ATTN_WIDTH), lambda b, i: (b * qb + i, 0))] + w_out,
        out_shape=[jax.ShapeDtypeStruct((dec_batch * dec_seq, ATTN_WIDTH), BF16)] + w_shapes,
        compiler_params=_cparams(("arbitrary", "arbitrary")),
        name="lat_attention",
    )(sink, proj, proj, cache_k, cache_v, *weights)


def _sgu_kernel(ul_ref, uh_ref, vl_ref, vh_ref, g_ref, w_ref, bt_ref, z_ref):
    half = SGU_WIDTH // 2
    per_half = SGU_GROUPS // 2
    u_refs = (ul_ref, uh_ref)
    for c in range(TM_SGU // CHUNK):
        rows = slice(c * CHUNK, (c + 1) * CHUNK)
        vl = vl_ref[rows, :].astype(F32)
        vh = vh_ref[rows, :].astype(F32)
        ssq = jnp.sum(vl * vl, axis=-1, keepdims=True) + jnp.sum(vh * vh, axis=-1, keepdims=True)
        r = lax.rsqrt(ssq * (1.0 / SGU_WIDTH) + EPS)
        vn = ((vl * r * g_ref[:, :half]).astype(BF16), (vh * r * g_ref[:, half:]).astype(BF16))
        for g in range(SGU_GROUPS):
            src = g // per_half
            cols = slice((g % per_half) * SGU_GROUP_DIM, (g % per_half + 1) * SGU_GROUP_DIM)
            mixed = jnp.dot(w_ref[g], vn[src][:, cols], preferred_element_type=F32) + bt_ref[:, g:g + 1]
            z_ref[rows, g * SGU_GROUP_DIM:(g + 1) * SGU_GROUP_DIM] = (
                u_refs[src][rows, cols].astype(F32) * mixed).astype(BF16)


def _sgu(proj, norm_g, w_s, b_t):
    t = proj.shape[0]
    half = SGU_WIDTH // 2
    u0 = (ATTN_WIDTH + 2 * KV_WIDTH) // half
    blocks = [pl.BlockSpec((TM_SGU, half), functools.partial(lambda i, c: (i, c), c=u0 + k)) for k in range(4)]
    return pl.pallas_call(
        _sgu_kernel,
        grid=(t // TM_SGU,),
        in_specs=blocks + [
            pl.BlockSpec((1, SGU_WIDTH), lambda i: (0, 0)),
            pl.BlockSpec((SGU_GROUPS, CHUNK, CHUNK), lambda i: (0, 0, 0)),
            pl.BlockSpec((CHUNK, SGU_GROUPS), lambda i: (0, 0)),
        ],
        out_specs=pl.BlockSpec((TM_SGU, SGU_WIDTH), lambda i: (i, 0)),
        out_shape=jax.ShapeDtypeStruct((t, SGU_WIDTH), BF16),
        compiler_params=_cparams(("arbitrary",)),
        name="sgu",
    )(proj, proj, proj, proj, norm_g, w_s, b_t)


def _residual_router(rows, x, acc, mod_ref, g2_ref, wrt_ref, brt_ref, x1_ref, h2_ref, lg_ref):
    x1 = x + mod_ref[2:3, :] * acc
    x1_ref[rows, :] = x1
    h2 = _rms_mod(x1, g2_ref[...], mod_ref[4:5, :], mod_ref[3:4, :])
    h2_ref[rows, :] = _pack_bf16_pairs(h2)
    lg = lax.dot_general(wrt_ref[...], h2.astype(BF16), (((1,), (1,)), ((), ())), preferred_element_type=F32)
    lg_ref[:, rows] = lg + brt_ref[...]


def _sub_blocks(ref):
    return [slice(r, r + TM_OUT) for r in range(0, ref.shape[0], TM_OUT)]


def _out_even_kernel(xp_ref, xs_ref, ap_ref, as_ref, z_ref, w_ref, mod_ref, g2_ref, wrt_ref, brt_ref,
                     x1_ref, h2_ref, lg_ref, *, n_prompt_blocks):
    is_prompt = pl.program_id(0) < n_prompt_blocks
    for rows in _sub_blocks(x1_ref):
        a = jnp.where(is_prompt, ap_ref[rows, :].astype(F32), as_ref[rows, :].astype(F32)).astype(BF16)
        acc = jnp.dot(a, w_ref[0:ATTN_WIDTH, :], preferred_element_type=F32)
        acc = acc + jnp.dot(z_ref[rows, :], w_ref[ATTN_WIDTH:, :], preferred_element_type=F32)
        x = jnp.where(is_prompt, xp_ref[rows, :], xs_ref[rows, :])
        _residual_router(rows, x, acc, mod_ref, g2_ref, wrt_ref, brt_ref, x1_ref, h2_ref, lg_ref)


def _out_odd_kernel(x_ref, hc_ref, w_ref, mod_ref, g2_ref, wrt_ref, brt_ref, x1_ref, h2_ref, lg_ref):
    for rows in _sub_blocks(x1_ref):
        acc = jnp.dot(hc_ref[rows, :], w_ref[...], preferred_element_type=F32)
        _residual_router(rows, x_ref[rows, :], acc, mod_ref, g2_ref, wrt_ref, brt_ref, x1_ref, h2_ref, lg_ref)


def _out_proj(xs, lhs, w, mods, gain2, wrt, brt, n_prompt, dec_seq):
    d = xs[0].shape[1]
    t = sum(x.shape[0] for x in xs)
    tm = TM_OUTPROJ
    npb = n_prompt // tm
    bps = dec_seq // tm
    nsb = t // tm - npb
    if len(xs) == 2:
        x_specs = [pl.BlockSpec((tm, d), lambda i: (jnp.minimum(i, npb - 1), 0)),
                   pl.BlockSpec((tm, d), lambda i: (jnp.maximum(i - npb, 0), 0))]
    else:
        x_specs = [pl.BlockSpec((tm, d), lambda i: (i, 0))]

    def mod_idx(i):
        return (jnp.where(i < npb, 0, 1 + (i - npb) // bps), 0, 0)

    if len(lhs) == 3:
        a_p, a_s, z = lhs
        body = functools.partial(_out_even_kernel, n_prompt_blocks=npb)
        lhs_specs = [
            pl.BlockSpec((tm, ATTN_WIDTH), lambda i: (jnp.minimum(i, npb - 1), 0)),
            pl.BlockSpec((tm, ATTN_WIDTH), lambda i: (jnp.clip(i - npb, 0, nsb - 1), 0)),
            pl.BlockSpec((tm, SGU_WIDTH), lambda i: (i, 0)),
        ]
    else:
        body = _out_odd_kernel
        lhs_specs = [pl.BlockSpec((tm, d), lambda i: (i, 0))]
    return pl.pallas_call(
        body,
        grid=(t // tm,),
        in_specs=x_specs + lhs_specs + [
            pl.BlockSpec((w.shape[0], d), lambda i: (0, 0), pipeline_mode=pl.Buffered(1)),
            pl.BlockSpec((None, 6, d), mod_idx),
            pl.BlockSpec((1, d), lambda i: (0, 0)),
            pl.BlockSpec((ROUTER_LANES, d), lambda i: (0, 0)),
            pl.BlockSpec((ROUTER_LANES, 1), lambda i: (0, 0)),
        ],
        out_specs=[
            pl.BlockSpec((tm, d), lambda i: (i, 0)),
            pl.BlockSpec((tm, d // 2), lambda i: (i, 0)),
            pl.BlockSpec((ROUTER_LANES, tm), lambda i: (0, i)),
        ],
        out_shape=[
            jax.ShapeDtypeStruct((t, d), F32),
            jax.ShapeDtypeStruct((t, d // 2), jnp.uint32),
            jax.ShapeDtypeStruct((ROUTER_LANES, t), F32),
        ],
        compiler_params=_cparams(("arbitrary",)),
        name="out_proj",
    )(*xs, *lhs, w, mods, gain2, wrt, brt)


def _first_argmax4(v0, v1, v2, v3):
    m = jnp.maximum(jnp.maximum(v0, v1), jnp.maximum(v2, v3))
    idx = jnp.where(v0 == m, 0.0, jnp.where(v1 == m, 1.0, jnp.where(v2 == m, 2.0, 3.0)))
    return m, idx


def _route_kernel(lg_ref, pos_ref, ea_ref, eb_ref, nv_ref, ends_ref, oh_scr, rank_scr, *, n_tokens):
    lg = lg_ref[...]
    rows = [lg[r:r + 1, :] for r in range(N_GROUPS_MOE + N_EXPERTS)]
    _, gidx = _first_argmax4(*rows[:N_GROUPS_MOE])
    e = []
    for k in range(EXPERTS_PER_GROUP):
        cand = [rows[N_GROUPS_MOE + g * EXPERTS_PER_GROUP + k] for g in range(N_GROUPS_MOE)]
        e.append(jnp.where(gidx == 0.0, cand[0], jnp.where(gidx == 1.0, cand[1], jnp.where(gidx == 2.0, cand[2], cand[3]))))
    _, l1 = _first_argmax4(*e)
    e2 = [jnp.where(l1 == float(k), -jnp.inf, e[k]) for k in range(EXPERTS_PER_GROUP)]
    _, l2 = _first_argmax4(*e2)
    lo = jnp.minimum(l1, l2)
    hi = jnp.maximum(l1, l2)
    pair = jnp.where(lo == 0.0, hi - 1.0, jnp.where(lo == 2.0, 4.0, jnp.where(hi == 3.0, 3.0, 5.0)))
    cls = gidx * float(PAIRS_PER_GROUP) + pair

    crow = lax.broadcasted_iota(jnp.int32, (CLASS_ROWS, n_tokens), 0).astype(F32)
    oh_scr[...] = jnp.where(crow == cls, 1.0, 0.0).astype(BF16)

    jj = lax.broadcasted_iota(jnp.int32, (CUM_CHUNK, CUM_CHUNK), 0)
    ii = lax.broadcasted_iota(jnp.int32, (CUM_CHUNK, CUM_CHUNK), 1)
    tri = jnp.where(jj < ii, 1.0, 0.0).astype(BF16)

    def chunk(c, carry):
        off = pl.multiple_of(c * CUM_CHUNK, CUM_CHUNK)
        oh = oh_scr[:, pl.ds(off, CUM_CHUNK)]
        ohf = oh.astype(F32)
        cum = jnp.dot(oh, tri, preferred_element_type=F32) + carry
        rank_scr[:, pl.ds(off, CUM_CHUNK)] = jnp.sum(cum * ohf, axis=0, keepdims=True)
        return carry + jnp.sum(ohf, axis=1, keepdims=True)

    counts = lax.fori_loop(0, n_tokens // CUM_CHUNK, chunk, jnp.zeros((CLASS_ROWS, 1), F32))
    padded = jnp.floor((counts + float(MOE_BLK - 1)) * (1.0 / MOE_BLK)) * float(MOE_BLK)
    run = jnp.zeros((1, 1), F32)
    starts_rows = []
    for c in range(CLASS_ROWS):
        starts_rows.append(run)
        run = run + padded[c:c + 1, :]
    starts = jnp.concatenate(starts_rows, axis=0)
    ends = starts + padded
    total = run

    ohf = oh_scr[...].astype(F32)
    pos = rank_scr[...] + jnp.sum(ohf * starts, axis=0, keepdims=True)
    pos_ref[...] = pos.astype(jnp.int32)

    bstart = lax.broadcasted_iota(jnp.int32, (CLASS_ROWS, MAX_MOE_BLOCKS), 1).astype(F32) * float(MOE_BLK)
    brow = lax.broadcasted_iota(jnp.int32, (CLASS_ROWS, MAX_MOE_BLOCKS), 0)
    done = jnp.where((ends <= bstart) & (brow < N_CLASSES), 1.0, 0.0)
    bcls = jnp.minimum(jnp.sum(done, axis=0, keepdims=True), float(N_CLASSES - 1))
    grp = (jnp.where(bcls >= 6.0, 1.0, 0.0) + jnp.where(bcls >= 12.0, 1.0, 0.0) + jnp.where(bcls >= 18.0, 1.0, 0.0))
    bp = bcls - grp * float(PAIRS_PER_GROUP)
    first = jnp.where(bp >= 3.0, 1.0, 0.0) + jnp.where(bp >= 4.0, 1.0, 0.0)
    second = jnp.where(bp < 3.0, bp + 1.0, jnp.where(bp < 5.0, 3.0, 1.0))
    ea_ref[...] = (grp * float(EXPERTS_PER_GROUP) + first).astype(jnp.int32)
    eb_ref[...] = (grp * float(EXPERTS_PER_GROUP) + second).astype(jnp.int32)
    own = jnp.where(brow.astype(F32) == bcls, starts + counts, 0.0)
    filled = jnp.sum(own, axis=0, keepdims=True) - bstart[0:1, :]
    nv_ref[...] = jnp.clip(filled, 0.0, float(MOE_BLK)).astype(jnp.int32)
    lane = lax.broadcasted_iota(jnp.int32, (1, MAX_MOE_BLOCKS), 1)
    ends_row = jnp.zeros((1, MAX_MOE_BLOCKS), F32)
    for c in range(N_CLASSES):
        ends_row = jnp.where(lane == c, starts_rows[c] + padded[c:c + 1, :], ends_row)
    ends_ref[...] = ends_row.astype(jnp.int32)


def _route(logits_t):
    t = logits_t.shape[1]
    rows = N_CLASSES
    i32 = jnp.int32
    return pl.pallas_call(
        functools.partial(_route_kernel, n_tokens=t),
        grid=(1,),
        in_specs=[pl.BlockSpec((rows, t), lambda i: (0, 0))],
        out_specs=[
            pl.BlockSpec((1, t), lambda i: (0, 0)),
        ] + [pl.BlockSpec((1, MAX_MOE_BLOCKS), lambda i: (0, 0))] * 4,
        out_shape=[jax.ShapeDtypeStruct((1, t), i32)] + [jax.ShapeDtypeStruct((1, MAX_MOE_BLOCKS), i32)] * 4,
        scratch_shapes=[pltpu.VMEM((CLASS_ROWS, t), BF16), pltpu.VMEM((1, t), F32)],
        compiler_params=_cparams(("arbitrary",)),
        name="route",
    )(logits_t)


DMA_THREADS = 2


def _row_copy(src_ref, src_row, dst_ref, dst_row, sem):
    return pltpu.make_async_copy(src_ref.at[pl.ds(src_row, 1), :], dst_ref.at[pl.ds(dst_row, 1), :], sem)


def _start_row_copies(n_rows, make_copy):
    for r in range(n_rows):
        make_copy(r).start(priority=r % DMA_THREADS)


def _dispatch_kernel(pos_ref, ends_ref, h_ref, o_ref, zbuf, sem, zsem):
    step = pl.program_id(0)

    @pl.when(step == 0)
    def _():
        zbuf[...] = jnp.zeros_like(zbuf)
        total = ends_ref[N_CLASSES - 1]
        n_free = (o_ref.shape[0] - total) // MOE_BLK

        def zero_block(first_row, phase):
            first = pl.multiple_of(first_row, MOE_BLK)
            getattr(pltpu.make_async_copy(zbuf, o_ref.at[pl.ds(first, MOE_BLK), :], zsem), phase)()

        for phase in ("start", "wait"):
            prev = 0
            for c in range(N_CLASSES):
                end = ends_ref[c]
                pl.when(end > prev)(functools.partial(zero_block, end - MOE_BLK, phase))
                prev = end

            def free_block(k, carry, phase=phase):
                zero_block(total + k * MOE_BLK, phase)
                return carry

            lax.fori_loop(0, n_free, free_block, 0)

    base = step * TM_OUT

    _start_row_copies(TM_OUT, lambda r: _row_copy(h_ref, r, o_ref, pos_ref[base + r], sem))
    pltpu.make_async_copy(h_ref, o_ref.at[pl.ds(0, TM_OUT), :], sem).wait()


def _dispatch(pos, ends, h2, n_rows):
    t, d = h2.shape
    return pl.pallas_call(
        _dispatch_kernel,
        grid_spec=pltpu.PrefetchScalarGridSpec(
            num_scalar_prefetch=2,
            grid=(t // TM_OUT,),
            in_specs=[pl.BlockSpec((TM_OUT, d), lambda i, pos, ends: (i, 0))],
            out_specs=pl.BlockSpec(memory_space=pl.ANY),
            scratch_shapes=[pltpu.VMEM((MOE_BLK, d), h2.dtype), pltpu.SemaphoreType.DMA(()),
                            pltpu.SemaphoreType.DMA(())],
        ),
        out_shape=jax.ShapeDtypeStruct((n_rows, d), h2.dtype),
        compiler_params=_cparams(("arbitrary",)),
        name="moe_dispatch",
    )(pos, ends, h2)


def _expert_ffn(x, wg_ref, wu_ref, wd_ref):
    gate = jnp.dot(x, wg_ref[...], preferred_element_type=F32)
    up = jnp.dot(x, wu_ref[...], preferred_element_type=F32)
    hidden = (gate * jax.nn.sigmoid(gate) * up).astype(BF16)
    return jnp.dot(hidden, wd_ref[...], preferred_element_type=F32)


def _moe_kernel(ea_ref, eb_ref, nv_ref, x_ref, wr_ref, br_ref, wga, wua, wda, wgb, wub, wdb, y_ref):
    b = pl.program_id(0)
    n_valid = nv_ref[b]

    def run(rows):
        x = _unpack_bf16_pairs(x_ref[rows, :])
        lg = jnp.dot(x, wr_ref[...], preferred_element_type=F32) + br_ref[...]
        lane = lax.broadcasted_iota(jnp.int32, lg.shape, 1)
        ea = ea_ref[b]
        eb = eb_ref[b]
        grp = ea // EXPERTS_PER_GROUP
        is_grp = lane < N_GROUPS_MOE
        gm = jnp.max(jnp.where(is_grp, lg, -jnp.inf), axis=-1, keepdims=True)
        ge = jnp.where(is_grp, jnp.exp(lg - gm), 0.0)
        p_g = jnp.sum(jnp.where(lane == grp, ge, 0.0), axis=-1, keepdims=True) / jnp.sum(ge, axis=-1, keepdims=True)
        la = jnp.sum(jnp.where(lane == N_GROUPS_MOE + ea, lg, 0.0), axis=-1, keepdims=True)
        lb = jnp.sum(jnp.where(lane == N_GROUPS_MOE + eb, lg, 0.0), axis=-1, keepdims=True)
        mm = jnp.maximum(la, lb)
        xa = jnp.exp(la - mm)
        xb = jnp.exp(lb - mm)
        ga = xa / (xa + xb) * p_g
        gb = xb / (xa + xb) * p_g
        ya = _expert_ffn(x, wga, wua, wda)
        yb = _expert_ffn(x, wgb, wub, wdb)
        y_ref[rows, :] = ya * ga + yb * gb

    half = MOE_BLK // 2

    @pl.when(n_valid > half)
    def _():
        run(slice(0, MOE_BLK))

    @pl.when((n_valid > 0) & (n_valid <= half))
    def _():
        run(slice(0, half))
        y_ref[half:, :] = jnp.zeros((MOE_BLK - half, y_ref.shape[1]), y_ref.dtype)

    @pl.when(n_valid == 0)
    def _():
        y_ref[...] = jnp.zeros_like(y_ref)


def _moe_experts(ea, eb, n_valid, hs, wr, br, w_gate, w_up, w_down, n_blocks):
    d = w_gate.shape[1]
    de = w_gate.shape[2]

    def wa_idx(b, ea, eb, nv):
        return (ea[b], 0, 0)

    def wb_idx(b, ea, eb, nv):
        return (eb[b], 0, 0)

    return pl.pallas_call(
        _moe_kernel,
        grid_spec=pltpu.PrefetchScalarGridSpec(
            num_scalar_prefetch=3,
            grid=(n_blocks,),
            in_specs=[
                pl.BlockSpec((MOE_BLK, hs.shape[1]), lambda b, ea, eb, nv: (jnp.where(nv[b] > 0, b, 0), 0)),
                pl.BlockSpec((d, ROUTER_LANES), lambda b, ea, eb, nv: (0, 0)),
                pl.BlockSpec((1, ROUTER_LANES), lambda b, ea, eb, nv: (0, 0)),
                pl.BlockSpec((None, d, de), wa_idx),
                pl.BlockSpec((None, d, de), wa_idx),
                pl.BlockSpec((None, de, d), wa_idx),
                pl.BlockSpec((None, d, de), wb_idx),
                pl.BlockSpec((None, d, de), wb_idx),
                pl.BlockSpec((None, de, d), wb_idx),
            ],
            out_specs=pl.BlockSpec((MOE_BLK, d), lambda b, ea, eb, nv: (b, 0)),
        ),
        out_shape=jax.ShapeDtypeStruct((n_blocks * MOE_BLK, d), F32),
        compiler_params=_cparams(("arbitrary",)),
        name="moe_experts",
    )(ea, eb, n_valid, hs, wr, br, w_gate, w_up, w_down, w_gate, w_up, w_down)


def _combine_kernel(pos_ref, ys_ref, x1_ref, mod_ref, gf_ref, o_ref, ybuf, sem, *, block_off, n_steps, final):
    i = pl.program_id(0)
    slot = i % 2

    def start_gather(step, into):
        base = (step + block_off) * TM_OUT
        _start_row_copies(
            TM_OUT, lambda r: _row_copy(ys_ref, pos_ref[base + r], ybuf.at[into], r, sem.at[into]))

    pl.when(i == 0)(lambda: start_gather(0, 0))
    pl.when(i + 1 < n_steps)(lambda: start_gather(i + 1, 1 - slot))
    pltpu.make_async_copy(ys_ref.at[pl.ds(0, TM_OUT), :], ybuf.at[slot], sem.at[slot]).wait()
    x2 = x1_ref[...] + mod_ref[5:6, :] * ybuf[slot]
    if final:
        ms = jnp.mean(x2 * x2, axis=-1, keepdims=True)
        x2 = x2 * lax.rsqrt(ms + EPS) * gf_ref[...]
    o_ref[...] = x2


def _combine(pos, ys, x1, mods, gain_f, n_prompt, dec_seq, row_off, n_rows, final):
    d = x1.shape[1]
    tm = TM_OUT
    npb = n_prompt // tm
    bps = dec_seq // tm
    boff = row_off // tm

    def mod_idx(i, pos):
        blk = i + boff
        return (jnp.where(blk < npb, 0, 1 + (blk - npb) // bps), 0, 0)

    return pl.pallas_call(
        functools.partial(_combine_kernel, block_off=boff, n_steps=n_rows // tm, final=final),
        grid_spec=pltpu.PrefetchScalarGridSpec(
            num_scalar_prefetch=1,
            grid=(n_rows // tm,),
            in_specs=[
                pl.BlockSpec(memory_space=pl.ANY),
                pl.BlockSpec((tm, d), lambda i, pos: (i + boff, 0)),
                pl.BlockSpec((None, 6, d), mod_idx),
                pl.BlockSpec((1, d), lambda i, pos: (0, 0)),
            ],
            out_specs=pl.BlockSpec((tm, d), lambda i, pos: (i, 0)),
            scratch_shapes=[pltpu.VMEM((2, tm, d), F32), pltpu.SemaphoreType.DMA((2,))],
        ),
        out_shape=jax.ShapeDtypeStruct((n_rows, d), F32),
        compiler_params=_cparams(("arbitrary",)),
        name="moe_combine",
    )(pos, ys, x1, mods, gain_f)


def _proj_glu_kernel(pos_ref, ys_ref, x1_ref, mod_prev_ref, mod_ref, g_ref, w_ref, o_ref, x2_ref, h_scr, ybuf, sem,
                     *, n_steps):
    i = pl.program_id(0)
    slot = i % 2
    tm = x1_ref.shape[0]

    def start_gather(step, into):
        base = step * tm
        _start_row_copies(tm, lambda r: _row_copy(ys_ref, pos_ref[base + r], ybuf.at[into], r, sem.at[into]))

    pl.when(i == 0)(lambda: start_gather(0, 0))
    pl.when(i + 1 < n_steps)(lambda: start_gather(i + 1, 1 - slot))
    pltpu.make_async_copy(ys_ref.at[pl.ds(0, tm), :], ybuf.at[slot], sem.at[slot]).wait()
    x2 = x1_ref[...] + mod_prev_ref[5:6, :] * ybuf[slot]
    x2_ref[...] = x2
    h_scr[...] = _rms_mod(x2, g_ref[...], mod_ref[1:2, :], mod_ref[0:1, :]).astype(BF16)
    width = w_ref.shape[1] // 2
    for j in range(width // TN_PROJ):
        cols = slice(j * TN_PROJ, (j + 1) * TN_PROJ)
        gcols = slice(width + j * TN_PROJ, width + (j + 1) * TN_PROJ)
        val = jnp.dot(h_scr[...], w_ref[:, cols], preferred_element_type=F32)
        gate = jnp.dot(h_scr[...], w_ref[:, gcols], preferred_element_type=F32)
        o_ref[:, cols] = (val * jax.nn.sigmoid(gate)).astype(BF16)


def _proj_glu(pos, ys, x1, mods_prev, mods, gain, w, n_prompt, dec_seq):
    t, d = x1.shape
    n = w.shape[1]
    tm = TM_PROJ
    npb = n_prompt // tm
    bps = dec_seq // tm

    def mod_idx(i, pos):
        return (jnp.where(i < npb, 0, 1 + (i - npb) // bps), 0, 0)

    return pl.pallas_call(
        functools.partial(_proj_glu_kernel, n_steps=t // tm),
        grid_spec=pltpu.PrefetchScalarGridSpec(
            num_scalar_prefetch=1,
            grid=(t // tm,),
            in_specs=[
                pl.BlockSpec(memory_space=pl.ANY),
                pl.BlockSpec((tm, d), lambda i, pos: (i, 0)),
                pl.BlockSpec((None, 6, d), mod_idx),
                pl.BlockSpec((None, 6, d), mod_idx),
                pl.BlockSpec((1, d), lambda i, pos: (0, 0)),
                pl.BlockSpec((d, n), lambda i, pos: (0, 0), pipeline_mode=pl.Buffered(1)),
            ],
            out_specs=[
                pl.BlockSpec((tm, n // 2), lambda i, pos: (i, 0)),
                pl.BlockSpec((tm, d), lambda i, pos: (i, 0)),
            ],
            scratch_shapes=[pltpu.VMEM((tm, d), BF16), pltpu.VMEM((2, tm, d), F32), pltpu.SemaphoreType.DMA((2,))],
        ),
        out_shape=[jax.ShapeDtypeStruct((t, n // 2), BF16), jax.ShapeDtypeStruct((t, d), F32)],
        compiler_params=_cparams(("arbitrary",)),
        name="proj_glu",
    )(pos, ys, x1, mods_prev, mods, gain, w)


CONV_ROWS = 64
CONV_COLS = 256
F32_SUBLANES = 8


def _conv_kernel(x_ref, prev_ref, next_ref, dw_ref, b_ref, g_ref, wg_ref, wu_ref, wd_ref, o_ref, wg_out, wu_out, wd_out,
                 xpad, shifted, ybuf, *, n_prompt_blocks, blocks_per_seq, steps_per_matrix):
    i = pl.program_id(0)
    _cast_slices(i, steps_per_matrix, (wg_ref, wu_ref, wd_ref), (wg_out, wu_out, wd_out))
    in_sample = i >= n_prompt_blocks
    j = i - n_prompt_blocks
    is_start = jnp.logical_or(jnp.logical_not(in_sample), j % blocks_per_seq == 0)
    is_end = jnp.logical_or(jnp.logical_not(in_sample), j % blocks_per_seq == blocks_per_seq - 1)
    tm = TM_OUT
    xpad[0:CONV_HALO, :] = jnp.where(is_start, 0.0, prev_ref[...].astype(F32))
    xpad[CONV_HALO:CONV_HALO + tm, :] = x_ref[...].astype(F32)
    xpad[CONV_HALO + tm:, :] = jnp.where(is_end, 0.0, next_ref[...].astype(F32))
    first = CONV_HALO - CONV_K // 2
    span = shifted.shape[1]

    def col_chunk(c, ssq):
        c0 = pl.multiple_of(c * CONV_COLS, CONV_COLS)
        cols = pl.ds(c0, CONV_COLS)
        w = dw_ref[:, cols]
        bias = b_ref[:, cols]
        for s in range(1, F32_SUBLANES):
            shifted[s - 1] = xpad[s:s + span, cols]
        parts = []
        for rc in range(tm // CONV_ROWS):
            r0 = rc * CONV_ROWS
            acc = jnp.zeros((CONV_ROWS, CONV_COLS), F32)
            for k in range(CONV_K):
                whole, s = divmod(first + k, F32_SUBLANES)
                base = whole * F32_SUBLANES + r0
                if s == 0:
                    src = xpad[base:base + CONV_ROWS, cols]
                else:
                    src = shifted[s - 1, base:base + CONV_ROWS, :]
                acc = acc + w[k:k + 1, :] * src
            acc = acc + bias
            ybuf[r0:r0 + CONV_ROWS, cols] = acc
            parts.append(jnp.sum(acc * acc, axis=-1, keepdims=True))
        return ssq + jnp.concatenate(parts, axis=0)

    d = x_ref.shape[1]
    ssq = lax.fori_loop(0, d // CONV_COLS, col_chunk, jnp.zeros((tm, 1), F32))
    y = ybuf[...] * lax.rsqrt(ssq * (1.0 / d) + EPS) * g_ref[...]
    o_ref[...] = (y * jax.nn.sigmoid(y)).astype(BF16)


def _conv_mix(xg, dw, dw_b, norm_g, n_prompt, seq, dec_seq, weights, layer):
    t, d = xg.shape
    tm = TM_OUT
    assert seq == tm, "context sequences must be exactly one conv row block"
    npb = n_prompt // tm
    bps = dec_seq // tm
    hb = tm // CONV_HALO
    last = t // CONV_HALO - 1
    spm, w_in, w_out, w_shapes = _cast_plan(weights, layer, t // tm, lambda i: i)
    return pl.pallas_call(
        functools.partial(_conv_kernel, n_prompt_blocks=npb, blocks_per_seq=bps, steps_per_matrix=spm),
        grid=(t // tm,),
        in_specs=[
            pl.BlockSpec((tm, d), lambda i: (i, 0)),
            pl.BlockSpec((CONV_HALO, d), lambda i: (jnp.maximum(i * hb - 1, 0), 0)),
            pl.BlockSpec((CONV_HALO, d), lambda i: (jnp.minimum((i + 1) * hb, last), 0)),
            pl.BlockSpec((dw.shape[0], d), lambda i: (0, 0)),
            pl.BlockSpec((1, d), lambda i: (0, 0)),
            pl.BlockSpec((1, d), lambda i: (0, 0)),
        ] + w_in,
        out_specs=[pl.BlockSpec((tm, d), lambda i: (i, 0))] + w_out,
        out_shape=[jax.ShapeDtypeStruct((t, d), BF16)] + w_shapes,
        scratch_shapes=[
            pltpu.VMEM((tm + 2 * CONV_HALO, d), F32),
            pltpu.VMEM((F32_SUBLANES - 1, tm + 2 * CONV_HALO - F32_SUBLANES, CONV_COLS), F32),
            pltpu.VMEM((tm, d), F32),
        ],
        compiler_params=_cparams(("arbitrary",)),
        name="conv_mix",
    )(xg, xg, xg, dw, dw_b, norm_g, *weights)


def _router_operands(w_rg, b_rg, w_re, b_re):
    d = w_rg.shape[0]
    used = N_GROUPS_MOE + N_EXPERTS
    w = jnp.concatenate([w_rg, w_re, jnp.zeros((d, ROUTER_LANES - used), F32)], axis=1)
    b = jnp.concatenate([b_rg, b_re, jnp.zeros((ROUTER_LANES - used,), F32)])
    return w.astype(BF16), b.reshape(1, ROUTER_LANES), w.T.astype(BF16), b.reshape(ROUTER_LANES, 1)


def _moe_rows(h2, logits_t, wr, br, wg, wu, wd):
    t = h2.shape[0]
    n_blocks = -(-(t + N_CLASSES * (MOE_BLK - 1)) // MOE_BLK)
    assert n_blocks <= MAX_MOE_BLOCKS
    pos, ea, eb, n_valid, ends = _route(logits_t)
    pos = pos.reshape(t)
    hs = _dispatch(pos, ends.reshape(-1), h2, n_blocks * MOE_BLK)
    ys = _moe_experts(ea.reshape(-1), eb.reshape(-1), n_valid.reshape(-1), hs, wr, br, wg, wu, wd, n_blocks)
    return pos, ys


def kernel(x_prompt, x_sample, cache_k, cache_v, c, c_ctx, w_in_even, attn_sink, sgu_norm, sgu_w, sgu_b,
           w_out_even, conv_w_in, conv_dw, conv_dw_b, conv_norm, conv_w_out, ada_w, ada_b, norm_mix, norm_ffn,
           router_group_w, router_group_b, router_expert_w, router_expert_b, expert_w_gate, expert_w_up,
           expert_w_down, final_norm):
    batch, seq, d = x_prompt.shape
    dec_batch, dec_seq, _ = x_sample.shape
    depth = ada_w.shape[0]
    n_prompt = batch * seq
    n_sample = dec_batch * dec_seq
    assert d == D_MODEL and depth == 2 and n_prompt % dec_seq == 0 and dec_seq % TM_PROJ == 0
    x_p = x_prompt.reshape(n_prompt, d)
    x_s = x_sample.reshape(n_sample, d)

    n_mod = -(-(1 + dec_batch) // 8) * 8
    cvec = jnp.concatenate([c_ctx[None, :], c, jnp.zeros((n_mod - 1 - dec_batch, d), F32)], axis=0)
    mods = _ada_params(cvec, ada_w, ada_b).reshape(depth, n_mod, 6, d)
    expert_w = (expert_w_gate, expert_w_up, expert_w_down)

    cos_t, sin_t = _rope_tables(dec_seq, TM_PROJ)
    proj, kv32 = _proj_even(x_p, x_s, mods[0], norm_mix[0][None, :], w_in_even[0].astype(BF16), cos_t, sin_t,
                            dec_seq // TM_PROJ)
    kv_col_block = ATTN_WIDTH // (2 * KV_WIDTH)
    sink = attn_sink[0]
    a_p = _ctx_attention(proj, sink, batch, seq, kv_col_block)
    past = cache_k.shape[2]
    ck = cache_k[:, 0].reshape(dec_batch, past, KV_WIDTH).astype(BF16)
    cv = cache_v[:, 0].reshape(dec_batch, past, KV_WIDTH).astype(BF16)
    a_s, *experts0 = _lat_attention(proj, sink, ck, cv, n_prompt, dec_batch, dec_seq, kv_col_block, expert_w, 0)
    z = _sgu(proj, sgu_norm[0][None, :], sgu_w[0].astype(BF16), sgu_b[0].T)

    wr0, br0, wrt0, brt0 = _router_operands(router_group_w[0], router_group_b[0], router_expert_w[0], router_expert_b[0])
    x1, h2, lg = _out_proj((x_p, x_s), (a_p, a_s, z), w_out_even[0].astype(BF16), mods[0], norm_ffn[0][None, :],
                           wrt0, brt0, n_prompt, dec_seq)
    pos, ys = _moe_rows(h2, lg, wr0, br0, *experts0)
    gain_f = final_norm[None, :]

    xg, x = _proj_glu(pos, ys, x1, mods[0], mods[1], norm_mix[1][None, :], conv_w_in[0].astype(BF16),
                      n_prompt, dec_seq)
    dw = jnp.concatenate([conv_dw[0], jnp.zeros((1, d), F32)], axis=0)
    hc, *experts1 = _conv_mix(xg, dw, conv_dw_b[0][None, :], conv_norm[0][None, :], n_prompt, seq, dec_seq,
                              expert_w, 1)
    wr1, br1, wrt1, brt1 = _router_operands(router_group_w[1], router_group_b[1], router_expert_w[1], router_expert_b[1])
    x1, h2, lg = _out_proj((x,), (hc,), conv_w_out[0].astype(BF16), mods[1], norm_ffn[1][None, :],
                           wrt1, brt1, n_prompt, dec_seq)
    pos, ys = _moe_rows(h2, lg, wr1, br1, *experts1)
    y_prompt = _combine(pos, ys, x1, mods[1], gain_f, n_prompt, dec_seq, 0, n_prompt, final=True)
    y_sample = _combine(pos, ys, x1, mods[1], gain_f, n_prompt, dec_seq, n_prompt, n_sample, final=True)

    new_k = kv32[:n_prompt, :KV_WIDTH].reshape(batch, 1, seq, N_KV_HEADS, HEAD_DIM)
    new_v = kv32[:n_prompt, KV_WIDTH:].reshape(batch, 1, seq, N_KV_HEADS, HEAD_DIM)
    return (y_prompt.reshape(batch, seq, d), y_sample.reshape(dec_batch, dec_seq, d), new_k, new_v)
```

```python
import functools

import jax
import jax.numpy as jnp
import numpy as np
from jax import lax
from jax.experimental import pallas as pl
from jax.experimental.pallas import tpu as pltpu

F32 = jnp.float32
BF16 = jnp.bfloat16

D_MODEL = 2048
GRID_W = 64
ATTN_BLK = 128
N_HEADS = 16
N_KV_HEADS = 4
Q_GROUPS = N_HEADS // N_KV_HEADS
HEAD_DIM = 64
ATTN_WIDTH = N_HEADS * HEAD_DIM
KV_WIDTH = N_KV_HEADS * HEAD_DIM
ROPE_BASE = 10000.0
SGU_GROUPS = 4
SGU_WIDTH = D_MODEL // 2
SGU_GROUP_DIM = SGU_WIDTH // SGU_GROUPS
CHUNK = 128
IN_WIDTH_EVEN = ATTN_WIDTH + 2 * KV_WIDTH + 2 * SGU_WIDTH
CONV_K = 31
CONV_HALO = 16
N_GROUPS_MOE = 4
EXPERTS_PER_GROUP = 4
N_EXPERTS = N_GROUPS_MOE * EXPERTS_PER_GROUP
PAIRS_PER_GROUP = 6
N_CLASSES = N_GROUPS_MOE * PAIRS_PER_GROUP
D_EXPERT = 512
MOE_BLK = 256
EPS = 1e-6
NEG_INF = -1e30

LANES = 128
ROUTER_LANES = 128
CLASS_ROWS = 32
MAX_MOE_BLOCKS = 128
TM_PROJ = 512
TN_PROJ = 512
TM_OUT = 256
TM_DISPATCH = 512
TM_OUTPROJ = 512
TM_SGU = 512
CUM_CHUNK = 256
VMEM_LIMIT = 56 * 1024 * 1024


def _cparams(sem):
    return pltpu.CompilerParams(dimension_semantics=sem, vmem_limit_bytes=VMEM_LIMIT)


def _pack_bf16_pairs(h):
    w = h.shape[1] // 2
    lo = pltpu.bitcast(h[:, :w].astype(BF16).astype(F32), jnp.uint32) >> 16
    hi = pltpu.bitcast(h[:, w:].astype(BF16).astype(F32), jnp.uint32) & jnp.uint32(0xFFFF0000)
    return hi | lo


def _unpack_bf16_pairs(u):
    lo = pltpu.bitcast(u << 16, F32)
    hi = pltpu.bitcast(u & jnp.uint32(0xFFFF0000), F32)
    return jnp.concatenate([lo, hi], axis=1).astype(BF16)


def _rms_mod(x, gain, scale, shift):
    ms = jnp.mean(x * x, axis=-1, keepdims=True)
    y = x * lax.rsqrt(ms + EPS) * gain
    return y * (1.0 + scale) + shift


def _ada_kernel(c_ref, w_ref, b_ref, o_ref):
    c = c_ref[...]
    s = (c * jax.nn.sigmoid(c)).astype(BF16)
    o_ref[...] = jnp.dot(s, w_ref[...].astype(BF16), preferred_element_type=F32) + b_ref[...]


def _ada_params(cvec, ada_w, ada_b):
    depth, d, n = ada_w.shape
    mp = cvec.shape[0]
    tn = 1024
    return pl.pallas_call(
        _ada_kernel,
        grid=(depth, n // tn),
        in_specs=[
            pl.BlockSpec((mp, d), lambda l, j: (0, 0)),
            pl.BlockSpec((None, d, tn), lambda l, j: (l, 0, j)),
            pl.BlockSpec((None, 1, tn), lambda l, j: (l, 0, j)),
        ],
        out_specs=pl.BlockSpec((None, mp, tn), lambda l, j: (l, 0, j)),
        out_shape=jax.ShapeDtypeStruct((depth, mp, n), F32),
        compiler_params=_cparams(("arbitrary", "arbitrary")),
        name="ada_params",
    )(cvec, ada_w, ada_b.reshape(depth, 1, n))


def _swap16(x):
    width = x.shape[-1]
    lane = lax.broadcasted_iota(jnp.int32, x.shape, 1)
    return jnp.where((lane % 32) < 16, pltpu.roll(x, width - 16, 1), pltpu.roll(x, 16, 1))


def _proj_even_kernel(xp_ref, xs_ref, mod_ref, g_ref, w_ref, cos_ref, sin_ref, o_ref, kv_ref, h_scr, *,
                      n_prompt_blocks):
    x = jnp.where(pl.program_id(0) < n_prompt_blocks, xp_ref[...], xs_ref[...])
    h_scr[...] = _rms_mod(x, g_ref[...], mod_ref[1:2, :], mod_ref[0:1, :]).astype(BF16)
    kv_tile = ATTN_WIDTH // TN_PROJ
    for j in range(w_ref.shape[1] // TN_PROJ):
        cols = slice(j * TN_PROJ, (j + 1) * TN_PROJ)
        acc = jnp.dot(h_scr[...], w_ref[:, cols], preferred_element_type=F32)
        if j <= kv_tile:
            tab = slice(0, TN_PROJ) if j < kv_tile else slice(TN_PROJ, 2 * TN_PROJ)
            o_ref[:, cols] = (acc * cos_ref[:, tab] + _swap16(acc) * sin_ref[:, tab]).astype(BF16)
            if j == kv_tile:
                kv_ref[...] = acc
        else:
            o_ref[:, cols] = jax.nn.gelu(acc).astype(BF16)


def _proj_even(x_p, x_s, mods, gain, w, cos_t, sin_t, blocks_per_seq):
    d = x_p.shape[1]
    n = w.shape[1]
    tm, tn = TM_PROJ, TN_PROJ
    npb = x_p.shape[0] // tm
    nsb = x_s.shape[0] // tm
    t = x_p.shape[0] + x_s.shape[0]

    def mod_idx(i):
        return (jnp.where(i < npb, 0, 1 + (i - npb) // blocks_per_seq), 0, 0)

    def tab_idx(i):
        return (jnp.where(i < npb, blocks_per_seq, (i - npb) % blocks_per_seq), 0)

    return pl.pallas_call(
        functools.partial(_proj_even_kernel, n_prompt_blocks=npb),
        grid=(npb + nsb,),
        in_specs=[
            pl.BlockSpec((tm, d), lambda i: (jnp.minimum(i, npb - 1), 0)),
            pl.BlockSpec((tm, d), lambda i: (jnp.maximum(i - npb, 0), 0)),
            pl.BlockSpec((None, 6, d), mod_idx),
            pl.BlockSpec((1, d), lambda i: (0, 0)),
            pl.BlockSpec((d, n), lambda i: (0, 0), pipeline_mode=pl.Buffered(1)),
            pl.BlockSpec((tm, 2 * tn), tab_idx),
            pl.BlockSpec((tm, 2 * tn), tab_idx),
        ],
        out_specs=[
            pl.BlockSpec((tm, n), lambda i: (i, 0)),
            pl.BlockSpec((tm, tn), lambda i: (i, 0)),
        ],
        out_shape=[jax.ShapeDtypeStruct((t, n), BF16), jax.ShapeDtypeStruct((t, tn), F32)],
        scratch_shapes=[pltpu.VMEM((tm, d), BF16)],
        compiler_params=_cparams(("arbitrary",)),
        name="proj_even",
    )(x_p, x_s, mods, gain, w, cos_t, sin_t)


def _rope_tables(dec_seq, tm):
    f32 = np.float32
    n = np.arange(dec_seq)
    row = (n // GRID_W).astype(f32)
    col = (n % GRID_W).astype(f32)
    nf = HEAD_DIM // 4
    inv_freq = np.power(f32(ROPE_BASE), -np.arange(nf, dtype=f32) / f32(nf)).astype(f32)
    ar = row[:, None] * inv_freq[None, :]
    ac = col[:, None] * inv_freq[None, :]
    cos_h = np.concatenate([np.cos(ar), np.cos(ar), np.cos(ac), np.cos(ac)], axis=-1).astype(f32)
    sin_h = np.concatenate([-np.sin(ar), np.sin(ar), -np.sin(ac), np.sin(ac)], axis=-1).astype(f32)
    scale = f32(HEAD_DIM ** -0.5)
    ones = np.ones((dec_seq, KV_WIDTH), f32)
    zeros = np.zeros((dec_seq, KV_WIDTH), f32)
    q_heads = TN_PROJ // HEAD_DIM
    cos_t = np.concatenate([np.tile(cos_h, (1, q_heads)) * scale, np.tile(cos_h, (1, N_KV_HEADS)), ones], axis=-1)
    sin_t = np.concatenate([np.tile(sin_h, (1, q_heads)) * scale, np.tile(sin_h, (1, N_KV_HEADS)), zeros], axis=-1)
    id_cos = np.concatenate([np.full((tm, TN_PROJ), scale, f32), np.ones((tm, TN_PROJ), f32)], axis=-1)
    id_sin = np.zeros((tm, 2 * TN_PROJ), f32)
    return (jnp.asarray(np.concatenate([cos_t, id_cos], axis=0), F32),
            jnp.asarray(np.concatenate([sin_t, id_sin], axis=0), F32))


def _cast_plan(weights, layer, n_steps, step_of):
    n_exp = weights[0].shape[1]
    steps_per_matrix = min(n_steps // len(weights), n_exp)
    assert steps_per_matrix >= 1 and n_exp % steps_per_matrix == 0
    e_blk = n_exp // steps_per_matrix

    def slice_of(k, idx):
        return jnp.clip(step_of(*idx) - k * steps_per_matrix, 0, steps_per_matrix - 1)

    in_specs = [pl.BlockSpec((None, e_blk) + w.shape[2:], lambda *idx, k=k: (layer, slice_of(k, idx), 0, 0))
                for k, w in enumerate(weights)]
    out_specs = [pl.BlockSpec((e_blk,) + w.shape[2:], lambda *idx, k=k: (slice_of(k, idx), 0, 0))
                 for k, w in enumerate(weights)]
    out_shapes = [jax.ShapeDtypeStruct(w.shape[1:], BF16) for w in weights]
    return steps_per_matrix, in_specs, out_specs, out_shapes


def _cast_slices(step, steps_per_matrix, srcs, dsts):
    for k, (src, dst) in enumerate(zip(srcs, dsts)):
        @pl.when((step >= k * steps_per_matrix) & (step < (k + 1) * steps_per_matrix))
        def _(src=src, dst=dst):
            dst[...] = src[...].astype(BF16)


def _sink_attend(q, keys, vals, masks, sink):
    scores = []
    for k, mask in zip(keys, masks):
        s = lax.dot_general(q, k, (((1,), (1,)), ((), ())), preferred_element_type=F32)
        if mask is not None:
            stacked = s.reshape(s.shape[0] // mask.shape[0], *mask.shape)
            s = jnp.where(mask[None], stacked, NEG_INF).reshape(s.shape)
        scores.append(s)
    m = sink
    for s in scores:
        m = jnp.maximum(m, jnp.max(s, axis=-1, keepdims=True))
    es = [jnp.exp(s - m) for s in scores]
    den = jnp.exp(sink - m)
    for e in es:
        den = den + jnp.sum(e, axis=-1, keepdims=True)
    out = None
    for e, v in zip(es, vals):
        o = jnp.dot(e.astype(BF16), v, preferred_element_type=F32)
        out = o if out is None else out + o
    return out * (1.0 / den)


def _grouped_heads_attend(sink_ref, q_ref, o_ref, kh, keys, vals, masks):
    rows = q_ref.shape[0]
    heads = [kh * Q_GROUPS + g for g in range(Q_GROUPS)]
    q = jnp.concatenate([q_ref[:, h * HEAD_DIM:(h + 1) * HEAD_DIM] for h in heads], axis=0)
    ridx = lax.broadcasted_iota(jnp.int32, (Q_GROUPS * rows, 1), 0)
    sink = jnp.full((Q_GROUPS * rows, 1), sink_ref[heads[-1]], F32)
    for g in range(Q_GROUPS - 2, -1, -1):
        sink = jnp.where(ridx < (g + 1) * rows, sink_ref[heads[g]], sink)
    o = _sink_attend(q, keys, vals, masks, sink)
    for g, h in enumerate(heads):
        o_ref[:, h * HEAD_DIM:(h + 1) * HEAD_DIM] = o[g * rows:(g + 1) * rows, :].astype(BF16)


def _ctx_attn_kernel(sink_ref, q_ref, kv_ref, o_ref):
    for kh in range(N_KV_HEADS):
        k = kv_ref[:, kh * HEAD_DIM:(kh + 1) * HEAD_DIM]
        v = kv_ref[:, KV_WIDTH + kh * HEAD_DIM:KV_WIDTH + (kh + 1) * HEAD_DIM]
        _grouped_heads_attend(sink_ref, q_ref, o_ref, kh, [k], [v], [None])


def _ctx_attention(proj, sink, batch, seq, kv_col_block):
    return pl.pallas_call(
        _ctx_attn_kernel,
        grid=(batch,),
        in_specs=[
            pl.BlockSpec(memory_space=pltpu.SMEM),
            pl.BlockSpec((seq, ATTN_WIDTH), lambda b: (b, 0)),
            pl.BlockSpec((seq, 2 * KV_WIDTH), lambda b: (b, kv_col_block)),
        ],
        out_specs=pl.BlockSpec((seq, ATTN_WIDTH), lambda b: (b, 0)),
        out_shape=jax.ShapeDtypeStruct((batch * seq, ATTN_WIDTH), BF16),
        compiler_params=_cparams(("arbitrary",)),
        name="ctx_attention",
    )(sink, proj, proj)


def _lat_attn_kernel(sink_ref, q_ref, kv_ref, ck_ref, cv_ref, wg_ref, wu_ref, wd_ref, o_ref, wg_out, wu_out, wd_out,
                     *, seq, steps_per_matrix):
    i = pl.program_id(1)
    _cast_slices(pl.program_id(0) * pl.num_programs(1) + i, steps_per_matrix,
                 (wg_ref, wu_ref, wd_ref), (wg_out, wu_out, wd_out))
    win = 3 * ATTN_BLK
    ws = pl.multiple_of(jnp.clip((i - 1) * ATTN_BLK, 0, seq - win), ATTN_BLK)
    shape = (ATTN_BLK, win)
    qpos = i * ATTN_BLK + lax.broadcasted_iota(jnp.int32, shape, 0)
    kpos = ws + lax.broadcasted_iota(jnp.int32, shape, 1)
    valid = jnp.abs(qpos - kpos) <= ATTN_BLK
    kvw = kv_ref[pl.ds(ws, win), :]
    for kh in range(N_KV_HEADS):
        k_loc = kvw[:, kh * HEAD_DIM:(kh + 1) * HEAD_DIM]
        v_loc = kvw[:, KV_WIDTH + kh * HEAD_DIM:KV_WIDTH + (kh + 1) * HEAD_DIM]
        k_ctx = ck_ref[:, kh * HEAD_DIM:(kh + 1) * HEAD_DIM]
        v_ctx = cv_ref[:, kh * HEAD_DIM:(kh + 1) * HEAD_DIM]
        _grouped_heads_attend(sink_ref, q_ref, o_ref, kh, [k_ctx, k_loc], [v_ctx, v_loc], [None, valid])


def _lat_attention(proj, sink, cache_k, cache_v, n_prompt, dec_batch, dec_seq, kv_col_block, weights, layer):
    qb = dec_seq // ATTN_BLK
    q_off = n_prompt // ATTN_BLK
    s_off = n_prompt // dec_seq
    past = cache_k.shape[1]
    spm, w_in, w_out, w_shapes = _cast_plan(weights, layer, dec_batch * qb, lambda b, i: b * qb + i)
    return pl.pallas_call(
        functools.partial(_lat_attn_kernel, seq=dec_seq, steps_per_matrix=spm),
        grid=(dec_batch, qb),
        in_specs=[
            pl.BlockSpec(memory_space=pltpu.SMEM),
            pl.BlockSpec((ATTN_BLK, ATTN_WIDTH), lambda b, i: (q_off + b * qb + i, 0)),
            pl.BlockSpec((dec_seq, 2 * KV_WIDTH), lambda b, i: (s_off + b, kv_col_block)),
            pl.BlockSpec((None, past, KV_WIDTH), lambda b, i: (b, 0, 0)),
            pl.BlockSpec((None, past, KV_WIDTH), lambda b, i: (b, 0, 0)),
        ] + w_in,
        out_specs=[pl.BlockSpec((ATTN_BLK, ATTN_WIDTH), lambda b, i: (b * qb + i, 0))] + w_out,
        out_shape=[jax.ShapeDtypeStruct((dec_batch * dec_seq, ATTN_WIDTH), BF16)] + w_shapes,
        compiler_params=_cparams(("arbitrary", "arbitrary")),
        name="lat_attention",
    )(sink, proj, proj, cache_k, cache_v, *weights)


def _sgu_kernel(ul_ref, uh_ref, vl_ref, vh_ref, g_ref, w_ref, bt_ref, z_ref):
    half = SGU_WIDTH // 2
    per_half = SGU_GROUPS // 2
    u_refs = (ul_ref, uh_ref)
    for c in range(TM_SGU // CHUNK):
        rows = slice(c * CHUNK, (c + 1) * CHUNK)
        vl = vl_ref[rows, :].astype(F32)
        vh = vh_ref[rows, :].astype(F32)
        ssq = jnp.sum(vl * vl, axis=-1, keepdims=True) + jnp.sum(vh * vh, axis=-1, keepdims=True)
        r = lax.rsqrt(ssq * (1.0 / SGU_WIDTH) + EPS)
        vn = ((vl * r * g_ref[:, :half]).astype(BF16), (vh * r * g_ref[:, half:]).astype(BF16))
        for g in range(SGU_GROUPS):
            src = g // per_half
            cols = slice((g % per_half) * SGU_GROUP_DIM, (g % per_half + 1) * SGU_GROUP_DIM)
            mixed = jnp.dot(w_ref[g], vn[src][:, cols], preferred_element_type=F32) + bt_ref[:, g:g + 1]
            z_ref[rows, g * SGU_GROUP_DIM:(g + 1) * SGU_GROUP_DIM] = (
                u_refs[src][rows, cols].astype(F32) * mixed).astype(BF16)


def _sgu(proj, norm_g, w_s, b_t):
    t = proj.shape[0]
    half = SGU_WIDTH // 2
    u0 = (ATTN_WIDTH + 2 * KV_WIDTH) // half
    blocks = [pl.BlockSpec((TM_SGU, half), functools.partial(lambda i, c: (i, c), c=u0 + k)) for k in range(4)]
    return pl.pallas_call(
        _sgu_kernel,
        grid=(t // TM_SGU,),
        in_specs=blocks + [
            pl.BlockSpec((1, SGU_WIDTH), lambda i: (0, 0)),
            pl.BlockSpec((SGU_GROUPS, CHUNK, CHUNK), lambda i: (0, 0, 0)),
            pl.BlockSpec((CHUNK, SGU_GROUPS), lambda i: (0, 0)),
        ],
        out_specs=pl.BlockSpec((TM_SGU, SGU_WIDTH), lambda i: (i, 0)),
        out_shape=jax.ShapeDtypeStruct((t, SGU_WIDTH), BF16),
        compiler_params=_cparams(("arbitrary",)),
        name="sgu",
    )(proj, proj, proj, proj, norm_g, w_s, b_t)


def _residual_router(rows, x, acc, mod_ref, g2_ref, wrt_ref, brt_ref, x1_ref, h2_ref, lg_ref):
    x1 = x + mod_ref[2:3, :] * acc
    x1_ref[rows, :] = x1
    h2 = _rms_mod(x1, g2_ref[...], mod_ref[4:5, :], mod_ref[3:4, :])
    h2_ref[rows, :] = _pack_bf16_pairs(h2)
    lg = lax.dot_general(wrt_ref[...], h2.astype(BF16), (((1,), (1,)), ((), ())), preferred_element_type=F32)
    lg_ref[:, rows] = lg + brt_ref[...]


def _sub_blocks(ref):
    return [slice(r, r + TM_OUT) for r in range(0, ref.shape[0], TM_OUT)]


def _out_even_kernel(xp_ref, xs_ref, ap_ref, as_ref, z_ref, w_ref, mod_ref, g2_ref, wrt_ref, brt_ref,
                     x1_ref, h2_ref, lg_ref, *, n_prompt_blocks):
    is_prompt = pl.program_id(0) < n_prompt_blocks
    for rows in _sub_blocks(x1_ref):
        a = jnp.where(is_prompt, ap_ref[rows, :].astype(F32), as_ref[rows, :].astype(F32)).astype(BF16)
        acc = jnp.dot(a, w_ref[0:ATTN_WIDTH, :], preferred_element_type=F32)
        acc = acc + jnp.dot(z_ref[rows, :], w_ref[ATTN_WIDTH:, :], preferred_element_type=F32)
        x = jnp.where(is_prompt, xp_ref[rows, :], xs_ref[rows, :])
        _residual_router(rows, x, acc, mod_ref, g2_ref, wrt_ref, brt_ref, x1_ref, h2_ref, lg_ref)


def _out_odd_kernel(x_ref, hc_ref, w_ref, mod_ref, g2_ref, wrt_ref, brt_ref, x1_ref, h2_ref, lg_ref):
    for rows in _sub_blocks(x1_ref):
        acc = jnp.dot(hc_ref[rows, :], w_ref[...], preferred_element_type=F32)
        _residual_router(rows, x_ref[rows, :], acc, mod_ref, g2_ref, wrt_ref, brt_ref, x1_ref, h2_ref, lg_ref)


def _out_proj(xs, lhs, w, mods, gain2, wrt, brt, n_prompt, dec_seq):
    d = xs[0].shape[1]
    t = sum(x.shape[0] for x in xs)
    tm = TM_OUTPROJ
    npb = n_prompt // tm
    bps = dec_seq // tm
    nsb = t // tm - npb
    if len(xs) == 2:
        x_specs = [pl.BlockSpec((tm, d), lambda i: (jnp.minimum(i, npb - 1), 0)),
                   pl.BlockSpec((tm, d), lambda i: (jnp.maximum(i - npb, 0), 0))]
    else:
        x_specs = [pl.BlockSpec((tm, d), lambda i: (i, 0))]

    def mod_idx(i):
        return (jnp.where(i < npb, 0, 1 + (i - npb) // bps), 0, 0)

    if len(lhs) == 3:
        a_p, a_s, z = lhs
        body = functools.partial(_out_even_kernel, n_prompt_blocks=npb)
        lhs_specs = [
            pl.BlockSpec((tm, ATTN_WIDTH), lambda i: (jnp.minimum(i, npb - 1), 0)),
            pl.BlockSpec((tm, ATTN_WIDTH), lambda i: (jnp.clip(i - npb, 0, nsb - 1), 0)),
            pl.BlockSpec((tm, SGU_WIDTH), lambda i: (i, 0)),
        ]
    else:
        body = _out_odd_kernel
        lhs_specs = [pl.BlockSpec((tm, d), lambda i: (i, 0))]
    return pl.pallas_call(
        body,
        grid=(t // tm,),
        in_specs=x_specs + lhs_specs + [
            pl.BlockSpec((w.shape[0], d), lambda i: (0, 0), pipeline_mode=pl.Buffered(1)),
            pl.BlockSpec((None, 6, d), mod_idx),
            pl.BlockSpec((1, d), lambda i: (0, 0)),
            pl.BlockSpec((ROUTER_LANES, d), lambda i: (0, 0)),
            pl.BlockSpec((ROUTER_LANES, 1), lambda i: (0, 0)),
        ],
        out_specs=[
            pl.BlockSpec((tm, d), lambda i: (i, 0)),
            pl.BlockSpec((tm, d // 2), lambda i: (i, 0)),
            pl.BlockSpec((ROUTER_LANES, tm), lambda i: (0, i)),
        ],
        out_shape=[
            jax.ShapeDtypeStruct((t, d), F32),
            jax.ShapeDtypeStruct((t, d // 2), jnp.uint32),
            jax.ShapeDtypeStruct((ROUTER_LANES, t), F32),
        ],
        compiler_params=_cparams(("arbitrary",)),
        name="out_proj",
    )(*xs, *lhs, w, mods, gain2, wrt, brt)


def _first_argmax4(v0, v1, v2, v3):
    m = jnp.maximum(jnp.maximum(v0, v1), jnp.maximum(v2, v3))
    idx = jnp.where(v0 == m, 0.0, jnp.where(v1 == m, 1.0, jnp.where(v2 == m, 2.0, 3.0)))
    return m, idx


def _route_kernel(lg_ref, pos_ref, ea_ref, eb_ref, nv_ref, ends_ref, oh_scr, rank_scr, *, n_tokens):
    lg = lg_ref[...]
    rows = [lg[r:r + 1, :] for r in range(N_GROUPS_MOE + N_EXPERTS)]
    _, gidx = _first_argmax4(*rows[:N_GROUPS_MOE])
    e = []
    for k in range(EXPERTS_PER_GROUP):
        cand = [rows[N_GROUPS_MOE + g * EXPERTS_PER_GROUP + k] for g in range(N_GROUPS_MOE)]
        e.append(jnp.where(gidx == 0.0, cand[0], jnp.where(gidx == 1.0, cand[1], jnp.where(gidx == 2.0, cand[2], cand[3]))))
    _, l1 = _first_argmax4(*e)
    e2 = [jnp.where(l1 == float(k), -jnp.inf, e[k]) for k in range(EXPERTS_PER_GROUP)]
    _, l2 = _first_argmax4(*e2)
    lo = jnp.minimum(l1, l2)
    hi = jnp.maximum(l1, l2)
    pair = jnp.where(lo == 0.0, hi - 1.0, jnp.where(lo == 2.0, 4.0, jnp.where(hi == 3.0, 3.0, 5.0)))
    cls = gidx * float(PAIRS_PER_GROUP) + pair

    crow = lax.broadcasted_iota(jnp.int32, (CLASS_ROWS, n_tokens), 0).astype(F32)
    oh_scr[...] = jnp.where(crow == cls, 1.0, 0.0).astype(BF16)

    jj = lax.broadcasted_iota(jnp.int32, (CUM_CHUNK, CUM_CHUNK), 0)
    ii = lax.broadcasted_iota(jnp.int32, (CUM_CHUNK, CUM_CHUNK), 1)
    tri = jnp.where(jj < ii, 1.0, 0.0).astype(BF16)

    def chunk(c, carry):
        off = pl.multiple_of(c * CUM_CHUNK, CUM_CHUNK)
        oh = oh_scr[:, pl.ds(off, CUM_CHUNK)]
        ohf = oh.astype(F32)
        cum = jnp.dot(oh, tri, preferred_element_type=F32) + carry
        rank_scr[:, pl.ds(off, CUM_CHUNK)] = jnp.sum(cum * ohf, axis=0, keepdims=True)
        return carry + jnp.sum(ohf, axis=1, keepdims=True)

    counts = lax.fori_loop(0, n_tokens // CUM_CHUNK, chunk, jnp.zeros((CLASS_ROWS, 1), F32))
    padded = jnp.floor((counts + float(MOE_BLK - 1)) * (1.0 / MOE_BLK)) * float(MOE_BLK)
    run = jnp.zeros((1, 1), F32)
    starts_rows = []
    for c in range(CLASS_ROWS):
        starts_rows.append(run)
        run = run + padded[c:c + 1, :]
    starts = jnp.concatenate(starts_rows, axis=0)
    ends = starts + padded
    total = run

    ohf = oh_scr[...].astype(F32)
    pos = rank_scr[...] + jnp.sum(ohf * starts, axis=0, keepdims=True)
    pos_ref[...] = pos.astype(jnp.int32)

    bstart = lax.broadcasted_iota(jnp.int32, (CLASS_ROWS, MAX_MOE_BLOCKS), 1).astype(F32) * float(MOE_BLK)
    brow = lax.broadcasted_iota(jnp.int32, (CLASS_ROWS, MAX_MOE_BLOCKS), 0)
    done = jnp.where((ends <= bstart) & (brow < N_CLASSES), 1.0, 0.0)
    bcls = jnp.minimum(jnp.sum(done, axis=0, keepdims=True), float(N_CLASSES - 1))
    grp = (jnp.where(bcls >= 6.0, 1.0, 0.0) + jnp.where(bcls >= 12.0, 1.0, 0.0) + jnp.where(bcls >= 18.0, 1.0, 0.0))
    bp = bcls - grp * float(PAIRS_PER_GROUP)
    first = jnp.where(bp >= 3.0, 1.0, 0.0) + jnp.where(bp >= 4.0, 1.0, 0.0)
    second = jnp.where(bp < 3.0, bp + 1.0, jnp.where(bp < 5.0, 3.0, 1.0))
    ea_ref[...] = (grp * float(EXPERTS_PER_GROUP) + first).astype(jnp.int32)
    eb_ref[...] = (grp * float(EXPERTS_PER_GROUP) + second).astype(jnp.int32)
    own = jnp.where(brow.astype(F32) == bcls, starts + counts, 0.0)
    filled = jnp.sum(own, axis=0, keepdims=True) - bstart[0:1, :]
    nv_ref[...] = jnp.clip(filled, 0.0, float(MOE_BLK)).astype(jnp.int32)
    lane = lax.broadcasted_iota(jnp.int32, (1, MAX_MOE_BLOCKS), 1)
    ends_row = jnp.zeros((1, MAX_MOE_BLOCKS), F32)
    for c in range(N_CLASSES):
        ends_row = jnp.where(lane == c, starts_rows[c] + padded[c:c + 1, :], ends_row)
    ends_ref[...] = ends_row.astype(jnp.int32)


def _route(logits_t):
    t = logits_t.shape[1]
    rows = N_CLASSES
    i32 = jnp.int32
    return pl.pallas_call(
        functools.partial(_route_kernel, n_tokens=t),
        grid=(1,),
        in_specs=[pl.BlockSpec((rows, t), lambda i: (0, 0))],
        out_specs=[
            pl.BlockSpec((1, t), lambda i: (0, 0)),
        ] + [pl.BlockSpec((1, MAX_MOE_BLOCKS), lambda i: (0, 0))] * 4,
        out_shape=[jax.ShapeDtypeStruct((1, t), i32)] + [jax.ShapeDtypeStruct((1, MAX_MOE_BLOCKS), i32)] * 4,
        scratch_shapes=[pltpu.VMEM((CLASS_ROWS, t), BF16), pltpu.VMEM((1, t), F32)],
        compiler_params=_cparams(("arbitrary",)),
        name="route",
    )(logits_t)


DMA_THREADS = 2


def _row_copy(src_ref, src_row, dst_ref, dst_row, sem):
    return pltpu.make_async_copy(src_ref.at[pl.ds(src_row, 1), :], dst_ref.at[pl.ds(dst_row, 1), :], sem)


def _start_row_copies(n_rows, make_copy):
    for r in range(n_rows):
        make_copy(r).start(priority=r % DMA_THREADS)


def _dispatch_kernel(pos_ref, ends_ref, h_ref, o_ref, zbuf, sem, zsem):
    step = pl.program_id(0)

    @pl.when(step == 0)
    def _():
        zbuf[...] = jnp.zeros_like(zbuf)
        total = ends_ref[N_CLASSES - 1]
        n_free = (o_ref.shape[0] - total) // MOE_BLK

        def zero_block(first_row, phase):
            first = pl.multiple_of(first_row, MOE_BLK)
            getattr(pltpu.make_async_copy(zbuf, o_ref.at[pl.ds(first, MOE_BLK), :], zsem), phase)()

        for phase in ("start", "wait"):
            prev = 0
            for c in range(N_CLASSES):
                end = ends_ref[c]
                pl.when(end > prev)(functools.partial(zero_block, end - MOE_BLK, phase))
                prev = end

            def free_block(k, carry, phase=phase):
                zero_block(total + k * MOE_BLK, phase)
                return carry

            lax.fori_loop(0, n_free, free_block, 0)

    rows = h_ref.shape[0]
    base = step * rows

    _start_row_copies(rows, lambda r: _row_copy(h_ref, r, o_ref, pos_ref[base + r], sem))
    pltpu.make_async_copy(h_ref, o_ref.at[pl.ds(0, rows), :], sem).wait()


def _dispatch(pos, ends, h2, n_rows):
    t, d = h2.shape
    return pl.pallas_call(
        _dispatch_kernel,
        grid_spec=pltpu.PrefetchScalarGridSpec(
            num_scalar_prefetch=2,
            grid=(t // TM_DISPATCH,),
            in_specs=[pl.BlockSpec((TM_DISPATCH, d), lambda i, pos, ends: (i, 0))],
            out_specs=pl.BlockSpec(memory_space=pl.ANY),
            scratch_shapes=[pltpu.VMEM((MOE_BLK, d), h2.dtype), pltpu.SemaphoreType.DMA(()),
                            pltpu.SemaphoreType.DMA(())],
        ),
        out_shape=jax.ShapeDtypeStruct((n_rows, d), h2.dtype),
        compiler_params=_cparams(("arbitrary",)),
        name="moe_dispatch",
    )(pos, ends, h2)


def _expert_ffn(x, wg_ref, wu_ref, wd_ref):
    gate = jnp.dot(x, wg_ref[...], preferred_element_type=F32)
    up = jnp.dot(x, wu_ref[...], preferred_element_type=F32)
    hidden = (gate * jax.nn.sigmoid(gate) * up).astype(BF16)
    return jnp.dot(hidden, wd_ref[...], preferred_element_type=F32)


def _moe_kernel(ea_ref, eb_ref, nv_ref, x_ref, wr_ref, br_ref, wga, wua, wda, wgb, wub, wdb, y_ref):
    b = pl.program_id(0)
    n_valid = nv_ref[b]

    def run(rows):
        x = _unpack_bf16_pairs(x_ref[rows, :])
        lg = jnp.dot(x, wr_ref[...], preferred_element_type=F32) + br_ref[...]
        lane = lax.broadcasted_iota(jnp.int32, lg.shape, 1)
        ea = ea_ref[b]
        eb = eb_ref[b]
        grp = ea // EXPERTS_PER_GROUP
        is_grp = lane < N_GROUPS_MOE
        gm = jnp.max(jnp.where(is_grp, lg, -jnp.inf), axis=-1, keepdims=True)
        ge = jnp.where(is_grp, jnp.exp(lg - gm), 0.0)
        p_g = jnp.sum(jnp.where(lane == grp, ge, 0.0), axis=-1, keepdims=True) / jnp.sum(ge, axis=-1, keepdims=True)
        la = jnp.sum(jnp.where(lane == N_GROUPS_MOE + ea, lg, 0.0), axis=-1, keepdims=True)
        lb = jnp.sum(jnp.where(lane == N_GROUPS_MOE + eb, lg, 0.0), axis=-1, keepdims=True)
        mm = jnp.maximum(la, lb)
        xa = jnp.exp(la - mm)
        xb = jnp.exp(lb - mm)
        ga = xa / (xa + xb) * p_g
        gb = xb / (xa + xb) * p_g
        ya = _expert_ffn(x, wga, wua, wda)
        yb = _expert_ffn(x, wgb, wub, wdb)
        y_ref[rows, :] = ya * ga + yb * gb

    half = MOE_BLK // 2

    @pl.when(n_valid > half)
    def _():
        run(slice(0, MOE_BLK))

    @pl.when((n_valid > 0) & (n_valid <= half))
    def _():
        run(slice(0, half))
        y_ref[half:, :] = jnp.zeros((MOE_BLK - half, y_ref.shape[1]), y_ref.dtype)

    @pl.when(n_valid == 0)
    def _():
        y_ref[...] = jnp.zeros_like(y_ref)


def _moe_experts(ea, eb, n_valid, hs, wr, br, w_gate, w_up, w_down, n_blocks):
    d = w_gate.shape[1]
    de = w_gate.shape[2]

    def wa_idx(b, ea, eb, nv):
        return (ea[b], 0, 0)

    def wb_idx(b, ea, eb, nv):
        return (eb[b], 0, 0)

    return pl.pallas_call(
        _moe_kernel,
        grid_spec=pltpu.PrefetchScalarGridSpec(
            num_scalar_prefetch=3,
            grid=(n_blocks,),
            in_specs=[
                pl.BlockSpec((MOE_BLK, hs.shape[1]), lambda b, ea, eb, nv: (jnp.where(nv[b] > 0, b, 0), 0)),
                pl.BlockSpec((d, ROUTER_LANES), lambda b, ea, eb, nv: (0, 0)),
                pl.BlockSpec((1, ROUTER_LANES), lambda b, ea, eb, nv: (0, 0)),
                pl.BlockSpec((None, d, de), wa_idx),
                pl.BlockSpec((None, d, de), wa_idx),
                pl.BlockSpec((None, de, d), wa_idx),
                pl.BlockSpec((None, d, de), wb_idx),
                pl.BlockSpec((None, d, de), wb_idx),
                pl.BlockSpec((None, de, d), wb_idx),
            ],
            out_specs=pl.BlockSpec((MOE_BLK, d), lambda b, ea, eb, nv: (b, 0)),
        ),
        out_shape=jax.ShapeDtypeStruct((n_blocks * MOE_BLK, d), F32),
        compiler_params=_cparams(("arbitrary",)),
        name="moe_experts",
    )(ea, eb, n_valid, hs, wr, br, w_gate, w_up, w_down, w_gate, w_up, w_down)


def _combine_kernel(pos_ref, ys_ref, x1_ref, mod_ref, gf_ref, o_ref, ybuf, sem, *, block_off, n_steps, final):
    i = pl.program_id(0)
    slot = i % 2

    def start_gather(step, into):
        base = (step + block_off) * TM_OUT
        _start_row_copies(
            TM_OUT, lambda r: _row_copy(ys_ref, pos_ref[base + r], ybuf.at[into], r, sem.at[into]))

    pl.when(i == 0)(lambda: start_gather(0, 0))
    pl.when(i + 1 < n_steps)(lambda: start_gather(i + 1, 1 - slot))
    pltpu.make_async_copy(ys_ref.at[pl.ds(0, TM_OUT), :], ybuf.at[slot], sem.at[slot]).wait()
    x2 = x1_ref[...] + mod_ref[5:6, :] * ybuf[slot]
    if final:
        ms = jnp.mean(x2 * x2, axis=-1, keepdims=True)
        x2 = x2 * lax.rsqrt(ms + EPS) * gf_ref[...]
    o_ref[...] = x2


def _combine(pos, ys, x1, mods, gain_f, n_prompt, dec_seq, row_off, n_rows, final):
    d = x1.shape[1]
    tm = TM_OUT
    npb = n_prompt // tm
    bps = dec_seq // tm
    boff = row_off // tm

    def mod_idx(i, pos):
        blk = i + boff
        return (jnp.where(blk < npb, 0, 1 + (blk - npb) // bps), 0, 0)

    return pl.pallas_call(
        functools.partial(_combine_kernel, block_off=boff, n_steps=n_rows // tm, final=final),
        grid_spec=pltpu.PrefetchScalarGridSpec(
            num_scalar_prefetch=1,
            grid=(n_rows // tm,),
            in_specs=[
                pl.BlockSpec(memory_space=pl.ANY),
                pl.BlockSpec((tm, d), lambda i, pos: (i + boff, 0)),
                pl.BlockSpec((None, 6, d), mod_idx),
                pl.BlockSpec((1, d), lambda i, pos: (0, 0)),
            ],
            out_specs=pl.BlockSpec((tm, d), lambda i, pos: (i, 0)),
            scratch_shapes=[pltpu.VMEM((2, tm, d), F32), pltpu.SemaphoreType.DMA((2,))],
        ),
        out_shape=jax.ShapeDtypeStruct((n_rows, d), F32),
        compiler_params=_cparams(("arbitrary",)),
        name="moe_combine",
    )(pos, ys, x1, mods, gain_f)


def _proj_glu_kernel(pos_ref, ys_ref, x1_ref, mod_prev_ref, mod_ref, g_ref, w_ref, o_ref, x2_ref, h_scr, ybuf, sem,
                     *, n_steps):
    i = pl.program_id(0)
    slot = i % 2
    tm = x1_ref.shape[0]

    def start_gather(step, into):
        base = step * tm
        _start_row_copies(tm, lambda r: _row_copy(ys_ref, pos_ref[base + r], ybuf.at[into], r, sem.at[into]))

    pl.when(i == 0)(lambda: start_gather(0, 0))
    pl.when(i + 1 < n_steps)(lambda: start_gather(i + 1, 1 - slot))
    pltpu.make_async_copy(ys_ref.at[pl.ds(0, tm), :], ybuf.at[slot], sem.at[slot]).wait()
    x2 = x1_ref[...] + mod_prev_ref[5:6, :] * ybuf[slot]
    x2_ref[...] = x2
    h_scr[...] = _rms_mod(x2, g_ref[...], mod_ref[1:2, :], mod_ref[0:1, :]).astype(BF16)
    width = w_ref.shape[1] // 2
    for j in range(width // TN_PROJ):
        cols = slice(j * TN_PROJ, (j + 1) * TN_PROJ)
        gcols = slice(width + j * TN_PROJ, width + (j + 1) * TN_PROJ)
        val = jnp.dot(h_scr[...], w_ref[:, cols], preferred_element_type=F32)
        gate = jnp.dot(h_scr[...], w_ref[:, gcols], preferred_element_type=F32)
        o_ref[:, cols] = (val * jax.nn.sigmoid(gate)).astype(BF16)


def _proj_glu(pos, ys, x1, mods_prev, mods, gain, w, n_prompt, dec_seq):
    t, d = x1.shape
    n = w.shape[1]
    tm = TM_PROJ
    npb = n_prompt // tm
    bps = dec_seq // tm

    def mod_idx(i, pos):
        return (jnp.where(i < npb, 0, 1 + (i - npb) // bps), 0, 0)

    return pl.pallas_call(
        functools.partial(_proj_glu_kernel, n_steps=t // tm),
        grid_spec=pltpu.PrefetchScalarGridSpec(
            num_scalar_prefetch=1,
            grid=(t // tm,),
            in_specs=[
                pl.BlockSpec(memory_space=pl.ANY),
                pl.BlockSpec((tm, d), lambda i, pos: (i, 0)),
                pl.BlockSpec((None, 6, d), mod_idx),
                pl.BlockSpec((None, 6, d), mod_idx),
                pl.BlockSpec((1, d), lambda i, pos: (0, 0)),
                pl.BlockSpec((d, n), lambda i, pos: (0, 0), pipeline_mode=pl.Buffered(1)),
            ],
            out_specs=[
                pl.BlockSpec((tm, n // 2), lambda i, pos: (i, 0)),
                pl.BlockSpec((tm, d), lambda i, pos: (i, 0)),
            ],
            scratch_shapes=[pltpu.VMEM((tm, d), BF16), pltpu.VMEM((2, tm, d), F32), pltpu.SemaphoreType.DMA((2,))],
        ),
        out_shape=[jax.ShapeDtypeStruct((t, n // 2), BF16), jax.ShapeDtypeStruct((t, d), F32)],
        compiler_params=_cparams(("arbitrary",)),
        name="proj_glu",
    )(pos, ys, x1, mods_prev, mods, gain, w)


CONV_ROWS = 64
CONV_COLS = 256
F32_SUBLANES = 8


def _conv_kernel(x_ref, prev_ref, next_ref, dw_ref, b_ref, g_ref, wg_ref, wu_ref, wd_ref, o_ref, wg_out, wu_out, wd_out,
                 xpad, shifted, ybuf, *, n_prompt_blocks, blocks_per_seq, steps_per_matrix):
    i = pl.program_id(0)
    _cast_slices(i, steps_per_matrix, (wg_ref, wu_ref, wd_ref), (wg_out, wu_out, wd_out))
    in_sample = i >= n_prompt_blocks
    j = i - n_prompt_blocks
    is_start = jnp.logical_or(jnp.logical_not(in_sample), j % blocks_per_seq == 0)
    is_end = jnp.logical_or(jnp.logical_not(in_sample), j % blocks_per_seq == blocks_per_seq - 1)
    tm = TM_OUT
    xpad[0:CONV_HALO, :] = jnp.where(is_start, 0.0, prev_ref[...].astype(F32))
    xpad[CONV_HALO:CONV_HALO + tm, :] = x_ref[...].astype(F32)
    xpad[CONV_HALO + tm:, :] = jnp.where(is_end, 0.0, next_ref[...].astype(F32))
    first = CONV_HALO - CONV_K // 2
    span = shifted.shape[1]

    def col_chunk(c, ssq):
        c0 = pl.multiple_of(c * CONV_COLS, CONV_COLS)
        cols = pl.ds(c0, CONV_COLS)
        w = dw_ref[:, cols]
        bias = b_ref[:, cols]
        for s in range(1, F32_SUBLANES):
            shifted[s - 1] = xpad[s:s + span, cols]
        parts = []
        for rc in range(tm // CONV_ROWS):
            r0 = rc * CONV_ROWS
            acc = jnp.zeros((CONV_ROWS, CONV_COLS), F32)
            for k in range(CONV_K):
                whole, s = divmod(first + k, F32_SUBLANES)
                base = whole * F32_SUBLANES + r0
                if s == 0:
                    src = xpad[base:base + CONV_ROWS, cols]
                else:
                    src = shifted[s - 1, base:base + CONV_ROWS, :]
                acc = acc + w[k:k + 1, :] * src
            acc = acc + bias
            ybuf[r0:r0 + CONV_ROWS, cols] = acc
            parts.append(jnp.sum(acc * acc, axis=-1, keepdims=True))
        return ssq + jnp.concatenate(parts, axis=0)

    d = x_ref.shape[1]
    ssq = lax.fori_loop(0, d // CONV_COLS, col_chunk, jnp.zeros((tm, 1), F32))
    y = ybuf[...] * lax.rsqrt(ssq * (1.0 / d) + EPS) * g_ref[...]
    o_ref[...] = (y * jax.nn.sigmoid(y)).astype(BF16)


def _conv_mix(xg, dw, dw_b, norm_g, n_prompt, seq, dec_seq, weights, layer):
    t, d = xg.shape
    tm = TM_OUT
    assert seq == tm, "context sequences must be exactly one conv row block"
    npb = n_prompt // tm
    bps = dec_seq // tm
    hb = tm // CONV_HALO
    last = t // CONV_HALO - 1
    spm, w_in, w_out, w_shapes = _cast_plan(weights, layer, t // tm, lambda i: i)
    return pl.pallas_call(
        functools.partial(_conv_kernel, n_prompt_blocks=npb, blocks_per_seq=bps, steps_per_matrix=spm),
        grid=(t // tm,),
        in_specs=[
            pl.BlockSpec((tm, d), lambda i: (i, 0)),
            pl.BlockSpec((CONV_HALO, d), lambda i: (jnp.maximum(i * hb - 1, 0), 0)),
            pl.BlockSpec((CONV_HALO, d), lambda i: (jnp.minimum((i + 1) * hb, last), 0)),
            pl.BlockSpec((dw.shape[0], d), lambda i: (0, 0)),
            pl.BlockSpec((1, d), lambda i: (0, 0)),
            pl.BlockSpec((1, d), lambda i: (0, 0)),
        ] + w_in,
        out_specs=[pl.BlockSpec((tm, d), lambda i: (i, 0))] + w_out,
        out_shape=[jax.ShapeDtypeStruct((t, d), BF16)] + w_shapes,
        scratch_shapes=[
            pltpu.VMEM((tm + 2 * CONV_HALO, d), F32),
            pltpu.VMEM((F32_SUBLANES - 1, tm + 2 * CONV_HALO - F32_SUBLANES, CONV_COLS), F32),
            pltpu.VMEM((tm, d), F32),
        ],
        compiler_params=_cparams(("arbitrary",)),
        name="conv_mix",
    )(xg, xg, xg, dw, dw_b, norm_g, *weights)


def _router_operands(w_rg, b_rg, w_re, b_re):
    d = w_rg.shape[0]
    used = N_GROUPS_MOE + N_EXPERTS
    w = jnp.concatenate([w_rg, w_re, jnp.zeros((d, ROUTER_LANES - used), F32)], axis=1)
    b = jnp.concatenate([b_rg, b_re, jnp.zeros((ROUTER_LANES - used,), F32)])
    return w.astype(BF16), b.reshape(1, ROUTER_LANES), w.T.astype(BF16), b.reshape(ROUTER_LANES, 1)


def _moe_rows(h2, logits_t, wr, br, wg, wu, wd):
    t = h2.shape[0]
    n_blocks = -(-(t + N_CLASSES * (MOE_BLK - 1)) // MOE_BLK)
    assert n_blocks <= MAX_MOE_BLOCKS
    pos, ea, eb, n_valid, ends = _route(logits_t)
    pos = pos.reshape(t)
    hs = _dispatch(pos, ends.reshape(-1), h2, n_blocks * MOE_BLK)
    ys = _moe_experts(ea.reshape(-1), eb.reshape(-1), n_valid.reshape(-1), hs, wr, br, wg, wu, wd, n_blocks)
    return pos, ys


def kernel(x_prompt, x_sample, cache_k, cache_v, c, c_ctx, w_in_even, attn_sink, sgu_norm, sgu_w, sgu_b,
           w_out_even, conv_w_in, conv_dw, conv_dw_b, conv_norm, conv_w_out, ada_w, ada_b, norm_mix, norm_ffn,
           router_group_w, router_group_b, router_expert_w, router_expert_b, expert_w_gate, expert_w_up,
           expert_w_down, final_norm):
    batch, seq, d = x_prompt.shape
    dec_batch, dec_seq, _ = x_sample.shape
    depth = ada_w.shape[0]
    n_prompt = batch * seq
    n_sample = dec_batch * dec_seq
    assert d == D_MODEL and depth == 2 and n_prompt % dec_seq == 0 and dec_seq % TM_PROJ == 0
    x_p = x_prompt.reshape(n_prompt, d)
    x_s = x_sample.reshape(n_sample, d)

    n_mod = -(-(1 + dec_batch) // 8) * 8
    cvec = jnp.concatenate([c_ctx[None, :], c, jnp.zeros((n_mod - 1 - dec_batch, d), F32)], axis=0)
    mods = _ada_params(cvec, ada_w, ada_b).reshape(depth, n_mod, 6, d)
    expert_w = (expert_w_gate, expert_w_up, expert_w_down)

    cos_t, sin_t = _rope_tables(dec_seq, TM_PROJ)
    proj, kv32 = _proj_even(x_p, x_s, mods[0], norm_mix[0][None, :], w_in_even[0].astype(BF16), cos_t, sin_t,
                            dec_seq // TM_PROJ)
    kv_col_block = ATTN_WIDTH // (2 * KV_WIDTH)
    sink = attn_sink[0]
    a_p = _ctx_attention(proj, sink, batch, seq, kv_col_block)
    past = cache_k.shape[2]
    ck = cache_k[:, 0].reshape(dec_batch, past, KV_WIDTH).astype(BF16)
    cv = cache_v[:, 0].reshape(dec_batch, past, KV_WIDTH).astype(BF16)
    a_s, *experts0 = _lat_attention(proj, sink, ck, cv, n_prompt, dec_batch, dec_seq, kv_col_block, expert_w, 0)
    z = _sgu(proj, sgu_norm[0][None, :], sgu_w[0].astype(BF16), sgu_b[0].T)

    wr0, br0, wrt0, brt0 = _router_operands(router_group_w[0], router_group_b[0], router_expert_w[0], router_expert_b[0])
    x1, h2, lg = _out_proj((x_p, x_s), (a_p, a_s, z), w_out_even[0].astype(BF16), mods[0], norm_ffn[0][None, :],
                           wrt0, brt0, n_prompt, dec_seq)
    pos, ys = _moe_rows(h2, lg, wr0, br0, *experts0)
    gain_f = final_norm[None, :]

    xg, x = _proj_glu(pos, ys, x1, mods[0], mods[1], norm_mix[1][None, :], conv_w_in[0].astype(BF16),
                      n_prompt, dec_seq)
    dw = jnp.concatenate([conv_dw[0], jnp.zeros((1, d), F32)], axis=0)
    hc, *experts1 = _conv_mix(xg, dw, conv_dw_b[0][None, :], conv_norm[0][None, :], n_prompt, seq, dec_seq,
                              expert_w, 1)
    wr1, br1, wrt1, brt1 = _router_operands(router_group_w[1], router_group_b[1], router_expert_w[1], router_expert_b[1])
    x1, h2, lg = _out_proj((x,), (hc,), conv_w_out[0].astype(BF16), mods[1], norm_ffn[1][None, :],
                           wrt1, brt1, n_prompt, dec_seq)
    pos, ys = _moe_rows(h2, lg, wr1, br1, *experts1)
    y_prompt = _combine(pos, ys, x1, mods[1], gain_f, n_prompt, dec_seq, 0, n_prompt, final=True)
    y_sample = _combine(pos, ys, x1, mods[1], gain_f, n_prompt, dec_seq, n_prompt, n_sample, final=True)

    new_k = kv32[:n_prompt, :KV_WIDTH].reshape(batch, 1, seq, N_KV_HEADS, HEAD_DIM)
    new_v = kv32[:n_prompt, KV_WIDTH:].reshape(batch, 1, seq, N_KV_HEADS, HEAD_DIM)
    return (y_prompt.reshape(batch, seq, d), y_sample.reshape(dec_batch, dec_seq, d), new_k, new_v)
```

```python
import functools

import jax
import jax.numpy as jnp
import numpy as np
from jax import lax
from jax.experimental import pallas as pl
from jax.experimental.pallas import tpu as pltpu

F32 = jnp.float32
BF16 = jnp.bfloat16

D_MODEL = 2048
GRID_W = 64
ATTN_BLK = 128
N_HEADS = 16
N_KV_HEADS = 4
Q_GROUPS = N_HEADS // N_KV_HEADS
HEAD_DIM = 64
ATTN_WIDTH = N_HEADS * HEAD_DIM
KV_WIDTH = N_KV_HEADS * HEAD_DIM
ROPE_BASE = 10000.0
SGU_GROUPS = 4
SGU_WIDTH = D_MODEL // 2
SGU_GROUP_DIM = SGU_WIDTH // SGU_GROUPS
CHUNK = 128
IN_WIDTH_EVEN = ATTN_WIDTH + 2 * KV_WIDTH + 2 * SGU_WIDTH
CONV_K = 31
CONV_HALO = 16
N_GROUPS_MOE = 4
EXPERTS_PER_GROUP = 4
N_EXPERTS = N_GROUPS_MOE * EXPERTS_PER_GROUP
PAIRS_PER_GROUP = 6
N_CLASSES = N_GROUPS_MOE * PAIRS_PER_GROUP
D_EXPERT = 512
MOE_BLK = 256
EPS = 1e-6
NEG_INF = -1e30

LANES = 128
ROUTER_LANES = 128
CLASS_ROWS = 32
MAX_MOE_BLOCKS = 128
TM_PROJ = 512
TN_PROJ = 512
TM_OUT = 256
TM_DISPATCH = 1024
TM_COMBINE = 512
TM_OUTPROJ = 512
TM_SGU = 512
CUM_CHUNK = 256
VMEM_LIMIT = 56 * 1024 * 1024


def _cparams(sem):
    return pltpu.CompilerParams(dimension_semantics=sem, vmem_limit_bytes=VMEM_LIMIT)


def _pack_bf16_pairs(h):
    w = h.shape[1] // 2
    lo = pltpu.bitcast(h[:, :w].astype(BF16).astype(F32), jnp.uint32) >> 16
    hi = pltpu.bitcast(h[:, w:].astype(BF16).astype(F32), jnp.uint32) & jnp.uint32(0xFFFF0000)
    return hi | lo


def _unpack_bf16_pairs(u):
    lo = pltpu.bitcast(u << 16, F32)
    hi = pltpu.bitcast(u & jnp.uint32(0xFFFF0000), F32)
    return jnp.concatenate([lo, hi], axis=1).astype(BF16)


def _rms_mod(x, gain, scale, shift):
    ms = jnp.mean(x * x, axis=-1, keepdims=True)
    y = x * lax.rsqrt(ms + EPS) * gain
    return y * (1.0 + scale) + shift


def _ada_kernel(c_ref, w_ref, b_ref, o_ref):
    c = c_ref[...]
    s = (c * jax.nn.sigmoid(c)).astype(BF16)
    o_ref[...] = jnp.dot(s, w_ref[...].astype(BF16), preferred_element_type=F32) + b_ref[...]


def _ada_params(cvec, ada_w, ada_b):
    depth, d, n = ada_w.shape
    mp = cvec.shape[0]
    tn = 1024
    return pl.pallas_call(
        _ada_kernel,
        grid=(depth, n // tn),
        in_specs=[
            pl.BlockSpec((mp, d), lambda l, j: (0, 0)),
            pl.BlockSpec((None, d, tn), lambda l, j: (l, 0, j)),
            pl.BlockSpec((None, 1, tn), lambda l, j: (l, 0, j)),
        ],
        out_specs=pl.BlockSpec((None, mp, tn), lambda l, j: (l, 0, j)),
        out_shape=jax.ShapeDtypeStruct((depth, mp, n), F32),
        compiler_params=_cparams(("arbitrary", "arbitrary")),
        name="ada_params",
    )(cvec, ada_w, ada_b.reshape(depth, 1, n))


def _swap16(x):
    width = x.shape[-1]
    lane = lax.broadcasted_iota(jnp.int32, x.shape, 1)
    return jnp.where((lane % 32) < 16, pltpu.roll(x, width - 16, 1), pltpu.roll(x, 16, 1))


def _proj_even_kernel(xp_ref, xs_ref, mod_ref, g_ref, w_ref, cos_ref, sin_ref, o_ref, kv_ref, h_scr, *,
                      n_prompt_blocks):
    x = jnp.where(pl.program_id(0) < n_prompt_blocks, xp_ref[...], xs_ref[...])
    h_scr[...] = _rms_mod(x, g_ref[...], mod_ref[1:2, :], mod_ref[0:1, :]).astype(BF16)
    kv_tile = ATTN_WIDTH // TN_PROJ
    for j in range(w_ref.shape[1] // TN_PROJ):
        cols = slice(j * TN_PROJ, (j + 1) * TN_PROJ)
        acc = jnp.dot(h_scr[...], w_ref[:, cols], preferred_element_type=F32)
        if j <= kv_tile:
            tab = slice(0, TN_PROJ) if j < kv_tile else slice(TN_PROJ, 2 * TN_PROJ)
            o_ref[:, cols] = (acc * cos_ref[:, tab] + _swap16(acc) * sin_ref[:, tab]).astype(BF16)
            if j == kv_tile:
                kv_ref[...] = acc
        else:
            o_ref[:, cols] = jax.nn.gelu(acc).astype(BF16)


def _proj_even(x_p, x_s, mods, gain, w, cos_t, sin_t, blocks_per_seq):
    d = x_p.shape[1]
    n = w.shape[1]
    tm, tn = TM_PROJ, TN_PROJ
    npb = x_p.shape[0] // tm
    nsb = x_s.shape[0] // tm
    t = x_p.shape[0] + x_s.shape[0]

    def mod_idx(i):
        return (jnp.where(i < npb, 0, 1 + (i - npb) // blocks_per_seq), 0, 0)

    def tab_idx(i):
        return (jnp.where(i < npb, blocks_per_seq, (i - npb) % blocks_per_seq), 0)

    return pl.pallas_call(
        functools.partial(_proj_even_kernel, n_prompt_blocks=npb),
        grid=(npb + nsb,),
        in_specs=[
            pl.BlockSpec((tm, d), lambda i: (jnp.minimum(i, npb - 1), 0)),
            pl.BlockSpec((tm, d), lambda i: (jnp.maximum(i - npb, 0), 0)),
            pl.BlockSpec((None, 6, d), mod_idx),
            pl.BlockSpec((1, d), lambda i: (0, 0)),
            pl.BlockSpec((d, n), lambda i: (0, 0), pipeline_mode=pl.Buffered(1)),
            pl.BlockSpec((tm, 2 * tn), tab_idx),
            pl.BlockSpec((tm, 2 * tn), tab_idx),
        ],
        out_specs=[
            pl.BlockSpec((tm, n), lambda i: (i, 0)),
            pl.BlockSpec((tm, tn), lambda i: (i, 0)),
        ],
        out_shape=[jax.ShapeDtypeStruct((t, n), BF16), jax.ShapeDtypeStruct((t, tn), F32)],
        scratch_shapes=[pltpu.VMEM((tm, d), BF16)],
        compiler_params=_cparams(("arbitrary",)),
        name="proj_even",
    )(x_p, x_s, mods, gain, w, cos_t, sin_t)


def _rope_tables(dec_seq, tm):
    f32 = np.float32
    n = np.arange(dec_seq)
    row = (n // GRID_W).astype(f32)
    col = (n % GRID_W).astype(f32)
    nf = HEAD_DIM // 4
    inv_freq = np.power(f32(ROPE_BASE), -np.arange(nf, dtype=f32) / f32(nf)).astype(f32)
    ar = row[:, None] * inv_freq[None, :]
    ac = col[:, None] * inv_freq[None, :]
    cos_h = np.concatenate([np.cos(ar), np.cos(ar), np.cos(ac), np.cos(ac)], axis=-1).astype(f32)
    sin_h = np.concatenate([-np.sin(ar), np.sin(ar), -np.sin(ac), np.sin(ac)], axis=-1).astype(f32)
    scale = f32(HEAD_DIM ** -0.5)
    ones = np.ones((dec_seq, KV_WIDTH), f32)
    zeros = np.zeros((dec_seq, KV_WIDTH), f32)
    q_heads = TN_PROJ // HEAD_DIM
    cos_t = np.concatenate([np.tile(cos_h, (1, q_heads)) * scale, np.tile(cos_h, (1, N_KV_HEADS)), ones], axis=-1)
    sin_t = np.concatenate([np.tile(sin_h, (1, q_heads)) * scale, np.tile(sin_h, (1, N_KV_HEADS)), zeros], axis=-1)
    id_cos = np.concatenate([np.full((tm, TN_PROJ), scale, f32), np.ones((tm, TN_PROJ), f32)], axis=-1)
    id_sin = np.zeros((tm, 2 * TN_PROJ), f32)
    return (jnp.asarray(np.concatenate([cos_t, id_cos], axis=0), F32),
            jnp.asarray(np.concatenate([sin_t, id_sin], axis=0), F32))


def _cast_plan(weights, layer, n_steps, step_of):
    n_exp = weights[0].shape[1]
    steps_per_matrix = min(n_steps // len(weights), n_exp)
    assert steps_per_matrix >= 1 and n_exp % steps_per_matrix == 0
    e_blk = n_exp // steps_per_matrix

    def slice_of(k, idx):
        return jnp.clip(step_of(*idx) - k * steps_per_matrix, 0, steps_per_matrix - 1)

    in_specs = [pl.BlockSpec((None, e_blk) + w.shape[2:], lambda *idx, k=k: (layer, slice_of(k, idx), 0, 0))
                for k, w in enumerate(weights)]
    out_specs = [pl.BlockSpec((e_blk,) + w.shape[2:], lambda *idx, k=k: (slice_of(k, idx), 0, 0))
                 for k, w in enumerate(weights)]
    out_shapes = [jax.ShapeDtypeStruct(w.shape[1:], BF16) for w in weights]
    return steps_per_matrix, in_specs, out_specs, out_shapes


def _cast_slices(step, steps_per_matrix, srcs, dsts):
    for k, (src, dst) in enumerate(zip(srcs, dsts)):
        @pl.when((step >= k * steps_per_matrix) & (step < (k + 1) * steps_per_matrix))
        def _(src=src, dst=dst):
            dst[...] = src[...].astype(BF16)


def _sink_attend(q, keys, vals, masks, sink):
    scores = []
    for k, mask in zip(keys, masks):
        s = lax.dot_general(q, k, (((1,), (1,)), ((), ())), preferred_element_type=F32)
        if mask is not None:
            stacked = s.reshape(s.shape[0] // mask.shape[0], *mask.shape)
            s = jnp.where(mask[None], stacked, NEG_INF).reshape(s.shape)
        scores.append(s)
    m = sink
    for s in scores:
        m = jnp.maximum(m, jnp.max(s, axis=-1, keepdims=True))
    es = [jnp.exp(s - m) for s in scores]
    den = jnp.exp(sink - m)
    for e in es:
        den = den + jnp.sum(e, axis=-1, keepdims=True)
    out = None
    for e, v in zip(es, vals):
        o = jnp.dot(e.astype(BF16), v, preferred_element_type=F32)
        out = o if out is None else out + o
    return out * (1.0 / den)


def _grouped_heads_attend(sink_ref, q_ref, o_ref, kh, keys, vals, masks):
    rows = q_ref.shape[0]
    heads = [kh * Q_GROUPS + g for g in range(Q_GROUPS)]
    q = jnp.concatenate([q_ref[:, h * HEAD_DIM:(h + 1) * HEAD_DIM] for h in heads], axis=0)
    ridx = lax.broadcasted_iota(jnp.int32, (Q_GROUPS * rows, 1), 0)
    sink = jnp.full((Q_GROUPS * rows, 1), sink_ref[heads[-1]], F32)
    for g in range(Q_GROUPS - 2, -1, -1):
        sink = jnp.where(ridx < (g + 1) * rows, sink_ref[heads[g]], sink)
    o = _sink_attend(q, keys, vals, masks, sink)
    for g, h in enumerate(heads):
        o_ref[:, h * HEAD_DIM:(h + 1) * HEAD_DIM] = o[g * rows:(g + 1) * rows, :].astype(BF16)


def _ctx_attn_kernel(sink_ref, q_ref, kv_ref, o_ref):
    for kh in range(N_KV_HEADS):
        k = kv_ref[:, kh * HEAD_DIM:(kh + 1) * HEAD_DIM]
        v = kv_ref[:, KV_WIDTH + kh * HEAD_DIM:KV_WIDTH + (kh + 1) * HEAD_DIM]
        _grouped_heads_attend(sink_ref, q_ref, o_ref, kh, [k], [v], [None])


def _ctx_attention(proj, sink, batch, seq, kv_col_block):
    return pl.pallas_call(
        _ctx_attn_kernel,
        grid=(batch,),
        in_specs=[
            pl.BlockSpec(memory_space=pltpu.SMEM),
            pl.BlockSpec((seq, ATTN_WIDTH), lambda b: (b, 0)),
            pl.BlockSpec((seq, 2 * KV_WIDTH), lambda b: (b, kv_col_block)),
        ],
        out_specs=pl.BlockSpec((seq, ATTN_WIDTH), lambda b: (b, 0)),
        out_shape=jax.ShapeDtypeStruct((batch * seq, ATTN_WIDTH), BF16),
        compiler_params=_cparams(("arbitrary",)),
        name="ctx_attention",
    )(sink, proj, proj)


def _lat_attn_kernel(sink_ref, q_ref, kv_ref, ck_ref, cv_ref, wg_ref, wu_ref, wd_ref, o_ref, wg_out, wu_out, wd_out,
                     *, seq, steps_per_matrix):
    i = pl.program_id(1)
    _cast_slices(pl.program_id(0) * pl.num_programs(1) + i, steps_per_matrix,
                 (wg_ref, wu_ref, wd_ref), (wg_out, wu_out, wd_out))
    win = 3 * ATTN_BLK
    ws = pl.multiple_of(jnp.clip((i - 1) * ATTN_BLK, 0, seq - win), ATTN_BLK)
    shape = (ATTN_BLK, win)
    qpos = i * ATTN_BLK + lax.broadcasted_iota(jnp.int32, shape, 0)
    kpos = ws + lax.broadcasted_iota(jnp.int32, shape, 1)
    valid = jnp.abs(qpos - kpos) <= ATTN_BLK
    kvw = kv_ref[pl.ds(ws, win), :]
    for kh in range(N_KV_HEADS):
        k_loc = kvw[:, kh * HEAD_DIM:(kh + 1) * HEAD_DIM]
        v_loc = kvw[:, KV_WIDTH + kh * HEAD_DIM:KV_WIDTH + (kh + 1) * HEAD_DIM]
        k_ctx = ck_ref[:, kh * HEAD_DIM:(kh + 1) * HEAD_DIM]
        v_ctx = cv_ref[:, kh * HEAD_DIM:(kh + 1) * HEAD_DIM]
        _grouped_heads_attend(sink_ref, q_ref, o_ref, kh, [k_ctx, k_loc], [v_ctx, v_loc], [None, valid])


def _lat_attention(proj, sink, cache_k, cache_v, n_prompt, dec_batch, dec_seq, kv_col_block, weights, layer):
    qb = dec_seq // ATTN_BLK
    q_off = n_prompt // ATTN_BLK
    s_off = n_prompt // dec_seq
    past = cache_k.shape[1]
    spm, w_in, w_out, w_shapes = _cast_plan(weights, layer, dec_batch * qb, lambda b, i: b * qb + i)
    return pl.pallas_call(
        functools.partial(_lat_attn_kernel, seq=dec_seq, steps_per_matrix=spm),
        grid=(dec_batch, qb),
        in_specs=[
            pl.BlockSpec(memory_space=pltpu.SMEM),
            pl.BlockSpec((ATTN_BLK, ATTN_WIDTH), lambda b, i: (q_off + b * qb + i, 0)),
            pl.BlockSpec((dec_seq, 2 * KV_WIDTH), lambda b, i: (s_off + b, kv_col_block)),
            pl.BlockSpec((None, past, KV_WIDTH), lambda b, i: (b, 0, 0)),
            pl.BlockSpec((None, past, KV_WIDTH), lambda b, i: (b, 0, 0)),
        ] + w_in,
        out_specs=[pl.BlockSpec((ATTN_BLK, ATTN_WIDTH), lambda b, i: (b * qb + i, 0))] + w_out,
        out_shape=[jax.ShapeDtypeStruct((dec_batch * dec_seq, ATTN_WIDTH), BF16)] + w_shapes,
        compiler_params=_cparams(("arbitrary", "arbitrary")),
        name="lat_attention",
    )(sink, proj, proj, cache_k, cache_v, *weights)


def _sgu_kernel(ul_ref, uh_ref, vl_ref, vh_ref, g_ref, w_ref, bt_ref, z_ref):
    half = SGU_WIDTH // 2
    per_half = SGU_GROUPS // 2
    u_refs = (ul_ref, uh_ref)
    for c in range(TM_SGU // CHUNK):
        rows = slice(c * CHUNK, (c + 1) * CHUNK)
        vl = vl_ref[rows, :].astype(F32)
        vh = vh_ref[rows, :].astype(F32)
        ssq = jnp.sum(vl * vl, axis=-1, keepdims=True) + jnp.sum(vh * vh, axis=-1, keepdims=True)
        r = lax.rsqrt(ssq * (1.0 / SGU_WIDTH) + EPS)
        vn = ((vl * r * g_ref[:, :half]).astype(BF16), (vh * r * g_ref[:, half:]).astype(BF16))
        for g in range(SGU_GROUPS):
            src = g // per_half
            cols = slice((g % per_half) * SGU_GROUP_DIM, (g % per_half + 1) * SGU_GROUP_DIM)
            mixed = jnp.dot(w_ref[g], vn[src][:, cols], preferred_element_type=F32) + bt_ref[:, g:g + 1]
            z_ref[rows, g * SGU_GROUP_DIM:(g + 1) * SGU_GROUP_DIM] = (
                u_refs[src][rows, cols].astype(F32) * mixed).astype(BF16)


def _sgu(proj, norm_g, w_s, b_t):
    t = proj.shape[0]
    half = SGU_WIDTH // 2
    u0 = (ATTN_WIDTH + 2 * KV_WIDTH) // half
    blocks = [pl.BlockSpec((TM_SGU, half), functools.partial(lambda i, c: (i, c), c=u0 + k)) for k in range(4)]
    return pl.pallas_call(
        _sgu_kernel,
        grid=(t // TM_SGU,),
        in_specs=blocks + [
            pl.BlockSpec((1, SGU_WIDTH), lambda i: (0, 0)),
            pl.BlockSpec((SGU_GROUPS, CHUNK, CHUNK), lambda i: (0, 0, 0)),
            pl.BlockSpec((CHUNK, SGU_GROUPS), lambda i: (0, 0)),
        ],
        out_specs=pl.BlockSpec((TM_SGU, SGU_WIDTH), lambda i: (i, 0)),
        out_shape=jax.ShapeDtypeStruct((t, SGU_WIDTH), BF16),
        compiler_params=_cparams(("arbitrary",)),
        name="sgu",
    )(proj, proj, proj, proj, norm_g, w_s, b_t)


def _residual_router(rows, x, acc, mod_ref, g2_ref, wrt_ref, brt_ref, x1_ref, h2_ref, lg_ref):
    x1 = x + mod_ref[2:3, :] * acc
    x1_ref[rows, :] = x1
    h2 = _rms_mod(x1, g2_ref[...], mod_ref[4:5, :], mod_ref[3:4, :])
    h2_ref[rows, :] = _pack_bf16_pairs(h2)
    lg = lax.dot_general(wrt_ref[...], h2.astype(BF16), (((1,), (1,)), ((), ())), preferred_element_type=F32)
    lg_ref[:, rows] = lg + brt_ref[...]


def _sub_blocks(ref):
    return [slice(r, r + TM_OUT) for r in range(0, ref.shape[0], TM_OUT)]


def _out_even_kernel(xp_ref, xs_ref, ap_ref, as_ref, z_ref, w_ref, mod_ref, g2_ref, wrt_ref, brt_ref,
                     x1_ref, h2_ref, lg_ref, *, n_prompt_blocks):
    is_prompt = pl.program_id(0) < n_prompt_blocks
    for rows in _sub_blocks(x1_ref):
        a = jnp.where(is_prompt, ap_ref[rows, :].astype(F32), as_ref[rows, :].astype(F32)).astype(BF16)
        acc = jnp.dot(a, w_ref[0:ATTN_WIDTH, :], preferred_element_type=F32)
        acc = acc + jnp.dot(z_ref[rows, :], w_ref[ATTN_WIDTH:, :], preferred_element_type=F32)
        x = jnp.where(is_prompt, xp_ref[rows, :], xs_ref[rows, :])
        _residual_router(rows, x, acc, mod_ref, g2_ref, wrt_ref, brt_ref, x1_ref, h2_ref, lg_ref)


def _out_odd_kernel(x_ref, hc_ref, w_ref, mod_ref, g2_ref, wrt_ref, brt_ref, x1_ref, h2_ref, lg_ref):
    for rows in _sub_blocks(x1_ref):
        acc = jnp.dot(hc_ref[rows, :], w_ref[...], preferred_element_type=F32)
        _residual_router(rows, x_ref[rows, :], acc, mod_ref, g2_ref, wrt_ref, brt_ref, x1_ref, h2_ref, lg_ref)


def _out_proj(xs, lhs, w, mods, gain2, wrt, brt, n_prompt, dec_seq):
    d = xs[0].shape[1]
    t = sum(x.shape[0] for x in xs)
    tm = TM_OUTPROJ
    npb = n_prompt // tm
    bps = dec_seq // tm
    nsb = t // tm - npb
    if len(xs) == 2:
        x_specs = [pl.BlockSpec((tm, d), lambda i: (jnp.minimum(i, npb - 1), 0)),
                   pl.BlockSpec((tm, d), lambda i: (jnp.maximum(i - npb, 0), 0))]
    else:
        x_specs = [pl.BlockSpec((tm, d), lambda i: (i, 0))]

    def mod_idx(i):
        return (jnp.where(i < npb, 0, 1 + (i - npb) // bps), 0, 0)

    if len(lhs) == 3:
        a_p, a_s, z = lhs
        body = functools.partial(_out_even_kernel, n_prompt_blocks=npb)
        lhs_specs = [
            pl.BlockSpec((tm, ATTN_WIDTH), lambda i: (jnp.minimum(i, npb - 1), 0)),
            pl.BlockSpec((tm, ATTN_WIDTH), lambda i: (jnp.clip(i - npb, 0, nsb - 1), 0)),
            pl.BlockSpec((tm, SGU_WIDTH), lambda i: (i, 0)),
        ]
    else:
        body = _out_odd_kernel
        lhs_specs = [pl.BlockSpec((tm, d), lambda i: (i, 0))]
    return pl.pallas_call(
        body,
        grid=(t // tm,),
        in_specs=x_specs + lhs_specs + [
            pl.BlockSpec((w.shape[0], d), lambda i: (0, 0), pipeline_mode=pl.Buffered(1)),
            pl.BlockSpec((None, 6, d), mod_idx),
            pl.BlockSpec((1, d), lambda i: (0, 0)),
            pl.BlockSpec((ROUTER_LANES, d), lambda i: (0, 0)),
            pl.BlockSpec((ROUTER_LANES, 1), lambda i: (0, 0)),
        ],
        out_specs=[
            pl.BlockSpec((tm, d), lambda i: (i, 0)),
            pl.BlockSpec((tm, d // 2), lambda i: (i, 0)),
            pl.BlockSpec((ROUTER_LANES, tm), lambda i: (0, i)),
        ],
        out_shape=[
            jax.ShapeDtypeStruct((t, d), F32),
            jax.ShapeDtypeStruct((t, d // 2), jnp.uint32),
            jax.ShapeDtypeStruct((ROUTER_LANES, t), F32),
        ],
        compiler_params=_cparams(("arbitrary",)),
        name="out_proj",
    )(*xs, *lhs, w, mods, gain2, wrt, brt)


def _first_argmax4(v0, v1, v2, v3):
    m = jnp.maximum(jnp.maximum(v0, v1), jnp.maximum(v2, v3))
    idx = jnp.where(v0 == m, 0.0, jnp.where(v1 == m, 1.0, jnp.where(v2 == m, 2.0, 3.0)))
    return m, idx


def _route_kernel(lg_ref, pos_ref, ea_ref, eb_ref, nv_ref, ends_ref, oh_scr, rank_scr, *, n_tokens):
    lg = lg_ref[...]
    rows = [lg[r:r + 1, :] for r in range(N_GROUPS_MOE + N_EXPERTS)]
    _, gidx = _first_argmax4(*rows[:N_GROUPS_MOE])
    e = []
    for k in range(EXPERTS_PER_GROUP):
        cand = [rows[N_GROUPS_MOE + g * EXPERTS_PER_GROUP + k] for g in range(N_GROUPS_MOE)]
        e.append(jnp.where(gidx == 0.0, cand[0], jnp.where(gidx == 1.0, cand[1], jnp.where(gidx == 2.0, cand[2], cand[3]))))
    _, l1 = _first_argmax4(*e)
    e2 = [jnp.where(l1 == float(k), -jnp.inf, e[k]) for k in range(EXPERTS_PER_GROUP)]
    _, l2 = _first_argmax4(*e2)
    lo = jnp.minimum(l1, l2)
    hi = jnp.maximum(l1, l2)
    pair = jnp.where(lo == 0.0, hi - 1.0, jnp.where(lo == 2.0, 4.0, jnp.where(hi == 3.0, 3.0, 5.0)))
    cls = gidx * float(PAIRS_PER_GROUP) + pair

    crow = lax.broadcasted_iota(jnp.int32, (CLASS_ROWS, n_tokens), 0).astype(F32)
    oh_scr[...] = jnp.where(crow == cls, 1.0, 0.0).astype(BF16)

    jj = lax.broadcasted_iota(jnp.int32, (CUM_CHUNK, CUM_CHUNK), 0)
    ii = lax.broadcasted_iota(jnp.int32, (CUM_CHUNK, CUM_CHUNK), 1)
    tri = jnp.where(jj < ii, 1.0, 0.0).astype(BF16)

    def chunk(c, carry):
        off = pl.multiple_of(c * CUM_CHUNK, CUM_CHUNK)
        oh = oh_scr[:, pl.ds(off, CUM_CHUNK)]
        ohf = oh.astype(F32)
        cum = jnp.dot(oh, tri, preferred_element_type=F32) + carry
        rank_scr[:, pl.ds(off, CUM_CHUNK)] = jnp.sum(cum * ohf, axis=0, keepdims=True)
        return carry + jnp.sum(ohf, axis=1, keepdims=True)

    counts = lax.fori_loop(0, n_tokens // CUM_CHUNK, chunk, jnp.zeros((CLASS_ROWS, 1), F32))
    padded = jnp.floor((counts + float(MOE_BLK - 1)) * (1.0 / MOE_BLK)) * float(MOE_BLK)
    run = jnp.zeros((1, 1), F32)
    starts_rows = []
    for c in range(CLASS_ROWS):
        starts_rows.append(run)
        run = run + padded[c:c + 1, :]
    starts = jnp.concatenate(starts_rows, axis=0)
    ends = starts + padded
    total = run

    ohf = oh_scr[...].astype(F32)
    pos = rank_scr[...] + jnp.sum(ohf * starts, axis=0, keepdims=True)
    pos_ref[...] = pos.astype(jnp.int32)

    bstart = lax.broadcasted_iota(jnp.int32, (CLASS_ROWS, MAX_MOE_BLOCKS), 1).astype(F32) * float(MOE_BLK)
    brow = lax.broadcasted_iota(jnp.int32, (CLASS_ROWS, MAX_MOE_BLOCKS), 0)
    done = jnp.where((ends <= bstart) & (brow < N_CLASSES), 1.0, 0.0)
    bcls = jnp.minimum(jnp.sum(done, axis=0, keepdims=True), float(N_CLASSES - 1))
    grp = (jnp.where(bcls >= 6.0, 1.0, 0.0) + jnp.where(bcls >= 12.0, 1.0, 0.0) + jnp.where(bcls >= 18.0, 1.0, 0.0))
    bp = bcls - grp * float(PAIRS_PER_GROUP)
    first = jnp.where(bp >= 3.0, 1.0, 0.0) + jnp.where(bp >= 4.0, 1.0, 0.0)
    second = jnp.where(bp < 3.0, bp + 1.0, jnp.where(bp < 5.0, 3.0, 1.0))
    ea_ref[...] = (grp * float(EXPERTS_PER_GROUP) + first).astype(jnp.int32)
    eb_ref[...] = (grp * float(EXPERTS_PER_GROUP) + second).astype(jnp.int32)
    own = jnp.where(brow.astype(F32) == bcls, starts + counts, 0.0)
    filled = jnp.sum(own, axis=0, keepdims=True) - bstart[0:1, :]
    nv_ref[...] = jnp.clip(filled, 0.0, float(MOE_BLK)).astype(jnp.int32)
    lane = lax.broadcasted_iota(jnp.int32, (1, MAX_MOE_BLOCKS), 1)
    ends_row = jnp.zeros((1, MAX_MOE_BLOCKS), F32)
    for c in range(N_CLASSES):
        ends_row = jnp.where(lane == c, starts_rows[c] + padded[c:c + 1, :], ends_row)
    ends_ref[...] = ends_row.astype(jnp.int32)


def _route(logits_t):
    t = logits_t.shape[1]
    rows = N_CLASSES
    i32 = jnp.int32
    return pl.pallas_call(
        functools.partial(_route_kernel, n_tokens=t),
        grid=(1,),
        in_specs=[pl.BlockSpec((rows, t), lambda i: (0, 0))],
        out_specs=[
            pl.BlockSpec((1, t), lambda i: (0, 0)),
        ] + [pl.BlockSpec((1, MAX_MOE_BLOCKS), lambda i: (0, 0))] * 4,
        out_shape=[jax.ShapeDtypeStruct((1, t), i32)] + [jax.ShapeDtypeStruct((1, MAX_MOE_BLOCKS), i32)] * 4,
        scratch_shapes=[pltpu.VMEM((CLASS_ROWS, t), BF16), pltpu.VMEM((1, t), F32)],
        compiler_params=_cparams(("arbitrary",)),
        name="route",
    )(logits_t)


DMA_THREADS = 2


def _row_copy(src_ref, src_row, dst_ref, dst_row, sem):
    return pltpu.make_async_copy(src_ref.at[pl.ds(src_row, 1), :], dst_ref.at[pl.ds(dst_row, 1), :], sem)


def _start_row_copies(n_rows, make_copy):
    for r in range(n_rows):
        make_copy(r).start(priority=r % DMA_THREADS)


def _dispatch_kernel(pos_ref, ends_ref, h_ref, o_ref, zbuf, sem, zsem):
    step = pl.program_id(0)

    @pl.when(step == 0)
    def _():
        zbuf[...] = jnp.zeros_like(zbuf)
        total = ends_ref[N_CLASSES - 1]
        n_free = (o_ref.shape[0] - total) // MOE_BLK

        def zero_block(first_row, phase):
            first = pl.multiple_of(first_row, MOE_BLK)
            getattr(pltpu.make_async_copy(zbuf, o_ref.at[pl.ds(first, MOE_BLK), :], zsem), phase)()

        for phase in ("start", "wait"):
            prev = 0
            for c in range(N_CLASSES):
                end = ends_ref[c]
                pl.when(end > prev)(functools.partial(zero_block, end - MOE_BLK, phase))
                prev = end

            def free_block(k, carry, phase=phase):
                zero_block(total + k * MOE_BLK, phase)
                return carry

            lax.fori_loop(0, n_free, free_block, 0)

    rows = h_ref.shape[0]
    base = step * rows

    _start_row_copies(rows, lambda r: _row_copy(h_ref, r, o_ref, pos_ref[base + r], sem))
    pltpu.make_async_copy(h_ref, o_ref.at[pl.ds(0, rows), :], sem).wait()


def _dispatch(pos, ends, h2, n_rows):
    t, d = h2.shape
    return pl.pallas_call(
        _dispatch_kernel,
        grid_spec=pltpu.PrefetchScalarGridSpec(
            num_scalar_prefetch=2,
            grid=(t // TM_DISPATCH,),
            in_specs=[pl.BlockSpec((TM_DISPATCH, d), lambda i, pos, ends: (i, 0))],
            out_specs=pl.BlockSpec(memory_space=pl.ANY),
            scratch_shapes=[pltpu.VMEM((MOE_BLK, d), h2.dtype), pltpu.SemaphoreType.DMA(()),
                            pltpu.SemaphoreType.DMA(())],
        ),
        out_shape=jax.ShapeDtypeStruct((n_rows, d), h2.dtype),
        compiler_params=_cparams(("arbitrary",)),
        name="moe_dispatch",
    )(pos, ends, h2)


def _expert_ffn(x, wg_ref, wu_ref, wd_ref):
    gate = jnp.dot(x, wg_ref[...], preferred_element_type=F32)
    up = jnp.dot(x, wu_ref[...], preferred_element_type=F32)
    hidden = (gate * jax.nn.sigmoid(gate) * up).astype(BF16)
    return jnp.dot(hidden, wd_ref[...], preferred_element_type=F32)


def _moe_kernel(ea_ref, eb_ref, nv_ref, x_ref, wr_ref, br_ref, wga, wua, wda, wgb, wub, wdb, y_ref):
    b = pl.program_id(0)
    n_valid = nv_ref[b]

    def run(rows):
        x = _unpack_bf16_pairs(x_ref[rows, :])
        lg = jnp.dot(x, wr_ref[...], preferred_element_type=F32) + br_ref[...]
        lane = lax.broadcasted_iota(jnp.int32, lg.shape, 1)
        ea = ea_ref[b]
        eb = eb_ref[b]
        grp = ea // EXPERTS_PER_GROUP
        is_grp = lane < N_GROUPS_MOE
        gm = jnp.max(jnp.where(is_grp, lg, -jnp.inf), axis=-1, keepdims=True)
        ge = jnp.where(is_grp, jnp.exp(lg - gm), 0.0)
        p_g = jnp.sum(jnp.where(lane == grp, ge, 0.0), axis=-1, keepdims=True) / jnp.sum(ge, axis=-1, keepdims=True)
        la = jnp.sum(jnp.where(lane == N_GROUPS_MOE + ea, lg, 0.0), axis=-1, keepdims=True)
        lb = jnp.sum(jnp.where(lane == N_GROUPS_MOE + eb, lg, 0.0), axis=-1, keepdims=True)
        mm = jnp.maximum(la, lb)
        xa = jnp.exp(la - mm)
        xb = jnp.exp(lb - mm)
        ga = xa / (xa + xb) * p_g
        gb = xb / (xa + xb) * p_g
        ya = _expert_ffn(x, wga, wua, wda)
        yb = _expert_ffn(x, wgb, wub, wdb)
        y_ref[rows, :] = ya * ga + yb * gb

    half = MOE_BLK // 2

    @pl.when(n_valid > half)
    def _():
        run(slice(0, MOE_BLK))

    @pl.when((n_valid > 0) & (n_valid <= half))
    def _():
        run(slice(0, half))
        y_ref[half:, :] = jnp.zeros((MOE_BLK - half, y_ref.shape[1]), y_ref.dtype)

    @pl.when(n_valid == 0)
    def _():
        y_ref[...] = jnp.zeros_like(y_ref)


def _moe_experts(ea, eb, n_valid, hs, wr, br, w_gate, w_up, w_down, n_blocks):
    d = w_gate.shape[1]
    de = w_gate.shape[2]

    def wa_idx(b, ea, eb, nv):
        return (ea[b], 0, 0)

    def wb_idx(b, ea, eb, nv):
        return (eb[b], 0, 0)

    return pl.pallas_call(
        _moe_kernel,
        grid_spec=pltpu.PrefetchScalarGridSpec(
            num_scalar_prefetch=3,
            grid=(n_blocks,),
            in_specs=[
                pl.BlockSpec((MOE_BLK, hs.shape[1]), lambda b, ea, eb, nv: (jnp.where(nv[b] > 0, b, 0), 0)),
                pl.BlockSpec((d, ROUTER_LANES), lambda b, ea, eb, nv: (0, 0)),
                pl.BlockSpec((1, ROUTER_LANES), lambda b, ea, eb, nv: (0, 0)),
                pl.BlockSpec((None, d, de), wa_idx),
                pl.BlockSpec((None, d, de), wa_idx),
                pl.BlockSpec((None, de, d), wa_idx),
                pl.BlockSpec((None, d, de), wb_idx),
                pl.BlockSpec((None, d, de), wb_idx),
                pl.BlockSpec((None, de, d), wb_idx),
            ],
            out_specs=pl.BlockSpec((MOE_BLK, d), lambda b, ea, eb, nv: (b, 0)),
        ),
        out_shape=jax.ShapeDtypeStruct((n_blocks * MOE_BLK, d), F32),
        compiler_params=_cparams(("arbitrary",)),
        name="moe_experts",
    )(ea, eb, n_valid, hs, wr, br, w_gate, w_up, w_down, w_gate, w_up, w_down)


def _combine_kernel(pos_ref, ys_ref, x1_ref, mod_ref, gf_ref, o_ref, ybuf, sem, *, block_off, n_steps, final):
    i = pl.program_id(0)
    slot = i % 2
    tm = x1_ref.shape[0]

    def start_gather(step, into):
        base = (step + block_off) * tm
        _start_row_copies(tm, lambda r: _row_copy(ys_ref, pos_ref[base + r], ybuf.at[into], r, sem.at[into]))

    pl.when(i == 0)(lambda: start_gather(0, 0))
    pl.when(i + 1 < n_steps)(lambda: start_gather(i + 1, 1 - slot))
    pltpu.make_async_copy(ys_ref.at[pl.ds(0, tm), :], ybuf.at[slot], sem.at[slot]).wait()
    x2 = x1_ref[...] + mod_ref[5:6, :] * ybuf[slot]
    if final:
        ms = jnp.mean(x2 * x2, axis=-1, keepdims=True)
        x2 = x2 * lax.rsqrt(ms + EPS) * gf_ref[...]
    o_ref[...] = x2


def _combine(pos, ys, x1, mods, gain_f, n_prompt, dec_seq, row_off, n_rows, final):
    d = x1.shape[1]
    tm = TM_COMBINE
    npb = n_prompt // tm
    bps = dec_seq // tm
    boff = row_off // tm

    def mod_idx(i, pos):
        blk = i + boff
        return (jnp.where(blk < npb, 0, 1 + (blk - npb) // bps), 0, 0)

    return pl.pallas_call(
        functools.partial(_combine_kernel, block_off=boff, n_steps=n_rows // tm, final=final),
        grid_spec=pltpu.PrefetchScalarGridSpec(
            num_scalar_prefetch=1,
            grid=(n_rows // tm,),
            in_specs=[
                pl.BlockSpec(memory_space=pl.ANY),
                pl.BlockSpec((tm, d), lambda i, pos: (i + boff, 0)),
                pl.BlockSpec((None, 6, d), mod_idx),
                pl.BlockSpec((1, d), lambda i, pos: (0, 0)),
            ],
            out_specs=pl.BlockSpec((tm, d), lambda i, pos: (i, 0)),
            scratch_shapes=[pltpu.VMEM((2, tm, d), F32), pltpu.SemaphoreType.DMA((2,))],
        ),
        out_shape=jax.ShapeDtypeStruct((n_rows, d), F32),
        compiler_params=_cparams(("arbitrary",)),
        name="moe_combine",
    )(pos, ys, x1, mods, gain_f)


def _proj_glu_kernel(pos_ref, ys_ref, x1_ref, mod_prev_ref, mod_ref, g_ref, w_ref, o_ref, x2_ref, h_scr, ybuf, sem,
                     *, n_steps):
    i = pl.program_id(0)
    slot = i % 2
    tm = x1_ref.shape[0]

    def start_gather(step, into):
        base = step * tm
        _start_row_copies(tm, lambda r: _row_copy(ys_ref, pos_ref[base + r], ybuf.at[into], r, sem.at[into]))

    pl.when(i == 0)(lambda: start_gather(0, 0))
    pl.when(i + 1 < n_steps)(lambda: start_gather(i + 1, 1 - slot))
    pltpu.make_async_copy(ys_ref.at[pl.ds(0, tm), :], ybuf.at[slot], sem.at[slot]).wait()
    x2 = x1_ref[...] + mod_prev_ref[5:6, :] * ybuf[slot]
    x2_ref[...] = x2
    h_scr[...] = _rms_mod(x2, g_ref[...], mod_ref[1:2, :], mod_ref[0:1, :]).astype(BF16)
    width = w_ref.shape[1] // 2
    for j in range(width // TN_PROJ):
        cols = slice(j * TN_PROJ, (j + 1) * TN_PROJ)
        gcols = slice(width + j * TN_PROJ, width + (j + 1) * TN_PROJ)
        val = jnp.dot(h_scr[...], w_ref[:, cols], preferred_element_type=F32)
        gate = jnp.dot(h_scr[...], w_ref[:, gcols], preferred_element_type=F32)
        o_ref[:, cols] = (val * jax.nn.sigmoid(gate)).astype(BF16)


def _proj_glu(pos, ys, x1, mods_prev, mods, gain, w, n_prompt, dec_seq):
    t, d = x1.shape
    n = w.shape[1]
    tm = TM_PROJ
    npb = n_prompt // tm
    bps = dec_seq // tm

    def mod_idx(i, pos):
        return (jnp.where(i < npb, 0, 1 + (i - npb) // bps), 0, 0)

    return pl.pallas_call(
        functools.partial(_proj_glu_kernel, n_steps=t // tm),
        grid_spec=pltpu.PrefetchScalarGridSpec(
            num_scalar_prefetch=1,
            grid=(t // tm,),
            in_specs=[
                pl.BlockSpec(memory_space=pl.ANY),
                pl.BlockSpec((tm, d), lambda i, pos: (i, 0)),
                pl.BlockSpec((None, 6, d), mod_idx),
                pl.BlockSpec((None, 6, d), mod_idx),
                pl.BlockSpec((1, d), lambda i, pos: (0, 0)),
                pl.BlockSpec((d, n), lambda i, pos: (0, 0), pipeline_mode=pl.Buffered(1)),
            ],
            out_specs=[
                pl.BlockSpec((tm, n // 2), lambda i, pos: (i, 0)),
                pl.BlockSpec((tm, d), lambda i, pos: (i, 0)),
            ],
            scratch_shapes=[pltpu.VMEM((tm, d), BF16), pltpu.VMEM((2, tm, d), F32), pltpu.SemaphoreType.DMA((2,))],
        ),
        out_shape=[jax.ShapeDtypeStruct((t, n // 2), BF16), jax.ShapeDtypeStruct((t, d), F32)],
        compiler_params=_cparams(("arbitrary",)),
        name="proj_glu",
    )(pos, ys, x1, mods_prev, mods, gain, w)


CONV_ROWS = 64
CONV_COLS = 256
F32_SUBLANES = 8


def _conv_kernel(x_ref, prev_ref, next_ref, dw_ref, b_ref, g_ref, wg_ref, wu_ref, wd_ref, o_ref, wg_out, wu_out, wd_out,
                 xpad, shifted, ybuf, *, n_prompt_blocks, blocks_per_seq, steps_per_matrix):
    i = pl.program_id(0)
    _cast_slices(i, steps_per_matrix, (wg_ref, wu_ref, wd_ref), (wg_out, wu_out, wd_out))
    in_sample = i >= n_prompt_blocks
    j = i - n_prompt_blocks
    is_start = jnp.logical_or(jnp.logical_not(in_sample), j % blocks_per_seq == 0)
    is_end = jnp.logical_or(jnp.logical_not(in_sample), j % blocks_per_seq == blocks_per_seq - 1)
    tm = TM_OUT
    xpad[0:CONV_HALO, :] = jnp.where(is_start, 0.0, prev_ref[...].astype(F32))
    xpad[CONV_HALO:CONV_HALO + tm, :] = x_ref[...].astype(F32)
    xpad[CONV_HALO + tm:, :] = jnp.where(is_end, 0.0, next_ref[...].astype(F32))
    first = CONV_HALO - CONV_K // 2
    span = shifted.shape[1]

    def col_chunk(c, ssq):
        c0 = pl.multiple_of(c * CONV_COLS, CONV_COLS)
        cols = pl.ds(c0, CONV_COLS)
        w = dw_ref[:, cols]
        bias = b_ref[:, cols]
        for s in range(1, F32_SUBLANES):
            shifted[s - 1] = xpad[s:s + span, cols]
        parts = []
        for rc in range(tm // CONV_ROWS):
            r0 = rc * CONV_ROWS
            acc = jnp.zeros((CONV_ROWS, CONV_COLS), F32)
            for k in range(CONV_K):
                whole, s = divmod(first + k, F32_SUBLANES)
                base = whole * F32_SUBLANES + r0
                if s == 0:
                    src = xpad[base:base + CONV_ROWS, cols]
                else:
                    src = shifted[s - 1, base:base + CONV_ROWS, :]
                acc = acc + w[k:k + 1, :] * src
            acc = acc + bias
            ybuf[r0:r0 + CONV_ROWS, cols] = acc
            parts.append(jnp.sum(acc * acc, axis=-1, keepdims=True))
        return ssq + jnp.concatenate(parts, axis=0)

    d = x_ref.shape[1]
    ssq = lax.fori_loop(0, d // CONV_COLS, col_chunk, jnp.zeros((tm, 1), F32))
    y = ybuf[...] * lax.rsqrt(ssq * (1.0 / d) + EPS) * g_ref[...]
    o_ref[...] = (y * jax.nn.sigmoid(y)).astype(BF16)


def _conv_mix(xg, dw, dw_b, norm_g, n_prompt, seq, dec_seq, weights, layer):
    t, d = xg.shape
    tm = TM_OUT
    assert seq == tm, "context sequences must be exactly one conv row block"
    npb = n_prompt // tm
    bps = dec_seq // tm
    hb = tm // CONV_HALO
    last = t // CONV_HALO - 1
    spm, w_in, w_out, w_shapes = _cast_plan(weights, layer, t // tm, lambda i: i)
    return pl.pallas_call(
        functools.partial(_conv_kernel, n_prompt_blocks=npb, blocks_per_seq=bps, steps_per_matrix=spm),
        grid=(t // tm,),
        in_specs=[
            pl.BlockSpec((tm, d), lambda i: (i, 0)),
            pl.BlockSpec((CONV_HALO, d), lambda i: (jnp.maximum(i * hb - 1, 0), 0)),
            pl.BlockSpec((CONV_HALO, d), lambda i: (jnp.minimum((i + 1) * hb, last), 0)),
            pl.BlockSpec((dw.shape[0], d), lambda i: (0, 0)),
            pl.BlockSpec((1, d), lambda i: (0, 0)),
            pl.BlockSpec((1, d), lambda i: (0, 0)),
        ] + w_in,
        out_specs=[pl.BlockSpec((tm, d), lambda i: (i, 0))] + w_out,
        out_shape=[jax.ShapeDtypeStruct((t, d), BF16)] + w_shapes,
        scratch_shapes=[
            pltpu.VMEM((tm + 2 * CONV_HALO, d), F32),
            pltpu.VMEM((F32_SUBLANES - 1, tm + 2 * CONV_HALO - F32_SUBLANES, CONV_COLS), F32),
            pltpu.VMEM((tm, d), F32),
        ],
        compiler_params=_cparams(("arbitrary",)),
        name="conv_mix",
    )(xg, xg, xg, dw, dw_b, norm_g, *weights)


def _router_operands(w_rg, b_rg, w_re, b_re):
    d = w_rg.shape[0]
    used = N_GROUPS_MOE + N_EXPERTS
    w = jnp.concatenate([w_rg, w_re, jnp.zeros((d, ROUTER_LANES - used), F32)], axis=1)
    b = jnp.concatenate([b_rg, b_re, jnp.zeros((ROUTER_LANES - used,), F32)])
    return w.astype(BF16), b.reshape(1, ROUTER_LANES), w.T.astype(BF16), b.reshape(ROUTER_LANES, 1)


def _moe_rows(h2, logits_t, wr, br, wg, wu, wd):
    t = h2.shape[0]
    n_blocks = -(-(t + N_CLASSES * (MOE_BLK - 1)) // MOE_BLK)
    assert n_blocks <= MAX_MOE_BLOCKS
    pos, ea, eb, n_valid, ends = _route(logits_t)
    pos = pos.reshape(t)
    hs = _dispatch(pos, ends.reshape(-1), h2, n_blocks * MOE_BLK)
    ys = _moe_experts(ea.reshape(-1), eb.reshape(-1), n_valid.reshape(-1), hs, wr, br, wg, wu, wd, n_blocks)
    return pos, ys


def kernel(x_prompt, x_sample, cache_k, cache_v, c, c_ctx, w_in_even, attn_sink, sgu_norm, sgu_w, sgu_b,
           w_out_even, conv_w_in, conv_dw, conv_dw_b, conv_norm, conv_w_out, ada_w, ada_b, norm_mix, norm_ffn,
           router_group_w, router_group_b, router_expert_w, router_expert_b, expert_w_gate, expert_w_up,
           expert_w_down, final_norm):
    batch, seq, d = x_prompt.shape
    dec_batch, dec_seq, _ = x_sample.shape
    depth = ada_w.shape[0]
    n_prompt = batch * seq
    n_sample = dec_batch * dec_seq
    assert d == D_MODEL and depth == 2 and n_prompt % dec_seq == 0 and dec_seq % TM_PROJ == 0
    x_p = x_prompt.reshape(n_prompt, d)
    x_s = x_sample.reshape(n_sample, d)

    n_mod = -(-(1 + dec_batch) // 8) * 8
    cvec = jnp.concatenate([c_ctx[None, :], c, jnp.zeros((n_mod - 1 - dec_batch, d), F32)], axis=0)
    mods = _ada_params(cvec, ada_w, ada_b).reshape(depth, n_mod, 6, d)
    expert_w = (expert_w_gate, expert_w_up, expert_w_down)

    cos_t, sin_t = _rope_tables(dec_seq, TM_PROJ)
    proj, kv32 = _proj_even(x_p, x_s, mods[0], norm_mix[0][None, :], w_in_even[0].astype(BF16), cos_t, sin_t,
                            dec_seq // TM_PROJ)
    kv_col_block = ATTN_WIDTH // (2 * KV_WIDTH)
    sink = attn_sink[0]
    a_p = _ctx_attention(proj, sink, batch, seq, kv_col_block)
    past = cache_k.shape[2]
    ck = cache_k[:, 0].reshape(dec_batch, past, KV_WIDTH).astype(BF16)
    cv = cache_v[:, 0].reshape(dec_batch, past, KV_WIDTH).astype(BF16)
    a_s, *experts0 = _lat_attention(proj, sink, ck, cv, n_prompt, dec_batch, dec_seq, kv_col_block, expert_w, 0)
    z = _sgu(proj, sgu_norm[0][None, :], sgu_w[0].astype(BF16), sgu_b[0].T)

    wr0, br0, wrt0, brt0 = _router_operands(router_group_w[0], router_group_b[0], router_expert_w[0], router_expert_b[0])
    x1, h2, lg = _out_proj((x_p, x_s), (a_p, a_s, z), w_out_even[0].astype(BF16), mods[0], norm_ffn[0][None, :],
                           wrt0, brt0, n_prompt, dec_seq)
    pos, ys = _moe_rows(h2, lg, wr0, br0, *experts0)
    gain_f = final_norm[None, :]

    xg, x = _proj_glu(pos, ys, x1, mods[0], mods[1], norm_mix[1][None, :], conv_w_in[0].astype(BF16),
                      n_prompt, dec_seq)
    dw = jnp.concatenate([conv_dw[0], jnp.zeros((1, d), F32)], axis=0)
    hc, *experts1 = _conv_mix(xg, dw, conv_dw_b[0][None, :], conv_norm[0][None, :], n_prompt, seq, dec_seq,
                              expert_w, 1)
    wr1, br1, wrt1, brt1 = _router_operands(router_group_w[1], router_group_b[1], router_expert_w[1], router_expert_b[1])
    x1, h2, lg = _out_proj((x,), (hc,), conv_w_out[0].astype(BF16), mods[1], norm_ffn[1][None, :],
                           wrt1, brt1, n_prompt, dec_seq)
    pos, ys = _moe_rows(h2, lg, wr1, br1, *experts1)
    y_prompt = _combine(pos, ys, x1, mods[1], gain_f, n_prompt, dec_seq, 0, n_prompt, final=True)
    y_sample = _combine(pos, ys, x1, mods[1], gain_f, n_prompt, dec_seq, n_prompt, n_sample, final=True)

    new_k = kv32[:n_prompt, :KV_WIDTH].reshape(batch, 1, seq, N_KV_HEADS, HEAD_DIM)
    new_v = kv32[:n_prompt, KV_WIDTH:].reshape(batch, 1, seq, N_KV_HEADS, HEAD_DIM)
    return (y_prompt.reshape(batch, seq, d), y_sample.reshape(dec_batch, dec_seq, d), new_k, new_v)
```

```python
import functools

import jax
import jax.numpy as jnp
import numpy as np
from jax import lax
from jax.experimental import pallas as pl
from jax.experimental.pallas import tpu as pltpu

F32 = jnp.float32
BF16 = jnp.bfloat16

D_MODEL = 2048
GRID_W = 64
ATTN_BLK = 128
N_HEADS = 16
N_KV_HEADS = 4
Q_GROUPS = N_HEADS // N_KV_HEADS
HEAD_DIM = 64
ATTN_WIDTH = N_HEADS * HEAD_DIM
KV_WIDTH = N_KV_HEADS * HEAD_DIM
ROPE_BASE = 10000.0
SGU_GROUPS = 4
SGU_WIDTH = D_MODEL // 2
SGU_GROUP_DIM = SGU_WIDTH // SGU_GROUPS
CHUNK = 128
CONV_K = 31
CONV_HALO = 16
N_GROUPS_MOE = 4
EXPERTS_PER_GROUP = 4
N_EXPERTS = N_GROUPS_MOE * EXPERTS_PER_GROUP
PAIRS_PER_GROUP = 6
N_CLASSES = N_GROUPS_MOE * PAIRS_PER_GROUP
MOE_BLK = 256
EPS = 1e-6
NEG_INF = -1e30

ROUTER_LANES = 128
CLASS_ROWS = 32
MAX_MOE_BLOCKS = 128
TM_PROJ = 512
TN_PROJ = 512
TM_OUT = 256
TM_DISPATCH = 1024
TM_COMBINE = 512
TM_OUTPROJ = 512
TM_SGU = 512
CUM_CHUNK = 256
VMEM_LIMIT = 56 * 1024 * 1024


def _cparams(sem):
    return pltpu.CompilerParams(dimension_semantics=sem, vmem_limit_bytes=VMEM_LIMIT)


def _pack_bf16_pairs(h):
    w = h.shape[1] // 2
    lo = pltpu.bitcast(h[:, :w].astype(BF16).astype(F32), jnp.uint32) >> 16
    hi = pltpu.bitcast(h[:, w:].astype(BF16).astype(F32), jnp.uint32) & jnp.uint32(0xFFFF0000)
    return hi | lo


def _unpack_bf16_pairs(u):
    lo = pltpu.bitcast(u << 16, F32)
    hi = pltpu.bitcast(u & jnp.uint32(0xFFFF0000), F32)
    return jnp.concatenate([lo, hi], axis=1).astype(BF16)


def _rms_mod(x, gain, scale, shift):
    ms = jnp.mean(x * x, axis=-1, keepdims=True)
    y = x * lax.rsqrt(ms + EPS) * gain
    return y * (1.0 + scale) + shift


def _ada_kernel(c_ref, w_ref, b_ref, o_ref):
    c = c_ref[...]
    s = (c * jax.nn.sigmoid(c)).astype(BF16)
    o_ref[...] = jnp.dot(s, w_ref[...].astype(BF16), preferred_element_type=F32) + b_ref[...]


def _ada_params(cvec, ada_w, ada_b):
    depth, d, n = ada_w.shape
    mp = cvec.shape[0]
    tn = 1024
    return pl.pallas_call(
        _ada_kernel,
        grid=(depth, n // tn),
        in_specs=[
            pl.BlockSpec((mp, d), lambda l, j: (0, 0)),
            pl.BlockSpec((None, d, tn), lambda l, j: (l, 0, j)),
            pl.BlockSpec((None, 1, tn), lambda l, j: (l, 0, j)),
        ],
        out_specs=pl.BlockSpec((None, mp, tn), lambda l, j: (l, 0, j)),
        out_shape=jax.ShapeDtypeStruct((depth, mp, n), F32),
        compiler_params=_cparams(("arbitrary", "arbitrary")),
        name="ada_params",
    )(cvec, ada_w, ada_b.reshape(depth, 1, n))


def _swap16(x):
    width = x.shape[-1]
    lane = lax.broadcasted_iota(jnp.int32, x.shape, 1)
    return jnp.where((lane % 32) < 16, pltpu.roll(x, width - 16, 1), pltpu.roll(x, 16, 1))


def _proj_even_kernel(xp_ref, xs_ref, mod_ref, g_ref, w_ref, cos_ref, sin_ref, o_ref, kv_ref, h_scr, *,
                      n_prompt_blocks):
    x = jnp.where(pl.program_id(0) < n_prompt_blocks, xp_ref[...], xs_ref[...])
    h_scr[...] = _rms_mod(x, g_ref[...], mod_ref[1:2, :], mod_ref[0:1, :]).astype(BF16)
    kv_tile = ATTN_WIDTH // TN_PROJ
    for j in range(w_ref.shape[1] // TN_PROJ):
        cols = slice(j * TN_PROJ, (j + 1) * TN_PROJ)
        acc = jnp.dot(h_scr[...], w_ref[:, cols], preferred_element_type=F32)
        if j <= kv_tile:
            tab = slice(0, TN_PROJ) if j < kv_tile else slice(TN_PROJ, 2 * TN_PROJ)
            o_ref[:, cols] = (acc * cos_ref[:, tab] + _swap16(acc) * sin_ref[:, tab]).astype(BF16)
            if j == kv_tile:
                kv_ref[...] = acc
        else:
            o_ref[:, cols] = jax.nn.gelu(acc).astype(BF16)


def _proj_even(x_p, x_s, mods, gain, w, cos_t, sin_t, blocks_per_seq):
    d = x_p.shape[1]
    n = w.shape[1]
    tm, tn = TM_PROJ, TN_PROJ
    npb = x_p.shape[0] // tm
    nsb = x_s.shape[0] // tm
    t = x_p.shape[0] + x_s.shape[0]

    def mod_idx(i):
        return (jnp.where(i < npb, 0, 1 + (i - npb) // blocks_per_seq), 0, 0)

    def tab_idx(i):
        return (jnp.where(i < npb, blocks_per_seq, (i - npb) % blocks_per_seq), 0)

    return pl.pallas_call(
        functools.partial(_proj_even_kernel, n_prompt_blocks=npb),
        grid=(npb + nsb,),
        in_specs=[
            pl.BlockSpec((tm, d), lambda i: (jnp.minimum(i, npb - 1), 0)),
            pl.BlockSpec((tm, d), lambda i: (jnp.maximum(i - npb, 0), 0)),
            pl.BlockSpec((None, 6, d), mod_idx),
            pl.BlockSpec((1, d), lambda i: (0, 0)),
            pl.BlockSpec((d, n), lambda i: (0, 0), pipeline_mode=pl.Buffered(1)),
            pl.BlockSpec((tm, 2 * tn), tab_idx),
            pl.BlockSpec((tm, 2 * tn), tab_idx),
        ],
        out_specs=[
            pl.BlockSpec((tm, n), lambda i: (i, 0)),
            pl.BlockSpec((tm, tn), lambda i: (i, 0)),
        ],
        out_shape=[jax.ShapeDtypeStruct((t, n), BF16), jax.ShapeDtypeStruct((t, tn), F32)],
        scratch_shapes=[pltpu.VMEM((tm, d), BF16)],
        compiler_params=_cparams(("arbitrary",)),
        name="proj_even",
    )(x_p, x_s, mods, gain, w, cos_t, sin_t)


def _rope_tables(dec_seq, tm):
    f32 = np.float32
    n = np.arange(dec_seq)
    row = (n // GRID_W).astype(f32)
    col = (n % GRID_W).astype(f32)
    nf = HEAD_DIM // 4
    inv_freq = np.power(f32(ROPE_BASE), -np.arange(nf, dtype=f32) / f32(nf)).astype(f32)
    ar = row[:, None] * inv_freq[None, :]
    ac = col[:, None] * inv_freq[None, :]
    cos_h = np.concatenate([np.cos(ar), np.cos(ar), np.cos(ac), np.cos(ac)], axis=-1).astype(f32)
    sin_h = np.concatenate([-np.sin(ar), np.sin(ar), -np.sin(ac), np.sin(ac)], axis=-1).astype(f32)
    scale = f32(HEAD_DIM ** -0.5)
    ones = np.ones((dec_seq, KV_WIDTH), f32)
    zeros = np.zeros((dec_seq, KV_WIDTH), f32)
    q_heads = TN_PROJ // HEAD_DIM
    cos_t = np.concatenate([np.tile(cos_h, (1, q_heads)) * scale, np.tile(cos_h, (1, N_KV_HEADS)), ones], axis=-1)
    sin_t = np.concatenate([np.tile(sin_h, (1, q_heads)) * scale, np.tile(sin_h, (1, N_KV_HEADS)), zeros], axis=-1)
    id_cos = np.concatenate([np.full((tm, TN_PROJ), scale, f32), np.ones((tm, TN_PROJ), f32)], axis=-1)
    id_sin = np.zeros((tm, 2 * TN_PROJ), f32)
    return (jnp.asarray(np.concatenate([cos_t, id_cos], axis=0), F32),
            jnp.asarray(np.concatenate([sin_t, id_sin], axis=0), F32))


def _cast_plan(weights, layer, n_steps, step_of):
    n_exp = weights[0].shape[1]
    steps_per_matrix = min(n_steps // len(weights), n_exp)
    assert steps_per_matrix >= 1 and n_exp % steps_per_matrix == 0
    e_blk = n_exp // steps_per_matrix

    def slice_of(k, idx):
        return jnp.clip(step_of(*idx) - k * steps_per_matrix, 0, steps_per_matrix - 1)

    in_specs = [pl.BlockSpec((None, e_blk) + w.shape[2:], lambda *idx, k=k: (layer, slice_of(k, idx), 0, 0))
                for k, w in enumerate(weights)]
    out_specs = [pl.BlockSpec((e_blk,) + w.shape[2:], lambda *idx, k=k: (slice_of(k, idx), 0, 0))
                 for k, w in enumerate(weights)]
    out_shapes = [jax.ShapeDtypeStruct(w.shape[1:], BF16) for w in weights]
    return steps_per_matrix, in_specs, out_specs, out_shapes


def _cast_slices(step, steps_per_matrix, srcs, dsts):
    for k, (src, dst) in enumerate(zip(srcs, dsts)):
        @pl.when((step >= k * steps_per_matrix) & (step < (k + 1) * steps_per_matrix))
        def _(src=src, dst=dst):
            dst[...] = src[...].astype(BF16)


def _sink_attend(q, keys, vals, masks, sink):
    scores = []
    for k, mask in zip(keys, masks):
        s = lax.dot_general(q, k, (((1,), (1,)), ((), ())), preferred_element_type=F32)
        if mask is not None:
            stacked = s.reshape(s.shape[0] // mask.shape[0], *mask.shape)
            s = jnp.where(mask[None], stacked, NEG_INF).reshape(s.shape)
        scores.append(s)
    m = sink
    for s in scores:
        m = jnp.maximum(m, jnp.max(s, axis=-1, keepdims=True))
    es = [jnp.exp(s - m) for s in scores]
    den = jnp.exp(sink - m)
    for e in es:
        den = den + jnp.sum(e, axis=-1, keepdims=True)
    out = None
    for e, v in zip(es, vals):
        o = jnp.dot(e.astype(BF16), v, preferred_element_type=F32)
        out = o if out is None else out + o
    return out * (1.0 / den)


def _grouped_heads_attend(sink_ref, q_ref, o_ref, kh, keys, vals, masks):
    rows = q_ref.shape[0]
    heads = [kh * Q_GROUPS + g for g in range(Q_GROUPS)]
    q = jnp.concatenate([q_ref[:, h * HEAD_DIM:(h + 1) * HEAD_DIM] for h in heads], axis=0)
    ridx = lax.broadcasted_iota(jnp.int32, (Q_GROUPS * rows, 1), 0)
    sink = jnp.full((Q_GROUPS * rows, 1), sink_ref[heads[-1]], F32)
    for g in range(Q_GROUPS - 2, -1, -1):
        sink = jnp.where(ridx < (g + 1) * rows, sink_ref[heads[g]], sink)
    o = _sink_attend(q, keys, vals, masks, sink)
    for g, h in enumerate(heads):
        o_ref[:, h * HEAD_DIM:(h + 1) * HEAD_DIM] = o[g * rows:(g + 1) * rows, :].astype(BF16)


def _ctx_attn_kernel(sink_ref, q_ref, kv_ref, o_ref):
    for kh in range(N_KV_HEADS):
        k = kv_ref[:, kh * HEAD_DIM:(kh + 1) * HEAD_DIM]
        v = kv_ref[:, KV_WIDTH + kh * HEAD_DIM:KV_WIDTH + (kh + 1) * HEAD_DIM]
        _grouped_heads_attend(sink_ref, q_ref, o_ref, kh, [k], [v], [None])


def _ctx_attention(proj, sink, batch, seq, kv_col_block):
    return pl.pallas_call(
        _ctx_attn_kernel,
        grid=(batch,),
        in_specs=[
            pl.BlockSpec(memory_space=pltpu.SMEM),
            pl.BlockSpec((seq, ATTN_WIDTH), lambda b: (b, 0)),
            pl.BlockSpec((seq, 2 * KV_WIDTH), lambda b: (b, kv_col_block)),
        ],
        out_specs=pl.BlockSpec((seq, ATTN_WIDTH), lambda b: (b, 0)),
        out_shape=jax.ShapeDtypeStruct((batch * seq, ATTN_WIDTH), BF16),
        compiler_params=_cparams(("arbitrary",)),
        name="ctx_attention",
    )(sink, proj, proj)


def _lat_attn_kernel(sink_ref, q_ref, kv_ref, ck_ref, cv_ref, wg_ref, wu_ref, wd_ref, o_ref, wg_out, wu_out, wd_out,
                     *, seq, steps_per_matrix):
    i = pl.program_id(1)
    _cast_slices(pl.program_id(0) * pl.num_programs(1) + i, steps_per_matrix,
                 (wg_ref, wu_ref, wd_ref), (wg_out, wu_out, wd_out))
    win = 3 * ATTN_BLK
    ws = pl.multiple_of(jnp.clip((i - 1) * ATTN_BLK, 0, seq - win), ATTN_BLK)
    shape = (ATTN_BLK, win)
    qpos = i * ATTN_BLK + lax.broadcasted_iota(jnp.int32, shape, 0)
    kpos = ws + lax.broadcasted_iota(jnp.int32, shape, 1)
    valid = jnp.abs(qpos - kpos) <= ATTN_BLK
    kvw = kv_ref[pl.ds(ws, win), :]
    for kh in range(N_KV_HEADS):
        k_loc = kvw[:, kh * HEAD_DIM:(kh + 1) * HEAD_DIM]
        v_loc = kvw[:, KV_WIDTH + kh * HEAD_DIM:KV_WIDTH + (kh + 1) * HEAD_DIM]
        k_ctx = ck_ref[:, kh * HEAD_DIM:(kh + 1) * HEAD_DIM]
        v_ctx = cv_ref[:, kh * HEAD_DIM:(kh + 1) * HEAD_DIM]
        _grouped_heads_attend(sink_ref, q_ref, o_ref, kh, [k_ctx, k_loc], [v_ctx, v_loc], [None, valid])


def _lat_attention(proj, sink, cache_k, cache_v, n_prompt, dec_batch, dec_seq, kv_col_block, weights, layer):
    qb = dec_seq // ATTN_BLK
    q_off = n_prompt // ATTN_BLK
    s_off = n_prompt // dec_seq
    past = cache_k.shape[1]
    spm, w_in, w_out, w_shapes = _cast_plan(weights, layer, dec_batch * qb, lambda b, i: b * qb + i)
    return pl.pallas_call(
        functools.partial(_lat_attn_kernel, seq=dec_seq, steps_per_matrix=spm),
        grid=(dec_batch, qb),
        in_specs=[
            pl.BlockSpec(memory_space=pltpu.SMEM),
            pl.BlockSpec((ATTN_BLK, ATTN_WIDTH), lambda b, i: (q_off + b * qb + i, 0)),
            pl.BlockSpec((dec_seq, 2 * KV_WIDTH), lambda b, i: (s_off + b, kv_col_block)),
            pl.BlockSpec((None, past, KV_WIDTH), lambda b, i: (b, 0, 0)),
            pl.BlockSpec((None, past, KV_WIDTH), lambda b, i: (b, 0, 0)),
        ] + w_in,
        out_specs=[pl.BlockSpec((ATTN_BLK, ATTN_WIDTH), lambda b, i: (b * qb + i, 0))] + w_out,
        out_shape=[jax.ShapeDtypeStruct((dec_batch * dec_seq, ATTN_WIDTH), BF16)] + w_shapes,
        compiler_params=_cparams(("arbitrary", "arbitrary")),
        name="lat_attention",
    )(sink, proj, proj, cache_k, cache_v, *weights)


def _sgu_kernel(ul_ref, uh_ref, vl_ref, vh_ref, g_ref, w_ref, bt_ref, z_ref):
    half = SGU_WIDTH // 2
    per_half = SGU_GROUPS // 2
    u_refs = (ul_ref, uh_ref)
    for c in range(TM_SGU // CHUNK):
        rows = slice(c * CHUNK, (c + 1) * CHUNK)
        vl = vl_ref[rows, :].astype(F32)
        vh = vh_ref[rows, :].astype(F32)
        ssq = jnp.sum(vl * vl, axis=-1, keepdims=True) + jnp.sum(vh * vh, axis=-1, keepdims=True)
        r = lax.rsqrt(ssq * (1.0 / SGU_WIDTH) + EPS)
        vn = ((vl * r * g_ref[:, :half]).astype(BF16), (vh * r * g_ref[:, half:]).astype(BF16))
        for g in range(SGU_GROUPS):
            src = g // per_half
            cols = slice((g % per_half) * SGU_GROUP_DIM, (g % per_half + 1) * SGU_GROUP_DIM)
            mixed = jnp.dot(w_ref[g], vn[src][:, cols], preferred_element_type=F32) + bt_ref[:, g:g + 1]
            z_ref[rows, g * SGU_GROUP_DIM:(g + 1) * SGU_GROUP_DIM] = (
                u_refs[src][rows, cols].astype(F32) * mixed).astype(BF16)


def _sgu(proj, norm_g, w_s, b_t):
    t = proj.shape[0]
    half = SGU_WIDTH // 2
    u0 = (ATTN_WIDTH + 2 * KV_WIDTH) // half
    blocks = [pl.BlockSpec((TM_SGU, half), functools.partial(lambda i, c: (i, c), c=u0 + k)) for k in range(4)]
    return pl.pallas_call(
        _sgu_kernel,
        grid=(t // TM_SGU,),
        in_specs=blocks + [
            pl.BlockSpec((1, SGU_WIDTH), lambda i: (0, 0)),
            pl.BlockSpec((SGU_GROUPS, CHUNK, CHUNK), lambda i: (0, 0, 0)),
            pl.BlockSpec((CHUNK, SGU_GROUPS), lambda i: (0, 0)),
        ],
        out_specs=pl.BlockSpec((TM_SGU, SGU_WIDTH), lambda i: (i, 0)),
        out_shape=jax.ShapeDtypeStruct((t, SGU_WIDTH), BF16),
        compiler_params=_cparams(("arbitrary",)),
        name="sgu",
    )(proj, proj, proj, proj, norm_g, w_s, b_t)


def _residual_router(rows, x, acc, mod_ref, g2_ref, wrt_ref, brt_ref, x1_ref, h2_ref, lg_ref):
    x1 = x + mod_ref[2:3, :] * acc
    x1_ref[rows, :] = x1
    h2 = _rms_mod(x1, g2_ref[...], mod_ref[4:5, :], mod_ref[3:4, :])
    h2_ref[rows, :] = _pack_bf16_pairs(h2)
    lg = lax.dot_general(wrt_ref[...], h2.astype(BF16), (((1,), (1,)), ((), ())), preferred_element_type=F32)
    lg_ref[:, rows] = lg + brt_ref[...]


def _sub_blocks(ref):
    return [slice(r, r + TM_OUT) for r in range(0, ref.shape[0], TM_OUT)]


def _out_even_kernel(xp_ref, xs_ref, ap_ref, as_ref, z_ref, w_ref, mod_ref, g2_ref, wrt_ref, brt_ref,
                     x1_ref, h2_ref, lg_ref, *, n_prompt_blocks):
    is_prompt = pl.program_id(0) < n_prompt_blocks
    for rows in _sub_blocks(x1_ref):
        a = jnp.where(is_prompt, ap_ref[rows, :].astype(F32), as_ref[rows, :].astype(F32)).astype(BF16)
        acc = jnp.dot(a, w_ref[0:ATTN_WIDTH, :], preferred_element_type=F32)
        acc = acc + jnp.dot(z_ref[rows, :], w_ref[ATTN_WIDTH:, :], preferred_element_type=F32)
        x = jnp.where(is_prompt, xp_ref[rows, :], xs_ref[rows, :])
        _residual_router(rows, x, acc, mod_ref, g2_ref, wrt_ref, brt_ref, x1_ref, h2_ref, lg_ref)


def _out_odd_kernel(x_ref, hc_ref, w_ref, mod_ref, g2_ref, wrt_ref, brt_ref, x1_ref, h2_ref, lg_ref):
    for rows in _sub_blocks(x1_ref):
        acc = jnp.dot(hc_ref[rows, :], w_ref[...], preferred_element_type=F32)
        _residual_router(rows, x_ref[rows, :], acc, mod_ref, g2_ref, wrt_ref, brt_ref, x1_ref, h2_ref, lg_ref)


def _out_proj(xs, lhs, w, mods, gain2, wrt, brt, n_prompt, dec_seq):
    d = xs[0].shape[1]
    t = sum(x.shape[0] for x in xs)
    tm = TM_OUTPROJ
    npb = n_prompt // tm
    bps = dec_seq // tm
    nsb = t // tm - npb
    if len(xs) == 2:
        x_specs = [pl.BlockSpec((tm, d), lambda i: (jnp.minimum(i, npb - 1), 0)),
                   pl.BlockSpec((tm, d), lambda i: (jnp.maximum(i - npb, 0), 0))]
    else:
        x_specs = [pl.BlockSpec((tm, d), lambda i: (i, 0))]

    def mod_idx(i):
        return (jnp.where(i < npb, 0, 1 + (i - npb) // bps), 0, 0)

    if len(lhs) == 3:
        a_p, a_s, z = lhs
        body = functools.partial(_out_even_kernel, n_prompt_blocks=npb)
        lhs_specs = [
            pl.BlockSpec((tm, ATTN_WIDTH), lambda i: (jnp.minimum(i, npb - 1), 0)),
            pl.BlockSpec((tm, ATTN_WIDTH), lambda i: (jnp.clip(i - npb, 0, nsb - 1), 0)),
            pl.BlockSpec((tm, SGU_WIDTH), lambda i: (i, 0)),
        ]
    else:
        body = _out_odd_kernel
        lhs_specs = [pl.BlockSpec((tm, d), lambda i: (i, 0))]
    return pl.pallas_call(
        body,
        grid=(t // tm,),
        in_specs=x_specs + lhs_specs + [
            pl.BlockSpec((w.shape[0], d), lambda i: (0, 0), pipeline_mode=pl.Buffered(1)),
            pl.BlockSpec((None, 6, d), mod_idx),
            pl.BlockSpec((1, d), lambda i: (0, 0)),
            pl.BlockSpec((ROUTER_LANES, d), lambda i: (0, 0)),
            pl.BlockSpec((ROUTER_LANES, 1), lambda i: (0, 0)),
        ],
        out_specs=[
            pl.BlockSpec((tm, d), lambda i: (i, 0)),
            pl.BlockSpec((tm, d // 2), lambda i: (i, 0)),
            pl.BlockSpec((ROUTER_LANES, tm), lambda i: (0, i)),
        ],
        out_shape=[
            jax.ShapeDtypeStruct((t, d), F32),
            jax.ShapeDtypeStruct((t, d // 2), jnp.uint32),
            jax.ShapeDtypeStruct((ROUTER_LANES, t), F32),
        ],
        compiler_params=_cparams(("arbitrary",)),
        name="out_proj",
    )(*xs, *lhs, w, mods, gain2, wrt, brt)


def _first_argmax4(v0, v1, v2, v3):
    m = jnp.maximum(jnp.maximum(v0, v1), jnp.maximum(v2, v3))
    idx = jnp.where(v0 == m, 0.0, jnp.where(v1 == m, 1.0, jnp.where(v2 == m, 2.0, 3.0)))
    return m, idx


def _route_kernel(lg_ref, pos_ref, ea_ref, eb_ref, nv_ref, ends_ref, oh_scr, rank_scr, *, n_tokens):
    lg = lg_ref[...]
    rows = [lg[r:r + 1, :] for r in range(N_GROUPS_MOE + N_EXPERTS)]
    _, gidx = _first_argmax4(*rows[:N_GROUPS_MOE])
    e = []
    for k in range(EXPERTS_PER_GROUP):
        cand = [rows[N_GROUPS_MOE + g * EXPERTS_PER_GROUP + k] for g in range(N_GROUPS_MOE)]
        e.append(jnp.where(gidx == 0.0, cand[0], jnp.where(gidx == 1.0, cand[1], jnp.where(gidx == 2.0, cand[2], cand[3]))))
    _, l1 = _first_argmax4(*e)
    e2 = [jnp.where(l1 == float(k), -jnp.inf, e[k]) for k in range(EXPERTS_PER_GROUP)]
    _, l2 = _first_argmax4(*e2)
    lo = jnp.minimum(l1, l2)
    hi = jnp.maximum(l1, l2)
    pair = jnp.where(lo == 0.0, hi - 1.0, jnp.where(lo == 2.0, 4.0, jnp.where(hi == 3.0, 3.0, 5.0)))
    cls = gidx * float(PAIRS_PER_GROUP) + pair

    crow = lax.broadcasted_iota(jnp.int32, (CLASS_ROWS, n_tokens), 0).astype(F32)
    oh_scr[...] = jnp.where(crow == cls, 1.0, 0.0).astype(BF16)

    jj = lax.broadcasted_iota(jnp.int32, (CUM_CHUNK, CUM_CHUNK), 0)
    ii = lax.broadcasted_iota(jnp.int32, (CUM_CHUNK, CUM_CHUNK), 1)
    tri = jnp.where(jj < ii, 1.0, 0.0).astype(BF16)

    def chunk(c, carry):
        off = pl.multiple_of(c * CUM_CHUNK, CUM_CHUNK)
        oh = oh_scr[:, pl.ds(off, CUM_CHUNK)]
        ohf = oh.astype(F32)
        cum = jnp.dot(oh, tri, preferred_element_type=F32) + carry
        rank_scr[:, pl.ds(off, CUM_CHUNK)] = jnp.sum(cum * ohf, axis=0, keepdims=True)
        return carry + jnp.sum(ohf, axis=1, keepdims=True)

    counts = lax.fori_loop(0, n_tokens // CUM_CHUNK, chunk, jnp.zeros((CLASS_ROWS, 1), F32))
    padded = jnp.floor((counts + float(MOE_BLK - 1)) * (1.0 / MOE_BLK)) * float(MOE_BLK)
    run = jnp.zeros((1, 1), F32)
    starts_rows = []
    for c in range(CLASS_ROWS):
        starts_rows.append(run)
        run = run + padded[c:c + 1, :]
    starts = jnp.concatenate(starts_rows, axis=0)
    ends = starts + padded
    total = run

    ohf = oh_scr[...].astype(F32)
    pos = rank_scr[...] + jnp.sum(ohf * starts, axis=0, keepdims=True)
    pos_ref[...] = pos.astype(jnp.int32)

    bstart = lax.broadcasted_iota(jnp.int32, (CLASS_ROWS, MAX_MOE_BLOCKS), 1).astype(F32) * float(MOE_BLK)
    brow = lax.broadcasted_iota(jnp.int32, (CLASS_ROWS, MAX_MOE_BLOCKS), 0)
    done = jnp.where((ends <= bstart) & (brow < N_CLASSES), 1.0, 0.0)
    bcls = jnp.minimum(jnp.sum(done, axis=0, keepdims=True), float(N_CLASSES - 1))
    grp = (jnp.where(bcls >= 6.0, 1.0, 0.0) + jnp.where(bcls >= 12.0, 1.0, 0.0) + jnp.where(bcls >= 18.0, 1.0, 0.0))
    bp = bcls - grp * float(PAIRS_PER_GROUP)
    first = jnp.where(bp >= 3.0, 1.0, 0.0) + jnp.where(bp >= 4.0, 1.0, 0.0)
    second = jnp.where(bp < 3.0, bp + 1.0, jnp.where(bp < 5.0, 3.0, 1.0))
    ea_ref[...] = (grp * float(EXPERTS_PER_GROUP) + first).astype(jnp.int32)
    eb_ref[...] = (grp * float(EXPERTS_PER_GROUP) + second).astype(jnp.int32)
    own = jnp.where(brow.astype(F32) == bcls, starts + counts, 0.0)
    filled = jnp.sum(own, axis=0, keepdims=True) - bstart[0:1, :]
    nv_ref[...] = jnp.clip(filled, 0.0, float(MOE_BLK)).astype(jnp.int32)
    lane = lax.broadcasted_iota(jnp.int32, (1, MAX_MOE_BLOCKS), 1)
    ends_row = jnp.zeros((1, MAX_MOE_BLOCKS), F32)
    for c in range(N_CLASSES):
        ends_row = jnp.where(lane == c, starts_rows[c] + padded[c:c + 1, :], ends_row)
    ends_ref[...] = ends_row.astype(jnp.int32)


def _route(logits_t):
    t = logits_t.shape[1]
    rows = N_CLASSES
    i32 = jnp.int32
    return pl.pallas_call(
        functools.partial(_route_kernel, n_tokens=t),
        grid=(1,),
        in_specs=[pl.BlockSpec((rows, t), lambda i: (0, 0))],
        out_specs=[
            pl.BlockSpec((1, t), lambda i: (0, 0)),
        ] + [pl.BlockSpec((1, MAX_MOE_BLOCKS), lambda i: (0, 0))] * 4,
        out_shape=[jax.ShapeDtypeStruct((1, t), i32)] + [jax.ShapeDtypeStruct((1, MAX_MOE_BLOCKS), i32)] * 4,
        scratch_shapes=[pltpu.VMEM((CLASS_ROWS, t), BF16), pltpu.VMEM((1, t), F32)],
        compiler_params=_cparams(("arbitrary",)),
        name="route",
    )(logits_t)


DMA_THREADS = 2


def _row_copy(src_ref, src_row, dst_ref, dst_row, sem):
    return pltpu.make_async_copy(src_ref.at[pl.ds(src_row, 1), :], dst_ref.at[pl.ds(dst_row, 1), :], sem)


def _start_row_copies(n_rows, make_copy):
    for r in range(n_rows):
        make_copy(r).start(priority=r % DMA_THREADS)


def _dispatch_kernel(pos_ref, ends_ref, h_ref, o_ref, zbuf, sem, zsem):
    step = pl.program_id(0)

    @pl.when(step == 0)
    def _():
        zbuf[...] = jnp.zeros_like(zbuf)
        total = ends_ref[N_CLASSES - 1]
        n_free = (o_ref.shape[0] - total) // MOE_BLK

        def zero_block(first_row, phase):
            first = pl.multiple_of(first_row, MOE_BLK)
            getattr(pltpu.make_async_copy(zbuf, o_ref.at[pl.ds(first, MOE_BLK), :], zsem), phase)()

        for phase in ("start", "wait"):
            prev = 0
            for c in range(N_CLASSES):
                end = ends_ref[c]
                pl.when(end > prev)(functools.partial(zero_block, end - MOE_BLK, phase))
                prev = end

            def free_block(k, carry, phase=phase):
                zero_block(total + k * MOE_BLK, phase)
                return carry

            lax.fori_loop(0, n_free, free_block, 0)

    rows = h_ref.shape[0]
    base = step * rows

    _start_row_copies(rows, lambda r: _row_copy(h_ref, r, o_ref, pos_ref[base + r], sem))
    pltpu.make_async_copy(h_ref, o_ref.at[pl.ds(0, rows), :], sem).wait()


def _dispatch(pos, ends, h2, n_rows):
    t, d = h2.shape
    return pl.pallas_call(
        _dispatch_kernel,
        grid_spec=pltpu.PrefetchScalarGridSpec(
            num_scalar_prefetch=2,
            grid=(t // TM_DISPATCH,),
            in_specs=[pl.BlockSpec((TM_DISPATCH, d), lambda i, pos, ends: (i, 0))],
            out_specs=pl.BlockSpec(memory_space=pl.ANY),
            scratch_shapes=[pltpu.VMEM((MOE_BLK, d), h2.dtype), pltpu.SemaphoreType.DMA(()),
                            pltpu.SemaphoreType.DMA(())],
        ),
        out_shape=jax.ShapeDtypeStruct((n_rows, d), h2.dtype),
        compiler_params=_cparams(("arbitrary",)),
        name="moe_dispatch",
    )(pos, ends, h2)


def _expert_ffn(x, wg_ref, wu_ref, wd_ref):
    gate = jnp.dot(x, wg_ref[...], preferred_element_type=F32)
    up = jnp.dot(x, wu_ref[...], preferred_element_type=F32)
    hidden = (gate * jax.nn.sigmoid(gate) * up).astype(BF16)
    return jnp.dot(hidden, wd_ref[...], preferred_element_type=F32)


def _moe_kernel(ea_ref, eb_ref, nv_ref, x_ref, wr_ref, br_ref, wga, wua, wda, wgb, wub, wdb, y_ref):
    b = pl.program_id(0)
    n_valid = nv_ref[b]

    def run(rows):
        x = _unpack_bf16_pairs(x_ref[rows, :])
        lg = jnp.dot(x, wr_ref[...], preferred_element_type=F32) + br_ref[...]
        lane = lax.broadcasted_iota(jnp.int32, lg.shape, 1)
        ea = ea_ref[b]
        eb = eb_ref[b]
        grp = ea // EXPERTS_PER_GROUP
        is_grp = lane < N_GROUPS_MOE
        gm = jnp.max(jnp.where(is_grp, lg, -jnp.inf), axis=-1, keepdims=True)
        ge = jnp.where(is_grp, jnp.exp(lg - gm), 0.0)
        p_g = jnp.sum(jnp.where(lane == grp, ge, 0.0), axis=-1, keepdims=True) / jnp.sum(ge, axis=-1, keepdims=True)
        la = jnp.sum(jnp.where(lane == N_GROUPS_MOE + ea, lg, 0.0), axis=-1, keepdims=True)
        lb = jnp.sum(jnp.where(lane == N_GROUPS_MOE + eb, lg, 0.0), axis=-1, keepdims=True)
        mm = jnp.maximum(la, lb)
        xa = jnp.exp(la - mm)
        xb = jnp.exp(lb - mm)
        ga = xa / (xa + xb) * p_g
        gb = xb / (xa + xb) * p_g
        ya = _expert_ffn(x, wga, wua, wda)
        yb = _expert_ffn(x, wgb, wub, wdb)
        y_ref[rows, :] = ya * ga + yb * gb

    half = MOE_BLK // 2

    @pl.when(n_valid > half)
    def _():
        run(slice(0, MOE_BLK))

    @pl.when((n_valid > 0) & (n_valid <= half))
    def _():
        run(slice(0, half))
        y_ref[half:, :] = jnp.zeros((MOE_BLK - half, y_ref.shape[1]), y_ref.dtype)

    @pl.when(n_valid == 0)
    def _():
        y_ref[...] = jnp.zeros_like(y_ref)


def _moe_experts(ea, eb, n_valid, hs, wr, br, w_gate, w_up, w_down, n_blocks):
    d = w_gate.shape[1]
    de = w_gate.shape[2]

    def wa_idx(b, ea, eb, nv):
        return (ea[b], 0, 0)

    def wb_idx(b, ea, eb, nv):
        return (eb[b], 0, 0)

    return pl.pallas_call(
        _moe_kernel,
        grid_spec=pltpu.PrefetchScalarGridSpec(
            num_scalar_prefetch=3,
            grid=(n_blocks,),
            in_specs=[
                pl.BlockSpec((MOE_BLK, hs.shape[1]), lambda b, ea, eb, nv: (jnp.where(nv[b] > 0, b, 0), 0)),
                pl.BlockSpec((d, ROUTER_LANES), lambda b, ea, eb, nv: (0, 0)),
                pl.BlockSpec((1, ROUTER_LANES), lambda b, ea, eb, nv: (0, 0)),
                pl.BlockSpec((None, d, de), wa_idx),
                pl.BlockSpec((None, d, de), wa_idx),
                pl.BlockSpec((None, de, d), wa_idx),
                pl.BlockSpec((None, d, de), wb_idx),
                pl.BlockSpec((None, d, de), wb_idx),
                pl.BlockSpec((None, de, d), wb_idx),
            ],
            out_specs=pl.BlockSpec((MOE_BLK, d), lambda b, ea, eb, nv: (b, 0)),
        ),
        out_shape=jax.ShapeDtypeStruct((n_blocks * MOE_BLK, d), F32),
        compiler_params=_cparams(("arbitrary",)),
        name="moe_experts",
    )(ea, eb, n_valid, hs, wr, br, w_gate, w_up, w_down, w_gate, w_up, w_down)


def _combine_kernel(pos_ref, ys_ref, x1_ref, mod_ref, gf_ref, o_ref, ybuf, sem, *, block_off, n_steps, final):
    i = pl.program_id(0)
    slot = i % 2
    tm = x1_ref.shape[0]

    def start_gather(step, into):
        base = (step + block_off) * tm
        _start_row_copies(tm, lambda r: _row_copy(ys_ref, pos_ref[base + r], ybuf.at[into], r, sem.at[into]))

    pl.when(i == 0)(lambda: start_gather(0, 0))
    pl.when(i + 1 < n_steps)(lambda: start_gather(i + 1, 1 - slot))
    pltpu.make_async_copy(ys_ref.at[pl.ds(0, tm), :], ybuf.at[slot], sem.at[slot]).wait()
    x2 = x1_ref[...] + mod_ref[5:6, :] * ybuf[slot]
    if final:
        ms = jnp.mean(x2 * x2, axis=-1, keepdims=True)
        x2 = x2 * lax.rsqrt(ms + EPS) * gf_ref[...]
    o_ref[...] = x2


def _combine(pos, ys, x1, mods, gain_f, n_prompt, dec_seq, row_off, n_rows, final):
    d = x1.shape[1]
    tm = TM_COMBINE
    npb = n_prompt // tm
    bps = dec_seq // tm
    boff = row_off // tm

    def mod_idx(i, pos):
        blk = i + boff
        return (jnp.where(blk < npb, 0, 1 + (blk - npb) // bps), 0, 0)

    return pl.pallas_call(
        functools.partial(_combine_kernel, block_off=boff, n_steps=n_rows // tm, final=final),
        grid_spec=pltpu.PrefetchScalarGridSpec(
            num_scalar_prefetch=1,
            grid=(n_rows // tm,),
            in_specs=[
                pl.BlockSpec(memory_space=pl.ANY),
                pl.BlockSpec((tm, d), lambda i, pos: (i + boff, 0)),
                pl.BlockSpec((None, 6, d), mod_idx),
                pl.BlockSpec((1, d), lambda i, pos: (0, 0)),
            ],
            out_specs=pl.BlockSpec((tm, d), lambda i, pos: (i, 0)),
            scratch_shapes=[pltpu.VMEM((2, tm, d), F32), pltpu.SemaphoreType.DMA((2,))],
        ),
        out_shape=jax.ShapeDtypeStruct((n_rows, d), F32),
        compiler_params=_cparams(("arbitrary",)),
        name="moe_combine",
    )(pos, ys, x1, mods, gain_f)


def _proj_glu_kernel(pos_ref, ys_ref, x1_ref, mod_prev_ref, mod_ref, g_ref, w_ref, o_ref, x2_ref, h_scr, ybuf, sem,
                     *, n_steps):
    i = pl.program_id(0)
    slot = i % 2
    tm = x1_ref.shape[0]

    def start_gather(step, into):
        base = step * tm
        _start_row_copies(tm, lambda r: _row_copy(ys_ref, pos_ref[base + r], ybuf.at[into], r, sem.at[into]))

    pl.when(i == 0)(lambda: start_gather(0, 0))
    pl.when(i + 1 < n_steps)(lambda: start_gather(i + 1, 1 - slot))
    pltpu.make_async_copy(ys_ref.at[pl.ds(0, tm), :], ybuf.at[slot], sem.at[slot]).wait()
    x2 = x1_ref[...] + mod_prev_ref[5:6, :] * ybuf[slot]
    x2_ref[...] = x2
    h_scr[...] = _rms_mod(x2, g_ref[...], mod_ref[1:2, :], mod_ref[0:1, :]).astype(BF16)
    width = w_ref.shape[1] // 2
    for j in range(width // TN_PROJ):
        cols = slice(j * TN_PROJ, (j + 1) * TN_PROJ)
        gcols = slice(width + j * TN_PROJ, width + (j + 1) * TN_PROJ)
        val = jnp.dot(h_scr[...], w_ref[:, cols], preferred_element_type=F32)
        gate = jnp.dot(h_scr[...], w_ref[:, gcols], preferred_element_type=F32)
        o_ref[:, cols] = (val * jax.nn.sigmoid(gate)).astype(BF16)


def _proj_glu(pos, ys, x1, mods_prev, mods, gain, w, n_prompt, dec_seq):
    t, d = x1.shape
    n = w.shape[1]
    tm = TM_PROJ
    npb = n_prompt // tm
    bps = dec_seq // tm

    def mod_idx(i, pos):
        return (jnp.where(i < npb, 0, 1 + (i - npb) // bps), 0, 0)

    return pl.pallas_call(
        functools.partial(_proj_glu_kernel, n_steps=t // tm),
        grid_spec=pltpu.PrefetchScalarGridSpec(
            num_scalar_prefetch=1,
            grid=(t // tm,),
            in_specs=[
                pl.BlockSpec(memory_space=pl.ANY),
                pl.BlockSpec((tm, d), lambda i, pos: (i, 0)),
                pl.BlockSpec((None, 6, d), mod_idx),
                pl.BlockSpec((None, 6, d), mod_idx),
                pl.BlockSpec((1, d), lambda i, pos: (0, 0)),
                pl.BlockSpec((d, n), lambda i, pos: (0, 0), pipeline_mode=pl.Buffered(1)),
            ],
            out_specs=[
                pl.BlockSpec((tm, n // 2), lambda i, pos: (i, 0)),
                pl.BlockSpec((tm, d), lambda i, pos: (i, 0)),
            ],
            scratch_shapes=[pltpu.VMEM((tm, d), BF16), pltpu.VMEM((2, tm, d), F32), pltpu.SemaphoreType.DMA((2,))],
        ),
        out_shape=[jax.ShapeDtypeStruct((t, n // 2), BF16), jax.ShapeDtypeStruct((t, d), F32)],
        compiler_params=_cparams(("arbitrary",)),
        name="proj_glu",
    )(pos, ys, x1, mods_prev, mods, gain, w)


CONV_ROWS = 128
CONV_COLS = 128
F32_SUBLANES = 8


def _conv_kernel(x_ref, prev_ref, next_ref, dw_ref, b_ref, g_ref, wg_ref, wu_ref, wd_ref, o_ref, wg_out, wu_out, wd_out,
                 xpad, shifted, ybuf, *, n_prompt_blocks, blocks_per_seq, steps_per_matrix):
    i = pl.program_id(0)
    _cast_slices(i, steps_per_matrix, (wg_ref, wu_ref, wd_ref), (wg_out, wu_out, wd_out))
    in_sample = i >= n_prompt_blocks
    j = i - n_prompt_blocks
    is_start = jnp.logical_or(jnp.logical_not(in_sample), j % blocks_per_seq == 0)
    is_end = jnp.logical_or(jnp.logical_not(in_sample), j % blocks_per_seq == blocks_per_seq - 1)
    tm = TM_OUT
    xpad[0:CONV_HALO, :] = jnp.where(is_start, 0.0, prev_ref[...].astype(F32))
    xpad[CONV_HALO:CONV_HALO + tm, :] = x_ref[...].astype(F32)
    xpad[CONV_HALO + tm:, :] = jnp.where(is_end, 0.0, next_ref[...].astype(F32))
    first = CONV_HALO - CONV_K // 2
    span = shifted.shape[1]

    def col_chunk(c, ssq):
        c0 = pl.multiple_of(c * CONV_COLS, CONV_COLS)
        cols = pl.ds(c0, CONV_COLS)
        w = dw_ref[:, cols]
        bias = b_ref[:, cols]
        for s in range(1, F32_SUBLANES):
            shifted[s - 1] = xpad[s:s + span, cols]
        parts = []
        for rc in range(tm // CONV_ROWS):
            r0 = rc * CONV_ROWS
            acc = jnp.zeros((CONV_ROWS, CONV_COLS), F32)
            for k in range(CONV_K):
                whole, s = divmod(first + k, F32_SUBLANES)
                base = whole * F32_SUBLANES + r0
                if s == 0:
                    src = xpad[base:base + CONV_ROWS, cols]
                else:
                    src = shifted[s - 1, base:base + CONV_ROWS, :]
                acc = acc + w[k:k + 1, :] * src
            acc = acc + bias
            ybuf[r0:r0 + CONV_ROWS, cols] = acc
            parts.append(jnp.sum(acc * acc, axis=-1, keepdims=True))
        return ssq + jnp.concatenate(parts, axis=0)

    d = x_ref.shape[1]
    ssq = lax.fori_loop(0, d // CONV_COLS, col_chunk, jnp.zeros((tm, 1), F32))
    y = ybuf[...] * lax.rsqrt(ssq * (1.0 / d) + EPS) * g_ref[...]
    o_ref[...] = (y * jax.nn.sigmoid(y)).astype(BF16)


def _conv_mix(xg, dw, dw_b, norm_g, n_prompt, seq, dec_seq, weights, layer):
    t, d = xg.shape
    tm = TM_OUT
    assert seq == tm, "context sequences must be exactly one conv row block"
    npb = n_prompt // tm
    bps = dec_seq // tm
    hb = tm // CONV_HALO
    last = t // CONV_HALO - 1
    spm, w_in, w_out, w_shapes = _cast_plan(weights, layer, t // tm, lambda i: i)
    return pl.pallas_call(
        functools.partial(_conv_kernel, n_prompt_blocks=npb, blocks_per_seq=bps, steps_per_matrix=spm),
        grid=(t // tm,),
        in_specs=[
            pl.BlockSpec((tm, d), lambda i: (i, 0)),
            pl.BlockSpec((CONV_HALO, d), lambda i: (jnp.maximum(i * hb - 1, 0), 0)),
            pl.BlockSpec((CONV_HALO, d), lambda i: (jnp.minimum((i + 1) * hb, last), 0)),
            pl.BlockSpec((dw.shape[0], d), lambda i: (0, 0)),
            pl.BlockSpec((1, d), lambda i: (0, 0)),
            pl.BlockSpec((1, d), lambda i: (0, 0)),
        ] + w_in,
        out_specs=[pl.BlockSpec((tm, d), lambda i: (i, 0))] + w_out,
        out_shape=[jax.ShapeDtypeStruct((t, d), BF16)] + w_shapes,
        scratch_shapes=[
            pltpu.VMEM((tm + 2 * CONV_HALO, d), F32),
            pltpu.VMEM((F32_SUBLANES - 1, tm + 2 * CONV_HALO - F32_SUBLANES, CONV_COLS), F32),
            pltpu.VMEM((tm, d), F32),
        ],
        compiler_params=_cparams(("arbitrary",)),
        name="conv_mix",
    )(xg, xg, xg, dw, dw_b, norm_g, *weights)


def _router_operands(w_rg, b_rg, w_re, b_re):
    d = w_rg.shape[0]
    used = N_GROUPS_MOE + N_EXPERTS
    w = jnp.concatenate([w_rg, w_re, jnp.zeros((d, ROUTER_LANES - used), F32)], axis=1)
    b = jnp.concatenate([b_rg, b_re, jnp.zeros((ROUTER_LANES - used,), F32)])
    return w.astype(BF16), b.reshape(1, ROUTER_LANES), w.T.astype(BF16), b.reshape(ROUTER_LANES, 1)


def _moe_rows(h2, logits_t, wr, br, wg, wu, wd):
    t = h2.shape[0]
    n_blocks = -(-(t + N_CLASSES * (MOE_BLK - 1)) // MOE_BLK)
    assert n_blocks <= MAX_MOE_BLOCKS
    pos, ea, eb, n_valid, ends = _route(logits_t)
    pos = pos.reshape(t)
    hs = _dispatch(pos, ends.reshape(-1), h2, n_blocks * MOE_BLK)
    ys = _moe_experts(ea.reshape(-1), eb.reshape(-1), n_valid.reshape(-1), hs, wr, br, wg, wu, wd, n_blocks)
    return pos, ys


def kernel(x_prompt, x_sample, cache_k, cache_v, c, c_ctx, w_in_even, attn_sink, sgu_norm, sgu_w, sgu_b,
           w_out_even, conv_w_in, conv_dw, conv_dw_b, conv_norm, conv_w_out, ada_w, ada_b, norm_mix, norm_ffn,
           router_group_w, router_group_b, router_expert_w, router_expert_b, expert_w_gate, expert_w_up,
           expert_w_down, final_norm):
    batch, seq, d = x_prompt.shape
    dec_batch, dec_seq, _ = x_sample.shape
    depth = ada_w.shape[0]
    n_prompt = batch * seq
    n_sample = dec_batch * dec_seq
    assert d == D_MODEL and depth == 2 and n_prompt % dec_seq == 0 and dec_seq % TM_PROJ == 0
    x_p = x_prompt.reshape(n_prompt, d)
    x_s = x_sample.reshape(n_sample, d)

    n_mod = -(-(1 + dec_batch) // 8) * 8
    cvec = jnp.concatenate([c_ctx[None, :], c, jnp.zeros((n_mod - 1 - dec_batch, d), F32)], axis=0)
    mods = _ada_params(cvec, ada_w, ada_b).reshape(depth, n_mod, 6, d)
    expert_w = (expert_w_gate, expert_w_up, expert_w_down)

    cos_t, sin_t = _rope_tables(dec_seq, TM_PROJ)
    proj, kv32 = _proj_even(x_p, x_s, mods[0], norm_mix[0][None, :], w_in_even[0].astype(BF16), cos_t, sin_t,
                            dec_seq // TM_PROJ)
    kv_col_block = ATTN_WIDTH // (2 * KV_WIDTH)
    sink = attn_sink[0]
    a_p = _ctx_attention(proj, sink, batch, seq, kv_col_block)
    past = cache_k.shape[2]
    ck = cache_k[:, 0].reshape(dec_batch, past, KV_WIDTH).astype(BF16)
    cv = cache_v[:, 0].reshape(dec_batch, past, KV_WIDTH).astype(BF16)
    a_s, *experts0 = _lat_attention(proj, sink, ck, cv, n_prompt, dec_batch, dec_seq, kv_col_block, expert_w, 0)
    z = _sgu(proj, sgu_norm[0][None, :], sgu_w[0].astype(BF16), sgu_b[0].T)

    wr0, br0, wrt0, brt0 = _router_operands(router_group_w[0], router_group_b[0], router_expert_w[0], router_expert_b[0])
    x1, h2, lg = _out_proj((x_p, x_s), (a_p, a_s, z), w_out_even[0].astype(BF16), mods[0], norm_ffn[0][None, :],
                           wrt0, brt0, n_prompt, dec_seq)
    pos, ys = _moe_rows(h2, lg, wr0, br0, *experts0)
    gain_f = final_norm[None, :]

    xg, x = _proj_glu(pos, ys, x1, mods[0], mods[1], norm_mix[1][None, :], conv_w_in[0].astype(BF16),
                      n_prompt, dec_seq)
    dw = jnp.concatenate([conv_dw[0], jnp.zeros((1, d), F32)], axis=0)
    hc, *experts1 = _conv_mix(xg, dw, conv_dw_b[0][None, :], conv_norm[0][None, :], n_prompt, seq, dec_seq,
                              expert_w, 1)
    wr1, br1, wrt1, brt1 = _router_operands(router_group_w[1], router_group_b[1], router_expert_w[1], router_expert_b[1])
    x1, h2, lg = _out_proj((x,), (hc,), conv_w_out[0].astype(BF16), mods[1], norm_ffn[1][None, :],
                           wrt1, brt1, n_prompt, dec_seq)
    pos, ys = _moe_rows(h2, lg, wr1, br1, *experts1)
    y_prompt = _combine(pos, ys, x1, mods[1], gain_f, n_prompt, dec_seq, 0, n_prompt, final=True)
    y_sample = _combine(pos, ys, x1, mods[1], gain_f, n_prompt, dec_seq, n_prompt, n_sample, final=True)

    new_k = kv32[:n_prompt, :KV_WIDTH].reshape(batch, 1, seq, N_KV_HEADS, HEAD_DIM)
    new_v = kv32[:n_prompt, KV_WIDTH:].reshape(batch, 1, seq, N_KV_HEADS, HEAD_DIM)
    return (y_prompt.reshape(batch, seq, d), y_sample.reshape(dec_batch, dec_seq, d), new_k, new_v)
```

```python
import functools

import jax
import jax.numpy as jnp
import numpy as np
from jax import lax
from jax.experimental import pallas as pl
from jax.experimental.pallas import tpu as pltpu

F32 = jnp.float32
BF16 = jnp.bfloat16

D_MODEL = 2048
GRID_W = 64
ATTN_BLK = 128
N_HEADS = 16
N_KV_HEADS = 4
Q_GROUPS = N_HEADS // N_KV_HEADS
HEAD_DIM = 64
ATTN_WIDTH = N_HEADS * HEAD_DIM
KV_WIDTH = N_KV_HEADS * HEAD_DIM
ROPE_BASE = 10000.0
SGU_GROUPS = 4
SGU_WIDTH = D_MODEL // 2
SGU_GROUP_DIM = SGU_WIDTH // SGU_GROUPS
CHUNK = 128
CONV_K = 31
CONV_HALO = 16
N_GROUPS_MOE = 4
EXPERTS_PER_GROUP = 4
N_EXPERTS = N_GROUPS_MOE * EXPERTS_PER_GROUP
PAIRS_PER_GROUP = 6
N_CLASSES = N_GROUPS_MOE * PAIRS_PER_GROUP
MOE_BLK = 256
EPS = 1e-6
NEG_INF = -1e30

ROUTER_LANES = 128
CLASS_ROWS = 32
MAX_MOE_BLOCKS = 128
TM_PROJ = 512
TN_PROJ = 512
TM_OUT = 256
TM_DISPATCH = 2048
TM_COMBINE = 512
TM_OUTPROJ = 512
TM_SGU = 512
CUM_CHUNK = 256
VMEM_LIMIT = 56 * 1024 * 1024


def _cparams(sem):
    return pltpu.CompilerParams(dimension_semantics=sem, vmem_limit_bytes=VMEM_LIMIT)


def _pack_bf16_pairs(h):
    w = h.shape[1] // 2
    lo = pltpu.bitcast(h[:, :w].astype(BF16).astype(F32), jnp.uint32) >> 16
    hi = pltpu.bitcast(h[:, w:].astype(BF16).astype(F32), jnp.uint32) & jnp.uint32(0xFFFF0000)
    return hi | lo


def _unpack_bf16_pairs(u):
    lo = pltpu.bitcast(u << 16, F32)
    hi = pltpu.bitcast(u & jnp.uint32(0xFFFF0000), F32)
    return jnp.concatenate([lo, hi], axis=1).astype(BF16)


def _rms_mod(x, gain, scale, shift):
    ms = jnp.mean(x * x, axis=-1, keepdims=True)
    y = x * lax.rsqrt(ms + EPS) * gain
    return y * (1.0 + scale) + shift


def _ada_kernel(c_ref, w_ref, b_ref, o_ref):
    c = c_ref[...]
    s = (c * jax.nn.sigmoid(c)).astype(BF16)
    o_ref[...] = jnp.dot(s, w_ref[...].astype(BF16), preferred_element_type=F32) + b_ref[...]


def _ada_params(cvec, ada_w, ada_b):
    depth, d, n = ada_w.shape
    mp = cvec.shape[0]
    tn = 1024
    return pl.pallas_call(
        _ada_kernel,
        grid=(depth, n // tn),
        in_specs=[
            pl.BlockSpec((mp, d), lambda l, j: (0, 0)),
            pl.BlockSpec((None, d, tn), lambda l, j: (l, 0, j)),
            pl.BlockSpec((None, 1, tn), lambda l, j: (l, 0, j)),
        ],
        out_specs=pl.BlockSpec((None, mp, tn), lambda l, j: (l, 0, j)),
        out_shape=jax.ShapeDtypeStruct((depth, mp, n), F32),
        compiler_params=_cparams(("arbitrary", "arbitrary")),
        name="ada_params",
    )(cvec, ada_w, ada_b.reshape(depth, 1, n))


def _swap16(x):
    width = x.shape[-1]
    lane = lax.broadcasted_iota(jnp.int32, x.shape, 1)
    return jnp.where((lane % 32) < 16, pltpu.roll(x, width - 16, 1), pltpu.roll(x, 16, 1))


def _proj_even_kernel(xp_ref, xs_ref, mod_ref, g_ref, w_ref, cos_ref, sin_ref, o_ref, kv_ref, h_scr, *,
                      n_prompt_blocks):
    x = jnp.where(pl.program_id(0) < n_prompt_blocks, xp_ref[...], xs_ref[...])
    h_scr[...] = _rms_mod(x, g_ref[...], mod_ref[1:2, :], mod_ref[0:1, :]).astype(BF16)
    kv_tile = ATTN_WIDTH // TN_PROJ
    for j in range(w_ref.shape[1] // TN_PROJ):
        cols = slice(j * TN_PROJ, (j + 1) * TN_PROJ)
        acc = jnp.dot(h_scr[...], w_ref[:, cols], preferred_element_type=F32)
        if j <= kv_tile:
            tab = slice(0, TN_PROJ) if j < kv_tile else slice(TN_PROJ, 2 * TN_PROJ)
            o_ref[:, cols] = (acc * cos_ref[:, tab] + _swap16(acc) * sin_ref[:, tab]).astype(BF16)
            if j == kv_tile:
                kv_ref[...] = acc
        else:
            o_ref[:, cols] = jax.nn.gelu(acc).astype(BF16)


def _proj_even(x_p, x_s, mods, gain, w, cos_t, sin_t, blocks_per_seq):
    d = x_p.shape[1]
    n = w.shape[1]
    tm, tn = TM_PROJ, TN_PROJ
    npb = x_p.shape[0] // tm
    nsb = x_s.shape[0] // tm
    t = x_p.shape[0] + x_s.shape[0]

    def mod_idx(i):
        return (jnp.where(i < npb, 0, 1 + (i - npb) // blocks_per_seq), 0, 0)

    def tab_idx(i):
        return (jnp.where(i < npb, blocks_per_seq, (i - npb) % blocks_per_seq), 0)

    return pl.pallas_call(
        functools.partial(_proj_even_kernel, n_prompt_blocks=npb),
        grid=(npb + nsb,),
        in_specs=[
            pl.BlockSpec((tm, d), lambda i: (jnp.minimum(i, npb - 1), 0)),
            pl.BlockSpec((tm, d), lambda i: (jnp.maximum(i - npb, 0), 0)),
            pl.BlockSpec((None, 6, d), mod_idx),
            pl.BlockSpec((1, d), lambda i: (0, 0)),
            pl.BlockSpec((d, n), lambda i: (0, 0), pipeline_mode=pl.Buffered(1)),
            pl.BlockSpec((tm, 2 * tn), tab_idx),
            pl.BlockSpec((tm, 2 * tn), tab_idx),
        ],
        out_specs=[
            pl.BlockSpec((tm, n), lambda i: (i, 0)),
            pl.BlockSpec((tm, tn), lambda i: (i, 0)),
        ],
        out_shape=[jax.ShapeDtypeStruct((t, n), BF16), jax.ShapeDtypeStruct((t, tn), F32)],
        scratch_shapes=[pltpu.VMEM((tm, d), BF16)],
        compiler_params=_cparams(("arbitrary",)),
        name="proj_even",
    )(x_p, x_s, mods, gain, w, cos_t, sin_t)


def _rope_tables(dec_seq, tm):
    f32 = np.float32
    n = np.arange(dec_seq)
    row = (n // GRID_W).astype(f32)
    col = (n % GRID_W).astype(f32)
    nf = HEAD_DIM // 4
    inv_freq = np.power(f32(ROPE_BASE), -np.arange(nf, dtype=f32) / f32(nf)).astype(f32)
    ar = row[:, None] * inv_freq[None, :]
    ac = col[:, None] * inv_freq[None, :]
    cos_h = np.concatenate([np.cos(ar), np.cos(ar), np.cos(ac), np.cos(ac)], axis=-1).astype(f32)
    sin_h = np.concatenate([-np.sin(ar), np.sin(ar), -np.sin(ac), np.sin(ac)], axis=-1).astype(f32)
    scale = f32(HEAD_DIM ** -0.5)
    ones = np.ones((dec_seq, KV_WIDTH), f32)
    zeros = np.zeros((dec_seq, KV_WIDTH), f32)
    q_heads = TN_PROJ // HEAD_DIM
    cos_t = np.concatenate([np.tile(cos_h, (1, q_heads)) * scale, np.tile(cos_h, (1, N_KV_HEADS)), ones], axis=-1)
    sin_t = np.concatenate([np.tile(sin_h, (1, q_heads)) * scale, np.tile(sin_h, (1, N_KV_HEADS)), zeros], axis=-1)
    id_cos = np.concatenate([np.full((tm, TN_PROJ), scale, f32), np.ones((tm, TN_PROJ), f32)], axis=-1)
    id_sin = np.zeros((tm, 2 * TN_PROJ), f32)
    return (jnp.asarray(np.concatenate([cos_t, id_cos], axis=0), F32),
            jnp.asarray(np.concatenate([sin_t, id_sin], axis=0), F32))


def _cast_plan(weights, layer, n_steps, step_of):
    n_exp = weights[0].shape[1]
    steps_per_matrix = min(n_steps // len(weights), n_exp)
    assert steps_per_matrix >= 1 and n_exp % steps_per_matrix == 0
    e_blk = n_exp // steps_per_matrix

    def slice_of(k, idx):
        return jnp.clip(step_of(*idx) - k * steps_per_matrix, 0, steps_per_matrix - 1)

    in_specs = [pl.BlockSpec((None, e_blk) + w.shape[2:], lambda *idx, k=k: (layer, slice_of(k, idx), 0, 0))
                for k, w in enumerate(weights)]
    out_specs = [pl.BlockSpec((e_blk,) + w.shape[2:], lambda *idx, k=k: (slice_of(k, idx), 0, 0))
                 for k, w in enumerate(weights)]
    out_shapes = [jax.ShapeDtypeStruct(w.shape[1:], BF16) for w in weights]
    return steps_per_matrix, in_specs, out_specs, out_shapes


def _cast_slices(step, steps_per_matrix, srcs, dsts):
    for k, (src, dst) in enumerate(zip(srcs, dsts)):
        @pl.when((step >= k * steps_per_matrix) & (step < (k + 1) * steps_per_matrix))
        def _(src=src, dst=dst):
            dst[...] = src[...].astype(BF16)


def _sink_attend(q, keys, vals, masks, sink):
    scores = []
    for k, mask in zip(keys, masks):
        s = lax.dot_general(q, k, (((1,), (1,)), ((), ())), preferred_element_type=F32)
        if mask is not None:
            stacked = s.reshape(s.shape[0] // mask.shape[0], *mask.shape)
            s = jnp.where(mask[None], stacked, NEG_INF).reshape(s.shape)
        scores.append(s)
    m = sink
    for s in scores:
        m = jnp.maximum(m, jnp.max(s, axis=-1, keepdims=True))
    es = [jnp.exp(s - m) for s in scores]
    den = jnp.exp(sink - m)
    for e in es:
        den = den + jnp.sum(e, axis=-1, keepdims=True)
    out = None
    for e, v in zip(es, vals):
        o = jnp.dot(e.astype(BF16), v, preferred_element_type=F32)
        out = o if out is None else out + o
    return out * (1.0 / den)


def _grouped_heads_attend(sink_ref, q_ref, o_ref, kh, keys, vals, masks):
    rows = q_ref.shape[0]
    heads = [kh * Q_GROUPS + g for g in range(Q_GROUPS)]
    q = jnp.concatenate([q_ref[:, h * HEAD_DIM:(h + 1) * HEAD_DIM] for h in heads], axis=0)
    ridx = lax.broadcasted_iota(jnp.int32, (Q_GROUPS * rows, 1), 0)
    sink = jnp.full((Q_GROUPS * rows, 1), sink_ref[heads[-1]], F32)
    for g in range(Q_GROUPS - 2, -1, -1):
        sink = jnp.where(ridx < (g + 1) * rows, sink_ref[heads[g]], sink)
    o = _sink_attend(q, keys, vals, masks, sink)
    for g, h in enumerate(heads):
        o_ref[:, h * HEAD_DIM:(h + 1) * HEAD_DIM] = o[g * rows:(g + 1) * rows, :].astype(BF16)


def _ctx_attn_kernel(sink_ref, q_ref, kv_ref, o_ref):
    for kh in range(N_KV_HEADS):
        k = kv_ref[:, kh * HEAD_DIM:(kh + 1) * HEAD_DIM]
        v = kv_ref[:, KV_WIDTH + kh * HEAD_DIM:KV_WIDTH + (kh + 1) * HEAD_DIM]
        _grouped_heads_attend(sink_ref, q_ref, o_ref, kh, [k], [v], [None])


def _ctx_attention(proj, sink, batch, seq, kv_col_block):
    return pl.pallas_call(
        _ctx_attn_kernel,
        grid=(batch,),
        in_specs=[
            pl.BlockSpec(memory_space=pltpu.SMEM),
            pl.BlockSpec((seq, ATTN_WIDTH), lambda b: (b, 0)),
            pl.BlockSpec((seq, 2 * KV_WIDTH), lambda b: (b, kv_col_block)),
        ],
        out_specs=pl.BlockSpec((seq, ATTN_WIDTH), lambda b: (b, 0)),
        out_shape=jax.ShapeDtypeStruct((batch * seq, ATTN_WIDTH), BF16),
        compiler_params=_cparams(("arbitrary",)),
        name="ctx_attention",
    )(sink, proj, proj)


def _lat_attn_kernel(sink_ref, q_ref, kv_ref, ck_ref, cv_ref, wg_ref, wu_ref, wd_ref, o_ref, wg_out, wu_out, wd_out,
                     *, seq, steps_per_matrix):
    i = pl.program_id(1)
    _cast_slices(pl.program_id(0) * pl.num_programs(1) + i, steps_per_matrix,
                 (wg_ref, wu_ref, wd_ref), (wg_out, wu_out, wd_out))
    win = 3 * ATTN_BLK
    ws = pl.multiple_of(jnp.clip((i - 1) * ATTN_BLK, 0, seq - win), ATTN_BLK)
    shape = (ATTN_BLK, win)
    qpos = i * ATTN_BLK + lax.broadcasted_iota(jnp.int32, shape, 0)
    kpos = ws + lax.broadcasted_iota(jnp.int32, shape, 1)
    valid = jnp.abs(qpos - kpos) <= ATTN_BLK
    kvw = kv_ref[pl.ds(ws, win), :]
    for kh in range(N_KV_HEADS):
        k_loc = kvw[:, kh * HEAD_DIM:(kh + 1) * HEAD_DIM]
        v_loc = kvw[:, KV_WIDTH + kh * HEAD_DIM:KV_WIDTH + (kh + 1) * HEAD_DIM]
        k_ctx = ck_ref[:, kh * HEAD_DIM:(kh + 1) * HEAD_DIM]
        v_ctx = cv_ref[:, kh * HEAD_DIM:(kh + 1) * HEAD_DIM]
        _grouped_heads_attend(sink_ref, q_ref, o_ref, kh, [k_ctx, k_loc], [v_ctx, v_loc], [None, valid])


def _lat_attention(proj, sink, cache_k, cache_v, n_prompt, dec_batch, dec_seq, kv_col_block, weights, layer):
    qb = dec_seq // ATTN_BLK
    q_off = n_prompt // ATTN_BLK
    s_off = n_prompt // dec_seq
    past = cache_k.shape[1]
    spm, w_in, w_out, w_shapes = _cast_plan(weights, layer, dec_batch * qb, lambda b, i: b * qb + i)
    return pl.pallas_call(
        functools.partial(_lat_attn_kernel, seq=dec_seq, steps_per_matrix=spm),
        grid=(dec_batch, qb),
        in_specs=[
            pl.BlockSpec(memory_space=pltpu.SMEM),
            pl.BlockSpec((ATTN_BLK, ATTN_WIDTH), lambda b, i: (q_off + b * qb + i, 0)),
            pl.BlockSpec((dec_seq, 2 * KV_WIDTH), lambda b, i: (s_off + b, kv_col_block)),
            pl.BlockSpec((None, past, KV_WIDTH), lambda b, i: (b, 0, 0)),
            pl.BlockSpec((None, past, KV_WIDTH), lambda b, i: (b, 0, 0)),
        ] + w_in,
        out_specs=[pl.BlockSpec((ATTN_BLK, ATTN_WIDTH), lambda b, i: (b * qb + i, 0))] + w_out,
        out_shape=[jax.ShapeDtypeStruct((dec_batch * dec_seq, ATTN_WIDTH), BF16)] + w_shapes,
        compiler_params=_cparams(("arbitrary", "arbitrary")),
        name="lat_attention",
    )(sink, proj, proj, cache_k, cache_v, *weights)


def _sgu_kernel(ul_ref, uh_ref, vl_ref, vh_ref, g_ref, w_ref, bt_ref, z_ref):
    half = SGU_WIDTH // 2
    per_half = SGU_GROUPS // 2
    u_refs = (ul_ref, uh_ref)
    for c in range(TM_SGU // CHUNK):
        rows = slice(c * CHUNK, (c + 1) * CHUNK)
        vl = vl_ref[rows, :].astype(F32)
        vh = vh_ref[rows, :].astype(F32)
        ssq = jnp.sum(vl * vl, axis=-1, keepdims=True) + jnp.sum(vh * vh, axis=-1, keepdims=True)
        r = lax.rsqrt(ssq * (1.0 / SGU_WIDTH) + EPS)
        vn = ((vl * r * g_ref[:, :half]).astype(BF16), (vh * r * g_ref[:, half:]).astype(BF16))
        for g in range(SGU_GROUPS):
            src = g // per_half
            cols = slice((g % per_half) * SGU_GROUP_DIM, (g % per_half + 1) * SGU_GROUP_DIM)
            mixed = jnp.dot(w_ref[g], vn[src][:, cols], preferred_element_type=F32) + bt_ref[:, g:g + 1]
            z_ref[rows, g * SGU_GROUP_DIM:(g + 1) * SGU_GROUP_DIM] = (
                u_refs[src][rows, cols].astype(F32) * mixed).astype(BF16)


def _sgu(proj, norm_g, w_s, b_t):
    t = proj.shape[0]
    half = SGU_WIDTH // 2
    u0 = (ATTN_WIDTH + 2 * KV_WIDTH) // half
    blocks = [pl.BlockSpec((TM_SGU, half), functools.partial(lambda i, c: (i, c), c=u0 + k)) for k in range(4)]
    return pl.pallas_call(
        _sgu_kernel,
        grid=(t // TM_SGU,),
        in_specs=blocks + [
            pl.BlockSpec((1, SGU_WIDTH), lambda i: (0, 0)),
            pl.BlockSpec((SGU_GROUPS, CHUNK, CHUNK), lambda i: (0, 0, 0)),
            pl.BlockSpec((CHUNK, SGU_GROUPS), lambda i: (0, 0)),
        ],
        out_specs=pl.BlockSpec((TM_SGU, SGU_WIDTH), lambda i: (i, 0)),
        out_shape=jax.ShapeDtypeStruct((t, SGU_WIDTH), BF16),
        compiler_params=_cparams(("arbitrary",)),
        name="sgu",
    )(proj, proj, proj, proj, norm_g, w_s, b_t)


def _residual_router(rows, x, acc, mod_ref, g2_ref, wrt_ref, brt_ref, x1_ref, h2_ref, lg_ref):
    x1 = x + mod_ref[2:3, :] * acc
    x1_ref[rows, :] = x1
    h2 = _rms_mod(x1, g2_ref[...], mod_ref[4:5, :], mod_ref[3:4, :])
    h2_ref[rows, :] = _pack_bf16_pairs(h2)
    lg = lax.dot_general(wrt_ref[...], h2.astype(BF16), (((1,), (1,)), ((), ())), preferred_element_type=F32)
    lg_ref[:, rows] = lg + brt_ref[...]


def _sub_blocks(ref):
    return [slice(r, r + TM_OUT) for r in range(0, ref.shape[0], TM_OUT)]


def _out_even_kernel(xp_ref, xs_ref, ap_ref, as_ref, z_ref, w_ref, mod_ref, g2_ref, wrt_ref, brt_ref,
                     x1_ref, h2_ref, lg_ref, *, n_prompt_blocks):
    is_prompt = pl.program_id(0) < n_prompt_blocks
    for rows in _sub_blocks(x1_ref):
        a = jnp.where(is_prompt, ap_ref[rows, :].astype(F32), as_ref[rows, :].astype(F32)).astype(BF16)
        acc = jnp.dot(a, w_ref[0:ATTN_WIDTH, :], preferred_element_type=F32)
        acc = acc + jnp.dot(z_ref[rows, :], w_ref[ATTN_WIDTH:, :], preferred_element_type=F32)
        x = jnp.where(is_prompt, xp_ref[rows, :], xs_ref[rows, :])
        _residual_router(rows, x, acc, mod_ref, g2_ref, wrt_ref, brt_ref, x1_ref, h2_ref, lg_ref)


def _out_odd_kernel(x_ref, hc_ref, w_ref, mod_ref, g2_ref, wrt_ref, brt_ref, x1_ref, h2_ref, lg_ref):
    for rows in _sub_blocks(x1_ref):
        acc = jnp.dot(hc_ref[rows, :], w_ref[...], preferred_element_type=F32)
        _residual_router(rows, x_ref[rows, :], acc, mod_ref, g2_ref, wrt_ref, brt_ref, x1_ref, h2_ref, lg_ref)


def _out_proj(xs, lhs, w, mods, gain2, wrt, brt, n_prompt, dec_seq):
    d = xs[0].shape[1]
    t = sum(x.shape[0] for x in xs)
    tm = TM_OUTPROJ
    npb = n_prompt // tm
    bps = dec_seq // tm
    nsb = t // tm - npb
    if len(xs) == 2:
        x_specs = [pl.BlockSpec((tm, d), lambda i: (jnp.minimum(i, npb - 1), 0)),
                   pl.BlockSpec((tm, d), lambda i: (jnp.maximum(i - npb, 0), 0))]
    else:
        x_specs = [pl.BlockSpec((tm, d), lambda i: (i, 0))]

    def mod_idx(i):
        return (jnp.where(i < npb, 0, 1 + (i - npb) // bps), 0, 0)

    if len(lhs) == 3:
        a_p, a_s, z = lhs
        body = functools.partial(_out_even_kernel, n_prompt_blocks=npb)
        lhs_specs = [
            pl.BlockSpec((tm, ATTN_WIDTH), lambda i: (jnp.minimum(i, npb - 1), 0)),
            pl.BlockSpec((tm, ATTN_WIDTH), lambda i: (jnp.clip(i - npb, 0, nsb - 1), 0)),
            pl.BlockSpec((tm, SGU_WIDTH), lambda i: (i, 0)),
        ]
    else:
        body = _out_odd_kernel
        lhs_specs = [pl.BlockSpec((tm, d), lambda i: (i, 0))]
    return pl.pallas_call(
        body,
        grid=(t // tm,),
        in_specs=x_specs + lhs_specs + [
            pl.BlockSpec((w.shape[0], d), lambda i: (0, 0), pipeline_mode=pl.Buffered(1)),
            pl.BlockSpec((None, 6, d), mod_idx),
            pl.BlockSpec((1, d), lambda i: (0, 0)),
            pl.BlockSpec((ROUTER_LANES, d), lambda i: (0, 0)),
            pl.BlockSpec((ROUTER_LANES, 1), lambda i: (0, 0)),
        ],
        out_specs=[
            pl.BlockSpec((tm, d), lambda i: (i, 0)),
            pl.BlockSpec((tm, d // 2), lambda i: (i, 0)),
            pl.BlockSpec((ROUTER_LANES, tm), lambda i: (0, i)),
        ],
        out_shape=[
            jax.ShapeDtypeStruct((t, d), F32),
            jax.ShapeDtypeStruct((t, d // 2), jnp.uint32),
            jax.ShapeDtypeStruct((ROUTER_LANES, t), F32),
        ],
        compiler_params=_cparams(("arbitrary",)),
        name="out_proj",
    )(*xs, *lhs, w, mods, gain2, wrt, brt)


def _first_argmax4(v0, v1, v2, v3):
    m = jnp.maximum(jnp.maximum(v0, v1), jnp.maximum(v2, v3))
    idx = jnp.where(v0 == m, 0.0, jnp.where(v1 == m, 1.0, jnp.where(v2 == m, 2.0, 3.0)))
    return m, idx


def _route_kernel(lg_ref, pos_ref, ea_ref, eb_ref, nv_ref, ends_ref, oh_scr, rank_scr, *, n_tokens):
    lg = lg_ref[...]
    rows = [lg[r:r + 1, :] for r in range(N_GROUPS_MOE + N_EXPERTS)]
    _, gidx = _first_argmax4(*rows[:N_GROUPS_MOE])
    e = []
    for k in range(EXPERTS_PER_GROUP):
        cand = [rows[N_GROUPS_MOE + g * EXPERTS_PER_GROUP + k] for g in range(N_GROUPS_MOE)]
        e.append(jnp.where(gidx == 0.0, cand[0], jnp.where(gidx == 1.0, cand[1], jnp.where(gidx == 2.0, cand[2], cand[3]))))
    _, l1 = _first_argmax4(*e)
    e2 = [jnp.where(l1 == float(k), -jnp.inf, e[k]) for k in range(EXPERTS_PER_GROUP)]
    _, l2 = _first_argmax4(*e2)
    lo = jnp.minimum(l1, l2)
    hi = jnp.maximum(l1, l2)
    pair = jnp.where(lo == 0.0, hi - 1.0, jnp.where(lo == 2.0, 4.0, jnp.where(hi == 3.0, 3.0, 5.0)))
    cls = gidx * float(PAIRS_PER_GROUP) + pair

    crow = lax.broadcasted_iota(jnp.int32, (CLASS_ROWS, n_tokens), 0).astype(F32)
    oh_scr[...] = jnp.where(crow == cls, 1.0, 0.0).astype(BF16)

    jj = lax.broadcasted_iota(jnp.int32, (CUM_CHUNK, CUM_CHUNK), 0)
    ii = lax.broadcasted_iota(jnp.int32, (CUM_CHUNK, CUM_CHUNK), 1)
    tri = jnp.where(jj < ii, 1.0, 0.0).astype(BF16)

    def chunk(c, carry):
        off = pl.multiple_of(c * CUM_CHUNK, CUM_CHUNK)
        oh = oh_scr[:, pl.ds(off, CUM_CHUNK)]
        ohf = oh.astype(F32)
        cum = jnp.dot(oh, tri, preferred_element_type=F32) + carry
        rank_scr[:, pl.ds(off, CUM_CHUNK)] = jnp.sum(cum * ohf, axis=0, keepdims=True)
        return carry + jnp.sum(ohf, axis=1, keepdims=True)

    counts = lax.fori_loop(0, n_tokens // CUM_CHUNK, chunk, jnp.zeros((CLASS_ROWS, 1), F32))
    padded = jnp.floor((counts + float(MOE_BLK - 1)) * (1.0 / MOE_BLK)) * float(MOE_BLK)
    run = jnp.zeros((1, 1), F32)
    starts_rows = []
    for c in range(CLASS_ROWS):
        starts_rows.append(run)
        run = run + padded[c:c + 1, :]
    starts = jnp.concatenate(starts_rows, axis=0)
    ends = starts + padded
    total = run

    ohf = oh_scr[...].astype(F32)
    pos = rank_scr[...] + jnp.sum(ohf * starts, axis=0, keepdims=True)
    pos_ref[...] = pos.astype(jnp.int32)

    bstart = lax.broadcasted_iota(jnp.int32, (CLASS_ROWS, MAX_MOE_BLOCKS), 1).astype(F32) * float(MOE_BLK)
    brow = lax.broadcasted_iota(jnp.int32, (CLASS_ROWS, MAX_MOE_BLOCKS), 0)
    done = jnp.where((ends <= bstart) & (brow < N_CLASSES), 1.0, 0.0)
    bcls = jnp.minimum(jnp.sum(done, axis=0, keepdims=True), float(N_CLASSES - 1))
    grp = (jnp.where(bcls >= 6.0, 1.0, 0.0) + jnp.where(bcls >= 12.0, 1.0, 0.0) + jnp.where(bcls >= 18.0, 1.0, 0.0))
    bp = bcls - grp * float(PAIRS_PER_GROUP)
    first = jnp.where(bp >= 3.0, 1.0, 0.0) + jnp.where(bp >= 4.0, 1.0, 0.0)
    second = jnp.where(bp < 3.0, bp + 1.0, jnp.where(bp < 5.0, 3.0, 1.0))
    ea_ref[...] = (grp * float(EXPERTS_PER_GROUP) + first).astype(jnp.int32)
    eb_ref[...] = (grp * float(EXPERTS_PER_GROUP) + second).astype(jnp.int32)
    own = jnp.where(brow.astype(F32) == bcls, starts + counts, 0.0)
    filled = jnp.sum(own, axis=0, keepdims=True) - bstart[0:1, :]
    nv_ref[...] = jnp.clip(filled, 0.0, float(MOE_BLK)).astype(jnp.int32)
    lane = lax.broadcasted_iota(jnp.int32, (1, MAX_MOE_BLOCKS), 1)
    ends_row = jnp.zeros((1, MAX_MOE_BLOCKS), F32)
    for c in range(N_CLASSES):
        ends_row = jnp.where(lane == c, starts_rows[c] + padded[c:c + 1, :], ends_row)
    ends_ref[...] = ends_row.astype(jnp.int32)


def _route(logits_t):
    t = logits_t.shape[1]
    rows = N_CLASSES
    i32 = jnp.int32
    return pl.pallas_call(
        functools.partial(_route_kernel, n_tokens=t),
        grid=(1,),
        in_specs=[pl.BlockSpec((rows, t), lambda i: (0, 0))],
        out_specs=[
            pl.BlockSpec((1, t), lambda i: (0, 0)),
        ] + [pl.BlockSpec((1, MAX_MOE_BLOCKS), lambda i: (0, 0))] * 4,
        out_shape=[jax.ShapeDtypeStruct((1, t), i32)] + [jax.ShapeDtypeStruct((1, MAX_MOE_BLOCKS), i32)] * 4,
        scratch_shapes=[pltpu.VMEM((CLASS_ROWS, t), BF16), pltpu.VMEM((1, t), F32)],
        compiler_params=_cparams(("arbitrary",)),
        name="route",
    )(logits_t)


DMA_THREADS = 2


def _row_copy(src_ref, src_row, dst_ref, dst_row, sem):
    return pltpu.make_async_copy(src_ref.at[pl.ds(src_row, 1), :], dst_ref.at[pl.ds(dst_row, 1), :], sem)


def _start_row_copies(n_rows, make_copy):
    for r in range(n_rows):
        make_copy(r).start(priority=r % DMA_THREADS)


def _dispatch_kernel(pos_ref, ends_ref, h_ref, o_ref, zbuf, sem, zsem):
    step = pl.program_id(0)

    @pl.when(step == 0)
    def _():
        zbuf[...] = jnp.zeros_like(zbuf)
        total = ends_ref[N_CLASSES - 1]
        n_free = (o_ref.shape[0] - total) // MOE_BLK

        def zero_block(first_row, phase):
            first = pl.multiple_of(first_row, MOE_BLK)
            getattr(pltpu.make_async_copy(zbuf, o_ref.at[pl.ds(first, MOE_BLK), :], zsem), phase)()

        for phase in ("start", "wait"):
            prev = 0
            for c in range(N_CLASSES):
                end = ends_ref[c]
                pl.when(end > prev)(functools.partial(zero_block, end - MOE_BLK, phase))
                prev = end

            def free_block(k, carry, phase=phase):
                zero_block(total + k * MOE_BLK, phase)
                return carry

            lax.fori_loop(0, n_free, free_block, 0)

    rows = h_ref.shape[0]
    base = step * rows

    _start_row_copies(rows, lambda r: _row_copy(h_ref, r, o_ref, pos_ref[base + r], sem))
    pltpu.make_async_copy(h_ref, o_ref.at[pl.ds(0, rows), :], sem).wait()


def _dispatch(pos, ends, h2, n_rows):
    t, d = h2.shape
    return pl.pallas_call(
        _dispatch_kernel,
        grid_spec=pltpu.PrefetchScalarGridSpec(
            num_scalar_prefetch=2,
            grid=(t // TM_DISPATCH,),
            in_specs=[pl.BlockSpec((TM_DISPATCH, d), lambda i, pos, ends: (i, 0))],
            out_specs=pl.BlockSpec(memory_space=pl.ANY),
            scratch_shapes=[pltpu.VMEM((MOE_BLK, d), h2.dtype), pltpu.SemaphoreType.DMA(()),
                            pltpu.SemaphoreType.DMA(())],
        ),
        out_shape=jax.ShapeDtypeStruct((n_rows, d), h2.dtype),
        compiler_params=_cparams(("arbitrary",)),
        name="moe_dispatch",
    )(pos, ends, h2)


def _expert_ffn(x, wg_ref, wu_ref, wd_ref):
    gate = jnp.dot(x, wg_ref[...], preferred_element_type=F32)
    up = jnp.dot(x, wu_ref[...], preferred_element_type=F32)
    hidden = (gate * jax.nn.sigmoid(gate) * up).astype(BF16)
    return jnp.dot(hidden, wd_ref[...], preferred_element_type=F32)


def _moe_kernel(ea_ref, eb_ref, nv_ref, x_ref, wr_ref, br_ref, wga, wua, wda, wgb, wub, wdb, y_ref):
    b = pl.program_id(0)
    n_valid = nv_ref[b]

    def run(rows):
        x = _unpack_bf16_pairs(x_ref[rows, :])
        lg = jnp.dot(x, wr_ref[...], preferred_element_type=F32) + br_ref[...]
        lane = lax.broadcasted_iota(jnp.int32, lg.shape, 1)
        ea = ea_ref[b]
        eb = eb_ref[b]
        grp = ea // EXPERTS_PER_GROUP
        is_grp = lane < N_GROUPS_MOE
        gm = jnp.max(jnp.where(is_grp, lg, -jnp.inf), axis=-1, keepdims=True)
        ge = jnp.where(is_grp, jnp.exp(lg - gm), 0.0)
        p_g = jnp.sum(jnp.where(lane == grp, ge, 0.0), axis=-1, keepdims=True) / jnp.sum(ge, axis=-1, keepdims=True)
        la = jnp.sum(jnp.where(lane == N_GROUPS_MOE + ea, lg, 0.0), axis=-1, keepdims=True)
        lb = jnp.sum(jnp.where(lane == N_GROUPS_MOE + eb, lg, 0.0), axis=-1, keepdims=True)
        mm = jnp.maximum(la, lb)
        xa = jnp.exp(la - mm)
        xb = jnp.exp(lb - mm)
        ga = xa / (xa + xb) * p_g
        gb = xb / (xa + xb) * p_g
        ya = _expert_ffn(x, wga, wua, wda)
        yb = _expert_ffn(x, wgb, wub, wdb)
        y_ref[rows, :] = ya * ga + yb * gb

    half = MOE_BLK // 2

    @pl.when(n_valid > half)
    def _():
        run(slice(0, MOE_BLK))

    @pl.when((n_valid > 0) & (n_valid <= half))
    def _():
        run(slice(0, half))
        y_ref[half:, :] = jnp.zeros((MOE_BLK - half, y_ref.shape[1]), y_ref.dtype)

    @pl.when(n_valid == 0)
    def _():
        y_ref[...] = jnp.zeros_like(y_ref)


def _moe_experts(ea, eb, n_valid, hs, wr, br, w_gate, w_up, w_down, n_blocks):
    d = w_gate.shape[1]
    de = w_gate.shape[2]

    def wa_idx(b, ea, eb, nv):
        return (ea[b], 0, 0)

    def wb_idx(b, ea, eb, nv):
        return (eb[b], 0, 0)

    return pl.pallas_call(
        _moe_kernel,
        grid_spec=pltpu.PrefetchScalarGridSpec(
            num_scalar_prefetch=3,
            grid=(n_blocks,),
            in_specs=[
                pl.BlockSpec((MOE_BLK, hs.shape[1]), lambda b, ea, eb, nv: (jnp.where(nv[b] > 0, b, 0), 0)),
                pl.BlockSpec((d, ROUTER_LANES), lambda b, ea, eb, nv: (0, 0)),
                pl.BlockSpec((1, ROUTER_LANES), lambda b, ea, eb, nv: (0, 0)),
                pl.BlockSpec((None, d, de), wa_idx),
                pl.BlockSpec((None, d, de), wa_idx),
                pl.BlockSpec((None, de, d), wa_idx),
                pl.BlockSpec((None, d, de), wb_idx),
                pl.BlockSpec((None, d, de), wb_idx),
                pl.BlockSpec((None, de, d), wb_idx),
            ],
            out_specs=pl.BlockSpec((MOE_BLK, d), lambda b, ea, eb, nv: (b, 0)),
        ),
        out_shape=jax.ShapeDtypeStruct((n_blocks * MOE_BLK, d), F32),
        compiler_params=_cparams(("arbitrary",)),
        name="moe_experts",
    )(ea, eb, n_valid, hs, wr, br, w_gate, w_up, w_down, w_gate, w_up, w_down)


def _combine_kernel(pos_ref, ys_ref, x1_ref, mod_ref, gf_ref, o_ref, ybuf, sem, *, block_off, n_steps, final):
    i = pl.program_id(0)
    slot = i % 2
    tm = x1_ref.shape[0]

    def start_gather(step, into):
        base = (step + block_off) * tm
        _start_row_copies(tm, lambda r: _row_copy(ys_ref, pos_ref[base + r], ybuf.at[into], r, sem.at[into]))

    pl.when(i == 0)(lambda: start_gather(0, 0))
    pl.when(i + 1 < n_steps)(lambda: start_gather(i + 1, 1 - slot))
    pltpu.make_async_copy(ys_ref.at[pl.ds(0, tm), :], ybuf.at[slot], sem.at[slot]).wait()
    x2 = x1_ref[...] + mod_ref[5:6, :] * ybuf[slot]
    if final:
        ms = jnp.mean(x2 * x2, axis=-1, keepdims=True)
        x2 = x2 * lax.rsqrt(ms + EPS) * gf_ref[...]
    o_ref[...] = x2


def _combine(pos, ys, x1, mods, gain_f, n_prompt, dec_seq, row_off, n_rows, final):
    d = x1.shape[1]
    tm = TM_COMBINE
    npb = n_prompt // tm
    bps = dec_seq // tm
    boff = row_off // tm

    def mod_idx(i, pos):
        blk = i + boff
        return (jnp.where(blk < npb, 0, 1 + (blk - npb) // bps), 0, 0)

    return pl.pallas_call(
        functools.partial(_combine_kernel, block_off=boff, n_steps=n_rows // tm, final=final),
        grid_spec=pltpu.PrefetchScalarGridSpec(
            num_scalar_prefetch=1,
            grid=(n_rows // tm,),
            in_specs=[
                pl.BlockSpec(memory_space=pl.ANY),
                pl.BlockSpec((tm, d), lambda i, pos: (i + boff, 0)),
                pl.BlockSpec((None, 6, d), mod_idx),
                pl.BlockSpec((1, d), lambda i, pos: (0, 0)),
            ],
            out_specs=pl.BlockSpec((tm, d), lambda i, pos: (i, 0)),
            scratch_shapes=[pltpu.VMEM((2, tm, d), F32), pltpu.SemaphoreType.DMA((2,))],
        ),
        out_shape=jax.ShapeDtypeStruct((n_rows, d), F32),
        compiler_params=_cparams(("arbitrary",)),
        name="moe_combine",
    )(pos, ys, x1, mods, gain_f)


def _proj_glu_kernel(pos_ref, ys_ref, x1_ref, mod_prev_ref, mod_ref, g_ref, w_ref, o_ref, x2_ref, h_scr, ybuf, sem,
                     *, n_steps):
    i = pl.program_id(0)
    slot = i % 2
    tm = x1_ref.shape[0]

    def start_gather(step, into):
        base = step * tm
        _start_row_copies(tm, lambda r: _row_copy(ys_ref, pos_ref[base + r], ybuf.at[into], r, sem.at[into]))

    pl.when(i == 0)(lambda: start_gather(0, 0))
    pl.when(i + 1 < n_steps)(lambda: start_gather(i + 1, 1 - slot))
    pltpu.make_async_copy(ys_ref.at[pl.ds(0, tm), :], ybuf.at[slot], sem.at[slot]).wait()
    x2 = x1_ref[...] + mod_prev_ref[5:6, :] * ybuf[slot]
    x2_ref[...] = x2
    h_scr[...] = _rms_mod(x2, g_ref[...], mod_ref[1:2, :], mod_ref[0:1, :]).astype(BF16)
    width = w_ref.shape[1] // 2
    for j in range(width // TN_PROJ):
        cols = slice(j * TN_PROJ, (j + 1) * TN_PROJ)
        gcols = slice(width + j * TN_PROJ, width + (j + 1) * TN_PROJ)
        val = jnp.dot(h_scr[...], w_ref[:, cols], preferred_element_type=F32)
        gate = jnp.dot(h_scr[...], w_ref[:, gcols], preferred_element_type=F32)
        o_ref[:, cols] = (val * jax.nn.sigmoid(gate)).astype(BF16)


def _proj_glu(pos, ys, x1, mods_prev, mods, gain, w, n_prompt, dec_seq):
    t, d = x1.shape
    n = w.shape[1]
    tm = TM_PROJ
    npb = n_prompt // tm
    bps = dec_seq // tm

    def mod_idx(i, pos):
        return (jnp.where(i < npb, 0, 1 + (i - npb) // bps), 0, 0)

    return pl.pallas_call(
        functools.partial(_proj_glu_kernel, n_steps=t // tm),
        grid_spec=pltpu.PrefetchScalarGridSpec(
            num_scalar_prefetch=1,
            grid=(t // tm,),
            in_specs=[
                pl.BlockSpec(memory_space=pl.ANY),
                pl.BlockSpec((tm, d), lambda i, pos: (i, 0)),
                pl.BlockSpec((None, 6, d), mod_idx),
                pl.BlockSpec((None, 6, d), mod_idx),
                pl.BlockSpec((1, d), lambda i, pos: (0, 0)),
                pl.BlockSpec((d, n), lambda i, pos: (0, 0), pipeline_mode=pl.Buffered(1)),
            ],
            out_specs=[
                pl.BlockSpec((tm, n // 2), lambda i, pos: (i, 0)),
                pl.BlockSpec((tm, d), lambda i, pos: (i, 0)),
            ],
            scratch_shapes=[pltpu.VMEM((tm, d), BF16), pltpu.VMEM((2, tm, d), F32), pltpu.SemaphoreType.DMA((2,))],
        ),
        out_shape=[jax.ShapeDtypeStruct((t, n // 2), BF16), jax.ShapeDtypeStruct((t, d), F32)],
        compiler_params=_cparams(("arbitrary",)),
        name="proj_glu",
    )(pos, ys, x1, mods_prev, mods, gain, w)


CONV_ROWS = 128
CONV_COLS = 128
F32_SUBLANES = 8


def _conv_kernel(x_ref, prev_ref, next_ref, dw_ref, b_ref, g_ref, wg_ref, wu_ref, wd_ref, o_ref, wg_out, wu_out, wd_out,
                 xpad, shifted, ybuf, *, n_prompt_blocks, blocks_per_seq, steps_per_matrix):
    i = pl.program_id(0)
    _cast_slices(i, steps_per_matrix, (wg_ref, wu_ref, wd_ref), (wg_out, wu_out, wd_out))
    in_sample = i >= n_prompt_blocks
    j = i - n_prompt_blocks
    is_start = jnp.logical_or(jnp.logical_not(in_sample), j % blocks_per_seq == 0)
    is_end = jnp.logical_or(jnp.logical_not(in_sample), j % blocks_per_seq == blocks_per_seq - 1)
    tm = TM_OUT
    xpad[0:CONV_HALO, :] = jnp.where(is_start, 0.0, prev_ref[...].astype(F32))
    xpad[CONV_HALO:CONV_HALO + tm, :] = x_ref[...].astype(F32)
    xpad[CONV_HALO + tm:, :] = jnp.where(is_end, 0.0, next_ref[...].astype(F32))
    first = CONV_HALO - CONV_K // 2
    span = shifted.shape[1]

    def col_chunk(c, ssq):
        c0 = pl.multiple_of(c * CONV_COLS, CONV_COLS)
        cols = pl.ds(c0, CONV_COLS)
        w = dw_ref[:, cols]
        bias = b_ref[:, cols]
        for s in range(1, F32_SUBLANES):
            shifted[s - 1] = xpad[s:s + span, cols]
        parts = []
        for rc in range(tm // CONV_ROWS):
            r0 = rc * CONV_ROWS
            acc = jnp.zeros((CONV_ROWS, CONV_COLS), F32)
            for k in range(CONV_K):
                whole, s = divmod(first + k, F32_SUBLANES)
                base = whole * F32_SUBLANES + r0
                if s == 0:
                    src = xpad[base:base + CONV_ROWS, cols]
                else:
                    src = shifted[s - 1, base:base + CONV_ROWS, :]
                acc = acc + w[k:k + 1, :] * src
            acc = acc + bias
            ybuf[r0:r0 + CONV_ROWS, cols] = acc
            parts.append(jnp.sum(acc * acc, axis=-1, keepdims=True))
        return ssq + jnp.concatenate(parts, axis=0)

    d = x_ref.shape[1]
    ssq = lax.fori_loop(0, d // CONV_COLS, col_chunk, jnp.zeros((tm, 1), F32))
    y = ybuf[...] * lax.rsqrt(ssq * (1.0 / d) + EPS) * g_ref[...]
    o_ref[...] = (y * jax.nn.sigmoid(y)).astype(BF16)


def _conv_mix(xg, dw, dw_b, norm_g, n_prompt, seq, dec_seq, weights, layer):
    t, d = xg.shape
    tm = TM_OUT
    assert seq == tm, "context sequences must be exactly one conv row block"
    npb = n_prompt // tm
    bps = dec_seq // tm
    hb = tm // CONV_HALO
    last = t // CONV_HALO - 1
    spm, w_in, w_out, w_shapes = _cast_plan(weights, layer, t // tm, lambda i: i)
    return pl.pallas_call(
        functools.partial(_conv_kernel, n_prompt_blocks=npb, blocks_per_seq=bps, steps_per_matrix=spm),
        grid=(t // tm,),
        in_specs=[
            pl.BlockSpec((tm, d), lambda i: (i, 0)),
            pl.BlockSpec((CONV_HALO, d), lambda i: (jnp.maximum(i * hb - 1, 0), 0)),
            pl.BlockSpec((CONV_HALO, d), lambda i: (jnp.minimum((i + 1) * hb, last), 0)),
            pl.BlockSpec((dw.shape[0], d), lambda i: (0, 0)),
            pl.BlockSpec((1, d), lambda i: (0, 0)),
            pl.BlockSpec((1, d), lambda i: (0, 0)),
        ] + w_in,
        out_specs=[pl.BlockSpec((tm, d), lambda i: (i, 0))] + w_out,
        out_shape=[jax.ShapeDtypeStruct((t, d), BF16)] + w_shapes,
        scratch_shapes=[
            pltpu.VMEM((tm + 2 * CONV_HALO, d), F32),
            pltpu.VMEM((F32_SUBLANES - 1, tm + 2 * CONV_HALO - F32_SUBLANES, CONV_COLS), F32),
            pltpu.VMEM((tm, d), F32),
        ],
        compiler_params=_cparams(("arbitrary",)),
        name="conv_mix",
    )(xg, xg, xg, dw, dw_b, norm_g, *weights)


def _router_operands(w_rg, b_rg, w_re, b_re):
    d = w_rg.shape[0]
    used = N_GROUPS_MOE + N_EXPERTS
    w = jnp.concatenate([w_rg, w_re, jnp.zeros((d, ROUTER_LANES - used), F32)], axis=1)
    b = jnp.concatenate([b_rg, b_re, jnp.zeros((ROUTER_LANES - used,), F32)])
    return w.astype(BF16), b.reshape(1, ROUTER_LANES), w.T.astype(BF16), b.reshape(ROUTER_LANES, 1)


def _moe_rows(h2, logits_t, wr, br, wg, wu, wd):
    t = h2.shape[0]
    n_blocks = -(-(t + N_CLASSES * (MOE_BLK - 1)) // MOE_BLK)
    assert n_blocks <= MAX_MOE_BLOCKS
    pos, ea, eb, n_valid, ends = _route(logits_t)
    pos = pos.reshape(t)
    hs = _dispatch(pos, ends.reshape(-1), h2, n_blocks * MOE_BLK)
    ys = _moe_experts(ea.reshape(-1), eb.reshape(-1), n_valid.reshape(-1), hs, wr, br, wg, wu, wd, n_blocks)
    return pos, ys


def kernel(x_prompt, x_sample, cache_k, cache_v, c, c_ctx, w_in_even, attn_sink, sgu_norm, sgu_w, sgu_b,
           w_out_even, conv_w_in, conv_dw, conv_dw_b, conv_norm, conv_w_out, ada_w, ada_b, norm_mix, norm_ffn,
           router_group_w, router_group_b, router_expert_w, router_expert_b, expert_w_gate, expert_w_up,
           expert_w_down, final_norm):
    batch, seq, d = x_prompt.shape
    dec_batch, dec_seq, _ = x_sample.shape
    depth = ada_w.shape[0]
    n_prompt = batch * seq
    n_sample = dec_batch * dec_seq
    assert d == D_MODEL and depth == 2 and n_prompt % dec_seq == 0 and dec_seq % TM_PROJ == 0
    x_p = x_prompt.reshape(n_prompt, d)
    x_s = x_sample.reshape(n_sample, d)

    n_mod = -(-(1 + dec_batch) // 8) * 8
    cvec = jnp.concatenate([c_ctx[None, :], c, jnp.zeros((n_mod - 1 - dec_batch, d), F32)], axis=0)
    mods = _ada_params(cvec, ada_w, ada_b).reshape(depth, n_mod, 6, d)
    expert_w = (expert_w_gate, expert_w_up, expert_w_down)

    cos_t, sin_t = _rope_tables(dec_seq, TM_PROJ)
    proj, kv32 = _proj_even(x_p, x_s, mods[0], norm_mix[0][None, :], w_in_even[0].astype(BF16), cos_t, sin_t,
                            dec_seq // TM_PROJ)
    kv_col_block = ATTN_WIDTH // (2 * KV_WIDTH)
    sink = attn_sink[0]
    a_p = _ctx_attention(proj, sink, batch, seq, kv_col_block)
    past = cache_k.shape[2]
    ck = cache_k[:, 0].reshape(dec_batch, past, KV_WIDTH).astype(BF16)
    cv = cache_v[:, 0].reshape(dec_batch, past, KV_WIDTH).astype(BF16)
    a_s, *experts0 = _lat_attention(proj, sink, ck, cv, n_prompt, dec_batch, dec_seq, kv_col_block, expert_w, 0)
    z = _sgu(proj, sgu_norm[0][None, :], sgu_w[0].astype(BF16), sgu_b[0].T)

    wr0, br0, wrt0, brt0 = _router_operands(router_group_w[0], router_group_b[0], router_expert_w[0], router_expert_b[0])
    x1, h2, lg = _out_proj((x_p, x_s), (a_p, a_s, z), w_out_even[0].astype(BF16), mods[0], norm_ffn[0][None, :],
                           wrt0, brt0, n_prompt, dec_seq)
    pos, ys = _moe_rows(h2, lg, wr0, br0, *experts0)
    gain_f = final_norm[None, :]

    xg, x = _proj_glu(pos, ys, x1, mods[0], mods[1], norm_mix[1][None, :], conv_w_in[0].astype(BF16),
                      n_prompt, dec_seq)
    dw = jnp.concatenate([conv_dw[0], jnp.zeros((1, d), F32)], axis=0)
    hc, *experts1 = _conv_mix(xg, dw, conv_dw_b[0][None, :], conv_norm[0][None, :], n_prompt, seq, dec_seq,
                              expert_w, 1)
    wr1, br1, wrt1, brt1 = _router_operands(router_group_w[1], router_group_b[1], router_expert_w[1], router_expert_b[1])
    x1, h2, lg = _out_proj((x,), (hc,), conv_w_out[0].astype(BF16), mods[1], norm_ffn[1][None, :],
                           wrt1, brt1, n_prompt, dec_seq)
    pos, ys = _moe_rows(h2, lg, wr1, br1, *experts1)
    y_prompt = _combine(pos, ys, x1, mods[1], gain_f, n_prompt, dec_seq, 0, n_prompt, final=True)
    y_sample = _combine(pos, ys, x1, mods[1], gain_f, n_prompt, dec_seq, n_prompt, n_sample, final=True)

    new_k = kv32[:n_prompt, :KV_WIDTH].reshape(batch, 1, seq, N_KV_HEADS, HEAD_DIM)
    new_v = kv32[:n_prompt, KV_WIDTH:].reshape(batch, 1, seq, N_KV_HEADS, HEAD_DIM)
    return (y_prompt.reshape(batch, seq, d), y_sample.reshape(dec_batch, dec_seq, d), new_k, new_v)
```

```python
import functools

import jax
import jax.numpy as jnp
import numpy as np
from jax import lax
from jax.experimental import pallas as pl
from jax.experimental.pallas import tpu as pltpu

F32 = jnp.float32
BF16 = jnp.bfloat16

D_MODEL = 2048
GRID_W = 64
ATTN_BLK = 128
N_HEADS = 16
N_KV_HEADS = 4
Q_GROUPS = N_HEADS // N_KV_HEADS
HEAD_DIM = 64
ATTN_WIDTH = N_HEADS * HEAD_DIM
KV_WIDTH = N_KV_HEADS * HEAD_DIM
ROPE_BASE = 10000.0
SGU_GROUPS = 4
SGU_WIDTH = D_MODEL // 2
SGU_GROUP_DIM = SGU_WIDTH // SGU_GROUPS
CHUNK = 128
CONV_K = 31
CONV_HALO = 16
N_GROUPS_MOE = 4
EXPERTS_PER_GROUP = 4
N_EXPERTS = N_GROUPS_MOE * EXPERTS_PER_GROUP
PAIRS_PER_GROUP = 6
N_CLASSES = N_GROUPS_MOE * PAIRS_PER_GROUP
MOE_BLK = 256
EPS = 1e-6
NEG_INF = -1e30
LOG2_E = 1.4426950408889634

ROUTER_LANES = 128
CLASS_ROWS = 32
MAX_MOE_BLOCKS = 128
TM_PROJ = 512
TN_PROJ = 512
TM_OUT = 256
TM_DISPATCH = 2048
TM_COMBINE = 512
TM_OUTPROJ = 512
TM_SGU = 512
CUM_CHUNK = 256
VMEM_LIMIT = 56 * 1024 * 1024


def _cparams(sem):
    return pltpu.CompilerParams(dimension_semantics=sem, vmem_limit_bytes=VMEM_LIMIT)


def _pack_bf16_pairs(h):
    w = h.shape[1] // 2
    lo = pltpu.bitcast(h[:, :w].astype(BF16).astype(F32), jnp.uint32) >> 16
    hi = pltpu.bitcast(h[:, w:].astype(BF16).astype(F32), jnp.uint32) & jnp.uint32(0xFFFF0000)
    return hi | lo


def _unpack_bf16_pairs(u):
    lo = pltpu.bitcast(u << 16, F32)
    hi = pltpu.bitcast(u & jnp.uint32(0xFFFF0000), F32)
    return jnp.concatenate([lo, hi], axis=1).astype(BF16)


def _rms_mod(x, gain, scale, shift):
    ms = jnp.mean(x * x, axis=-1, keepdims=True)
    y = x * lax.rsqrt(ms + EPS) * gain
    return y * (1.0 + scale) + shift


def _ada_kernel(c_ref, w_ref, b_ref, o_ref):
    c = c_ref[...]
    s = (c * jax.nn.sigmoid(c)).astype(BF16)
    o_ref[...] = jnp.dot(s, w_ref[...].astype(BF16), preferred_element_type=F32) + b_ref[...]


def _ada_params(cvec, ada_w, ada_b):
    depth, d, n = ada_w.shape
    mp = cvec.shape[0]
    tn = 1024
    return pl.pallas_call(
        _ada_kernel,
        grid=(depth, n // tn),
        in_specs=[
            pl.BlockSpec((mp, d), lambda l, j: (0, 0)),
            pl.BlockSpec((None, d, tn), lambda l, j: (l, 0, j)),
            pl.BlockSpec((None, 1, tn), lambda l, j: (l, 0, j)),
        ],
        out_specs=pl.BlockSpec((None, mp, tn), lambda l, j: (l, 0, j)),
        out_shape=jax.ShapeDtypeStruct((depth, mp, n), F32),
        compiler_params=_cparams(("arbitrary", "arbitrary")),
        name="ada_params",
    )(cvec, ada_w, ada_b.reshape(depth, 1, n))


def _swap16(x):
    width = x.shape[-1]
    lane = lax.broadcasted_iota(jnp.int32, x.shape, 1)
    return jnp.where((lane % 32) < 16, pltpu.roll(x, width - 16, 1), pltpu.roll(x, 16, 1))


def _proj_even_kernel(xp_ref, xs_ref, mod_ref, g_ref, w_ref, cos_ref, sin_ref, o_ref, kv_ref, h_scr, *,
                      n_prompt_blocks):
    x = jnp.where(pl.program_id(0) < n_prompt_blocks, xp_ref[...], xs_ref[...])
    h_scr[...] = _rms_mod(x, g_ref[...], mod_ref[1:2, :], mod_ref[0:1, :]).astype(BF16)
    kv_tile = ATTN_WIDTH // TN_PROJ
    for j in range(w_ref.shape[1] // TN_PROJ):
        cols = slice(j * TN_PROJ, (j + 1) * TN_PROJ)
        acc = jnp.dot(h_scr[...], w_ref[:, cols], preferred_element_type=F32)
        if j <= kv_tile:
            tab = slice(0, TN_PROJ) if j < kv_tile else slice(TN_PROJ, 2 * TN_PROJ)
            o_ref[:, cols] = (acc * cos_ref[:, tab] + _swap16(acc) * sin_ref[:, tab]).astype(BF16)
            if j == kv_tile:
                kv_ref[...] = acc
        else:
            o_ref[:, cols] = jax.nn.gelu(acc).astype(BF16)


def _proj_even(x_p, x_s, mods, gain, w, cos_t, sin_t, blocks_per_seq):
    d = x_p.shape[1]
    n = w.shape[1]
    tm, tn = TM_PROJ, TN_PROJ
    npb = x_p.shape[0] // tm
    nsb = x_s.shape[0] // tm
    t = x_p.shape[0] + x_s.shape[0]

    def mod_idx(i):
        return (jnp.where(i < npb, 0, 1 + (i - npb) // blocks_per_seq), 0, 0)

    def tab_idx(i):
        return (jnp.where(i < npb, blocks_per_seq, (i - npb) % blocks_per_seq), 0)

    return pl.pallas_call(
        functools.partial(_proj_even_kernel, n_prompt_blocks=npb),
        grid=(npb + nsb,),
        in_specs=[
            pl.BlockSpec((tm, d), lambda i: (jnp.minimum(i, npb - 1), 0)),
            pl.BlockSpec((tm, d), lambda i: (jnp.maximum(i - npb, 0), 0)),
            pl.BlockSpec((None, 6, d), mod_idx),
            pl.BlockSpec((1, d), lambda i: (0, 0)),
            pl.BlockSpec((d, n), lambda i: (0, 0), pipeline_mode=pl.Buffered(1)),
            pl.BlockSpec((tm, 2 * tn), tab_idx),
            pl.BlockSpec((tm, 2 * tn), tab_idx),
        ],
        out_specs=[
            pl.BlockSpec((tm, n), lambda i: (i, 0)),
            pl.BlockSpec((tm, tn), lambda i: (i, 0)),
        ],
        out_shape=[jax.ShapeDtypeStruct((t, n), BF16), jax.ShapeDtypeStruct((t, tn), F32)],
        scratch_shapes=[pltpu.VMEM((tm, d), BF16)],
        compiler_params=_cparams(("arbitrary",)),
        name="proj_even",
    )(x_p, x_s, mods, gain, w, cos_t, sin_t)


def _rope_tables(dec_seq, tm):
    f32 = np.float32
    n = np.arange(dec_seq)
    row = (n // GRID_W).astype(f32)
    col = (n % GRID_W).astype(f32)
    nf = HEAD_DIM // 4
    inv_freq = np.power(f32(ROPE_BASE), -np.arange(nf, dtype=f32) / f32(nf)).astype(f32)
    ar = row[:, None] * inv_freq[None, :]
    ac = col[:, None] * inv_freq[None, :]
    cos_h = np.concatenate([np.cos(ar), np.cos(ar), np.cos(ac), np.cos(ac)], axis=-1).astype(f32)
    sin_h = np.concatenate([-np.sin(ar), np.sin(ar), -np.sin(ac), np.sin(ac)], axis=-1).astype(f32)
    scale = f32(HEAD_DIM ** -0.5 * LOG2_E)
    ones = np.ones((dec_seq, KV_WIDTH), f32)
    zeros = np.zeros((dec_seq, KV_WIDTH), f32)
    q_heads = TN_PROJ // HEAD_DIM
    cos_t = np.concatenate([np.tile(cos_h, (1, q_heads)) * scale, np.tile(cos_h, (1, N_KV_HEADS)), ones], axis=-1)
    sin_t = np.concatenate([np.tile(sin_h, (1, q_heads)) * scale, np.tile(sin_h, (1, N_KV_HEADS)), zeros], axis=-1)
    id_cos = np.concatenate([np.full((tm, TN_PROJ), scale, f32), np.ones((tm, TN_PROJ), f32)], axis=-1)
    id_sin = np.zeros((tm, 2 * TN_PROJ), f32)
    return (jnp.asarray(np.concatenate([cos_t, id_cos], axis=0), F32),
            jnp.asarray(np.concatenate([sin_t, id_sin], axis=0), F32))


def _cast_plan(weights, layer, n_steps, step_of):
    n_exp = weights[0].shape[1]
    steps_per_matrix = min(n_steps // len(weights), n_exp)
    assert steps_per_matrix >= 1 and n_exp % steps_per_matrix == 0
    e_blk = n_exp // steps_per_matrix

    def slice_of(k, idx):
        return jnp.clip(step_of(*idx) - k * steps_per_matrix, 0, steps_per_matrix - 1)

    in_specs = [pl.BlockSpec((None, e_blk) + w.shape[2:], lambda *idx, k=k: (layer, slice_of(k, idx), 0, 0))
                for k, w in enumerate(weights)]
    out_specs = [pl.BlockSpec((e_blk,) + w.shape[2:], lambda *idx, k=k: (slice_of(k, idx), 0, 0))
                 for k, w in enumerate(weights)]
    out_shapes = [jax.ShapeDtypeStruct(w.shape[1:], BF16) for w in weights]
    return steps_per_matrix, in_specs, out_specs, out_shapes


def _cast_slices(step, steps_per_matrix, srcs, dsts):
    for k, (src, dst) in enumerate(zip(srcs, dsts)):
        @pl.when((step >= k * steps_per_matrix) & (step < (k + 1) * steps_per_matrix))
        def _(src=src, dst=dst):
            dst[...] = src[...].astype(BF16)


def _sink_attend(q, keys, vals, masks, sink):
    scores = []
    for k, mask in zip(keys, masks):
        s = lax.dot_general(q, k, (((1,), (1,)), ((), ())), preferred_element_type=F32)
        if mask is not None:
            stacked = s.reshape(s.shape[0] // mask.shape[0], *mask.shape)
            s = jnp.where(mask[None], stacked, NEG_INF).reshape(s.shape)
        scores.append(s)
    m = sink
    for s in scores:
        m = jnp.maximum(m, jnp.max(s, axis=-1, keepdims=True))
    es = [jnp.exp2(s - m) for s in scores]
    den = jnp.exp2(sink - m)
    for e in es:
        den = den + jnp.sum(e, axis=-1, keepdims=True)
    out = None
    for e, v in zip(es, vals):
        o = jnp.dot(e.astype(BF16), v, preferred_element_type=F32)
        out = o if out is None else out + o
    return out * (1.0 / den)


def _grouped_heads_attend(sink_ref, q_ref, o_ref, kh, keys, vals, masks):
    rows = q_ref.shape[0]
    heads = [kh * Q_GROUPS + g for g in range(Q_GROUPS)]
    q = jnp.concatenate([q_ref[:, h * HEAD_DIM:(h + 1) * HEAD_DIM] for h in heads], axis=0)
    ridx = lax.broadcasted_iota(jnp.int32, (Q_GROUPS * rows, 1), 0)
    sink = jnp.full((Q_GROUPS * rows, 1), sink_ref[heads[-1]], F32)
    for g in range(Q_GROUPS - 2, -1, -1):
        sink = jnp.where(ridx < (g + 1) * rows, sink_ref[heads[g]], sink)
    sink = sink * LOG2_E
    o = _sink_attend(q, keys, vals, masks, sink)
    for g, h in enumerate(heads):
        o_ref[:, h * HEAD_DIM:(h + 1) * HEAD_DIM] = o[g * rows:(g + 1) * rows, :].astype(BF16)


def _ctx_attn_kernel(sink_ref, q_ref, kv_ref, o_ref):
    for kh in range(N_KV_HEADS):
        k = kv_ref[:, kh * HEAD_DIM:(kh + 1) * HEAD_DIM]
        v = kv_ref[:, KV_WIDTH + kh * HEAD_DIM:KV_WIDTH + (kh + 1) * HEAD_DIM]
        _grouped_heads_attend(sink_ref, q_ref, o_ref, kh, [k], [v], [None])


def _ctx_attention(proj, sink, batch, seq, kv_col_block):
    return pl.pallas_call(
        _ctx_attn_kernel,
        grid=(batch,),
        in_specs=[
            pl.BlockSpec(memory_space=pltpu.SMEM),
            pl.BlockSpec((seq, ATTN_WIDTH), lambda b: (b, 0)),
            pl.BlockSpec((seq, 2 * KV_WIDTH), lambda b: (b, kv_col_block)),
        ],
        out_specs=pl.BlockSpec((seq, ATTN_WIDTH), lambda b: (b, 0)),
        out_shape=jax.ShapeDtypeStruct((batch * seq, ATTN_WIDTH), BF16),
        compiler_params=_cparams(("arbitrary",)),
        name="ctx_attention",
    )(sink, proj, proj)


def _lat_attn_kernel(sink_ref, q_ref, kv_ref, ck_ref, cv_ref, wg_ref, wu_ref, wd_ref, o_ref, wg_out, wu_out, wd_out,
                     *, seq, steps_per_matrix):
    i = pl.program_id(1)
    _cast_slices(pl.program_id(0) * pl.num_programs(1) + i, steps_per_matrix,
                 (wg_ref, wu_ref, wd_ref), (wg_out, wu_out, wd_out))
    win = 3 * ATTN_BLK
    ws = pl.multiple_of(jnp.clip((i - 1) * ATTN_BLK, 0, seq - win), ATTN_BLK)
    shape = (ATTN_BLK, win)
    qpos = i * ATTN_BLK + lax.broadcasted_iota(jnp.int32, shape, 0)
    kpos = ws + lax.broadcasted_iota(jnp.int32, shape, 1)
    valid = jnp.abs(qpos - kpos) <= ATTN_BLK
    kvw = kv_ref[pl.ds(ws, win), :]
    for kh in range(N_KV_HEADS):
        k_loc = kvw[:, kh * HEAD_DIM:(kh + 1) * HEAD_DIM]
        v_loc = kvw[:, KV_WIDTH + kh * HEAD_DIM:KV_WIDTH + (kh + 1) * HEAD_DIM]
        k_ctx = ck_ref[:, kh * HEAD_DIM:(kh + 1) * HEAD_DIM]
        v_ctx = cv_ref[:, kh * HEAD_DIM:(kh + 1) * HEAD_DIM]
        _grouped_heads_attend(sink_ref, q_ref, o_ref, kh, [k_ctx, k_loc], [v_ctx, v_loc], [None, valid])


def _lat_attention(proj, sink, cache_k, cache_v, n_prompt, dec_batch, dec_seq, kv_col_block, weights, layer):
    qb = dec_seq // ATTN_BLK
    q_off = n_prompt // ATTN_BLK
    s_off = n_prompt // dec_seq
    past = cache_k.shape[1]
    spm, w_in, w_out, w_shapes = _cast_plan(weights, layer, dec_batch * qb, lambda b, i: b * qb + i)
    return pl.pallas_call(
        functools.partial(_lat_attn_kernel, seq=dec_seq, steps_per_matrix=spm),
        grid=(dec_batch, qb),
        in_specs=[
            pl.BlockSpec(memory_space=pltpu.SMEM),
            pl.BlockSpec((ATTN_BLK, ATTN_WIDTH), lambda b, i: (q_off + b * qb + i, 0)),
            pl.BlockSpec((dec_seq, 2 * KV_WIDTH), lambda b, i: (s_off + b, kv_col_block)),
            pl.BlockSpec((None, past, KV_WIDTH), lambda b, i: (b, 0, 0)),
            pl.BlockSpec((None, past, KV_WIDTH), lambda b, i: (b, 0, 0)),
        ] + w_in,
        out_specs=[pl.BlockSpec((ATTN_BLK, ATTN_WIDTH), lambda b, i: (b * qb + i, 0))] + w_out,
        out_shape=[jax.ShapeDtypeStruct((dec_batch * dec_seq, ATTN_WIDTH), BF16)] + w_shapes,
        compiler_params=_cparams(("arbitrary", "arbitrary")),
        name="lat_attention",
    )(sink, proj, proj, cache_k, cache_v, *weights)


def _sgu_kernel(ul_ref, uh_ref, vl_ref, vh_ref, g_ref, w_ref, bt_ref, z_ref):
    half = SGU_WIDTH // 2
    per_half = SGU_GROUPS // 2
    u_refs = (ul_ref, uh_ref)
    for c in range(TM_SGU // CHUNK):
        rows = slice(c * CHUNK, (c + 1) * CHUNK)
        vl = vl_ref[rows, :].astype(F32)
        vh = vh_ref[rows, :].astype(F32)
        ssq = jnp.sum(vl * vl, axis=-1, keepdims=True) + jnp.sum(vh * vh, axis=-1, keepdims=True)
        r = lax.rsqrt(ssq * (1.0 / SGU_WIDTH) + EPS)
        vn = ((vl * r * g_ref[:, :half]).astype(BF16), (vh * r * g_ref[:, half:]).astype(BF16))
        for g in range(SGU_GROUPS):
            src = g // per_half
            cols = slice((g % per_half) * SGU_GROUP_DIM, (g % per_half + 1) * SGU_GROUP_DIM)
            mixed = jnp.dot(w_ref[g], vn[src][:, cols], preferred_element_type=F32) + bt_ref[:, g:g + 1]
            z_ref[rows, g * SGU_GROUP_DIM:(g + 1) * SGU_GROUP_DIM] = (
                u_refs[src][rows, cols].astype(F32) * mixed).astype(BF16)


def _sgu(proj, norm_g, w_s, b_t):
    t = proj.shape[0]
    half = SGU_WIDTH // 2
    u0 = (ATTN_WIDTH + 2 * KV_WIDTH) // half
    blocks = [pl.BlockSpec((TM_SGU, half), functools.partial(lambda i, c: (i, c), c=u0 + k)) for k in range(4)]
    return pl.pallas_call(
        _sgu_kernel,
        grid=(t // TM_SGU,),
        in_specs=blocks + [
            pl.BlockSpec((1, SGU_WIDTH), lambda i: (0, 0)),
            pl.BlockSpec((SGU_GROUPS, CHUNK, CHUNK), lambda i: (0, 0, 0)),
            pl.BlockSpec((CHUNK, SGU_GROUPS), lambda i: (0, 0)),
        ],
        out_specs=pl.BlockSpec((TM_SGU, SGU_WIDTH), lambda i: (i, 0)),
        out_shape=jax.ShapeDtypeStruct((t, SGU_WIDTH), BF16),
        compiler_params=_cparams(("arbitrary",)),
        name="sgu",
    )(proj, proj, proj, proj, norm_g, w_s, b_t)


def _residual_router(rows, x, acc, mod_ref, g2_ref, wrt_ref, brt_ref, x1_ref, h2_ref, lg_ref):
    x1 = x + mod_ref[2:3, :] * acc
    x1_ref[rows, :] = x1
    h2 = _rms_mod(x1, g2_ref[...], mod_ref[4:5, :], mod_ref[3:4, :])
    h2_ref[rows, :] = _pack_bf16_pairs(h2)
    lg = lax.dot_general(wrt_ref[...], h2.astype(BF16), (((1,), (1,)), ((), ())), preferred_element_type=F32)
    lg_ref[:, rows] = lg + brt_ref[...]


def _sub_blocks(ref):
    return [slice(r, r + TM_OUT) for r in range(0, ref.shape[0], TM_OUT)]


def _out_even_kernel(xp_ref, xs_ref, ap_ref, as_ref, z_ref, w_ref, mod_ref, g2_ref, wrt_ref, brt_ref,
                     x1_ref, h2_ref, lg_ref, *, n_prompt_blocks):
    is_prompt = pl.program_id(0) < n_prompt_blocks
    for rows in _sub_blocks(x1_ref):
        a = jnp.where(is_prompt, ap_ref[rows, :].astype(F32), as_ref[rows, :].astype(F32)).astype(BF16)
        acc = jnp.dot(a, w_ref[0:ATTN_WIDTH, :], preferred_element_type=F32)
        acc = acc + jnp.dot(z_ref[rows, :], w_ref[ATTN_WIDTH:, :], preferred_element_type=F32)
        x = jnp.where(is_prompt, xp_ref[rows, :], xs_ref[rows, :])
        _residual_router(rows, x, acc, mod_ref, g2_ref, wrt_ref, brt_ref, x1_ref, h2_ref, lg_ref)


def _out_odd_kernel(x_ref, hc_ref, w_ref, mod_ref, g2_ref, wrt_ref, brt_ref, x1_ref, h2_ref, lg_ref):
    for rows in _sub_blocks(x1_ref):
        acc = jnp.dot(hc_ref[rows, :], w_ref[...], preferred_element_type=F32)
        _residual_router(rows, x_ref[rows, :], acc, mod_ref, g2_ref, wrt_ref, brt_ref, x1_ref, h2_ref, lg_ref)


def _out_proj(xs, lhs, w, mods, gain2, wrt, brt, n_prompt, dec_seq):
    d = xs[0].shape[1]
    t = sum(x.shape[0] for x in xs)
    tm = TM_OUTPROJ
    npb = n_prompt // tm
    bps = dec_seq // tm
    nsb = t // tm - npb
    if len(xs) == 2:
        x_specs = [pl.BlockSpec((tm, d), lambda i: (jnp.minimum(i, npb - 1), 0)),
                   pl.BlockSpec((tm, d), lambda i: (jnp.maximum(i - npb, 0), 0))]
    else:
        x_specs = [pl.BlockSpec((tm, d), lambda i: (i, 0))]

    def mod_idx(i):
        return (jnp.where(i < npb, 0, 1 + (i - npb) // bps), 0, 0)

    if len(lhs) == 3:
        a_p, a_s, z = lhs
        body = functools.partial(_out_even_kernel, n_prompt_blocks=npb)
        lhs_specs = [
            pl.BlockSpec((tm, ATTN_WIDTH), lambda i: (jnp.minimum(i, npb - 1), 0)),
            pl.BlockSpec((tm, ATTN_WIDTH), lambda i: (jnp.clip(i - npb, 0, nsb - 1), 0)),
            pl.BlockSpec((tm, SGU_WIDTH), lambda i: (i, 0)),
        ]
    else:
        body = _out_odd_kernel
        lhs_specs = [pl.BlockSpec((tm, d), lambda i: (i, 0))]
    return pl.pallas_call(
        body,
        grid=(t // tm,),
        in_specs=x_specs + lhs_specs + [
            pl.BlockSpec((w.shape[0], d), lambda i: (0, 0), pipeline_mode=pl.Buffered(1)),
            pl.BlockSpec((None, 6, d), mod_idx),
            pl.BlockSpec((1, d), lambda i: (0, 0)),
            pl.BlockSpec((ROUTER_LANES, d), lambda i: (0, 0)),
            pl.BlockSpec((ROUTER_LANES, 1), lambda i: (0, 0)),
        ],
        out_specs=[
            pl.BlockSpec((tm, d), lambda i: (i, 0)),
            pl.BlockSpec((tm, d // 2), lambda i: (i, 0)),
            pl.BlockSpec((ROUTER_LANES, tm), lambda i: (0, i)),
        ],
        out_shape=[
            jax.ShapeDtypeStruct((t, d), F32),
            jax.ShapeDtypeStruct((t, d // 2), jnp.uint32),
            jax.ShapeDtypeStruct((ROUTER_LANES, t), F32),
        ],
        compiler_params=_cparams(("arbitrary",)),
        name="out_proj",
    )(*xs, *lhs, w, mods, gain2, wrt, brt)


def _first_argmax4(v0, v1, v2, v3):
    m = jnp.maximum(jnp.maximum(v0, v1), jnp.maximum(v2, v3))
    idx = jnp.where(v0 == m, 0.0, jnp.where(v1 == m, 1.0, jnp.where(v2 == m, 2.0, 3.0)))
    return m, idx


def _route_kernel(lg_ref, pos_ref, ea_ref, eb_ref, nv_ref, ends_ref, oh_scr, rank_scr, *, n_tokens):
    lg = lg_ref[...]
    rows = [lg[r:r + 1, :] for r in range(N_GROUPS_MOE + N_EXPERTS)]
    _, gidx = _first_argmax4(*rows[:N_GROUPS_MOE])
    e = []
    for k in range(EXPERTS_PER_GROUP):
        cand = [rows[N_GROUPS_MOE + g * EXPERTS_PER_GROUP + k] for g in range(N_GROUPS_MOE)]
        e.append(jnp.where(gidx == 0.0, cand[0], jnp.where(gidx == 1.0, cand[1], jnp.where(gidx == 2.0, cand[2], cand[3]))))
    _, l1 = _first_argmax4(*e)
    e2 = [jnp.where(l1 == float(k), -jnp.inf, e[k]) for k in range(EXPERTS_PER_GROUP)]
    _, l2 = _first_argmax4(*e2)
    lo = jnp.minimum(l1, l2)
    hi = jnp.maximum(l1, l2)
    pair = jnp.where(lo == 0.0, hi - 1.0, jnp.where(lo == 2.0, 4.0, jnp.where(hi == 3.0, 3.0, 5.0)))
    cls = gidx * float(PAIRS_PER_GROUP) + pair

    crow = lax.broadcasted_iota(jnp.int32, (CLASS_ROWS, n_tokens), 0).astype(F32)
    oh_scr[...] = jnp.where(crow == cls, 1.0, 0.0).astype(BF16)

    jj = lax.broadcasted_iota(jnp.int32, (CUM_CHUNK, CUM_CHUNK), 0)
    ii = lax.broadcasted_iota(jnp.int32, (CUM_CHUNK, CUM_CHUNK), 1)
    tri = jnp.where(jj < ii, 1.0, 0.0).astype(BF16)

    def chunk(c, carry):
        off = pl.multiple_of(c * CUM_CHUNK, CUM_CHUNK)
        oh = oh_scr[:, pl.ds(off, CUM_CHUNK)]
        ohf = oh.astype(F32)
        cum = jnp.dot(oh, tri, preferred_element_type=F32) + carry
        rank_scr[:, pl.ds(off, CUM_CHUNK)] = jnp.sum(cum * ohf, axis=0, keepdims=True)
        return carry + jnp.sum(ohf, axis=1, keepdims=True)

    counts = lax.fori_loop(0, n_tokens // CUM_CHUNK, chunk, jnp.zeros((CLASS_ROWS, 1), F32))
    padded = jnp.floor((counts + float(MOE_BLK - 1)) * (1.0 / MOE_BLK)) * float(MOE_BLK)
    run = jnp.zeros((1, 1), F32)
    starts_rows = []
    for c in range(CLASS_ROWS):
        starts_rows.append(run)
        run = run + padded[c:c + 1, :]
    starts = jnp.concatenate(starts_rows, axis=0)
    ends = starts + padded
    total = run

    ohf = oh_scr[...].astype(F32)
    pos = rank_scr[...] + jnp.sum(ohf * starts, axis=0, keepdims=True)
    pos_ref[...] = pos.astype(jnp.int32)

    bstart = lax.broadcasted_iota(jnp.int32, (CLASS_ROWS, MAX_MOE_BLOCKS), 1).astype(F32) * float(MOE_BLK)
    brow = lax.broadcasted_iota(jnp.int32, (CLASS_ROWS, MAX_MOE_BLOCKS), 0)
    done = jnp.where((ends <= bstart) & (brow < N_CLASSES), 1.0, 0.0)
    bcls = jnp.minimum(jnp.sum(done, axis=0, keepdims=True), float(N_CLASSES - 1))
    grp = (jnp.where(bcls >= 6.0, 1.0, 0.0) + jnp.where(bcls >= 12.0, 1.0, 0.0) + jnp.where(bcls >= 18.0, 1.0, 0.0))
    bp = bcls - grp * float(PAIRS_PER_GROUP)
    first = jnp.where(bp >= 3.0, 1.0, 0.0) + jnp.where(bp >= 4.0, 1.0, 0.0)
    second = jnp.where(bp < 3.0, bp + 1.0, jnp.where(bp < 5.0, 3.0, 1.0))
    ea_ref[...] = (grp * float(EXPERTS_PER_GROUP) + first).astype(jnp.int32)
    eb_ref[...] = (grp * float(EXPERTS_PER_GROUP) + second).astype(jnp.int32)
    own = jnp.where(brow.astype(F32) == bcls, starts + counts, 0.0)
    filled = jnp.sum(own, axis=0, keepdims=True) - bstart[0:1, :]
    nv_ref[...] = jnp.clip(filled, 0.0, float(MOE_BLK)).astype(jnp.int32)
    lane = lax.broadcasted_iota(jnp.int32, (1, MAX_MOE_BLOCKS), 1)
    ends_row = jnp.zeros((1, MAX_MOE_BLOCKS), F32)
    for c in range(N_CLASSES):
        ends_row = jnp.where(lane == c, starts_rows[c] + padded[c:c + 1, :], ends_row)
    ends_ref[...] = ends_row.astype(jnp.int32)


def _route(logits_t):
    t = logits_t.shape[1]
    rows = N_CLASSES
    i32 = jnp.int32
    return pl.pallas_call(
        functools.partial(_route_kernel, n_tokens=t),
        grid=(1,),
        in_specs=[pl.BlockSpec((rows, t), lambda i: (0, 0))],
        out_specs=[
            pl.BlockSpec((1, t), lambda i: (0, 0)),
        ] + [pl.BlockSpec((1, MAX_MOE_BLOCKS), lambda i: (0, 0))] * 4,
        out_shape=[jax.ShapeDtypeStruct((1, t), i32)] + [jax.ShapeDtypeStruct((1, MAX_MOE_BLOCKS), i32)] * 4,
        scratch_shapes=[pltpu.VMEM((CLASS_ROWS, t), BF16), pltpu.VMEM((1, t), F32)],
        compiler_params=_cparams(("arbitrary",)),
        name="route",
    )(logits_t)


DMA_THREADS = 2


def _row_copy(src_ref, src_row, dst_ref, dst_row, sem):
    return pltpu.make_async_copy(src_ref.at[pl.ds(src_row, 1), :], dst_ref.at[pl.ds(dst_row, 1), :], sem)


def _start_row_copies(n_rows, make_copy):
    for r in range(n_rows):
        make_copy(r).start(priority=r % DMA_THREADS)


def _dispatch_kernel(pos_ref, ends_ref, h_ref, o_ref, zbuf, sem, zsem):
    step = pl.program_id(0)

    @pl.when(step == 0)
    def _():
        zbuf[...] = jnp.zeros_like(zbuf)
        total = ends_ref[N_CLASSES - 1]
        n_free = (o_ref.shape[0] - total) // MOE_BLK

        def zero_block(first_row, phase):
            first = pl.multiple_of(first_row, MOE_BLK)
            getattr(pltpu.make_async_copy(zbuf, o_ref.at[pl.ds(first, MOE_BLK), :], zsem), phase)()

        for phase in ("start", "wait"):
            prev = 0
            for c in range(N_CLASSES):
                end = ends_ref[c]
                pl.when(end > prev)(functools.partial(zero_block, end - MOE_BLK, phase))
                prev = end

            def free_block(k, carry, phase=phase):
                zero_block(total + k * MOE_BLK, phase)
                return carry

            lax.fori_loop(0, n_free, free_block, 0)

    rows = h_ref.shape[0]
    base = step * rows

    _start_row_copies(rows, lambda r: _row_copy(h_ref, r, o_ref, pos_ref[base + r], sem))
    pltpu.make_async_copy(h_ref, o_ref.at[pl.ds(0, rows), :], sem).wait()


def _dispatch(pos, ends, h2, n_rows):
    t, d = h2.shape
    return pl.pallas_call(
        _dispatch_kernel,
        grid_spec=pltpu.PrefetchScalarGridSpec(
            num_scalar_prefetch=2,
            grid=(t // TM_DISPATCH,),
            in_specs=[pl.BlockSpec((TM_DISPATCH, d), lambda i, pos, ends: (i, 0))],
            out_specs=pl.BlockSpec(memory_space=pl.ANY),
            scratch_shapes=[pltpu.VMEM((MOE_BLK, d), h2.dtype), pltpu.SemaphoreType.DMA(()),
                            pltpu.SemaphoreType.DMA(())],
        ),
        out_shape=jax.ShapeDtypeStruct((n_rows, d), h2.dtype),
        compiler_params=_cparams(("arbitrary",)),
        name="moe_dispatch",
    )(pos, ends, h2)


def _expert_ffn(x, wg_ref, wu_ref, wd_ref):
    gate = jnp.dot(x, wg_ref[...], preferred_element_type=F32)
    up = jnp.dot(x, wu_ref[...], preferred_element_type=F32)
    hidden = (gate * jax.nn.sigmoid(gate) * up).astype(BF16)
    return jnp.dot(hidden, wd_ref[...], preferred_element_type=F32)


def _moe_kernel(ea_ref, eb_ref, nv_ref, x_ref, wr_ref, br_ref, wga, wua, wda, wgb, wub, wdb, y_ref):
    b = pl.program_id(0)
    n_valid = nv_ref[b]

    def run(rows):
        x = _unpack_bf16_pairs(x_ref[rows, :])
        lg = jnp.dot(x, wr_ref[...], preferred_element_type=F32) + br_ref[...]
        lane = lax.broadcasted_iota(jnp.int32, lg.shape, 1)
        ea = ea_ref[b]
        eb = eb_ref[b]
        grp = ea // EXPERTS_PER_GROUP
        is_grp = lane < N_GROUPS_MOE
        gm = jnp.max(jnp.where(is_grp, lg, -jnp.inf), axis=-1, keepdims=True)
        ge = jnp.where(is_grp, jnp.exp(lg - gm), 0.0)
        p_g = jnp.sum(jnp.where(lane == grp, ge, 0.0), axis=-1, keepdims=True) / jnp.sum(ge, axis=-1, keepdims=True)
        la = jnp.sum(jnp.where(lane == N_GROUPS_MOE + ea, lg, 0.0), axis=-1, keepdims=True)
        lb = jnp.sum(jnp.where(lane == N_GROUPS_MOE + eb, lg, 0.0), axis=-1, keepdims=True)
        mm = jnp.maximum(la, lb)
        xa = jnp.exp(la - mm)
        xb = jnp.exp(lb - mm)
        ga = xa / (xa + xb) * p_g
        gb = xb / (xa + xb) * p_g
        ya = _expert_ffn(x, wga, wua, wda)
        yb = _expert_ffn(x, wgb, wub, wdb)
        y_ref[rows, :] = ya * ga + yb * gb

    half = MOE_BLK // 2

    @pl.when(n_valid > half)
    def _():
        run(slice(0, MOE_BLK))

    @pl.when((n_valid > 0) & (n_valid <= half))
    def _():
        run(slice(0, half))
        y_ref[half:, :] = jnp.zeros((MOE_BLK - half, y_ref.shape[1]), y_ref.dtype)

    @pl.when(n_valid == 0)
    def _():
        y_ref[...] = jnp.zeros_like(y_ref)


def _moe_experts(ea, eb, n_valid, hs, wr, br, w_gate, w_up, w_down, n_blocks):
    d = w_gate.shape[1]
    de = w_gate.shape[2]

    def wa_idx(b, ea, eb, nv):
        return (ea[b], 0, 0)

    def wb_idx(b, ea, eb, nv):
        return (eb[b], 0, 0)

    return pl.pallas_call(
        _moe_kernel,
        grid_spec=pltpu.PrefetchScalarGridSpec(
            num_scalar_prefetch=3,
            grid=(n_blocks,),
            in_specs=[
                pl.BlockSpec((MOE_BLK, hs.shape[1]), lambda b, ea, eb, nv: (jnp.where(nv[b] > 0, b, 0), 0)),
                pl.BlockSpec((d, ROUTER_LANES), lambda b, ea, eb, nv: (0, 0)),
                pl.BlockSpec((1, ROUTER_LANES), lambda b, ea, eb, nv: (0, 0)),
                pl.BlockSpec((None, d, de), wa_idx),
                pl.BlockSpec((None, d, de), wa_idx),
                pl.BlockSpec((None, de, d), wa_idx),
                pl.BlockSpec((None, d, de), wb_idx),
                pl.BlockSpec((None, d, de), wb_idx),
                pl.BlockSpec((None, de, d), wb_idx),
            ],
            out_specs=pl.BlockSpec((MOE_BLK, d), lambda b, ea, eb, nv: (b, 0)),
        ),
        out_shape=jax.ShapeDtypeStruct((n_blocks * MOE_BLK, d), F32),
        compiler_params=_cparams(("arbitrary",)),
        name="moe_experts",
    )(ea, eb, n_valid, hs, wr, br, w_gate, w_up, w_down, w_gate, w_up, w_down)


def _combine_kernel(pos_ref, ys_ref, x1_ref, mod_ref, gf_ref, o_ref, ybuf, sem, *, block_off, n_steps, final):
    i = pl.program_id(0)
    slot = i % 2
    tm = x1_ref.shape[0]

    def start_gather(step, into):
        base = (step + block_off) * tm
        _start_row_copies(tm, lambda r: _row_copy(ys_ref, pos_ref[base + r], ybuf.at[into], r, sem.at[into]))

    pl.when(i == 0)(lambda: start_gather(0, 0))
    pl.when(i + 1 < n_steps)(lambda: start_gather(i + 1, 1 - slot))
    pltpu.make_async_copy(ys_ref.at[pl.ds(0, tm), :], ybuf.at[slot], sem.at[slot]).wait()
    x2 = x1_ref[...] + mod_ref[5:6, :] * ybuf[slot]
    if final:
        ms = jnp.mean(x2 * x2, axis=-1, keepdims=True)
        x2 = x2 * lax.rsqrt(ms + EPS) * gf_ref[...]
    o_ref[...] = x2


def _combine(pos, ys, x1, mods, gain_f, n_prompt, dec_seq, row_off, n_rows, final):
    d = x1.shape[1]
    tm = TM_COMBINE
    npb = n_prompt // tm
    bps = dec_seq // tm
    boff = row_off // tm

    def mod_idx(i, pos):
        blk = i + boff
        return (jnp.where(blk < npb, 0, 1 + (blk - npb) // bps), 0, 0)

    return pl.pallas_call(
        functools.partial(_combine_kernel, block_off=boff, n_steps=n_rows // tm, final=final),
        grid_spec=pltpu.PrefetchScalarGridSpec(
            num_scalar_prefetch=1,
            grid=(n_rows // tm,),
            in_specs=[
                pl.BlockSpec(memory_space=pl.ANY),
                pl.BlockSpec((tm, d), lambda i, pos: (i + boff, 0)),
                pl.BlockSpec((None, 6, d), mod_idx),
                pl.BlockSpec((1, d), lambda i, pos: (0, 0)),
            ],
            out_specs=pl.BlockSpec((tm, d), lambda i, pos: (i, 0)),
            scratch_shapes=[pltpu.VMEM((2, tm, d), F32), pltpu.SemaphoreType.DMA((2,))],
        ),
        out_shape=jax.ShapeDtypeStruct((n_rows, d), F32),
        compiler_params=_cparams(("arbitrary",)),
        name="moe_combine",
    )(pos, ys, x1, mods, gain_f)


def _proj_glu_kernel(pos_ref, ys_ref, x1_ref, mod_prev_ref, mod_ref, g_ref, w_ref, o_ref, x2_ref, h_scr, ybuf, sem,
                     *, n_steps):
    i = pl.program_id(0)
    slot = i % 2
    tm = x1_ref.shape[0]

    def start_gather(step, into):
        base = step * tm
        _start_row_copies(tm, lambda r: _row_copy(ys_ref, pos_ref[base + r], ybuf.at[into], r, sem.at[into]))

    pl.when(i == 0)(lambda: start_gather(0, 0))
    pl.when(i + 1 < n_steps)(lambda: start_gather(i + 1, 1 - slot))
    pltpu.make_async_copy(ys_ref.at[pl.ds(0, tm), :], ybuf.at[slot], sem.at[slot]).wait()
    x2 = x1_ref[...] + mod_prev_ref[5:6, :] * ybuf[slot]
    x2_ref[...] = x2
    h_scr[...] = _rms_mod(x2, g_ref[...], mod_ref[1:2, :], mod_ref[0:1, :]).astype(BF16)
    width = w_ref.shape[1] // 2
    for j in range(width // TN_PROJ):
        cols = slice(j * TN_PROJ, (j + 1) * TN_PROJ)
        gcols = slice(width + j * TN_PROJ, width + (j + 1) * TN_PROJ)
        val = jnp.dot(h_scr[...], w_ref[:, cols], preferred_element_type=F32)
        gate = jnp.dot(h_scr[...], w_ref[:, gcols], preferred_element_type=F32)
        o_ref[:, cols] = (val * jax.nn.sigmoid(gate)).astype(BF16)


def _proj_glu(pos, ys, x1, mods_prev, mods, gain, w, n_prompt, dec_seq):
    t, d = x1.shape
    n = w.shape[1]
    tm = TM_PROJ
    npb = n_prompt // tm
    bps = dec_seq // tm

    def mod_idx(i, pos):
        return (jnp.where(i < npb, 0, 1 + (i - npb) // bps), 0, 0)

    return pl.pallas_call(
        functools.partial(_proj_glu_kernel, n_steps=t // tm),
        grid_spec=pltpu.PrefetchScalarGridSpec(
            num_scalar_prefetch=1,
            grid=(t // tm,),
            in_specs=[
                pl.BlockSpec(memory_space=pl.ANY),
                pl.BlockSpec((tm, d), lambda i, pos: (i, 0)),
                pl.BlockSpec((None, 6, d), mod_idx),
                pl.BlockSpec((None, 6, d), mod_idx),
                pl.BlockSpec((1, d), lambda i, pos: (0, 0)),
                pl.BlockSpec((d, n), lambda i, pos: (0, 0), pipeline_mode=pl.Buffered(1)),
            ],
            out_specs=[
                pl.BlockSpec((tm, n // 2), lambda i, pos: (i, 0)),
                pl.BlockSpec((tm, d), lambda i, pos: (i, 0)),
            ],
            scratch_shapes=[pltpu.VMEM((tm, d), BF16), pltpu.VMEM((2, tm, d), F32), pltpu.SemaphoreType.DMA((2,))],
        ),
        out_shape=[jax.ShapeDtypeStruct((t, n // 2), BF16), jax.ShapeDtypeStruct((t, d), F32)],
        compiler_params=_cparams(("arbitrary",)),
        name="proj_glu",
    )(pos, ys, x1, mods_prev, mods, gain, w)


CONV_ROWS = 128
CONV_COLS = 128
F32_SUBLANES = 8


def _conv_kernel(x_ref, prev_ref, next_ref, dw_ref, b_ref, g_ref, wg_ref, wu_ref, wd_ref, o_ref, wg_out, wu_out, wd_out,
                 xpad, shifted, ybuf, *, n_prompt_blocks, blocks_per_seq, steps_per_matrix):
    i = pl.program_id(0)
    _cast_slices(i, steps_per_matrix, (wg_ref, wu_ref, wd_ref), (wg_out, wu_out, wd_out))
    in_sample = i >= n_prompt_blocks
    j = i - n_prompt_blocks
    is_start = jnp.logical_or(jnp.logical_not(in_sample), j % blocks_per_seq == 0)
    is_end = jnp.logical_or(jnp.logical_not(in_sample), j % blocks_per_seq == blocks_per_seq - 1)
    tm = TM_OUT
    xpad[0:CONV_HALO, :] = jnp.where(is_start, 0.0, prev_ref[...].astype(F32))
    xpad[CONV_HALO:CONV_HALO + tm, :] = x_ref[...].astype(F32)
    xpad[CONV_HALO + tm:, :] = jnp.where(is_end, 0.0, next_ref[...].astype(F32))
    first = CONV_HALO - CONV_K // 2
    span = shifted.shape[1]

    def col_chunk(c, ssq):
        c0 = pl.multiple_of(c * CONV_COLS, CONV_COLS)
        cols = pl.ds(c0, CONV_COLS)
        w = dw_ref[:, cols]
        bias = b_ref[:, cols]
        for s in range(1, F32_SUBLANES):
            shifted[s - 1] = xpad[s:s + span, cols]
        parts = []
        for rc in range(tm // CONV_ROWS):
            r0 = rc * CONV_ROWS
            acc = jnp.zeros((CONV_ROWS, CONV_COLS), F32)
            for k in range(CONV_K):
                whole, s = divmod(first + k, F32_SUBLANES)
                base = whole * F32_SUBLANES + r0
                if s == 0:
                    src = xpad[base:base + CONV_ROWS, cols]
                else:
                    src = shifted[s - 1, base:base + CONV_ROWS, :]
                acc = acc + w[k:k + 1, :] * src
            acc = acc + bias
            ybuf[r0:r0 + CONV_ROWS, cols] = acc
            parts.append(jnp.sum(acc * acc, axis=-1, keepdims=True))
        return ssq + jnp.concatenate(parts, axis=0)

    d = x_ref.shape[1]
    ssq = lax.fori_loop(0, d // CONV_COLS, col_chunk, jnp.zeros((tm, 1), F32))
    y = ybuf[...] * lax.rsqrt(ssq * (1.0 / d) + EPS) * g_ref[...]
    o_ref[...] = (y * jax.nn.sigmoid(y)).astype(BF16)


def _conv_mix(xg, dw, dw_b, norm_g, n_prompt, seq, dec_seq, weights, layer):
    t, d = xg.shape
    tm = TM_OUT
    assert seq == tm, "context sequences must be exactly one conv row block"
    npb = n_prompt // tm
    bps = dec_seq // tm
    hb = tm // CONV_HALO
    last = t // CONV_HALO - 1
    spm, w_in, w_out, w_shapes = _cast_plan(weights, layer, t // tm, lambda i: i)
    return pl.pallas_call(
        functools.partial(_conv_kernel, n_prompt_blocks=npb, blocks_per_seq=bps, steps_per_matrix=spm),
        grid=(t // tm,),
        in_specs=[
            pl.BlockSpec((tm, d), lambda i: (i, 0)),
            pl.BlockSpec((CONV_HALO, d), lambda i: (jnp.maximum(i * hb - 1, 0), 0)),
            pl.BlockSpec((CONV_HALO, d), lambda i: (jnp.minimum((i + 1) * hb, last), 0)),
            pl.BlockSpec((dw.shape[0], d), lambda i: (0, 0)),
            pl.BlockSpec((1, d), lambda i: (0, 0)),
            pl.BlockSpec((1, d), lambda i: (0, 0)),
        ] + w_in,
        out_specs=[pl.BlockSpec((tm, d), lambda i: (i, 0))] + w_out,
        out_shape=[jax.ShapeDtypeStruct((t, d), BF16)] + w_shapes,
        scratch_shapes=[
            pltpu.VMEM((tm + 2 * CONV_HALO, d), F32),
            pltpu.VMEM((F32_SUBLANES - 1, tm + 2 * CONV_HALO - F32_SUBLANES, CONV_COLS), F32),
            pltpu.VMEM((tm, d), F32),
        ],
        compiler_params=_cparams(("arbitrary",)),
        name="conv_mix",
    )(xg, xg, xg, dw, dw_b, norm_g, *weights)


def _router_operands(w_rg, b_rg, w_re, b_re):
    d = w_rg.shape[0]
    used = N_GROUPS_MOE + N_EXPERTS
    w = jnp.concatenate([w_rg, w_re, jnp.zeros((d, ROUTER_LANES - used), F32)], axis=1)
    b = jnp.concatenate([b_rg, b_re, jnp.zeros((ROUTER_LANES - used,), F32)])
    return w.astype(BF16), b.reshape(1, ROUTER_LANES), w.T.astype(BF16), b.reshape(ROUTER_LANES, 1)


def _moe_rows(h2, logits_t, wr, br, wg, wu, wd):
    t = h2.shape[0]
    n_blocks = -(-(t + N_CLASSES * (MOE_BLK - 1)) // MOE_BLK)
    assert n_blocks <= MAX_MOE_BLOCKS
    pos, ea, eb, n_valid, ends = _route(logits_t)
    pos = pos.reshape(t)
    hs = _dispatch(pos, ends.reshape(-1), h2, n_blocks * MOE_BLK)
    ys = _moe_experts(ea.reshape(-1), eb.reshape(-1), n_valid.reshape(-1), hs, wr, br, wg, wu, wd, n_blocks)
    return pos, ys


def kernel(x_prompt, x_sample, cache_k, cache_v, c, c_ctx, w_in_even, attn_sink, sgu_norm, sgu_w, sgu_b,
           w_out_even, conv_w_in, conv_dw, conv_dw_b, conv_norm, conv_w_out, ada_w, ada_b, norm_mix, norm_ffn,
           router_group_w, router_group_b, router_expert_w, router_expert_b, expert_w_gate, expert_w_up,
           expert_w_down, final_norm):
    batch, seq, d = x_prompt.shape
    dec_batch, dec_seq, _ = x_sample.shape
    depth = ada_w.shape[0]
    n_prompt = batch * seq
    n_sample = dec_batch * dec_seq
    assert d == D_MODEL and depth == 2 and n_prompt % dec_seq == 0 and dec_seq % TM_PROJ == 0
    x_p = x_prompt.reshape(n_prompt, d)
    x_s = x_sample.reshape(n_sample, d)

    n_mod = -(-(1 + dec_batch) // 8) * 8
    cvec = jnp.concatenate([c_ctx[None, :], c, jnp.zeros((n_mod - 1 - dec_batch, d), F32)], axis=0)
    mods = _ada_params(cvec, ada_w, ada_b).reshape(depth, n_mod, 6, d)
    expert_w = (expert_w_gate, expert_w_up, expert_w_down)

    cos_t, sin_t = _rope_tables(dec_seq, TM_PROJ)
    proj, kv32 = _proj_even(x_p, x_s, mods[0], norm_mix[0][None, :], w_in_even[0].astype(BF16), cos_t, sin_t,
                            dec_seq // TM_PROJ)
    kv_col_block = ATTN_WIDTH // (2 * KV_WIDTH)
    sink = attn_sink[0]
    a_p = _ctx_attention(proj, sink, batch, seq, kv_col_block)
    past = cache_k.shape[2]
    ck = cache_k[:, 0].reshape(dec_batch, past, KV_WIDTH).astype(BF16)
    cv = cache_v[:, 0].reshape(dec_batch, past, KV_WIDTH).astype(BF16)
    a_s, *experts0 = _lat_attention(proj, sink, ck, cv, n_prompt, dec_batch, dec_seq, kv_col_block, expert_w, 0)
    z = _sgu(proj, sgu_norm[0][None, :], sgu_w[0].astype(BF16), sgu_b[0].T)

    wr0, br0, wrt0, brt0 = _router_operands(router_group_w[0], router_group_b[0], router_expert_w[0], router_expert_b[0])
    x1, h2, lg = _out_proj((x_p, x_s), (a_p, a_s, z), w_out_even[0].astype(BF16), mods[0], norm_ffn[0][None, :],
                           wrt0, brt0, n_prompt, dec_seq)
    pos, ys = _moe_rows(h2, lg, wr0, br0, *experts0)
    gain_f = final_norm[None, :]

    xg, x = _proj_glu(pos, ys, x1, mods[0], mods[1], norm_mix[1][None, :], conv_w_in[0].astype(BF16),
                      n_prompt, dec_seq)
    dw = jnp.concatenate([conv_dw[0], jnp.zeros((1, d), F32)], axis=0)
    hc, *experts1 = _conv_mix(xg, dw, conv_dw_b[0][None, :], conv_norm[0][None, :], n_prompt, seq, dec_seq,
                              expert_w, 1)
    wr1, br1, wrt1, brt1 = _router_operands(router_group_w[1], router_group_b[1], router_expert_w[1], router_expert_b[1])
    x1, h2, lg = _out_proj((x,), (hc,), conv_w_out[0].astype(BF16), mods[1], norm_ffn[1][None, :],
                           wrt1, brt1, n_prompt, dec_seq)
    pos, ys = _moe_rows(h2, lg, wr1, br1, *experts1)
    y_prompt = _combine(pos, ys, x1, mods[1], gain_f, n_prompt, dec_seq, 0, n_prompt, final=True)
    y_sample = _combine(pos, ys, x1, mods[1], gain_f, n_prompt, dec_seq, n_prompt, n_sample, final=True)

    new_k = kv32[:n_prompt, :KV_WIDTH].reshape(batch, 1, seq, N_KV_HEADS, HEAD_DIM)
    new_v = kv32[:n_prompt, KV_WIDTH:].reshape(batch, 1, seq, N_KV_HEADS, HEAD_DIM)
    return (y_prompt.reshape(batch, seq, d), y_sample.reshape(dec_batch, dec_seq, d), new_k, new_v)
```
